```python
import math
import jax, jax.numpy as jnp
from jax import lax
import numpy as np

D_MODEL = 4096
BATCH = 4
SEQ = 4096
DEPTH = 1
DEC_BATCH = 16
DEC_SEQ = 32
PAST_LEN = 1024

CHUNK = 64
Q_BLOCK = 128
HEAD_DIM = 128
ROPE_THETA = 10000.0
EPS = 1e-6
DSA_HEADS = 16
DSA_KV_HEADS = 4
IDX_HEADS = 16
IDX_DIM = 128
DSA_TOPK = 256
DIFF_HEADS = 8
DIFF_D = 128
DSA_WIDTH = DSA_HEADS * HEAD_DIM
DIFF_WIDTH = DIFF_HEADS * 2 * DIFF_D
MIX_WIDTH = DSA_WIDTH + DIFF_WIDTH
IN_SIZES = (DSA_WIDTH, DSA_KV_HEADS * HEAD_DIM, DSA_KV_HEADS * HEAD_DIM, IDX_HEADS * IDX_DIM, IDX_DIM, IDX_HEADS, DIFF_WIDTH, DIFF_WIDTH, DIFF_WIDTH)
IN_WIDTH = sum(IN_SIZES)
PEER_HEADS = 8
PEER_N_KEYS = 128
PEER_EXPERTS = PEER_N_KEYS * PEER_N_KEYS
PEER_DK = 256
PEER_HALF = PEER_DK // 2
PEER_TOPK = 16
PEER_BLOCK = 128

kernel_name = "hybrid_dsa_diffattn_peer_stream_step"


def lambda_init(layer):
    return 0.8 - 0.6 * math.exp(-0.3 * layer)


def rmsnorm(x, g):
    xf = x.astype(jnp.float32)
    y = xf * lax.rsqrt(jnp.mean(xf * xf, axis=-1, keepdims=True) + EPS)
    return (y * g.astype(jnp.float32)).astype(x.dtype)


def rope(x, pos):
    half = x.shape[-1] // 2
    inv = ROPE_THETA ** (-jnp.arange(half, dtype=jnp.float32) / half)
    ang = pos.astype(jnp.float32)[:, None] * inv[None, :]
    ang = ang.reshape(ang.shape[0], *([1] * (x.ndim - 3)), half)
    cos, sin = jnp.cos(ang), jnp.sin(ang)
    x1 = x[..., :half].astype(jnp.float32)
    x2 = x[..., half:].astype(jnp.float32)
    return jnp.concatenate([x1 * cos - x2 * sin, x2 * cos + x1 * sin], axis=-1).astype(x.dtype)


def chunk_visible(q_pos, k_pos):
    return (k_pos[None, :] // CHUNK) <= (q_pos[:, None] // CHUNK)


def map_query_blocks(fn, q_arrays, q_pos):
    n_blk = q_pos.shape[0] // Q_BLOCK

    def split(a):
        a = a.reshape(a.shape[0], n_blk, Q_BLOCK, *a.shape[2:])
        return jnp.moveaxis(a, 1, 0)

    xs = (tuple(split(a) for a in q_arrays), q_pos.reshape(n_blk, Q_BLOCK))
    out = lax.map(lambda xp: fn(*xp[0], xp[1]), xs)
    out = jnp.moveaxis(out, 0, 1)
    return out.reshape(out.shape[0], n_blk * Q_BLOCK, *out.shape[3:])


def dsa_attend(q, qi, wi, q_pos, k, v, ki, k_pos, n_sel):
    b, t = q.shape[0], q.shape[1]
    visible = chunk_visible(q_pos, k_pos)
    logits = jnp.einsum('bthd,bsd->bths', qi, ki, preferred_element_type=jnp.float32) * (IDX_DIM ** -0.5)
    score = jnp.einsum('bths,bth->bts', jax.nn.relu(logits), wi.astype(jnp.float32)) * (IDX_HEADS ** -0.5)
    score = jnp.where(visible[None], score, -jnp.inf)
    _, sel = lax.top_k(score, n_sel)
    sel_ok = (k_pos[sel] // CHUNK) <= (q_pos[None, :, None] // CHUNK)
    gather = jax.vmap(lambda rows, idx: rows[idx])
    k_sel = gather(k, sel)
    v_sel = gather(v, sel)
    qg = q.reshape(b, t, DSA_KV_HEADS, DSA_HEADS // DSA_KV_HEADS, HEAD_DIM)
    s = jnp.einsum('btgrd,btngd->btgrn', qg, k_sel, preferred_element_type=jnp.float32) * (HEAD_DIM ** -0.5)
    s = jnp.where(sel_ok[:, :, None, None, :], s, -jnp.inf)
    p = jax.nn.softmax(s, axis=-1).astype(v.dtype)
    o = jnp.einsum('btgrn,btngd->btgrd', p, v_sel)
    return o.reshape(b, t, DSA_WIDTH)


def diff_attend(dq, q_pos, dk, dv, k_pos, lam):
    visible = chunk_visible(q_pos, k_pos)
    s = jnp.einsum('bthcd,bshcd->bhcts', dq, dk, preferred_element_type=jnp.float32) * (DIFF_D ** -0.5)
    s = jnp.where(visible[None, None, None], s, -jnp.inf)
    p = jax.nn.softmax(s, axis=-1)
    a = (p[:, :, 0] - lam * p[:, :, 1]).astype(dv.dtype)
    return jnp.einsum('bhts,bshe->bthe', a, dv)


def peer_ffn(h, w_query, sub_keys, u, v):
    lead = h.shape[:-1]
    xt = h.reshape(-1, h.shape[-1])
    n_tok = xt.shape[0]
    q = (xt @ w_query).reshape(n_tok, PEER_HEADS, 2, PEER_HALF)
    s = jnp.einsum('thcd,hcnd->thcn', q, sub_keys, preferred_element_type=jnp.float32)
    s1, i1 = lax.top_k(s[:, :, 0], PEER_TOPK)
    s2, i2 = lax.top_k(s[:, :, 1], PEER_TOPK)
    cand = (s1[..., :, None] + s2[..., None, :]).reshape(n_tok, PEER_HEADS, PEER_TOPK * PEER_TOPK)
    best, flat = lax.top_k(cand, PEER_TOPK)
    expert = (jnp.take_along_axis(i1, flat // PEER_TOPK, axis=-1) * PEER_N_KEYS
              + jnp.take_along_axis(i2, flat % PEER_TOPK, axis=-1))
    gate = jax.nn.softmax(best, axis=-1)
    m = PEER_HEADS * PEER_TOPK
    expert = expert.reshape(n_tok, m)
    gate = gate.reshape(n_tok, m)
    n_pad = (-n_tok) % PEER_BLOCK
    n_blk = (n_tok + n_pad) // PEER_BLOCK

    def pad_block(a):
        a = jnp.pad(a, ((0, n_pad),) + ((0, 0),) * (a.ndim - 1))
        return a.reshape(n_blk, PEER_BLOCK, *a.shape[1:])

    def expert_block(args):
        xb, eb, gb = args
        act = jnp.einsum('td,tmd->tm', xb, u[eb], preferred_element_type=jnp.float32)
        wgt = (gb * jax.nn.gelu(act)).astype(v.dtype)
        return jnp.einsum('tm,tmd->td', wgt, v[eb])

    out = lax.map(expert_block, (pad_block(xt), pad_block(expert), pad_block(gate)))
    out = out.reshape(n_blk * PEER_BLOCK, -1)[:n_tok]
    return out.reshape(*lead, -1).astype(h.dtype)


def layer_forward(x, c, past, layer, w_ada, b_ada, g_norm_mix, g_norm_ffn, w_in,
                  lam_q1, lam_k1, lam_q2, lam_k2, g_subln, w_out, peer_wq, peer_keys, peer_u, peer_v):
    b, t = x.shape[0], x.shape[1]
    p_len = 0 if past is None else past[0].shape[1]
    pos = p_len + jnp.arange(t)
    k_pos = jnp.arange(p_len + t)
    n_sel = min(DSA_TOPK, (p_len + t) // 4)

    mod = jnp.einsum('bd,de->be', jax.nn.silu(c), w_ada) + b_ada
    sh1, sc1, ga1, sh2, sc2, ga2 = jnp.split(mod[:, None, :], 6, axis=-1)

    h = rmsnorm(x, g_norm_mix) * (1 + sc1) + sh1
    proj = h @ w_in
    split_at = [int(o) for o in np.cumsum(IN_SIZES)[:-1]]
    q, k, v, qi, ki, wi, dq, dk, dv = jnp.split(proj, split_at, axis=-1)
    q = rope(q.reshape(b, t, DSA_HEADS, HEAD_DIM), pos)
    k = rope(k.reshape(b, t, DSA_KV_HEADS, HEAD_DIM), pos)
    v = v.reshape(b, t, DSA_KV_HEADS, HEAD_DIM)
    qi = rope(qi.reshape(b, t, IDX_HEADS, IDX_DIM), pos)
    ki = rope(ki, pos)
    dq = rope(dq.reshape(b, t, DIFF_HEADS, 2, DIFF_D), pos)
    dk = rope(dk.reshape(b, t, DIFF_HEADS, 2, DIFF_D), pos)
    dv = dv.reshape(b, t, DIFF_HEADS, 2 * DIFF_D)
    new_rows = (k, v, ki, dk, dv)

    if past is None:
        k_all, v_all, ki_all, dk_all, dv_all = new_rows
    else:
        k_all, v_all, ki_all, dk_all, dv_all = (jnp.concatenate([pr, nr], axis=1) for pr, nr in zip(past, new_rows))

    lam = (jnp.exp(jnp.sum(lam_q1.astype(jnp.float32) * lam_k1.astype(jnp.float32)))
           - jnp.exp(jnp.sum(lam_q2.astype(jnp.float32) * lam_k2.astype(jnp.float32)))
           + lambda_init(layer))

    dsa_fn = lambda qb, qib, wib, qp: dsa_attend(qb, qib, wib, qp, k_all, v_all, ki_all, k_pos, n_sel)
    diff_fn = lambda dqb, qp: diff_attend(dqb, qp, dk_all, dv_all, k_pos, lam)
    if past is None:
        a_out = map_query_blocks(dsa_fn, (q, qi, wi), pos)
        d_out = map_query_blocks(diff_fn, (dq,), pos)
    else:
        a_out = dsa_fn(q, qi, wi, pos)
        d_out = diff_fn(dq, pos)
    d_out = (rmsnorm(d_out, g_subln) * (1.0 - lambda_init(layer))).reshape(b, t, DIFF_WIDTH)

    mix = jnp.concatenate([a_out, d_out.astype(a_out.dtype)], axis=-1) @ w_out
    x = x + ga1 * mix
    h2 = rmsnorm(x, g_norm_ffn) * (1 + sc2) + sh2
    x = x + ga2 * peer_ffn(h2, peer_wq, peer_keys, peer_u, peer_v)
    return x, new_rows


def setup_inputs(seed: int = 0) -> dict:
    key = jax.random.key(seed)
    ks = jax.random.split(key, 25)
    f = jnp.float32

    def nrm(k, shape, scale):
        return jax.random.normal(k, shape, f) * scale

    return {
        'x_prompt': nrm(ks[0], (BATCH, SEQ, D_MODEL), 1.0),
        'x_sample': nrm(ks[1], (DEC_BATCH, DEC_SEQ, D_MODEL), 1.0),
        'cache_dsa_k': nrm(ks[2], (DEPTH, DEC_BATCH, PAST_LEN, DSA_KV_HEADS, HEAD_DIM), 1.0),
        'cache_dsa_v': nrm(ks[3], (DEPTH, DEC_BATCH, PAST_LEN, DSA_KV_HEADS, HEAD_DIM), 1.0),
        'cache_idx_k': nrm(ks[4], (DEPTH, DEC_BATCH, PAST_LEN, IDX_DIM), 1.0),
        'cache_diff_k': nrm(ks[5], (DEPTH, DEC_BATCH, PAST_LEN, DIFF_HEADS, 2, DIFF_D), 1.0),
        'cache_diff_v': nrm(ks[6], (DEPTH, DEC_BATCH, PAST_LEN, DIFF_HEADS, 2 * DIFF_D), 1.0),
        'c_prompt': nrm(ks[7], (BATCH, D_MODEL), 1.0),
        'c_sample': nrm(ks[8], (DEC_BATCH, D_MODEL), 1.0),
        'w_ada': nrm(ks[9], (DEPTH, D_MODEL, 6 * D_MODEL), 0.5 * D_MODEL ** -0.5),
        'b_ada': nrm(ks[10], (DEPTH, 6 * D_MODEL), 0.02),
        'g_norm_mix': 1.0 + nrm(ks[11], (DEPTH, D_MODEL), 0.02),
        'g_norm_ffn': 1.0 + nrm(ks[12], (DEPTH, D_MODEL), 0.02),
        'w_in': nrm(ks[13], (DEPTH, D_MODEL, IN_WIDTH), D_MODEL ** -0.5),
        'diff_lambda_q1': nrm(ks[14], (DEPTH, DIFF_D), 0.1),
        'diff_lambda_k1': nrm(ks[15], (DEPTH, DIFF_D), 0.1),
        'diff_lambda_q2': nrm(ks[16], (DEPTH, DIFF_D), 0.1),
        'diff_lambda_k2': nrm(ks[17], (DEPTH, DIFF_D), 0.1),
        'g_diff_subln': 1.0 + nrm(ks[18], (DEPTH, 2 * DIFF_D), 0.02),
        'w_out': nrm(ks[19], (DEPTH, MIX_WIDTH, D_MODEL), MIX_WIDTH ** -0.5),
        'peer_w_query': nrm(ks[20], (DEPTH, D_MODEL, PEER_HEADS * PEER_DK), D_MODEL ** -0.5),
        'peer_sub_keys': nrm(ks[21], (DEPTH, PEER_HEADS, 2, PEER_N_KEYS, PEER_HALF), PEER_HALF ** -0.5),
        'peer_u': nrm(ks[22], (DEPTH, PEER_EXPERTS, D_MODEL), D_MODEL ** -0.5),
        'peer_v': nrm(ks[23], (DEPTH, PEER_EXPERTS, D_MODEL), PEER_HEADS ** -0.5),
        'g_final': 1.0 + nrm(ks[24], (D_MODEL,), 0.02),
    }


def reference(x_prompt, x_sample, cache_dsa_k, cache_dsa_v, cache_idx_k, cache_diff_k, cache_diff_v,
              c_prompt, c_sample, w_ada, b_ada, g_norm_mix, g_norm_ffn, w_in,
              diff_lambda_q1, diff_lambda_k1, diff_lambda_q2, diff_lambda_k2, g_diff_subln, w_out,
              peer_w_query, peer_sub_keys, peer_u, peer_v, g_final):
    hp, hs = x_prompt, x_sample
    rows_p, rows_s = [], []
    for l in range(DEPTH):
        lw = (w_ada[l], b_ada[l], g_norm_mix[l], g_norm_ffn[l], w_in[l],
              diff_lambda_q1[l], diff_lambda_k1[l], diff_lambda_q2[l], diff_lambda_k2[l],
              g_diff_subln[l], w_out[l], peer_w_query[l], peer_sub_keys[l], peer_u[l], peer_v[l])
        hp, rp = layer_forward(hp, c_prompt, None, l, *lw)
        past = (cache_dsa_k[l], cache_dsa_v[l], cache_idx_k[l], cache_diff_k[l], cache_diff_v[l])
        hs, rs = layer_forward(hs, c_sample, past, l, *lw)
        rows_p.append(rp)
        rows_s.append(rs)
    y_prompt = rmsnorm(hp, g_final)
    y_sample = rmsnorm(hs, g_final)
    new_dsa_k_p = jnp.stack([r[0] for r in rows_p])
    new_dsa_v_p = jnp.stack([r[1] for r in rows_p])
    new_idx_k_p = jnp.stack([r[2] for r in rows_p])
    new_diff_k_p = jnp.stack([r[3] for r in rows_p])
    new_diff_v_p = jnp.stack([r[4] for r in rows_p])
    new_dsa_k_s = jnp.stack([r[0] for r in rows_s])
    new_dsa_v_s = jnp.stack([r[1] for r in rows_s])
    new_idx_k_s = jnp.stack([r[2] for r in rows_s])
    new_diff_k_s = jnp.stack([r[3] for r in rows_s])
    new_diff_v_s = jnp.stack([r[4] for r in rows_s])
    return (y_prompt, y_sample, new_dsa_k_p, new_dsa_v_p, new_idx_k_p, new_diff_k_p, new_diff_v_p,
            new_dsa_k_s, new_dsa_v_s, new_idx_k_s, new_diff_k_s, new_diff_v_s)
```

```python
import functools
import math

import jax
import jax.numpy as jnp
from jax import lax
from jax.experimental import pallas as pl
from jax.experimental.pallas import tpu as pltpu

CHUNK = 64
HEAD_DIM = 128
ROPE_THETA = 10000.0
EPS = 1e-6
DSA_HEADS = 16
IDX_HEADS = 16
DSA_TOPK = 256
PEER_TOPK = 16
LANES = 128
KEY_CHUNK = 512
NEG_BIG = -1e30
INT_MIN = -(2 ** 31)
VMEM_LIMIT = 56 * 1024 * 1024

BF16 = jnp.bfloat16
F32 = jnp.float32


def _params(sem, vmem=VMEM_LIMIT):
    return pltpu.CompilerParams(dimension_semantics=sem, vmem_limit_bytes=vmem)


def _lambda_init(layer):
    return 0.8 - 0.6 * math.exp(-0.3 * layer)


def _nt_dot(a, b):
    return lax.dot_general(a, b, (((1,), (1,)), ((), ())), preferred_element_type=F32)


def _adaln_kernel(c_ref, w_ref, b_ref, o_ref):
    c = c_ref[...]
    a = (c * jax.nn.sigmoid(c)).astype(BF16)
    o_ref[...] = jnp.dot(a, w_ref[...].astype(BF16), preferred_element_type=F32) + b_ref[...]


def _adaln(c_pad, w_ada, b_ada):
    bp, d = c_pad.shape
    n = w_ada.shape[1]
    tn = 512
    return pl.pallas_call(
        _adaln_kernel,
        grid=(n // tn,),
        in_specs=[pl.BlockSpec((bp, d), lambda j: (0, 0)),
                  pl.BlockSpec((d, tn), lambda j: (0, j)),
                  pl.BlockSpec((1, tn), lambda j: (0, j))],
        out_specs=pl.BlockSpec((bp, tn), lambda j: (0, j)),
        out_shape=jax.ShapeDtypeStruct((bp, n), F32),
        compiler_params=_params(("arbitrary",)),
        name="adaln",
    )(c_pad, w_ada, b_ada.reshape(1, n))


def _norm_mod_kernel(x_ref, g_ref, sc_ref, sh_ref, o_ref, *, per_token, transpose):
    x = x_ref[...]
    y = x * lax.rsqrt(jnp.mean(x * x, axis=-1, keepdims=True) + EPS) * g_ref[...]
    sc = sc_ref[...] if per_token else sc_ref[0]
    sh = sh_ref[...] if per_token else sh_ref[0]
    h = y * (1.0 + sc) + sh
    if transpose:
        o_ref[...] = h.T.astype(o_ref.dtype)
    else:
        o_ref[...] = h.astype(o_ref.dtype)


def _mod_spec(mod, per_token, tm, tn, col_block, rows_per_batch):
    if per_token:
        return pl.BlockSpec((tm, tn), lambda m, n=0, cb=col_block: (m, cb + n))
    tiles_per_batch = rows_per_batch // tm
    return pl.BlockSpec((1, 1, tn), lambda m, n=0, cb=col_block: (m // tiles_per_batch, 0, cb + n))


def _norm_mod(x2, g, mod, *, which, per_token, rows_per_batch, tm, transpose):
    t, d = x2.shape
    sh_blk, sc_blk = (0, 1) if which == 1 else (3, 4)
    out_shape = (d, t) if transpose else (t, d)
    out_spec = pl.BlockSpec((d, tm), lambda m: (0, m)) if transpose else pl.BlockSpec((tm, d), lambda m: (m, 0))
    return pl.pallas_call(
        functools.partial(_norm_mod_kernel, per_token=per_token, transpose=transpose),
        grid=(t // tm,),
        in_specs=[pl.BlockSpec((tm, d), lambda m: (m, 0)),
                  pl.BlockSpec((1, d), lambda m: (0, 0)),
                  _mod_spec(mod, per_token, tm, d, sc_blk, rows_per_batch),
                  _mod_spec(mod, per_token, tm, d, sh_blk, rows_per_batch)],
        out_specs=out_spec,
        out_shape=jax.ShapeDtypeStruct(out_shape, BF16),
        compiler_params=_params(("arbitrary",)),
        name="norm_mod_t" if transpose else "norm_mod",
    )(x2, g.reshape(1, d), mod, mod)


def _rope_tile(acc, cos, sin, n_chunks):
    outs = []
    for j in range(n_chunks):
        xj = acc[:, j * LANES:(j + 1) * LANES]
        outs.append(xj * cos + pltpu.roll(xj, LANES // 2, axis=1) * sin)
    return outs[0] if n_chunks == 1 else jnp.concatenate(outs, axis=1)


def _proj_kernel(*refs, rope_cols, scale, out_f32, out_bf16):
    h_ref, w_ref, cos_ref, sin_ref = refs[:4]
    outs = refs[4:]
    acc = jnp.dot(h_ref[...], w_ref[...], preferred_element_type=F32)
    tn = acc.shape[1]
    if rope_cols:
        roped = _rope_tile(acc[:, :rope_cols], cos_ref[...], sin_ref[...], rope_cols // LANES)
        acc = roped if rope_cols == tn else jnp.concatenate([roped, acc[:, rope_cols:]], axis=1)
    k = 0
    if out_f32:
        outs[k][...] = acc
        k += 1
    if out_bf16:
        outs[k][...] = (acc * scale if scale != 1.0 else acc).astype(BF16)


def _proj(h, w, cos, sin, *, rope_cols, scale=1.0, out_f32, out_bf16, tm, tn, name):
    t, d = h.shape
    n = w.shape[1]
    assert rope_cols in (0, tn) or n == tn
    pos_tiles = cos.shape[0] // tm
    out_shape, out_specs = [], []
    for want, dt in ((out_f32, F32), (out_bf16, BF16)):
        if want:
            out_shape.append(jax.ShapeDtypeStruct((t, n), dt))
            out_specs.append(pl.BlockSpec((tm, tn), lambda m, j: (m, j)))
    return pl.pallas_call(
        functools.partial(_proj_kernel, rope_cols=rope_cols, scale=scale, out_f32=out_f32, out_bf16=out_bf16),
        grid=(t // tm, n // tn),
        in_specs=[pl.BlockSpec((tm, d), lambda m, j: (m, 0)),
                  pl.BlockSpec((d, tn), lambda m, j: (0, j)),
                  pl.BlockSpec((tm, LANES), lambda m, j: (m % pos_tiles, 0)),
                  pl.BlockSpec((tm, LANES), lambda m, j: (m % pos_tiles, 0))],
        out_specs=out_specs,
        out_shape=out_shape,
        compiler_params=_params(("arbitrary", "arbitrary")),
        name=name,
    )(h, w, cos, sin)


def _visible(q_start, n_q, key_start, n_k, n_valid_keys):
    qpos = q_start + lax.broadcasted_iota(jnp.int32, (n_q, 1), 0)
    kpos = key_start + lax.broadcasted_iota(jnp.int32, (1, n_k), 1)
    return ((kpos >> 6) <= (qpos >> 6)) & (kpos < n_valid_keys)


def _num_key_chunks(q_start, n_q, n_valid_keys, n_chunks_total):
    last_visible = (((q_start + n_q - 1) >> 6) + 1) * CHUNK
    last_visible = jnp.minimum(last_visible, n_valid_keys)
    return jnp.minimum((last_visible + KEY_CHUNK - 1) // KEY_CHUNK, n_chunks_total)


def _ordered_key(x):
    b = pltpu.bitcast(x, jnp.int32)
    return b ^ ((b >> 31) & jnp.int32(0x7FFFFFFF))


def _dsa_kernel(q_ref, qi_ref, wi_ref, k_ref, v_ref, ki_ref, o_ref, key_scr, bias_scr, *,
                tq, n_sel, p_len, n_valid_keys, n_idx_heads, n_kv_heads, rep):
    n_chunks_total = key_scr.shape[0]
    q_start = p_len + pl.program_id(1) * tq
    n_ch = _num_key_chunks(q_start, tq, n_valid_keys, n_chunks_total)

    def idx_body(j, carry):
        k0 = pl.multiple_of(j * KEY_CHUNK, KEY_CHUNK)
        kib = ki_ref[pl.ds(k0, KEY_CHUNK), :]
        score = jnp.zeros((tq, KEY_CHUNK), F32)
        wi = wi_ref[:, LANES:2 * LANES] * (n_idx_heads ** -0.5)
        for h in range(n_idx_heads):
            lg = _nt_dot(qi_ref[:, h * LANES:(h + 1) * LANES], kib)
            score = score + jnp.maximum(lg, 0.0) * wi[:, h:h + 1]
        vis = _visible(q_start, tq, k0, KEY_CHUNK, n_valid_keys)
        key_scr[j] = _ordered_key(jnp.where(vis, score, -jnp.inf))
        return carry

    lax.fori_loop(0, n_ch, idx_body, 0)

    def bit_body(b, t):
        cand_u = t | lax.shift_left(jnp.int32(1), 31 - b)
        cand_s = cand_u ^ jnp.int32(INT_MIN)

        def cnt_body(j, cnt):
            ones = jnp.where(key_scr[j] >= cand_s, 1, 0)
            for c in range(KEY_CHUNK // LANES):
                cnt = cnt + ones[:, c * LANES:(c + 1) * LANES]
            return cnt

        cnt = lax.fori_loop(0, n_ch, cnt_body, jnp.zeros((tq, LANES), jnp.int32))
        total = jnp.sum(cnt, axis=1, keepdims=True)
        return jnp.where(total >= n_sel, cand_u, t)

    thr_u = lax.fori_loop(0, 32, bit_body, jnp.zeros((tq, 1), jnp.int32))
    thr = thr_u ^ jnp.int32(INT_MIN)

    def bias_body(j, carry):
        k0 = j * KEY_CHUNK
        vis = _visible(q_start, tq, k0, KEY_CHUNK, n_valid_keys)
        sel = (key_scr[j] >= thr) & vis
        bias_scr[j] = jnp.where(sel, 0.0, NEG_BIG)
        return carry

    lax.fori_loop(0, n_ch, bias_body, 0)

    for g in range(n_kv_heads):
        qs = jnp.concatenate(
            [q_ref[:, (g * rep + r) * LANES:(g * rep + r + 1) * LANES] for r in range(rep)], axis=0)

        def att_body(j, carry, g=g, qs=qs):
            m, l, acc = carry
            k0 = pl.multiple_of(j * KEY_CHUNK, KEY_CHUNK)
            kb = k_ref[pl.ds(k0, KEY_CHUNK), g * LANES:(g + 1) * LANES]
            vb = v_ref[pl.ds(k0, KEY_CHUNK), g * LANES:(g + 1) * LANES]
            bias = bias_scr[j]
            s = _nt_dot(qs, kb) + jnp.concatenate([bias] * rep, axis=0)
            m_new = jnp.maximum(m, jnp.max(s, axis=1, keepdims=True))
            p = jnp.exp(s - m_new)
            alpha = jnp.exp(m - m_new)
            l = alpha * l + jnp.sum(p, axis=1, keepdims=True)
            acc = alpha * acc + jnp.dot(p.astype(BF16), vb, preferred_element_type=F32)
            return m_new, l, acc

        init = (jnp.full((rep * tq, 1), NEG_BIG, F32), jnp.zeros((rep * tq, 1), F32),
                jnp.zeros((rep * tq, LANES), F32))
        _, l, acc = lax.fori_loop(0, n_ch, att_body, init)
        o = acc / l
        for r in range(rep):
            o_ref[:, (g * rep + r) * LANES:(g * rep + r + 1) * LANES] = o[r * tq:(r + 1) * tq].astype(o_ref.dtype)


def _dsa_attention(qs_all, kiw_f32, k_bf, v_bf, ki_bf, *, batch, q_len, k_len, n_valid_keys, p_len, tq,
                   n_heads, n_kv_heads, n_idx_heads, n_sel):
    width = n_heads * HEAD_DIM
    nq = q_len // tq
    n_chunks = k_len // KEY_CHUNK
    rep = n_heads // n_kv_heads
    kv_w = n_kv_heads * HEAD_DIM
    return pl.pallas_call(
        functools.partial(_dsa_kernel, tq=tq, n_sel=n_sel, p_len=p_len, n_valid_keys=n_valid_keys,
                          n_idx_heads=n_idx_heads, n_kv_heads=n_kv_heads, rep=rep),
        grid=(batch, nq),
        in_specs=[pl.BlockSpec((tq, width), lambda b, i: (b * nq + i, 0)),
                  pl.BlockSpec((tq, width), lambda b, i: (b * nq + i, 1)),
                  pl.BlockSpec((tq, 2 * LANES), lambda b, i: (b * nq + i, 0)),
                  pl.BlockSpec((k_len, kv_w), lambda b, i: (b, 0)),
                  pl.BlockSpec((k_len, kv_w), lambda b, i: (b, 0)),
                  pl.BlockSpec((k_len, LANES), lambda b, i: (b, 0))],
        out_specs=pl.BlockSpec((tq, width), lambda b, i: (b * nq + i, 0)),
        out_shape=jax.ShapeDtypeStruct((batch * q_len, width), BF16),
        scratch_shapes=[pltpu.VMEM((n_chunks, tq, KEY_CHUNK), jnp.int32),
                        pltpu.VMEM((n_chunks, tq, KEY_CHUNK), F32)],
        compiler_params=_params(("arbitrary", "arbitrary")),
        name="dsa_attention",
    )(qs_all, qs_all, kiw_f32, k_bf, v_bf, ki_bf)


def _diff_kernel(dq_ref, dk_ref, dv_ref, lq1_ref, lk1_ref, lq2_ref, lk2_ref, g_ref, o_ref, bias_scr, *,
                 tq, p_len, n_valid_keys, lam_init):
    n_chunks_total = bias_scr.shape[0]
    q_start = p_len + pl.program_id(2) * tq
    n_ch = _num_key_chunks(q_start, tq, n_valid_keys, n_chunks_total)
    dv_w = 2 * HEAD_DIM

    lam = (jnp.exp(jnp.sum(lq1_ref[...] * lk1_ref[...], axis=1, keepdims=True))
           - jnp.exp(jnp.sum(lq2_ref[...] * lk2_ref[...], axis=1, keepdims=True)) + lam_init)

    def bias_body(j, carry):
        vis = _visible(q_start, tq, j * KEY_CHUNK, KEY_CHUNK, n_valid_keys)
        bias_scr[j] = jnp.where(vis, 0.0, NEG_BIG)
        return carry

    lax.fori_loop(0, n_ch, bias_body, 0)

    q0 = dq_ref[:, :LANES]
    q1 = dq_ref[:, LANES:]

    def att_body(j, carry):
        k0 = pl.multiple_of(j * KEY_CHUNK, KEY_CHUNK)
        vb = dv_ref[pl.ds(k0, KEY_CHUNK), :]
        bias = bias_scr[j]
        new = []
        for c, qc in ((0, q0), (1, q1)):
            m, l, acc = carry[3 * c:3 * c + 3]
            kb = dk_ref[pl.ds(k0, KEY_CHUNK), c * LANES:(c + 1) * LANES]
            s = _nt_dot(qc, kb) + bias
            m_new = jnp.maximum(m, jnp.max(s, axis=1, keepdims=True))
            p = jnp.exp(s - m_new)
            alpha = jnp.exp(m - m_new)
            l = alpha * l + jnp.sum(p, axis=1, keepdims=True)
            acc = alpha * acc + jnp.dot(p.astype(BF16), vb, preferred_element_type=F32)
            new += [m_new, l, acc]
        return tuple(new)

    one = (jnp.full((tq, 1), NEG_BIG, F32), jnp.zeros((tq, 1), F32), jnp.zeros((tq, dv_w), F32))
    _, l0, a0, _, l1, a1 = lax.fori_loop(0, n_ch, att_body, one + one)
    o = a0 / l0 - lam * (a1 / l1)
    o = o * lax.rsqrt(jnp.mean(o * o, axis=-1, keepdims=True) + EPS) * g_ref[...]
    o_ref[...] = (o * (1.0 - lam_init)).astype(o_ref.dtype)


def _diff_attention(qs_all, dk_bf, dv_bf, lams, g_subln, *, batch, q_len, k_len, n_valid_keys, p_len, tq,
                    n_heads, lam_init):
    hw = 2 * HEAD_DIM
    width = n_heads * hw
    nq = q_len // tq
    n_chunks = k_len // KEY_CHUNK
    assert qs_all.shape[1] == 3 * width
    dq_col0 = 2 * n_heads
    vec = pl.BlockSpec((1, HEAD_DIM), lambda b, h, i: (0, 0))
    return pl.pallas_call(
        functools.partial(_diff_kernel, tq=tq, p_len=p_len, n_valid_keys=n_valid_keys, lam_init=lam_init),
        grid=(batch, n_heads, nq),
        in_specs=[pl.BlockSpec((tq, hw), lambda b, h, i: (b * nq + i, dq_col0 + h)),
                  pl.BlockSpec((k_len, hw), lambda b, h, i: (b, h)),
                  pl.BlockSpec((k_len, hw), lambda b, h, i: (b, h)),
                  vec, vec, vec, vec,
                  pl.BlockSpec((1, hw), lambda b, h, i: (0, 0))],
        out_specs=pl.BlockSpec((tq, hw), lambda b, h, i: (b * nq + i, h)),
        out_shape=jax.ShapeDtypeStruct((batch * q_len, width), BF16),
        scratch_shapes=[pltpu.VMEM((n_chunks, tq, KEY_CHUNK), F32)],
        compiler_params=_params(("arbitrary", "arbitrary", "arbitrary")),
        name="diff_attention",
    )(qs_all, dk_bf, dv_bf, *[v.reshape(1, HEAD_DIM) for v in lams], g_subln.reshape(1, hw))


def _outproj_kernel(a_ref, d_ref, wa_ref, wd_ref, x_ref, ga_ref, o_ref, *, per_token):
    mix = (jnp.dot(a_ref[...], wa_ref[...], preferred_element_type=F32)
           + jnp.dot(d_ref[...], wd_ref[...], preferred_element_type=F32))
    ga = ga_ref[...] if per_token else ga_ref[0]
    o_ref[...] = x_ref[...] + ga * mix


def _outproj(a_out, d_out, w_out_bf, x2, mod, *, per_token, rows_per_batch, tm, tn):
    t, d = x2.shape
    wa = a_out.shape[1]
    wd = d_out.shape[1]
    assert wa == wd
    cb = 2 * (d // tn)
    return pl.pallas_call(
        functools.partial(_outproj_kernel, per_token=per_token),
        grid=(t // tm, d // tn),
        in_specs=[pl.BlockSpec((tm, wa), lambda m, n: (m, 0)),
                  pl.BlockSpec((tm, wd), lambda m, n: (m, 0)),
                  pl.BlockSpec((wa, tn), lambda m, n: (0, n)),
                  pl.BlockSpec((wd, tn), lambda m, n: (1, n)),
                  pl.BlockSpec((tm, tn), lambda m, n: (m, n)),
                  _mod_spec(mod, per_token, tm, tn, cb, rows_per_batch)],
        out_specs=pl.BlockSpec((tm, tn), lambda m, n: (m, n)),
        out_shape=jax.ShapeDtypeStruct((t, d), F32),
        compiler_params=_params(("arbitrary", "arbitrary")),
        name="outproj",
    )(a_out, d_out, w_out_bf, w_out_bf, x2, mod)


def _top_rows(x, k):
    tops = []
    for _ in range(k):
        mx = jnp.max(x, axis=0, keepdims=True)
        tops.append(mx)
        x = jnp.where(x == mx, -jnp.inf, x)
    return tops


def _peer_route_kernel(h_ref, wq_ref, keys_ref, s1_ref, s2_ref, e1_ref, e2_ref, thr_ref, s1_scr, top_scr):
    c = pl.program_id(1) % 2
    q_t = jnp.dot(wq_ref[...], h_ref[...], preferred_element_type=F32)
    s_t = jnp.dot(keys_ref[0].astype(BF16), q_t.astype(BF16), preferred_element_type=F32)
    tops = jnp.concatenate(_top_rows(s_t, PEER_TOPK), axis=0)

    @pl.when(c == 0)
    def _():
        s1_scr[...] = s_t
        top_scr[...] = tops

    @pl.when(c == 1)
    def _():
        top1 = top_scr[...]
        cand = jnp.concatenate([top1[i:i + 1, :] + tops for i in range(PEER_TOPK)], axis=0)
        best = _top_rows(cand, PEER_TOPK)
        m = best[0]
        z = jnp.zeros_like(m)
        for bk in best:
            z = z + jnp.exp(bk - m)
        s1 = s1_scr[...]
        s1_ref[0] = s1
        s2_ref[0] = s_t
        e1_ref[0] = jnp.exp(s1 - top1[0:1, :]) / z
        e2_ref[0] = jnp.exp(s_t - tops[0:1, :])
        thr_ref[0] = best[PEER_TOPK - 1]


def _peer_route(h_t, wq_t_bf, sub_keys, *, tm):
    d, t = h_t.shape
    heads, _, n_keys, half = sub_keys.shape
    keys2 = sub_keys.reshape(heads * 2, n_keys, half)
    tab = jax.ShapeDtypeStruct((heads, n_keys, t), F32)
    tab_spec = pl.BlockSpec((1, n_keys, tm), lambda m, hc: (hc // 2, 0, m))
    return pl.pallas_call(
        _peer_route_kernel,
        grid=(t // tm, heads * 2),
        in_specs=[pl.BlockSpec((d, tm), lambda m, hc: (0, m)),
                  pl.BlockSpec((half, d), lambda m, hc: (hc, 0)),
                  pl.BlockSpec((1, n_keys, half), lambda m, hc: (hc, 0, 0))],
        out_specs=[tab_spec, tab_spec, tab_spec, tab_spec,
                   pl.BlockSpec((1, 1, tm), lambda m, hc: (hc // 2, 0, m))],
        out_shape=[tab, tab, tab, tab, jax.ShapeDtypeStruct((heads, 1, t), F32)],
        scratch_shapes=[pltpu.VMEM((n_keys, tm), F32), pltpu.VMEM((PEER_TOPK, tm), F32)],
        compiler_params=_params(("arbitrary", "arbitrary")),
        name="peer_route",
    )(h_t, wq_t_bf, keys2)


def _gelu_tanh(x):
    return 0.5 * x * (1.0 + jnp.tanh(0.7978845608028654 * (x + 0.044715 * x * x * x)))


def _peer_ffn_kernel(h_ref, u_ref, v_ref, s1_ref, s2_ref, e1_ref, e2_ref, thr_ref, o_ref, *, n_keys):
    e = pl.program_id(1)
    te = u_ref.shape[0]
    heads = s1_ref.shape[0]
    act = jnp.dot(u_ref[...], h_ref[...], preferred_element_type=F32)
    gel = _gelu_tanh(act)
    w_rows = []
    for ai in range(te // n_keys):
        a = e * (te // n_keys) + ai
        gate = jnp.zeros((n_keys, act.shape[1]), F32)
        for h in range(heads):
            s1a = s1_ref[h, pl.ds(a, 1), :]
            e1a = e1_ref[h, pl.ds(a, 1), :]
            keep = (s2_ref[h] + s1a) >= thr_ref[h]
            gate = gate + jnp.where(keep, e2_ref[h] * e1a, 0.0)
        w_rows.append((gate * gel[ai * n_keys:(ai + 1) * n_keys, :]).astype(BF16))
    w_t = w_rows[0] if len(w_rows) == 1 else jnp.concatenate(w_rows, axis=0)
    contrib = lax.dot_general(w_t, v_ref[...], (((0,), (0,)), ((), ())), preferred_element_type=F32)

    @pl.when(e == 0)
    def _():
        o_ref[...] = contrib

    @pl.when(e != 0)
    def _():
        o_ref[...] += contrib


def _peer_ffn(h_t, u_bf, v_bf, s1, s2, e1, e2, thr, *, tm, te):
    d, t = h_t.shape
    n_exp = u_bf.shape[0]
    heads, n_keys, _ = s1.shape
    tab_spec = pl.BlockSpec((heads, n_keys, tm), lambda m, e: (0, 0, m))
    return pl.pallas_call(
        functools.partial(_peer_ffn_kernel, n_keys=n_keys),
        grid=(t // tm, n_exp // te),
        in_specs=[pl.BlockSpec((d, tm), lambda m, e: (0, m)),
                  pl.BlockSpec((te, d), lambda m, e: (e, 0)),
                  pl.BlockSpec((te, d), lambda m, e: (e, 0)),
                  tab_spec, tab_spec, tab_spec, tab_spec,
                  pl.BlockSpec((heads, 1, tm), lambda m, e: (0, 0, m))],
        out_specs=pl.BlockSpec((tm, d), lambda m, e: (m, 0)),
        out_shape=jax.ShapeDtypeStruct((t, d), F32),
        compiler_params=_params(("arbitrary", "arbitrary")),
        name="peer_ffn",
    )(h_t, u_bf, v_bf, s1, s2, e1, e2, thr)


def _final_kernel(x_ref, p_ref, ga_ref, g_ref, o_ref, *, per_token, normalize):
    ga = ga_ref[...] if per_token else ga_ref[0]
    x = x_ref[...] + ga * p_ref[...]
    if normalize:
        x = x * lax.rsqrt(jnp.mean(x * x, axis=-1, keepdims=True) + EPS) * g_ref[...]
    o_ref[...] = x


def _final(x1, peer, mod, g_final, *, per_token, rows_per_batch, tm, normalize):
    t, d = x1.shape
    row = pl.BlockSpec((tm, d), lambda m: (m, 0))
    return pl.pallas_call(
        functools.partial(_final_kernel, per_token=per_token, normalize=normalize),
        grid=(t // tm,),
        in_specs=[row, row, _mod_spec(mod, per_token, tm, d, 5, rows_per_batch),
                  pl.BlockSpec((1, d), lambda m: (0, 0))],
        out_specs=row,
        out_shape=jax.ShapeDtypeStruct((t, d), F32),
        compiler_params=_params(("arbitrary",)),
        name="final",
    )(x1, peer, mod, g_final.reshape(1, d))


def _rope_tables(pos):
    half = HEAD_DIM // 2
    inv = ROPE_THETA ** (-jnp.arange(half, dtype=F32) / half)
    ang = pos.astype(F32)[:, None] * inv[None, :]
    cos, sin = jnp.cos(ang), jnp.sin(ang)
    return jnp.concatenate([cos, cos], axis=1), jnp.concatenate([-sin, sin], axis=1)


def _pick_tile(n, prefs):
    for p in prefs:
        if n % p == 0:
            return p
    return n


def _layer(x, mod_rows, past, layer, last_layer, w, dims):
    b, s, d = x.shape
    t = b * s
    n_heads, n_kv, n_idx, n_diff = dims
    p_len = 0 if past is None else past[0].shape[1]
    n_keys_valid = p_len + s
    n_sel = min(DSA_TOPK, n_keys_valid // 4)
    x2 = x.reshape(t, d)

    per_token = s % 256 != 0
    if per_token:
        mod = jnp.repeat(mod_rows, s, axis=0)
    else:
        mod = mod_rows.reshape(b, 1, 6 * d)
    tm_big = _pick_tile(t if per_token else s, (1024, 512, 256, 128))
    tm_mid = _pick_tile(t if per_token else s, (512, 256, 128))
    tm_small = _pick_tile(t if per_token else s, (256, 128))

    pos = p_len + jnp.arange(s)
    cos, sin = _rope_tables(pos)
    if per_token:
        cos, sin = jnp.tile(cos, (b, 1)), jnp.tile(sin, (b, 1))

    h = _norm_mod(x2, w["g_norm_mix"], mod, which=1, per_token=per_token, rows_per_batch=s, tm=tm_mid,
                  transpose=False)

    proj = functools.partial(_proj, h, cos=cos, sin=sin, tm=tm_big)
    wq = n_heads * HEAD_DIM
    (qs_all,) = proj(w["w_qs"], rope_cols=512, scale=HEAD_DIM ** -0.5, out_f32=False, out_bf16=True, tn=512,
                     name="proj_q")
    k_f, k_b = proj(w["w_k"], rope_cols=512, out_f32=True, out_bf16=True, tn=512, name="proj_k")
    v_f, v_b = proj(w["w_v"], rope_cols=0, out_f32=True, out_bf16=True, tn=512, name="proj_v")
    dk_f, dk_b = proj(w["w_dk"], rope_cols=512, out_f32=True, out_bf16=True, tn=512, name="proj_dk")
    dv_f, dv_b = proj(w["w_dv"], rope_cols=0, out_f32=True, out_bf16=True, tn=512, name="proj_dv")
    kiw_f, kiw_b = proj(w["w_kiw"], rope_cols=LANES, out_f32=True, out_bf16=True, tn=2 * LANES, name="proj_kiw")

    ki_f = kiw_f[:, :LANES]
    new_rows = (k_f.reshape(b, s, n_kv, HEAD_DIM), v_f.reshape(b, s, n_kv, HEAD_DIM), ki_f.reshape(b, s, LANES),
                dk_f.reshape(b, s, n_diff, 2, HEAD_DIM), dv_f.reshape(b, s, n_diff, 2 * HEAD_DIM))

    if past is None:
        k_len = s
        k_all, v_all, ki_all, dk_all, dv_all = k_b, v_b, kiw_b, dk_b, dv_b
    else:
        k_len = -(-n_keys_valid // KEY_CHUNK) * KEY_CHUNK

        def join(cache, new, width):
            new = new.reshape(b, s, -1)[:, :, :width]
            both = jnp.concatenate([cache.reshape(b, p_len, width).astype(BF16), new], axis=1)
            both = jnp.pad(both, ((0, 0), (0, k_len - n_keys_valid), (0, 0)))
            return both.reshape(b * k_len, width)

        k_all = join(past[0], k_b, n_kv * HEAD_DIM)
        v_all = join(past[1], v_b, n_kv * HEAD_DIM)
        ki_all = join(past[2], kiw_b, LANES)
        dk_all = join(past[3], dk_b, n_diff * 2 * HEAD_DIM)
        dv_all = join(past[4], dv_b, n_diff * 2 * HEAD_DIM)

    tq_a = _pick_tile(s, (128, 64, 32))
    a_out = _dsa_attention(qs_all, kiw_f, k_all, v_all, ki_all, batch=b, q_len=s, k_len=k_len,
                           n_valid_keys=n_keys_valid, p_len=p_len, tq=tq_a, n_heads=n_heads, n_kv_heads=n_kv,
                           n_idx_heads=n_idx, n_sel=n_sel)
    tq_d = _pick_tile(s, (256, 128, 64, 32))
    d_out = _diff_attention(qs_all, dk_all, dv_all, w["lams"], w["g_subln"], batch=b, q_len=s, k_len=k_len,
                            n_valid_keys=n_keys_valid, p_len=p_len, tq=tq_d, n_heads=n_diff,
                            lam_init=_lambda_init(layer))

    x1 = _outproj(a_out, d_out, w["w_out"], x2, mod, per_token=per_token, rows_per_batch=s, tm=tm_big, tn=512)

    h2_t = _norm_mod(x1, w["g_norm_ffn"], mod, which=2, per_token=per_token, rows_per_batch=s, tm=tm_mid,
                     transpose=True)
    s1, s2, e1, e2, thr = _peer_route(h2_t, w["peer_wq_t"], w["peer_keys"], tm=tm_mid)
    peer = _peer_ffn(h2_t, w["peer_u"], w["peer_v"], s1, s2, e1, e2, thr, tm=tm_mid, te=256)
    x_out = _final(x1, peer, mod, w["g_final"], per_token=per_token, rows_per_batch=s, tm=tm_small,
                   normalize=last_layer)
    return x_out.reshape(b, s, d), new_rows


def kernel(x_prompt, x_sample, cache_dsa_k, cache_dsa_v, cache_idx_k, cache_diff_k, cache_diff_v, c_prompt, c_sample, w_ada, b_ada, g_norm_mix, g_norm_ffn, w_in, diff_lambda_q1, diff_lambda_k1, diff_lambda_q2, diff_lambda_k2, g_diff_subln, w_out, peer_w_query, peer_sub_keys, peer_u, peer_v, g_final):
    depth = w_in.shape[0]
    d = x_prompt.shape[-1]
    bp, bs = x_prompt.shape[0], x_sample.shape[0]
    n_kv = cache_dsa_k.shape[3]
    n_diff = cache_diff_k.shape[3]
    n_heads, n_idx_heads = DSA_HEADS, IDX_HEADS
    qw, kvw, dw = n_heads * HEAD_DIM, n_kv * HEAD_DIM, n_diff * 2 * HEAD_DIM
    assert w_in.shape[2] == qw + 2 * kvw + n_idx_heads * LANES + LANES + n_idx_heads + 3 * dw
    dims = (n_heads, n_kv, n_idx_heads, n_diff)

    hp, hs = x_prompt, x_sample
    rows_p, rows_s = [], []
    n_c = bp + bs
    c_pad = jnp.pad(jnp.concatenate([c_prompt, c_sample], axis=0), ((0, (-n_c) % 16), (0, 0)))
    for l in range(depth):
        mod_all = _adaln(c_pad, w_ada[l], b_ada[l])
        win = w_in[l]
        o = 0
        seg = {}
        for name, width in (("q", qw), ("k", kvw), ("v", kvw), ("qi", n_idx_heads * LANES), ("ki", LANES),
                            ("wi", n_idx_heads), ("dq", dw), ("dk", dw), ("dv", dw)):
            seg[name] = win[:, o:o + width]
            o += width
        weights = {
            "w_qs": jnp.concatenate([seg["q"], seg["qi"], seg["dq"]], axis=1).astype(BF16),
            "w_k": seg["k"].astype(BF16), "w_v": seg["v"].astype(BF16),
            "w_dk": seg["dk"].astype(BF16), "w_dv": seg["dv"].astype(BF16),
            "w_kiw": jnp.concatenate([seg["ki"], seg["wi"], jnp.zeros((d, LANES - n_idx_heads), F32)],
                                     axis=1).astype(BF16),
            "w_out": w_out[l].astype(BF16),
            "peer_wq_t": peer_w_query[l].T.astype(BF16),
            "peer_keys": peer_sub_keys[l],
            "peer_u": peer_u[l].astype(BF16), "peer_v": peer_v[l].astype(BF16),
            "g_norm_mix": g_norm_mix[l], "g_norm_ffn": g_norm_ffn[l], "g_subln": g_diff_subln[l],
            "lams": (diff_lambda_q1[l], diff_lambda_k1[l], diff_lambda_q2[l], diff_lambda_k2[l]),
            "g_final": g_final,
        }
        last = l == depth - 1
        hp, rp = _layer(hp, mod_all[:bp], None, l, last, weights, dims)
        past = (cache_dsa_k[l], cache_dsa_v[l], cache_idx_k[l], cache_diff_k[l], cache_diff_v[l])
        hs, rs = _layer(hs, mod_all[bp:bp + bs], past, l, last, weights, dims)
        rows_p.append(rp)
        rows_s.append(rs)
    stack = lambda rows, i: jnp.stack([r[i] for r in rows])
    return (hp, hs) + tuple(stack(rows_p, i) for i in range(5)) + tuple(stack(rows_s, i) for i in range(5))
```

```python
import functools
import math

import jax
import jax.numpy as jnp
from jax import lax
from jax.experimental import pallas as pl
from jax.experimental.pallas import tpu as pltpu

CHUNK = 64
HEAD_DIM = 128
ROPE_THETA = 10000.0
EPS = 1e-6
DSA_HEADS = 16
IDX_HEADS = 16
DSA_TOPK = 256
PEER_TOPK = 16
LANES = 128
KEY_CHUNK = 512
NEG_BIG = -1e30
INT_MIN = -(2 ** 31)
VMEM_LIMIT = 56 * 1024 * 1024

BF16 = jnp.bfloat16
F32 = jnp.float32


def _params(sem, vmem=VMEM_LIMIT):
    return pltpu.CompilerParams(dimension_semantics=sem, vmem_limit_bytes=vmem)


def _lambda_init(layer):
    return 0.8 - 0.6 * math.exp(-0.3 * layer)


def _nt_dot(a, b):
    return lax.dot_general(a, b, (((1,), (1,)), ((), ())), preferred_element_type=F32)


def _adaln_kernel(c_ref, w_ref, b_ref, o_ref):
    c = c_ref[...]
    a = (c * jax.nn.sigmoid(c)).astype(BF16)
    o_ref[...] = jnp.dot(a, w_ref[...].astype(BF16), preferred_element_type=F32) + b_ref[...]


def _adaln(c_pad, w_ada, b_ada):
    bp, d = c_pad.shape
    n = w_ada.shape[1]
    tn = 512
    return pl.pallas_call(
        _adaln_kernel,
        grid=(n // tn,),
        in_specs=[pl.BlockSpec((bp, d), lambda j: (0, 0)),
                  pl.BlockSpec((d, tn), lambda j: (0, j)),
                  pl.BlockSpec((1, tn), lambda j: (0, j))],
        out_specs=pl.BlockSpec((bp, tn), lambda j: (0, j)),
        out_shape=jax.ShapeDtypeStruct((bp, n), F32),
        compiler_params=_params(("arbitrary",)),
        name="adaln",
    )(c_pad, w_ada, b_ada.reshape(1, n))


def _norm_mod_kernel(x_ref, g_ref, sc_ref, sh_ref, o_ref, *, per_token, transpose):
    x = x_ref[...]
    y = x * lax.rsqrt(jnp.mean(x * x, axis=-1, keepdims=True) + EPS) * g_ref[...]
    sc = sc_ref[...] if per_token else sc_ref[0]
    sh = sh_ref[...] if per_token else sh_ref[0]
    h = y * (1.0 + sc) + sh
    if transpose:
        o_ref[...] = h.T.astype(o_ref.dtype)
    else:
        o_ref[...] = h.astype(o_ref.dtype)


def _mod_spec(mod, per_token, tm, tn, col_block, rows_per_batch):
    if per_token:
        return pl.BlockSpec((tm, tn), lambda m, n=0, cb=col_block: (m, cb + n))
    tiles_per_batch = rows_per_batch // tm
    return pl.BlockSpec((1, 1, tn), lambda m, n=0, cb=col_block: (m // tiles_per_batch, 0, cb + n))


def _norm_mod(x2, g, mod, *, which, per_token, rows_per_batch, tm, transpose):
    t, d = x2.shape
    sh_blk, sc_blk = (0, 1) if which == 1 else (3, 4)
    out_shape = (d, t) if transpose else (t, d)
    out_spec = pl.BlockSpec((d, tm), lambda m: (0, m)) if transpose else pl.BlockSpec((tm, d), lambda m: (m, 0))
    return pl.pallas_call(
        functools.partial(_norm_mod_kernel, per_token=per_token, transpose=transpose),
        grid=(t // tm,),
        in_specs=[pl.BlockSpec((tm, d), lambda m: (m, 0)),
                  pl.BlockSpec((1, d), lambda m: (0, 0)),
                  _mod_spec(mod, per_token, tm, d, sc_blk, rows_per_batch),
                  _mod_spec(mod, per_token, tm, d, sh_blk, rows_per_batch)],
        out_specs=out_spec,
        out_shape=jax.ShapeDtypeStruct(out_shape, BF16),
        compiler_params=_params(("arbitrary",)),
        name="norm_mod_t" if transpose else "norm_mod",
    )(x2, g.reshape(1, d), mod, mod)


def _rope_tile(acc, cos, sin, n_chunks):
    outs = []
    for j in range(n_chunks):
        xj = acc[:, j * LANES:(j + 1) * LANES]
        outs.append(xj * cos + pltpu.roll(xj, LANES // 2, axis=1) * sin)
    return outs[0] if n_chunks == 1 else jnp.concatenate(outs, axis=1)


def _proj_kernel(*refs, rope_cols, scale, out_f32, out_bf16):
    h_ref, w_ref, cos_ref, sin_ref = refs[:4]
    outs = refs[4:]
    acc = jnp.dot(h_ref[...], w_ref[...], preferred_element_type=F32)
    tn = acc.shape[1]
    if rope_cols:
        roped = _rope_tile(acc[:, :rope_cols], cos_ref[...], sin_ref[...], rope_cols // LANES)
        acc = roped if rope_cols == tn else jnp.concatenate([roped, acc[:, rope_cols:]], axis=1)
    k = 0
    if out_f32:
        outs[k][...] = acc
        k += 1
    if out_bf16:
        outs[k][...] = (acc * scale if scale != 1.0 else acc).astype(BF16)


def _proj(h, w, cos, sin, *, rope_cols, scale=1.0, out_f32, out_bf16, tm, tn, name):
    t, d = h.shape
    n = w.shape[1]
    assert rope_cols in (0, tn) or n == tn
    pos_tiles = cos.shape[0] // tm
    out_shape, out_specs = [], []
    for want, dt in ((out_f32, F32), (out_bf16, BF16)):
        if want:
            out_shape.append(jax.ShapeDtypeStruct((t, n), dt))
            out_specs.append(pl.BlockSpec((tm, tn), lambda m, j: (m, j)))
    return pl.pallas_call(
        functools.partial(_proj_kernel, rope_cols=rope_cols, scale=scale, out_f32=out_f32, out_bf16=out_bf16),
        grid=(t // tm, n // tn),
        in_specs=[pl.BlockSpec((tm, d), lambda m, j: (m, 0)),
                  pl.BlockSpec((d, tn), lambda m, j: (0, j)),
                  pl.BlockSpec((tm, LANES), lambda m, j: (m % pos_tiles, 0)),
                  pl.BlockSpec((tm, LANES), lambda m, j: (m % pos_tiles, 0))],
        out_specs=out_specs,
        out_shape=out_shape,
        compiler_params=_params(("arbitrary", "arbitrary")),
        name=name,
    )(h, w, cos, sin)


def _visible(q_start, n_q, key_start, n_k, n_valid_keys):
    qpos = q_start + lax.broadcasted_iota(jnp.int32, (n_q, 1), 0)
    kpos = key_start + lax.broadcasted_iota(jnp.int32, (1, n_k), 1)
    return ((kpos >> 6) <= (qpos >> 6)) & (kpos < n_valid_keys)


def _num_key_chunks(q_start, n_q, n_valid_keys, n_chunks_total):
    last_visible = (((q_start + n_q - 1) >> 6) + 1) * CHUNK
    last_visible = jnp.minimum(last_visible, n_valid_keys)
    return jnp.minimum((last_visible + KEY_CHUNK - 1) // KEY_CHUNK, n_chunks_total)


def _ordered_key(x):
    b = pltpu.bitcast(x, jnp.int32)
    return b ^ ((b >> 31) & jnp.int32(0x7FFFFFFF))


def _dsa_kernel(q_ref, qi_ref, wi_ref, k_ref, v_ref, ki_ref, o_ref, key_scr, bias_scr, *,
                tq, n_sel, p_len, n_valid_keys, n_idx_heads, n_kv_heads, rep):
    n_chunks_total = key_scr.shape[0]
    q_start = p_len + pl.program_id(1) * tq
    n_ch = _num_key_chunks(q_start, tq, n_valid_keys, n_chunks_total)

    def idx_body(j, carry):
        k0 = pl.multiple_of(j * KEY_CHUNK, KEY_CHUNK)
        kib = ki_ref[pl.ds(k0, KEY_CHUNK), :]
        score = jnp.zeros((tq, KEY_CHUNK), F32)
        wi = wi_ref[:, LANES:2 * LANES] * (n_idx_heads ** -0.5)
        for h in range(n_idx_heads):
            lg = _nt_dot(qi_ref[:, h * LANES:(h + 1) * LANES], kib)
            score = score + jnp.maximum(lg, 0.0) * wi[:, h:h + 1]
        vis = _visible(q_start, tq, k0, KEY_CHUNK, n_valid_keys)
        key_scr[j] = _ordered_key(jnp.where(vis, score, -jnp.inf))
        return carry

    lax.fori_loop(0, n_ch, idx_body, 0)

    def bit_body(b, t):
        cand_u = t | lax.shift_left(jnp.int32(1), 31 - b)
        cand_s = cand_u ^ jnp.int32(INT_MIN)

        def cnt_body(j, cnt):
            ones = jnp.where(key_scr[j] >= cand_s, 1, 0)
            for c in range(KEY_CHUNK // LANES):
                cnt = cnt + ones[:, c * LANES:(c + 1) * LANES]
            return cnt

        cnt = lax.fori_loop(0, n_ch, cnt_body, jnp.zeros((tq, LANES), jnp.int32))
        total = jnp.sum(cnt, axis=1, keepdims=True)
        return jnp.where(total >= n_sel, cand_u, t)

    thr_u = lax.fori_loop(0, 32, bit_body, jnp.zeros((tq, 1), jnp.int32))
    thr = thr_u ^ jnp.int32(INT_MIN)

    def bias_body(j, carry):
        k0 = j * KEY_CHUNK
        vis = _visible(q_start, tq, k0, KEY_CHUNK, n_valid_keys)
        sel = (key_scr[j] >= thr) & vis
        bias_scr[j] = jnp.where(sel, 0.0, NEG_BIG)
        return carry

    lax.fori_loop(0, n_ch, bias_body, 0)

    for g in range(n_kv_heads):
        qs = jnp.concatenate(
            [q_ref[:, (g * rep + r) * LANES:(g * rep + r + 1) * LANES] for r in range(rep)], axis=0)

        def att_body(j, carry, g=g, qs=qs):
            m, l, acc = carry
            k0 = pl.multiple_of(j * KEY_CHUNK, KEY_CHUNK)
            kb = k_ref[pl.ds(k0, KEY_CHUNK), g * LANES:(g + 1) * LANES]
            vb = v_ref[pl.ds(k0, KEY_CHUNK), g * LANES:(g + 1) * LANES]
            bias = bias_scr[j]
            s = _nt_dot(qs, kb) + jnp.concatenate([bias] * rep, axis=0)
            m_new = jnp.maximum(m, jnp.max(s, axis=1, keepdims=True))
            p = jnp.exp(s - m_new)
            alpha = jnp.exp(m - m_new)
            l = alpha * l + jnp.sum(p, axis=1, keepdims=True)
            acc = alpha * acc + jnp.dot(p.astype(BF16), vb, preferred_element_type=F32)
            return m_new, l, acc

        init = (jnp.full((rep * tq, 1), NEG_BIG, F32), jnp.zeros((rep * tq, 1), F32),
                jnp.zeros((rep * tq, LANES), F32))
        _, l, acc = lax.fori_loop(0, n_ch, att_body, init)
        o = acc / l
        for r in range(rep):
            o_ref[:, (g * rep + r) * LANES:(g * rep + r + 1) * LANES] = o[r * tq:(r + 1) * tq].astype(o_ref.dtype)


def _dsa_attention(qs_all, kiw_f32, k_bf, v_bf, ki_bf, *, batch, q_len, k_len, n_valid_keys, p_len, tq,
                   n_heads, n_kv_heads, n_idx_heads, n_sel):
    width = n_heads * HEAD_DIM
    nq = q_len // tq
    n_chunks = k_len // KEY_CHUNK
    rep = n_heads // n_kv_heads
    kv_w = n_kv_heads * HEAD_DIM
    return pl.pallas_call(
        functools.partial(_dsa_kernel, tq=tq, n_sel=n_sel, p_len=p_len, n_valid_keys=n_valid_keys,
                          n_idx_heads=n_idx_heads, n_kv_heads=n_kv_heads, rep=rep),
        grid=(batch, nq),
        in_specs=[pl.BlockSpec((tq, width), lambda b, i: (b * nq + i, 0)),
                  pl.BlockSpec((tq, width), lambda b, i: (b * nq + i, 1)),
                  pl.BlockSpec((tq, 2 * LANES), lambda b, i: (b * nq + i, 0)),
                  pl.BlockSpec((k_len, kv_w), lambda b, i: (b, 0)),
                  pl.BlockSpec((k_len, kv_w), lambda b, i: (b, 0)),
                  pl.BlockSpec((k_len, LANES), lambda b, i: (b, 0))],
        out_specs=pl.BlockSpec((tq, width), lambda b, i: (b * nq + i, 0)),
        out_shape=jax.ShapeDtypeStruct((batch * q_len, width), BF16),
        scratch_shapes=[pltpu.VMEM((n_chunks, tq, KEY_CHUNK), jnp.int32),
                        pltpu.VMEM((n_chunks, tq, KEY_CHUNK), F32)],
        compiler_params=_params(("arbitrary", "arbitrary")),
        name="dsa_attention",
    )(qs_all, qs_all, kiw_f32, k_bf, v_bf, ki_bf)


def _diff_kernel(dq_ref, dk_ref, dv_ref, lq1_ref, lk1_ref, lq2_ref, lk2_ref, g_ref, o_ref, bias_scr, *,
                 tq, p_len, n_valid_keys, lam_init):
    n_chunks_total = bias_scr.shape[0]
    q_start = p_len + pl.program_id(2) * tq
    n_ch = _num_key_chunks(q_start, tq, n_valid_keys, n_chunks_total)
    dv_w = 2 * HEAD_DIM

    lam = (jnp.exp(jnp.sum(lq1_ref[...] * lk1_ref[...], axis=1, keepdims=True))
           - jnp.exp(jnp.sum(lq2_ref[...] * lk2_ref[...], axis=1, keepdims=True)) + lam_init)

    def bias_body(j, carry):
        vis = _visible(q_start, tq, j * KEY_CHUNK, KEY_CHUNK, n_valid_keys)
        bias_scr[j] = jnp.where(vis, 0.0, NEG_BIG)
        return carry

    lax.fori_loop(0, n_ch, bias_body, 0)

    q0 = dq_ref[:, :LANES]
    q1 = dq_ref[:, LANES:]

    def att_body(j, carry):
        k0 = pl.multiple_of(j * KEY_CHUNK, KEY_CHUNK)
        vb = dv_ref[pl.ds(k0, KEY_CHUNK), :]
        bias = bias_scr[j]
        new = []
        for c, qc in ((0, q0), (1, q1)):
            m, l, acc = carry[3 * c:3 * c + 3]
            kb = dk_ref[pl.ds(k0, KEY_CHUNK), c * LANES:(c + 1) * LANES]
            s = _nt_dot(qc, kb) + bias
            m_new = jnp.maximum(m, jnp.max(s, axis=1, keepdims=True))
            p = jnp.exp(s - m_new)
            alpha = jnp.exp(m - m_new)
            l = alpha * l + jnp.sum(p, axis=1, keepdims=True)
            acc = alpha * acc + jnp.dot(p.astype(BF16), vb, preferred_element_type=F32)
            new += [m_new, l, acc]
        return tuple(new)

    one = (jnp.full((tq, 1), NEG_BIG, F32), jnp.zeros((tq, 1), F32), jnp.zeros((tq, dv_w), F32))
    _, l0, a0, _, l1, a1 = lax.fori_loop(0, n_ch, att_body, one + one)
    o = a0 / l0 - lam * (a1 / l1)
    o = o * lax.rsqrt(jnp.mean(o * o, axis=-1, keepdims=True) + EPS) * g_ref[...]
    o_ref[...] = (o * (1.0 - lam_init)).astype(o_ref.dtype)


def _diff_attention(qs_all, dk_bf, dv_bf, lams, g_subln, *, batch, q_len, k_len, n_valid_keys, p_len, tq,
                    n_heads, lam_init):
    hw = 2 * HEAD_DIM
    width = n_heads * hw
    nq = q_len // tq
    n_chunks = k_len // KEY_CHUNK
    assert qs_all.shape[1] == 3 * width
    dq_col0 = 2 * n_heads
    vec = pl.BlockSpec((1, HEAD_DIM), lambda b, h, i: (0, 0))
    return pl.pallas_call(
        functools.partial(_diff_kernel, tq=tq, p_len=p_len, n_valid_keys=n_valid_keys, lam_init=lam_init),
        grid=(batch, n_heads, nq),
        in_specs=[pl.BlockSpec((tq, hw), lambda b, h, i: (b * nq + i, dq_col0 + h)),
                  pl.BlockSpec((k_len, hw), lambda b, h, i: (b, h)),
                  pl.BlockSpec((k_len, hw), lambda b, h, i: (b, h)),
                  vec, vec, vec, vec,
                  pl.BlockSpec((1, hw), lambda b, h, i: (0, 0))],
        out_specs=pl.BlockSpec((tq, hw), lambda b, h, i: (b * nq + i, h)),
        out_shape=jax.ShapeDtypeStruct((batch * q_len, width), BF16),
        scratch_shapes=[pltpu.VMEM((n_chunks, tq, KEY_CHUNK), F32)],
        compiler_params=_params(("arbitrary", "arbitrary", "arbitrary")),
        name="diff_attention",
    )(qs_all, dk_bf, dv_bf, *[v.reshape(1, HEAD_DIM) for v in lams], g_subln.reshape(1, hw))


def _outproj_kernel(a_ref, d_ref, wa_ref, wd_ref, x_ref, ga_ref, o_ref, *, per_token):
    mix = (jnp.dot(a_ref[...], wa_ref[...], preferred_element_type=F32)
           + jnp.dot(d_ref[...], wd_ref[...], preferred_element_type=F32))
    ga = ga_ref[...] if per_token else ga_ref[0]
    o_ref[...] = x_ref[...] + ga * mix


def _outproj(a_out, d_out, w_out_bf, x2, mod, *, per_token, rows_per_batch, tm, tn):
    t, d = x2.shape
    wa = a_out.shape[1]
    wd = d_out.shape[1]
    assert wa == wd
    cb = 2 * (d // tn)
    return pl.pallas_call(
        functools.partial(_outproj_kernel, per_token=per_token),
        grid=(t // tm, d // tn),
        in_specs=[pl.BlockSpec((tm, wa), lambda m, n: (m, 0)),
                  pl.BlockSpec((tm, wd), lambda m, n: (m, 0)),
                  pl.BlockSpec((wa, tn), lambda m, n: (0, n)),
                  pl.BlockSpec((wd, tn), lambda m, n: (1, n)),
                  pl.BlockSpec((tm, tn), lambda m, n: (m, n)),
                  _mod_spec(mod, per_token, tm, tn, cb, rows_per_batch)],
        out_specs=pl.BlockSpec((tm, tn), lambda m, n: (m, n)),
        out_shape=jax.ShapeDtypeStruct((t, d), F32),
        compiler_params=_params(("arbitrary", "arbitrary")),
        name="outproj",
    )(a_out, d_out, w_out_bf, w_out_bf, x2, mod)


def _top_rows(x, k, with_rank=False):
    tops = []
    rank = jnp.full(x.shape, float(k), F32) if with_rank else None
    for i in range(k):
        mx = jnp.max(x, axis=0, keepdims=True)
        tops.append(mx)
        hit = x == mx
        if with_rank:
            rank = jnp.where(hit, float(i), rank)
        x = jnp.where(hit, -jnp.inf, x)
    return (tops, rank) if with_rank else tops


def _peer_route_kernel(h_ref, wq_ref, keys_ref, cnt_ref, e1_ref, rank_ref, e2_ref, s1_scr, top_scr):
    c = pl.program_id(1) % 2
    q_t = jnp.dot(wq_ref[...], h_ref[...], preferred_element_type=F32)
    s_t = jnp.dot(keys_ref[0].astype(BF16), q_t.astype(BF16), preferred_element_type=F32)

    @pl.when(c == 0)
    def _():
        s1_scr[...] = s_t
        top_scr[...] = jnp.concatenate(_top_rows(s_t, PEER_TOPK), axis=0)

    @pl.when(c == 1)
    def _():
        tops2, rank2 = _top_rows(s_t, PEER_TOPK, with_rank=True)
        tops = jnp.concatenate(tops2, axis=0)
        top1 = top_scr[...]
        cand = jnp.concatenate([top1[0:1, :] + tops]
                               + [top1[i:i + 1, :] + tops[:PEER_TOPK // 2] for i in range(1, PEER_TOPK)], axis=0)
        best = _top_rows(cand, PEER_TOPK)
        m = best[0]
        z = jnp.zeros_like(m)
        for bk in best:
            z = z + jnp.exp(bk - m)
        thr = best[PEER_TOPK - 1]
        s1 = s1_scr[...]
        cnt = jnp.zeros_like(s1)
        for j in range(PEER_TOPK):
            cnt = cnt + jnp.where(s1 + tops2[j] >= thr, 1.0, 0.0)
        cnt_ref[0] = cnt
        e1_ref[0] = jnp.exp(s1 - top1[0:1, :]) / z
        rank_ref[0] = rank2.astype(rank_ref.dtype)
        e2_ref[0] = jnp.exp(s_t - tops2[0]).astype(e2_ref.dtype)


def _peer_route(h_t, wq_t_bf, sub_keys, *, tm):
    d, t = h_t.shape
    heads, _, n_keys, half = sub_keys.shape
    keys2 = sub_keys.reshape(heads * 2, n_keys, half)
    tab_spec = pl.BlockSpec((1, n_keys, tm), lambda m, hc: (hc // 2, 0, m))
    f32_tab = jax.ShapeDtypeStruct((heads, n_keys, t), F32)
    bf_tab = jax.ShapeDtypeStruct((heads, n_keys, t), BF16)
    return pl.pallas_call(
        _peer_route_kernel,
        grid=(t // tm, heads * 2),
        in_specs=[pl.BlockSpec((d, tm), lambda m, hc: (0, m)),
                  pl.BlockSpec((half, d), lambda m, hc: (hc, 0)),
                  pl.BlockSpec((1, n_keys, half), lambda m, hc: (hc, 0, 0))],
        out_specs=[tab_spec, tab_spec, tab_spec, tab_spec],
        out_shape=[f32_tab, f32_tab, bf_tab, bf_tab],
        scratch_shapes=[pltpu.VMEM((n_keys, tm), F32), pltpu.VMEM((PEER_TOPK, tm), F32)],
        compiler_params=_params(("arbitrary", "arbitrary")),
        name="peer_route",
    )(h_t, wq_t_bf, keys2)


def _gelu_tanh(x):
    return 0.5 * x * (1.0 + jnp.tanh(0.7978845608028654 * (x + 0.044715 * x * x * x)))


def _rows_bf16(row, n_rows):
    tile = jnp.broadcast_to(row, (16, row.shape[1])).astype(BF16)
    return jnp.concatenate([tile] * (n_rows // 16), axis=0)


def _peer_ffn_kernel(h_ref, u_ref, v_ref, cnt_ref, e1_ref, rank_ref, e2_ref, o_ref, w_scr, *, n_keys, n_blocks):
    e = pl.program_id(1)
    te, tm = w_scr.shape[1:]
    heads = cnt_ref.shape[0]
    a0 = jnp.minimum(e, n_blocks - 1) * (te // n_keys)

    @pl.when(e == 0)
    def _():
        o_ref[...] = jnp.zeros_like(o_ref)
        w_scr[1] = jnp.zeros((te, tm), BF16)

    def step(rd, wr):
        o_ref[...] += lax.dot_general(w_scr[rd], v_ref[...], (((0,), (0,)), ((), ())),
                                      preferred_element_type=F32)
        act = jnp.dot(u_ref[...], h_ref[...], preferred_element_type=F32)
        strip = min(tm, LANES)
        for ai in range(te // n_keys):
            rows = slice(ai * n_keys, (ai + 1) * n_keys)
            cnt_a = [cnt_ref[h, pl.ds(a0 + ai, 1), :] for h in range(heads)]
            e1_a = [e1_ref[h, pl.ds(a0 + ai, 1), :] for h in range(heads)]
            for c in range(tm // strip):
                cols = slice(c * strip, (c + 1) * strip)
                gate = jnp.zeros((n_keys, strip), BF16)
                for h in range(heads):
                    cnt_b = _rows_bf16(cnt_a[h][:, cols], n_keys)
                    e1_b = _rows_bf16(e1_a[h][:, cols], n_keys)
                    keep = rank_ref[h, :, cols] < cnt_b
                    gate = gate + jnp.where(keep, e2_ref[h, :, cols] * e1_b, jnp.zeros_like(e1_b))
                w_scr[wr, rows, cols] = gate * _gelu_tanh(act[rows, cols]).astype(BF16)

    @pl.when(e % 2 == 0)
    def _():
        step(1, 0)

    @pl.when(e % 2 == 1)
    def _():
        step(0, 1)


def _peer_ffn(h_t, u_bf, v_bf, cnt, e1, rank, e2, *, tm, te):
    d, t = h_t.shape
    n_exp = u_bf.shape[0]
    n_blocks = n_exp // te
    heads, n_keys, _ = cnt.shape
    once = pl.Buffered(1)
    tab_spec = pl.BlockSpec((heads, n_keys, tm), lambda m, e: (0, 0, m), pipeline_mode=once)
    return pl.pallas_call(
        functools.partial(_peer_ffn_kernel, n_keys=n_keys, n_blocks=n_blocks),
        grid=(t // tm, n_blocks + 1),
        in_specs=[pl.BlockSpec((d, tm), lambda m, e: (0, m), pipeline_mode=once),
                  pl.BlockSpec((te, d), lambda m, e: (jnp.minimum(e, n_blocks - 1), 0)),
                  pl.BlockSpec((te, d), lambda m, e: (jnp.maximum(e - 1, 0), 0)),
                  tab_spec, tab_spec, tab_spec, tab_spec],
        out_specs=pl.BlockSpec((tm, d), lambda m, e: (m, 0)),
        out_shape=jax.ShapeDtypeStruct((t, d), F32),
        scratch_shapes=[pltpu.VMEM((2, te, tm), BF16)],
        compiler_params=_params(("arbitrary", "arbitrary")),
        name="peer_ffn",
    )(h_t, u_bf, v_bf, cnt, e1, rank, e2)


def _final_kernel(x_ref, p_ref, ga_ref, g_ref, o_ref, *, per_token, normalize):
    ga = ga_ref[...] if per_token else ga_ref[0]
    x = x_ref[...] + ga * p_ref[...]
    if normalize:
        x = x * lax.rsqrt(jnp.mean(x * x, axis=-1, keepdims=True) + EPS) * g_ref[...]
    o_ref[...] = x


def _final(x1, peer, mod, g_final, *, per_token, rows_per_batch, tm, normalize):
    t, d = x1.shape
    row = pl.BlockSpec((tm, d), lambda m: (m, 0))
    return pl.pallas_call(
        functools.partial(_final_kernel, per_token=per_token, normalize=normalize),
        grid=(t // tm,),
        in_specs=[row, row, _mod_spec(mod, per_token, tm, d, 5, rows_per_batch),
                  pl.BlockSpec((1, d), lambda m: (0, 0))],
        out_specs=row,
        out_shape=jax.ShapeDtypeStruct((t, d), F32),
        compiler_params=_params(("arbitrary",)),
        name="final",
    )(x1, peer, mod, g_final.reshape(1, d))


def _rope_tables(pos):
    half = HEAD_DIM // 2
    inv = ROPE_THETA ** (-jnp.arange(half, dtype=F32) / half)
    ang = pos.astype(F32)[:, None] * inv[None, :]
    cos, sin = jnp.cos(ang), jnp.sin(ang)
    return jnp.concatenate([cos, cos], axis=1), jnp.concatenate([-sin, sin], axis=1)


def _pick_tile(n, prefs):
    for p in prefs:
        if n % p == 0:
            return p
    return n


def _layer(x, mod_rows, past, layer, last_layer, w, dims):
    b, s, d = x.shape
    t = b * s
    n_heads, n_kv, n_idx, n_diff = dims
    p_len = 0 if past is None else past[0].shape[1]
    n_keys_valid = p_len + s
    n_sel = min(DSA_TOPK, n_keys_valid // 4)
    x2 = x.reshape(t, d)

    per_token = s % 256 != 0
    if per_token:
        mod = jnp.repeat(mod_rows, s, axis=0)
    else:
        mod = mod_rows.reshape(b, 1, 6 * d)
    tm_big = _pick_tile(t if per_token else s, (1024, 512, 256, 128))
    tm_mid = _pick_tile(t if per_token else s, (512, 256, 128))
    tm_small = _pick_tile(t if per_token else s, (256, 128))

    pos = p_len + jnp.arange(s)
    cos, sin = _rope_tables(pos)
    if per_token:
        cos, sin = jnp.tile(cos, (b, 1)), jnp.tile(sin, (b, 1))

    h = _norm_mod(x2, w["g_norm_mix"], mod, which=1, per_token=per_token, rows_per_batch=s, tm=tm_mid,
                  transpose=False)

    proj = functools.partial(_proj, h, cos=cos, sin=sin, tm=tm_big)
    wq = n_heads * HEAD_DIM
    (qs_all,) = proj(w["w_qs"], rope_cols=512, scale=HEAD_DIM ** -0.5, out_f32=False, out_bf16=True, tn=512,
                     name="proj_q")
    k_f, k_b = proj(w["w_k"], rope_cols=512, out_f32=True, out_bf16=True, tn=512, name="proj_k")
    v_f, v_b = proj(w["w_v"], rope_cols=0, out_f32=True, out_bf16=True, tn=512, name="proj_v")
    dk_f, dk_b = proj(w["w_dk"], rope_cols=512, out_f32=True, out_bf16=True, tn=512, name="proj_dk")
    dv_f, dv_b = proj(w["w_dv"], rope_cols=0, out_f32=True, out_bf16=True, tn=512, name="proj_dv")
    kiw_f, kiw_b = proj(w["w_kiw"], rope_cols=LANES, out_f32=True, out_bf16=True, tn=2 * LANES, name="proj_kiw")

    ki_f = kiw_f[:, :LANES]
    new_rows = (k_f.reshape(b, s, n_kv, HEAD_DIM), v_f.reshape(b, s, n_kv, HEAD_DIM), ki_f.reshape(b, s, LANES),
                dk_f.reshape(b, s, n_diff, 2, HEAD_DIM), dv_f.reshape(b, s, n_diff, 2 * HEAD_DIM))

    if past is None:
        k_len = s
        k_all, v_all, ki_all, dk_all, dv_all = k_b, v_b, kiw_b, dk_b, dv_b
    else:
        k_len = -(-n_keys_valid // KEY_CHUNK) * KEY_CHUNK

        def join(cache, new, width):
            new = new.reshape(b, s, -1)[:, :, :width]
            both = jnp.concatenate([cache.reshape(b, p_len, width).astype(BF16), new], axis=1)
            both = jnp.pad(both, ((0, 0), (0, k_len - n_keys_valid), (0, 0)))
            return both.reshape(b * k_len, width)

        k_all = join(past[0], k_b, n_kv * HEAD_DIM)
        v_all = join(past[1], v_b, n_kv * HEAD_DIM)
        ki_all = join(past[2], kiw_b, LANES)
        dk_all = join(past[3], dk_b, n_diff * 2 * HEAD_DIM)
        dv_all = join(past[4], dv_b, n_diff * 2 * HEAD_DIM)

    tq_a = _pick_tile(s, (128, 64, 32))
    a_out = _dsa_attention(qs_all, kiw_f, k_all, v_all, ki_all, batch=b, q_len=s, k_len=k_len,
                           n_valid_keys=n_keys_valid, p_len=p_len, tq=tq_a, n_heads=n_heads, n_kv_heads=n_kv,
                           n_idx_heads=n_idx, n_sel=n_sel)
    tq_d = _pick_tile(s, (256, 128, 64, 32))
    d_out = _diff_attention(qs_all, dk_all, dv_all, w["lams"], w["g_subln"], batch=b, q_len=s, k_len=k_len,
                            n_valid_keys=n_keys_valid, p_len=p_len, tq=tq_d, n_heads=n_diff,
                            lam_init=_lambda_init(layer))

    x1 = _outproj(a_out, d_out, w["w_out"], x2, mod, per_token=per_token, rows_per_batch=s, tm=tm_big, tn=512)

    h2_t = _norm_mod(x1, w["g_norm_ffn"], mod, which=2, per_token=per_token, rows_per_batch=s, tm=tm_mid,
                     transpose=True)
    cnt, e1, rank, e2 = _peer_route(h2_t, w["peer_wq_t"], w["peer_keys"], tm=tm_mid)
    peer = _peer_ffn(h2_t, w["peer_u"], w["peer_v"], cnt, e1, rank, e2, tm=tm_mid, te=512)
    x_out = _final(x1, peer, mod, w["g_final"], per_token=per_token, rows_per_batch=s, tm=tm_small,
                   normalize=last_layer)
    return x_out.reshape(b, s, d), new_rows


def kernel(x_prompt, x_sample, cache_dsa_k, cache_dsa_v, cache_idx_k, cache_diff_k, cache_diff_v, c_prompt, c_sample, w_ada, b_ada, g_norm_mix, g_norm_ffn, w_in, diff_lambda_q1, diff_lambda_k1, diff_lambda_q2, diff_lambda_k2, g_diff_subln, w_out, peer_w_query, peer_sub_keys, peer_u, peer_v, g_final):
    depth = w_in.shape[0]
    d = x_prompt.shape[-1]
    bp, bs = x_prompt.shape[0], x_sample.shape[0]
    n_kv = cache_dsa_k.shape[3]
    n_diff = cache_diff_k.shape[3]
    n_heads, n_idx_heads = DSA_HEADS, IDX_HEADS
    qw, kvw, dw = n_heads * HEAD_DIM, n_kv * HEAD_DIM, n_diff * 2 * HEAD_DIM
    assert w_in.shape[2] == qw + 2 * kvw + n_idx_heads * LANES + LANES + n_idx_heads + 3 * dw
    dims = (n_heads, n_kv, n_idx_heads, n_diff)

    hp, hs = x_prompt, x_sample
    rows_p, rows_s = [], []
    n_c = bp + bs
    c_pad = jnp.pad(jnp.concatenate([c_prompt, c_sample], axis=0), ((0, (-n_c) % 16), (0, 0)))
    for l in range(depth):
        mod_all = _adaln(c_pad, w_ada[l], b_ada[l])
        win = w_in[l]
        o = 0
        seg = {}
        for name, width in (("q", qw), ("k", kvw), ("v", kvw), ("qi", n_idx_heads * LANES), ("ki", LANES),
                            ("wi", n_idx_heads), ("dq", dw), ("dk", dw), ("dv", dw)):
            seg[name] = win[:, o:o + width]
            o += width
        weights = {
            "w_qs": jnp.concatenate([seg["q"], seg["qi"], seg["dq"]], axis=1).astype(BF16),
            "w_k": seg["k"].astype(BF16), "w_v": seg["v"].astype(BF16),
            "w_dk": seg["dk"].astype(BF16), "w_dv": seg["dv"].astype(BF16),
            "w_kiw": jnp.concatenate([seg["ki"], seg["wi"], jnp.zeros((d, LANES - n_idx_heads), F32)],
                                     axis=1).astype(BF16),
            "w_out": w_out[l].astype(BF16),
            "peer_wq_t": peer_w_query[l].T.astype(BF16),
            "peer_keys": peer_sub_keys[l],
            "peer_u": peer_u[l].astype(BF16), "peer_v": peer_v[l].astype(BF16),
            "g_norm_mix": g_norm_mix[l], "g_norm_ffn": g_norm_ffn[l], "g_subln": g_diff_subln[l],
            "lams": (diff_lambda_q1[l], diff_lambda_k1[l], diff_lambda_q2[l], diff_lambda_k2[l]),
            "g_final": g_final,
        }
        last = l == depth - 1
        hp, rp = _layer(hp, mod_all[:bp], None, l, last, weights, dims)
        past = (cache_dsa_k[l], cache_dsa_v[l], cache_idx_k[l], cache_diff_k[l], cache_diff_v[l])
        hs, rs = _layer(hs, mod_all[bp:bp + bs], past, l, last, weights, dims)
        rows_p.append(rp)
        rows_s.append(rs)
    stack = lambda rows, i: jnp.stack([r[i] for r in rows])
    return (hp, hs) + tuple(stack(rows_p, i) for i in range(5)) + tuple(stack(rows_s, i) for i in range(5))
```

```python
import functools
import math

import jax
import jax.numpy as jnp
from jax import lax
from jax.experimental import pallas as pl
from jax.experimental.pallas import tpu as pltpu

CHUNK = 64
HEAD_DIM = 128
ROPE_THETA = 10000.0
EPS = 1e-6
DSA_HEADS = 16
IDX_HEADS = 16
DSA_TOPK = 256
PEER_TOPK = 16
LANES = 128
KEY_CHUNK = 512
NEG_BIG = -1e30
INT_MIN = -(2 ** 31)
KEY_NEG_INF = INT_MIN + 0x7FFFFF
VMEM_LIMIT = 56 * 1024 * 1024

BF16 = jnp.bfloat16
F32 = jnp.float32


def _params(sem, vmem=VMEM_LIMIT):
    return pltpu.CompilerParams(dimension_semantics=sem, vmem_limit_bytes=vmem)


def _lambda_init(layer):
    return 0.8 - 0.6 * math.exp(-0.3 * layer)


def _nt_dot(a, b):
    return lax.dot_general(a, b, (((1,), (1,)), ((), ())), preferred_element_type=F32)


def _adaln_kernel(c_ref, w_ref, b_ref, o_ref):
    c = c_ref[...]
    a = (c * jax.nn.sigmoid(c)).astype(BF16)
    o_ref[...] = jnp.dot(a, w_ref[...].astype(BF16), preferred_element_type=F32) + b_ref[...]


def _adaln(c_pad, w_ada, b_ada):
    bp, d = c_pad.shape
    n = w_ada.shape[1]
    tn = 512
    return pl.pallas_call(
        _adaln_kernel,
        grid=(n // tn,),
        in_specs=[pl.BlockSpec((bp, d), lambda j: (0, 0)),
                  pl.BlockSpec((d, tn), lambda j: (0, j)),
                  pl.BlockSpec((1, tn), lambda j: (0, j))],
        out_specs=pl.BlockSpec((bp, tn), lambda j: (0, j)),
        out_shape=jax.ShapeDtypeStruct((bp, n), F32),
        compiler_params=_params(("arbitrary",)),
        name="adaln",
    )(c_pad, w_ada, b_ada.reshape(1, n))


def _norm_mod_kernel(x_ref, g_ref, sc_ref, sh_ref, o_ref, *, per_token, transpose):
    x = x_ref[...]
    y = x * lax.rsqrt(jnp.mean(x * x, axis=-1, keepdims=True) + EPS) * g_ref[...]
    sc = sc_ref[...] if per_token else sc_ref[0]
    sh = sh_ref[...] if per_token else sh_ref[0]
    h = y * (1.0 + sc) + sh
    if transpose:
        o_ref[...] = h.T.astype(o_ref.dtype)
    else:
        o_ref[...] = h.astype(o_ref.dtype)


def _mod_spec(mod, per_token, tm, tn, col_block, rows_per_batch):
    if per_token:
        return pl.BlockSpec((tm, tn), lambda m, n=0, cb=col_block: (m, cb + n))
    tiles_per_batch = rows_per_batch // tm
    return pl.BlockSpec((1, 1, tn), lambda m, n=0, cb=col_block: (m // tiles_per_batch, 0, cb + n))


def _norm_mod(x2, g, mod, *, which, per_token, rows_per_batch, tm, transpose):
    t, d = x2.shape
    sh_blk, sc_blk = (0, 1) if which == 1 else (3, 4)
    out_shape = (d, t) if transpose else (t, d)
    out_spec = pl.BlockSpec((d, tm), lambda m: (0, m)) if transpose else pl.BlockSpec((tm, d), lambda m: (m, 0))
    return pl.pallas_call(
        functools.partial(_norm_mod_kernel, per_token=per_token, transpose=transpose),
        grid=(t // tm,),
        in_specs=[pl.BlockSpec((tm, d), lambda m: (m, 0)),
                  pl.BlockSpec((1, d), lambda m: (0, 0)),
                  _mod_spec(mod, per_token, tm, d, sc_blk, rows_per_batch),
                  _mod_spec(mod, per_token, tm, d, sh_blk, rows_per_batch)],
        out_specs=out_spec,
        out_shape=jax.ShapeDtypeStruct(out_shape, BF16),
        compiler_params=_params(("arbitrary",)),
        name="norm_mod_t" if transpose else "norm_mod",
    )(x2, g.reshape(1, d), mod, mod)


def _rope_tile(acc, cos, sin, n_chunks):
    outs = []
    for j in range(n_chunks):
        xj = acc[:, j * LANES:(j + 1) * LANES]
        outs.append(xj * cos + pltpu.roll(xj, LANES // 2, axis=1) * sin)
    return outs[0] if n_chunks == 1 else jnp.concatenate(outs, axis=1)


def _proj_kernel(*refs, rope_cols, scale, out_f32, out_bf16):
    h_ref, w_ref, cos_ref, sin_ref = refs[:4]
    outs = refs[4:]
    acc = jnp.dot(h_ref[...], w_ref[...], preferred_element_type=F32)
    tn = acc.shape[1]
    if rope_cols:
        roped = _rope_tile(acc[:, :rope_cols], cos_ref[...], sin_ref[...], rope_cols // LANES)
        acc = roped if rope_cols == tn else jnp.concatenate([roped, acc[:, rope_cols:]], axis=1)
    k = 0
    if out_f32:
        outs[k][...] = acc
        k += 1
    if out_bf16:
        outs[k][...] = (acc * scale if scale != 1.0 else acc).astype(BF16)


def _proj(h, w, cos, sin, *, rope_cols, scale=1.0, out_f32, out_bf16, tm, tn, name):
    t, d = h.shape
    n = w.shape[1]
    assert rope_cols in (0, tn) or n == tn
    pos_tiles = cos.shape[0] // tm
    out_shape, out_specs = [], []
    for want, dt in ((out_f32, F32), (out_bf16, BF16)):
        if want:
            out_shape.append(jax.ShapeDtypeStruct((t, n), dt))
            out_specs.append(pl.BlockSpec((tm, tn), lambda m, j: (m, j)))
    return pl.pallas_call(
        functools.partial(_proj_kernel, rope_cols=rope_cols, scale=scale, out_f32=out_f32, out_bf16=out_bf16),
        grid=(t // tm, n // tn),
        in_specs=[pl.BlockSpec((tm, d), lambda m, j: (m, 0)),
                  pl.BlockSpec((d, tn), lambda m, j: (0, j)),
                  pl.BlockSpec((tm, LANES), lambda m, j: (m % pos_tiles, 0)),
                  pl.BlockSpec((tm, LANES), lambda m, j: (m % pos_tiles, 0))],
        out_specs=out_specs,
        out_shape=out_shape,
        compiler_params=_params(("arbitrary", "arbitrary")),
        name=name,
    )(h, w, cos, sin)


def _visible(q_start, n_q, key_start, n_k, n_valid_keys):
    qpos = q_start + lax.broadcasted_iota(jnp.int32, (n_q, 1), 0)
    kpos = key_start + lax.broadcasted_iota(jnp.int32, (1, n_k), 1)
    return ((kpos >> 6) <= (qpos >> 6)) & (kpos < n_valid_keys)


def _num_key_chunks(q_start, n_q, n_valid_keys, n_chunks_total):
    last_visible = (((q_start + n_q - 1) >> 6) + 1) * CHUNK
    last_visible = jnp.minimum(last_visible, n_valid_keys)
    return jnp.minimum((last_visible + KEY_CHUNK - 1) // KEY_CHUNK, n_chunks_total)


def _ordered_key(x):
    b = pltpu.bitcast(x, jnp.int32)
    return b ^ ((b >> 31) & jnp.int32(0x7FFFFFFF))


def _dsa_kernel(q_ref, qi_ref, wi_ref, k_ref, v_ref, ki_ref, o_ref, key_scr, bias_scr, *,
                tq, n_sel, p_len, n_valid_keys, n_idx_heads, n_kv_heads, rep):
    n_chunks_total = key_scr.shape[0]
    q_start = p_len + pl.program_id(1) * tq
    n_ch = _num_key_chunks(q_start, tq, n_valid_keys, n_chunks_total)

    def idx_body(j, carry):
        k0 = pl.multiple_of(j * KEY_CHUNK, KEY_CHUNK)
        kib = ki_ref[pl.ds(k0, KEY_CHUNK), :]
        score = jnp.zeros((tq, KEY_CHUNK), F32)
        wi = wi_ref[:, LANES:2 * LANES] * (n_idx_heads ** -0.5)
        for h in range(n_idx_heads):
            lg = _nt_dot(qi_ref[:, h * LANES:(h + 1) * LANES], kib)
            score = score + jnp.maximum(lg, 0.0) * wi[:, h:h + 1]
        vis = _visible(q_start, tq, k0, KEY_CHUNK, n_valid_keys)
        key_scr[j] = _ordered_key(jnp.where(vis, score, -jnp.inf))
        return carry

    lax.fori_loop(0, n_ch, idx_body, 0)

    def bit_body(b, t):
        cand_u = t | lax.shift_left(jnp.int32(1), 31 - b)
        cand_s = cand_u ^ jnp.int32(INT_MIN)

        def cnt_body(j, cnt):
            ones = jnp.where(key_scr[j] >= cand_s, 1, 0)
            for c in range(KEY_CHUNK // LANES):
                cnt = cnt + ones[:, c * LANES:(c + 1) * LANES]
            return cnt

        cnt = lax.fori_loop(0, n_ch, cnt_body, jnp.zeros((tq, LANES), jnp.int32))
        total = jnp.sum(cnt, axis=1, keepdims=True)
        return jnp.where(total >= n_sel, cand_u, t)

    thr_u = lax.fori_loop(0, 32, bit_body, jnp.zeros((tq, 1), jnp.int32))
    thr = thr_u ^ jnp.int32(INT_MIN)

    def bias_body(j, carry):
        k0 = j * KEY_CHUNK
        vis = _visible(q_start, tq, k0, KEY_CHUNK, n_valid_keys)
        sel = (key_scr[j] >= thr) & vis
        bias_scr[j] = jnp.where(sel, 0.0, NEG_BIG)
        return carry

    lax.fori_loop(0, n_ch, bias_body, 0)

    for g in range(n_kv_heads):
        qs = jnp.concatenate(
            [q_ref[:, (g * rep + r) * LANES:(g * rep + r + 1) * LANES] for r in range(rep)], axis=0)

        def att_body(j, carry, g=g, qs=qs):
            m, l, acc = carry
            k0 = pl.multiple_of(j * KEY_CHUNK, KEY_CHUNK)
            kb = k_ref[pl.ds(k0, KEY_CHUNK), g * LANES:(g + 1) * LANES]
            vb = v_ref[pl.ds(k0, KEY_CHUNK), g * LANES:(g + 1) * LANES]
            bias = bias_scr[j]
            s = _nt_dot(qs, kb) + jnp.concatenate([bias] * rep, axis=0)
            m_new = jnp.maximum(m, jnp.max(s, axis=1, keepdims=True))
            p = jnp.exp(s - m_new)
            alpha = jnp.exp(m - m_new)
            l = alpha * l + jnp.sum(p, axis=1, keepdims=True)
            acc = alpha * acc + jnp.dot(p.astype(BF16), vb, preferred_element_type=F32)
            return m_new, l, acc

        init = (jnp.full((rep * tq, 1), NEG_BIG, F32), jnp.zeros((rep * tq, 1), F32),
                jnp.zeros((rep * tq, LANES), F32))
        _, l, acc = lax.fori_loop(0, n_ch, att_body, init)
        o = acc / l
        for r in range(rep):
            o_ref[:, (g * rep + r) * LANES:(g * rep + r + 1) * LANES] = o[r * tq:(r + 1) * tq].astype(o_ref.dtype)


def _dsa_attention(qs_all, kiw_f32, k_bf, v_bf, ki_bf, *, batch, q_len, k_len, n_valid_keys, p_len, tq,
                   n_heads, n_kv_heads, n_idx_heads, n_sel):
    width = n_heads * HEAD_DIM
    nq = q_len // tq
    n_chunks = k_len // KEY_CHUNK
    rep = n_heads // n_kv_heads
    kv_w = n_kv_heads * HEAD_DIM
    return pl.pallas_call(
        functools.partial(_dsa_kernel, tq=tq, n_sel=n_sel, p_len=p_len, n_valid_keys=n_valid_keys,
                          n_idx_heads=n_idx_heads, n_kv_heads=n_kv_heads, rep=rep),
        grid=(batch, nq),
        in_specs=[pl.BlockSpec((tq, width), lambda b, i: (b * nq + i, 0)),
                  pl.BlockSpec((tq, width), lambda b, i: (b * nq + i, 1)),
                  pl.BlockSpec((tq, 2 * LANES), lambda b, i: (b * nq + i, 0)),
                  pl.BlockSpec((k_len, kv_w), lambda b, i: (b, 0)),
                  pl.BlockSpec((k_len, kv_w), lambda b, i: (b, 0)),
                  pl.BlockSpec((k_len, LANES), lambda b, i: (b, 0))],
        out_specs=pl.BlockSpec((tq, width), lambda b, i: (b * nq + i, 0)),
        out_shape=jax.ShapeDtypeStruct((batch * q_len, width), BF16),
        scratch_shapes=[pltpu.VMEM((n_chunks, tq, KEY_CHUNK), jnp.int32),
                        pltpu.VMEM((n_chunks, tq, KEY_CHUNK), F32)],
        compiler_params=_params(("arbitrary", "arbitrary")),
        name="dsa_attention",
    )(qs_all, qs_all, kiw_f32, k_bf, v_bf, ki_bf)


def _diff_kernel(dq_ref, dk_ref, dv_ref, lq1_ref, lk1_ref, lq2_ref, lk2_ref, g_ref, o_ref, bias_scr, *,
                 tq, p_len, n_valid_keys, lam_init):
    n_chunks_total = bias_scr.shape[0]
    q_start = p_len + pl.program_id(2) * tq
    n_ch = _num_key_chunks(q_start, tq, n_valid_keys, n_chunks_total)
    dv_w = 2 * HEAD_DIM

    lam = (jnp.exp(jnp.sum(lq1_ref[...] * lk1_ref[...], axis=1, keepdims=True))
           - jnp.exp(jnp.sum(lq2_ref[...] * lk2_ref[...], axis=1, keepdims=True)) + lam_init)

    def bias_body(j, carry):
        vis = _visible(q_start, tq, j * KEY_CHUNK, KEY_CHUNK, n_valid_keys)
        bias_scr[j] = jnp.where(vis, 0.0, NEG_BIG)
        return carry

    lax.fori_loop(0, n_ch, bias_body, 0)

    q0 = dq_ref[:, :LANES]
    q1 = dq_ref[:, LANES:]

    def att_body(j, carry):
        k0 = pl.multiple_of(j * KEY_CHUNK, KEY_CHUNK)
        vb = dv_ref[pl.ds(k0, KEY_CHUNK), :]
        bias = bias_scr[j]
        new = []
        for c, qc in ((0, q0), (1, q1)):
            m, l, acc = carry[3 * c:3 * c + 3]
            kb = dk_ref[pl.ds(k0, KEY_CHUNK), c * LANES:(c + 1) * LANES]
            s = _nt_dot(qc, kb) + bias
            m_new = jnp.maximum(m, jnp.max(s, axis=1, keepdims=True))
            p = jnp.exp(s - m_new)
            alpha = jnp.exp(m - m_new)
            l = alpha * l + jnp.sum(p, axis=1, keepdims=True)
            acc = alpha * acc + jnp.dot(p.astype(BF16), vb, preferred_element_type=F32)
            new += [m_new, l, acc]
        return tuple(new)

    one = (jnp.full((tq, 1), NEG_BIG, F32), jnp.zeros((tq, 1), F32), jnp.zeros((tq, dv_w), F32))
    _, l0, a0, _, l1, a1 = lax.fori_loop(0, n_ch, att_body, one + one)
    o = a0 / l0 - lam * (a1 / l1)
    o = o * lax.rsqrt(jnp.mean(o * o, axis=-1, keepdims=True) + EPS) * g_ref[...]
    o_ref[...] = (o * (1.0 - lam_init)).astype(o_ref.dtype)


def _diff_attention(qs_all, dk_bf, dv_bf, lams, g_subln, *, batch, q_len, k_len, n_valid_keys, p_len, tq,
                    n_heads, lam_init):
    hw = 2 * HEAD_DIM
    width = n_heads * hw
    nq = q_len // tq
    n_chunks = k_len // KEY_CHUNK
    assert qs_all.shape[1] == 3 * width
    dq_col0 = 2 * n_heads
    vec = pl.BlockSpec((1, HEAD_DIM), lambda b, h, i: (0, 0))
    return pl.pallas_call(
        functools.partial(_diff_kernel, tq=tq, p_len=p_len, n_valid_keys=n_valid_keys, lam_init=lam_init),
        grid=(batch, n_heads, nq),
        in_specs=[pl.BlockSpec((tq, hw), lambda b, h, i: (b * nq + i, dq_col0 + h)),
                  pl.BlockSpec((k_len, hw), lambda b, h, i: (b, h)),
                  pl.BlockSpec((k_len, hw), lambda b, h, i: (b, h)),
                  vec, vec, vec, vec,
                  pl.BlockSpec((1, hw), lambda b, h, i: (0, 0))],
        out_specs=pl.BlockSpec((tq, hw), lambda b, h, i: (b * nq + i, h)),
        out_shape=jax.ShapeDtypeStruct((batch * q_len, width), BF16),
        scratch_shapes=[pltpu.VMEM((n_chunks, tq, KEY_CHUNK), F32)],
        compiler_params=_params(("arbitrary", "arbitrary", "arbitrary")),
        name="diff_attention",
    )(qs_all, dk_bf, dv_bf, *[v.reshape(1, HEAD_DIM) for v in lams], g_subln.reshape(1, hw))


def _fill_invisible(x, fill, q_start, n_q, key_start, n_valid_keys, all_keys_valid):
    n_k, n_lanes = x.shape
    kpos = key_start + lax.broadcasted_iota(jnp.int32, (n_k, n_lanes), 0)
    qpos = q_start + jnp.minimum(lax.broadcasted_iota(jnp.int32, (1, n_lanes), 1), n_q - 1)
    x = jnp.where((kpos >> 6) <= (qpos >> 6), x, fill)
    return x if all_keys_valid else jnp.where(kpos < n_valid_keys, x, fill)


def _num_full_chunks(q_start, n_valid_keys):
    return jnp.minimum(((q_start >> 6) + 1) * CHUNK, n_valid_keys) // KEY_CHUNK


def _t_bf16(x, n_lanes):
    x = x.astype(F32)
    if x.shape[0] < n_lanes:
        x = jnp.concatenate([x, jnp.zeros((n_lanes - x.shape[0], x.shape[1]), F32)], axis=0)
    return x.T.astype(BF16)


def _col_partial(x, op):
    parts = [x[r * 8:(r + 1) * 8] for r in range(x.shape[0] // 8)]
    while len(parts) > 1:
        parts = [op(parts[i], parts[i + 1]) for i in range(0, len(parts) - 1, 2)] + parts[len(parts) & ~1:]
    return parts[0]


def _col_reduce(x, op):
    return (jnp.max if op is jnp.maximum else jnp.sum)(_col_partial(x, op), axis=0, keepdims=True)


def _row_to_cols(row):
    return jnp.broadcast_to(row, (LANES, row.shape[1])).T


def _dsa_t_kernel(q_ref, qi_ref, wi_ref, k_ref, vaug_ref, ki_ref, o_ref,
                  qit_scr, qt_scr, key_scr, acc_scr, *,
                  tq, tl, n_sel, p_len, n_valid_keys, all_keys_valid, n_idx_heads, n_kv_heads, rep):
    n_chunks_total = key_scr.shape[0]
    q_start = p_len + pl.program_id(1) * tq
    n_ch = _num_key_chunks(q_start, tq, n_valid_keys, n_chunks_total)
    heads_per_dot = 4

    for h in range(n_idx_heads):
        qit_scr[:, h * tl:(h + 1) * tl] = _t_bf16(qi_ref[:, h * LANES:(h + 1) * LANES], tl)
    for g in range(n_kv_heads):
        for r in range(rep):
            hh = g * rep + r
            qt_scr[g, :, r * tl:(r + 1) * tl] = _t_bf16(q_ref[:, hh * LANES:(hh + 1) * LANES], tl)
    wi = wi_ref[:, LANES:2 * LANES] * (n_idx_heads ** -0.5)
    if tq < tl:
        wi = jnp.concatenate([wi, jnp.zeros((tl - tq, LANES), F32)], axis=0)
    wi_t = wi.T

    def idx_body(j, carry):
        k0 = pl.multiple_of(j * KEY_CHUNK, KEY_CHUNK)
        kib = ki_ref[pl.ds(k0, KEY_CHUNK), :]
        score = jnp.zeros((KEY_CHUNK, tl), F32)
        for h0 in range(0, n_idx_heads, heads_per_dot):
            lg = jnp.dot(kib, qit_scr[:, h0 * tl:(h0 + heads_per_dot) * tl], preferred_element_type=F32)
            for h in range(h0, h0 + heads_per_dot):
                score = score + jnp.maximum(lg[:, (h - h0) * tl:(h - h0 + 1) * tl], 0.0) * wi_t[h:h + 1, :]
        score = _fill_invisible(score, -jnp.inf, q_start, tq, k0, n_valid_keys, all_keys_valid)
        key_scr[j] = _ordered_key(score)
        return carry

    lax.fori_loop(0, n_ch, idx_body, 0)

    def bit_body(b, t):
        cand_u = t | lax.shift_left(jnp.int32(1), 31 - b)
        cand_s = cand_u ^ jnp.int32(INT_MIN)

        def cnt_body(j, cnt):
            ones = jnp.where(key_scr[j] >= cand_s, 1, 0)
            return cnt + _col_partial(ones, jnp.add)

        cnt = lax.fori_loop(0, n_ch, cnt_body, jnp.zeros((8, tl), jnp.int32))
        total = jnp.sum(cnt, axis=0, keepdims=True)
        return jnp.where(total >= n_sel, cand_u, t)

    thr_u = lax.fori_loop(0, 32, bit_body, jnp.zeros((1, tl), jnp.int32))
    thr = jnp.maximum(thr_u ^ jnp.int32(INT_MIN), KEY_NEG_INF + 1)

    acc_scr[...] = jnp.zeros_like(acc_scr)

    def att_body(j, ms):
        k0 = pl.multiple_of(j * KEY_CHUNK, KEY_CHUNK)
        bias = jnp.where(key_scr[j] >= thr, 0.0, NEG_BIG)
        bias = jnp.concatenate([bias] * rep, axis=1)
        new = []
        for g in range(n_kv_heads):
            kb = k_ref[pl.ds(k0, KEY_CHUNK), g * LANES:(g + 1) * LANES]
            s = jnp.dot(kb, qt_scr[g], preferred_element_type=F32) + bias
            m_new = jnp.maximum(ms[g], _col_reduce(s, jnp.maximum))
            p = jnp.exp(s - m_new).astype(BF16)
            va = vaug_ref[pl.ds(k0, KEY_CHUNK), g * 2 * LANES:(g + 1) * 2 * LANES]
            alpha = _row_to_cols(jnp.exp(ms[g] - m_new))
            acc_scr[g] = (acc_scr[g] * jnp.concatenate([alpha, alpha], axis=1)
                          + lax.dot_general(p, va, (((0,), (0,)), ((), ())), preferred_element_type=F32))
            new.append(m_new)
        return tuple(new)

    lax.fori_loop(0, n_ch, att_body, tuple(jnp.full((1, rep * tl), NEG_BIG, F32) for _ in range(n_kv_heads)))

    for g in range(n_kv_heads):
        acc = acc_scr[g]
        o = acc[:, :LANES] / acc[:, LANES:]
        for r in range(rep):
            hh = g * rep + r
            o_ref[:, hh * LANES:(hh + 1) * LANES] = o[r * tl:r * tl + tq].astype(o_ref.dtype)


def _dsa_attention_t(qs_all, kiw_f32, k_bf, vaug_bf, ki_bf, *, batch, q_len, k_len, n_valid_keys, p_len, tq,
                     n_heads, n_kv_heads, n_idx_heads, n_sel):
    width = n_heads * HEAD_DIM
    nq = q_len // tq
    tl = max(tq, LANES)
    n_chunks = k_len // KEY_CHUNK
    rep = n_heads // n_kv_heads
    kv_w = n_kv_heads * HEAD_DIM
    return pl.pallas_call(
        functools.partial(_dsa_t_kernel, tq=tq, tl=tl, n_sel=n_sel, p_len=p_len, n_valid_keys=n_valid_keys,
                          all_keys_valid=n_valid_keys == k_len,
                          n_idx_heads=n_idx_heads, n_kv_heads=n_kv_heads, rep=rep),
        grid=(batch, nq),
        in_specs=[pl.BlockSpec((tq, width), lambda b, i: (b * nq + i, 0)),
                  pl.BlockSpec((tq, width), lambda b, i: (b * nq + i, 1)),
                  pl.BlockSpec((tq, 2 * LANES), lambda b, i: (b * nq + i, 0)),
                  pl.BlockSpec((k_len, kv_w), lambda b, i: (b, 0)),
                  pl.BlockSpec((k_len, 2 * kv_w), lambda b, i: (b, 0)),
                  pl.BlockSpec((k_len, LANES), lambda b, i: (b, 0))],
        out_specs=pl.BlockSpec((tq, width), lambda b, i: (b * nq + i, 0)),
        out_shape=jax.ShapeDtypeStruct((batch * q_len, width), BF16),
        scratch_shapes=[pltpu.VMEM((HEAD_DIM, n_idx_heads * tl), BF16),
                        pltpu.VMEM((n_kv_heads, HEAD_DIM, rep * tl), BF16),
                        pltpu.VMEM((n_chunks, KEY_CHUNK, tl), jnp.int32),
                        pltpu.VMEM((n_kv_heads, rep * tl, 2 * LANES), F32)],
        compiler_params=_params(("arbitrary", "arbitrary")),
        name="dsa_attention",
    )(qs_all, qs_all, kiw_f32, k_bf, vaug_bf, ki_bf)


def _diff_t_kernel(dq_ref, dk_ref, dv_ref, lq1_ref, lk1_ref, lq2_ref, lk2_ref, g_ref, o_ref,
                   acc_scr, *, tq, tl, hp, n_chunks_total, p_len, n_valid_keys, all_keys_valid, lam_init):
    q_start = p_len + pl.program_id(2) * tq
    n_ch = _num_key_chunks(q_start, tq, n_valid_keys, n_chunks_total)
    hw = 2 * HEAD_DIM
    n_maps = 2 * hp

    lam = (jnp.exp(jnp.sum(lq1_ref[...] * lk1_ref[...], axis=1, keepdims=True))
           - jnp.exp(jnp.sum(lq2_ref[...] * lk2_ref[...], axis=1, keepdims=True)) + lam_init)
    qt = [_t_bf16(dq_ref[:, c * LANES:(c + 1) * LANES], tl) for c in range(n_maps)]

    acc_scr[...] = jnp.zeros_like(acc_scr)

    def att_body(j, carry, masked):
        k0 = pl.multiple_of(j * KEY_CHUNK, KEY_CHUNK)
        new = []
        for c in range(n_maps):
            m, l = carry[2 * c:2 * c + 2]
            kb = dk_ref[pl.ds(k0, KEY_CHUNK), c * LANES:(c + 1) * LANES]
            vb = dv_ref[pl.ds(k0, KEY_CHUNK), (c // 2) * hw:(c // 2 + 1) * hw]
            s = jnp.dot(kb, qt[c], preferred_element_type=F32)
            if masked:
                s = _fill_invisible(s, NEG_BIG, q_start, tq, k0, n_valid_keys, all_keys_valid)
            m_new = jnp.maximum(m, _col_reduce(s, jnp.maximum))
            p = jnp.exp(s - m_new)
            alpha = jnp.exp(m - m_new)
            alpha_c = _row_to_cols(alpha)
            acc_scr[c] = (acc_scr[c] * jnp.concatenate([alpha_c, alpha_c], axis=1)
                          + lax.dot_general(p.astype(BF16), vb, (((0,), (0,)), ((), ())),
                                            preferred_element_type=F32))
            new += [m_new, alpha * l + _col_reduce(p, jnp.add)]
        return tuple(new)

    one = (jnp.full((1, tl), NEG_BIG, F32), jnp.zeros((1, tl), F32))
    n_full = jnp.minimum(_num_full_chunks(q_start, n_valid_keys), n_ch)
    carry = lax.fori_loop(0, n_full, functools.partial(att_body, masked=False), one * n_maps)
    carry = lax.fori_loop(n_full, n_ch, functools.partial(att_body, masked=True), carry)

    def normalised(c):
        l_cols = _row_to_cols(carry[2 * c + 1])
        return acc_scr[c] / jnp.concatenate([l_cols, l_cols], axis=1)

    for h in range(hp):
        o = (normalised(2 * h) - lam * normalised(2 * h + 1))[:tq]
        o = o * lax.rsqrt(jnp.mean(o * o, axis=-1, keepdims=True) + EPS) * g_ref[...]
        o_ref[:, h * hw:(h + 1) * hw] = (o * (1.0 - lam_init)).astype(o_ref.dtype)


def _diff_attention_t(qs_all, dk_bf, dv_bf, lams, g_subln, *, batch, q_len, k_len, n_valid_keys, p_len, tq,
                      n_heads, lam_init):
    hw = 2 * HEAD_DIM
    hp = 2 if n_heads % 2 == 0 else 1
    width = n_heads * hw
    nq = q_len // tq
    tl = max(tq, LANES)
    n_chunks = k_len // KEY_CHUNK
    assert qs_all.shape[1] == 3 * width
    dq_col0 = 2 * n_heads // hp
    vec = pl.BlockSpec((1, HEAD_DIM), lambda b, h, i: (0, 0))
    return pl.pallas_call(
        functools.partial(_diff_t_kernel, tq=tq, tl=tl, hp=hp, n_chunks_total=n_chunks, p_len=p_len,
                          n_valid_keys=n_valid_keys, all_keys_valid=n_valid_keys == k_len, lam_init=lam_init),
        grid=(batch, n_heads // hp, nq),
        in_specs=[pl.BlockSpec((tq, hp * hw), lambda b, h, i: (b * nq + i, dq_col0 + h)),
                  pl.BlockSpec((k_len, hp * hw), lambda b, h, i: (b, h)),
                  pl.BlockSpec((k_len, hp * hw), lambda b, h, i: (b, h)),
                  vec, vec, vec, vec,
                  pl.BlockSpec((1, hw), lambda b, h, i: (0, 0))],
        out_specs=pl.BlockSpec((tq, hp * hw), lambda b, h, i: (b * nq + i, h)),
        out_shape=jax.ShapeDtypeStruct((batch * q_len, width), BF16),
        scratch_shapes=[pltpu.VMEM((2 * hp, tl, hw), F32)],
        compiler_params=_params(("arbitrary", "arbitrary", "arbitrary")),
        name="diff_attention",
    )(qs_all, dk_bf, dv_bf, *[v.reshape(1, HEAD_DIM) for v in lams], g_subln.reshape(1, hw))


def _outproj_kernel(a_ref, d_ref, wa_ref, wd_ref, x_ref, ga_ref, o_ref, *, per_token):
    mix = (jnp.dot(a_ref[...], wa_ref[...], preferred_element_type=F32)
           + jnp.dot(d_ref[...], wd_ref[...], preferred_element_type=F32))
    ga = ga_ref[...] if per_token else ga_ref[0]
    o_ref[...] = x_ref[...] + ga * mix


def _outproj(a_out, d_out, w_out_bf, x2, mod, *, per_token, rows_per_batch, tm, tn):
    t, d = x2.shape
    wa = a_out.shape[1]
    wd = d_out.shape[1]
    assert wa == wd
    cb = 2 * (d // tn)
    return pl.pallas_call(
        functools.partial(_outproj_kernel, per_token=per_token),
        grid=(t // tm, d // tn),
        in_specs=[pl.BlockSpec((tm, wa), lambda m, n: (m, 0)),
                  pl.BlockSpec((tm, wd), lambda m, n: (m, 0)),
                  pl.BlockSpec((wa, tn), lambda m, n: (0, n)),
                  pl.BlockSpec((wd, tn), lambda m, n: (1, n)),
                  pl.BlockSpec((tm, tn), lambda m, n: (m, n)),
                  _mod_spec(mod, per_token, tm, tn, cb, rows_per_batch)],
        out_specs=pl.BlockSpec((tm, tn), lambda m, n: (m, n)),
        out_shape=jax.ShapeDtypeStruct((t, d), F32),
        compiler_params=_params(("arbitrary", "arbitrary")),
        name="outproj",
    )(a_out, d_out, w_out_bf, w_out_bf, x2, mod)


def _top_rows(x, k, with_rank=False):
    tops = []
    rank = jnp.full(x.shape, float(k), F32) if with_rank else None
    for i in range(k):
        mx = jnp.max(x, axis=0, keepdims=True)
        tops.append(mx)
        hit = x == mx
        if with_rank:
            rank = jnp.where(hit, float(i), rank)
        x = jnp.where(hit, -jnp.inf, x)
    return (tops, rank) if with_rank else tops


def _peer_route_kernel(h_ref, wq_ref, keys_ref, cnt_ref, e1_ref, rank_ref, e2_ref, s1_scr, top_scr):
    c = pl.program_id(1) % 2
    q_t = jnp.dot(wq_ref[...], h_ref[...], preferred_element_type=F32)
    s_t = jnp.dot(keys_ref[0].astype(BF16), q_t.astype(BF16), preferred_element_type=F32)

    @pl.when(c == 0)
    def _():
        s1_scr[...] = s_t
        top_scr[...] = jnp.concatenate(_top_rows(s_t, PEER_TOPK), axis=0)

    @pl.when(c == 1)
    def _():
        tops2, rank2 = _top_rows(s_t, PEER_TOPK, with_rank=True)
        tops = jnp.concatenate(tops2, axis=0)
        top1 = top_scr[...]
        cand = jnp.concatenate([top1[0:1, :] + tops]
                               + [top1[i:i + 1, :] + tops[:PEER_TOPK // 2] for i in range(1, PEER_TOPK)], axis=0)
        best = _top_rows(cand, PEER_TOPK)
        m = best[0]
        z = jnp.zeros_like(m)
        for bk in best:
            z = z + jnp.exp(bk - m)
        thr = best[PEER_TOPK - 1]
        s1 = s1_scr[...]
        cnt = jnp.zeros_like(s1)
        for j in range(PEER_TOPK):
            cnt = cnt + jnp.where(s1 + tops2[j] >= thr, 1.0, 0.0)
        cnt_ref[0] = cnt
        e1_ref[0] = jnp.exp(s1 - top1[0:1, :]) / z
        rank_ref[0] = rank2.astype(rank_ref.dtype)
        e2_ref[0] = jnp.exp(s_t - tops2[0]).astype(e2_ref.dtype)


def _peer_route(h_t, wq_t_bf, sub_keys, *, tm):
    d, t = h_t.shape
    heads, _, n_keys, half = sub_keys.shape
    keys2 = sub_keys.reshape(heads * 2, n_keys, half)
    tab_spec = pl.BlockSpec((1, n_keys, tm), lambda m, hc: (hc // 2, 0, m))
    f32_tab = jax.ShapeDtypeStruct((heads, n_keys, t), F32)
    bf_tab = jax.ShapeDtypeStruct((heads, n_keys, t), BF16)
    return pl.pallas_call(
        _peer_route_kernel,
        grid=(t // tm, heads * 2),
        in_specs=[pl.BlockSpec((d, tm), lambda m, hc: (0, m)),
                  pl.BlockSpec((half, d), lambda m, hc: (hc, 0)),
                  pl.BlockSpec((1, n_keys, half), lambda m, hc: (hc, 0, 0))],
        out_specs=[tab_spec, tab_spec, tab_spec, tab_spec],
        out_shape=[f32_tab, f32_tab, bf_tab, bf_tab],
        scratch_shapes=[pltpu.VMEM((n_keys, tm), F32), pltpu.VMEM((PEER_TOPK, tm), F32)],
        compiler_params=_params(("arbitrary", "arbitrary")),
        name="peer_route",
    )(h_t, wq_t_bf, keys2)


def _gelu_tanh(x):
    return 0.5 * x * (1.0 + jnp.tanh(0.7978845608028654 * (x + 0.044715 * x * x * x)))


def _rows_bf16(row, n_rows):
    tile = jnp.broadcast_to(row, (16, row.shape[1])).astype(BF16)
    return jnp.concatenate([tile] * (n_rows // 16), axis=0)


def _peer_ffn_kernel(h_ref, u_ref, v_ref, cnt_ref, e1_ref, rank_ref, e2_ref, o_ref, w_scr, *, n_keys, n_blocks):
    e = pl.program_id(1)
    te, tm = w_scr.shape[1:]
    heads = cnt_ref.shape[0]
    a0 = jnp.minimum(e, n_blocks - 1) * (te // n_keys)

    @pl.when(e == 0)
    def _():
        o_ref[...] = jnp.zeros_like(o_ref)
        w_scr[1] = jnp.zeros((te, tm), BF16)

    def step(rd, wr):
        o_ref[...] += lax.dot_general(w_scr[rd], v_ref[...], (((0,), (0,)), ((), ())),
                                      preferred_element_type=F32)
        act = jnp.dot(u_ref[...], h_ref[...], preferred_element_type=F32)
        strip = min(tm, LANES)
        for ai in range(te // n_keys):
            rows = slice(ai * n_keys, (ai + 1) * n_keys)
            cnt_a = [cnt_ref[h, pl.ds(a0 + ai, 1), :] for h in range(heads)]
            e1_a = [e1_ref[h, pl.ds(a0 + ai, 1), :] for h in range(heads)]
            for c in range(tm // strip):
                cols = slice(c * strip, (c + 1) * strip)
                gate = jnp.zeros((n_keys, strip), BF16)
                for h in range(heads):
                    cnt_b = _rows_bf16(cnt_a[h][:, cols], n_keys)
                    e1_b = _rows_bf16(e1_a[h][:, cols], n_keys)
                    keep = rank_ref[h, :, cols] < cnt_b
                    gate = gate + jnp.where(keep, e2_ref[h, :, cols] * e1_b, jnp.zeros_like(e1_b))
                w_scr[wr, rows, cols] = gate * _gelu_tanh(act[rows, cols]).astype(BF16)

    @pl.when(e % 2 == 0)
    def _():
        step(1, 0)

    @pl.when(e % 2 == 1)
    def _():
        step(0, 1)


def _peer_ffn(h_t, u_bf, v_bf, cnt, e1, rank, e2, *, tm, te):
    d, t = h_t.shape
    n_exp = u_bf.shape[0]
    n_blocks = n_exp // te
    heads, n_keys, _ = cnt.shape
    once = pl.Buffered(1)
    tab_spec = pl.BlockSpec((heads, n_keys, tm), lambda m, e: (0, 0, m), pipeline_mode=once)
    return pl.pallas_call(
        functools.partial(_peer_ffn_kernel, n_keys=n_keys, n_blocks=n_blocks),
        grid=(t // tm, n_blocks + 1),
        in_specs=[pl.BlockSpec((d, tm), lambda m, e: (0, m), pipeline_mode=once),
                  pl.BlockSpec((te, d), lambda m, e: (jnp.minimum(e, n_blocks - 1), 0)),
                  pl.BlockSpec((te, d), lambda m, e: (jnp.maximum(e - 1, 0), 0)),
                  tab_spec, tab_spec, tab_spec, tab_spec],
        out_specs=pl.BlockSpec((tm, d), lambda m, e: (m, 0)),
        out_shape=jax.ShapeDtypeStruct((t, d), F32),
        scratch_shapes=[pltpu.VMEM((2, te, tm), BF16)],
        compiler_params=_params(("arbitrary", "arbitrary")),
        name="peer_ffn",
    )(h_t, u_bf, v_bf, cnt, e1, rank, e2)


def _final_kernel(x_ref, p_ref, ga_ref, g_ref, o_ref, *, per_token, normalize):
    ga = ga_ref[...] if per_token else ga_ref[0]
    x = x_ref[...] + ga * p_ref[...]
    if normalize:
        x = x * lax.rsqrt(jnp.mean(x * x, axis=-1, keepdims=True) + EPS) * g_ref[...]
    o_ref[...] = x


def _final(x1, peer, mod, g_final, *, per_token, rows_per_batch, tm, normalize):
    t, d = x1.shape
    row = pl.BlockSpec((tm, d), lambda m: (m, 0))
    return pl.pallas_call(
        functools.partial(_final_kernel, per_token=per_token, normalize=normalize),
        grid=(t // tm,),
        in_specs=[row, row, _mod_spec(mod, per_token, tm, d, 5, rows_per_batch),
                  pl.BlockSpec((1, d), lambda m: (0, 0))],
        out_specs=row,
        out_shape=jax.ShapeDtypeStruct((t, d), F32),
        compiler_params=_params(("arbitrary",)),
        name="final",
    )(x1, peer, mod, g_final.reshape(1, d))


def _rope_tables(pos):
    half = HEAD_DIM // 2
    inv = ROPE_THETA ** (-jnp.arange(half, dtype=F32) / half)
    ang = pos.astype(F32)[:, None] * inv[None, :]
    cos, sin = jnp.cos(ang), jnp.sin(ang)
    return jnp.concatenate([cos, cos], axis=1), jnp.concatenate([-sin, sin], axis=1)


def _pick_tile(n, prefs):
    for p in prefs:
        if n % p == 0:
            return p
    return n


def _layer(x, mod_rows, past, layer, last_layer, w, dims):
    b, s, d = x.shape
    t = b * s
    n_heads, n_kv, n_idx, n_diff = dims
    p_len = 0 if past is None else past[0].shape[1]
    n_keys_valid = p_len + s
    n_sel = min(DSA_TOPK, n_keys_valid // 4)
    x2 = x.reshape(t, d)

    per_token = s % 256 != 0
    if per_token:
        mod = jnp.repeat(mod_rows, s, axis=0)
    else:
        mod = mod_rows.reshape(b, 1, 6 * d)
    tm_big = _pick_tile(t if per_token else s, (1024, 512, 256, 128))
    tm_mid = _pick_tile(t if per_token else s, (512, 256, 128))
    tm_small = _pick_tile(t if per_token else s, (256, 128))

    pos = p_len + jnp.arange(s)
    cos, sin = _rope_tables(pos)
    if per_token:
        cos, sin = jnp.tile(cos, (b, 1)), jnp.tile(sin, (b, 1))

    h = _norm_mod(x2, w["g_norm_mix"], mod, which=1, per_token=per_token, rows_per_batch=s, tm=tm_mid,
                  transpose=False)

    proj = functools.partial(_proj, h, cos=cos, sin=sin, tm=tm_big)
    wq = n_heads * HEAD_DIM
    (qs_all,) = proj(w["w_qs"], rope_cols=512, scale=HEAD_DIM ** -0.5, out_f32=False, out_bf16=True, tn=512,
                     name="proj_q")
    k_f, k_b = proj(w["w_k"], rope_cols=512, out_f32=True, out_bf16=True, tn=512, name="proj_k")
    v_f, v_b = proj(w["w_v"], rope_cols=0, out_f32=True, out_bf16=True, tn=512, name="proj_v")
    dk_f, dk_b = proj(w["w_dk"], rope_cols=512, out_f32=True, out_bf16=True, tn=512, name="proj_dk")
    dv_f, dv_b = proj(w["w_dv"], rope_cols=0, out_f32=True, out_bf16=True, tn=512, name="proj_dv")
    kiw_f, kiw_b = proj(w["w_kiw"], rope_cols=LANES, out_f32=True, out_bf16=True, tn=2 * LANES, name="proj_kiw")

    ki_f = kiw_f[:, :LANES]
    new_rows = (k_f.reshape(b, s, n_kv, HEAD_DIM), v_f.reshape(b, s, n_kv, HEAD_DIM), ki_f.reshape(b, s, LANES),
                dk_f.reshape(b, s, n_diff, 2, HEAD_DIM), dv_f.reshape(b, s, n_diff, 2 * HEAD_DIM))

    if past is None:
        k_len = s
        k_all, v_all, ki_all, dk_all, dv_all = k_b, v_b, kiw_b, dk_b, dv_b
    else:
        k_len = -(-n_keys_valid // KEY_CHUNK) * KEY_CHUNK

        def join(cache, new, width):
            new = new.reshape(b, s, -1)[:, :, :width]
            both = jnp.concatenate([cache.reshape(b, p_len, width).astype(BF16), new], axis=1)
            both = jnp.pad(both, ((0, 0), (0, k_len - n_keys_valid), (0, 0)))
            return both.reshape(b * k_len, width)

        k_all = join(past[0], k_b, n_kv * HEAD_DIM)
        v_all = join(past[1], v_b, n_kv * HEAD_DIM)
        ki_all = join(past[2], kiw_b, LANES)
        dk_all = join(past[3], dk_b, n_diff * 2 * HEAD_DIM)
        dv_all = join(past[4], dv_b, n_diff * 2 * HEAD_DIM)

    v3 = v_all.reshape(-1, n_kv, HEAD_DIM)
    vaug_all = jnp.concatenate([v3, jnp.ones_like(v3)], axis=2).reshape(-1, n_kv * 2 * HEAD_DIM)
    tq_a = _pick_tile(s, (128, 64, 32))
    a_out = _dsa_attention_t(qs_all, kiw_f, k_all, vaug_all, ki_all, batch=b, q_len=s, k_len=k_len,
                             n_valid_keys=n_keys_valid, p_len=p_len, tq=tq_a, n_heads=n_heads, n_kv_heads=n_kv,
                             n_idx_heads=n_idx, n_sel=n_sel)
    tq_d = _pick_tile(s, (256, 128, 64, 32))
    d_out = _diff_attention_t(qs_all, dk_all, dv_all, w["lams"], w["g_subln"], batch=b, q_len=s, k_len=k_len,
                              n_valid_keys=n_keys_valid, p_len=p_len, tq=tq_d, n_heads=n_diff,
                              lam_init=_lambda_init(layer))

    x1 = _outproj(a_out, d_out, w["w_out"], x2, mod, per_token=per_token, rows_per_batch=s, tm=tm_big, tn=512)

    h2_t = _norm_mod(x1, w["g_norm_ffn"], mod, which=2, per_token=per_token, rows_per_batch=s, tm=tm_mid,
                     transpose=True)
    cnt, e1, rank, e2 = _peer_route(h2_t, w["peer_wq_t"], w["peer_keys"], tm=tm_mid)
    peer = _peer_ffn(h2_t, w["peer_u"], w["peer_v"], cnt, e1, rank, e2, tm=tm_mid, te=512)
    x_out = _final(x1, peer, mod, w["g_final"], per_token=per_token, rows_per_batch=s, tm=tm_small,
                   normalize=last_layer)
    return x_out.reshape(b, s, d), new_rows


def kernel(x_prompt, x_sample, cache_dsa_k, cache_dsa_v, cache_idx_k, cache_diff_k, cache_diff_v, c_prompt, c_sample, w_ada, b_ada, g_norm_mix, g_norm_ffn, w_in, diff_lambda_q1, diff_lambda_k1, diff_lambda_q2, diff_lambda_k2, g_diff_subln, w_out, peer_w_query, peer_sub_keys, peer_u, peer_v, g_final):
    depth = w_in.shape[0]
    d = x_prompt.shape[-1]
    bp, bs = x_prompt.shape[0], x_sample.shape[0]
    n_kv = cache_dsa_k.shape[3]
    n_diff = cache_diff_k.shape[3]
    n_heads, n_idx_heads = DSA_HEADS, IDX_HEADS
    qw, kvw, dw = n_heads * HEAD_DIM, n_kv * HEAD_DIM, n_diff * 2 * HEAD_DIM
    assert w_in.shape[2] == qw + 2 * kvw + n_idx_heads * LANES + LANES + n_idx_heads + 3 * dw
    dims = (n_heads, n_kv, n_idx_heads, n_diff)

    hp, hs = x_prompt, x_sample
    rows_p, rows_s = [], []
    n_c = bp + bs
    c_pad = jnp.pad(jnp.concatenate([c_prompt, c_sample], axis=0), ((0, (-n_c) % 16), (0, 0)))
    for l in range(depth):
        mod_all = _adaln(c_pad, w_ada[l], b_ada[l])
        win = w_in[l]
        o = 0
        seg = {}
        for name, width in (("q", qw), ("k", kvw), ("v", kvw), ("qi", n_idx_heads * LANES), ("ki", LANES),
                            ("wi", n_idx_heads), ("dq", dw), ("dk", dw), ("dv", dw)):
            seg[name] = win[:, o:o + width]
            o += width
        weights = {
            "w_qs": jnp.concatenate([seg["q"], seg["qi"], seg["dq"]], axis=1).astype(BF16),
            "w_k": seg["k"].astype(BF16), "w_v": seg["v"].astype(BF16),
            "w_dk": seg["dk"].astype(BF16), "w_dv": seg["dv"].astype(BF16),
            "w_kiw": jnp.concatenate([seg["ki"], seg["wi"], jnp.zeros((d, LANES - n_idx_heads), F32)],
                                     axis=1).astype(BF16),
            "w_out": w_out[l].astype(BF16),
            "peer_wq_t": peer_w_query[l].T.astype(BF16),
            "peer_keys": peer_sub_keys[l],
            "peer_u": peer_u[l].astype(BF16), "peer_v": peer_v[l].astype(BF16),
            "g_norm_mix": g_norm_mix[l], "g_norm_ffn": g_norm_ffn[l], "g_subln": g_diff_subln[l],
            "lams": (diff_lambda_q1[l], diff_lambda_k1[l], diff_lambda_q2[l], diff_lambda_k2[l]),
            "g_final": g_final,
        }
        last = l == depth - 1
        hp, rp = _layer(hp, mod_all[:bp], None, l, last, weights, dims)
        past = (cache_dsa_k[l], cache_dsa_v[l], cache_idx_k[l], cache_diff_k[l], cache_diff_v[l])
        hs, rs = _layer(hs, mod_all[bp:bp + bs], past, l, last, weights, dims)
        rows_p.append(rp)
        rows_s.append(rs)
    stack = lambda rows, i: jnp.stack([r[i] for r in rows])
    return (hp, hs) + tuple(stack(rows_p, i) for i in range(5)) + tuple(stack(rows_s, i) for i in range(5))
```

```python
import functools
import math

import jax
import jax.numpy as jnp
from jax import lax
from jax.experimental import pallas as pl
from jax.experimental.pallas import tpu as pltpu

CHUNK = 64
HEAD_DIM = 128
ROPE_THETA = 10000.0
EPS = 1e-6
DSA_HEADS = 16
IDX_HEADS = 16
DSA_TOPK = 256
PEER_TOPK = 16
LANES = 128
KEY_CHUNK = 512
NEG_BIG = -1e30
INT_MIN = -(2 ** 31)
KEY_NEG_INF = INT_MIN + 0x7FFFFF
VMEM_LIMIT = 56 * 1024 * 1024

BF16 = jnp.bfloat16
F32 = jnp.float32


def _params(sem, vmem=VMEM_LIMIT):
    return pltpu.CompilerParams(dimension_semantics=sem, vmem_limit_bytes=vmem)


def _lambda_init(layer):
    return 0.8 - 0.6 * math.exp(-0.3 * layer)


def _nt_dot(a, b):
    return lax.dot_general(a, b, (((1,), (1,)), ((), ())), preferred_element_type=F32)


def _adaln_kernel(c_ref, w_ref, b_ref, o_ref):
    c = c_ref[...]
    a = (c * jax.nn.sigmoid(c)).astype(BF16)
    o_ref[...] = jnp.dot(a, w_ref[...].astype(BF16), preferred_element_type=F32) + b_ref[...]


def _adaln(c_pad, w_ada, b_ada):
    bp, d = c_pad.shape
    n = w_ada.shape[1]
    tn = 512
    return pl.pallas_call(
        _adaln_kernel,
        grid=(n // tn,),
        in_specs=[pl.BlockSpec((bp, d), lambda j: (0, 0)),
                  pl.BlockSpec((d, tn), lambda j: (0, j)),
                  pl.BlockSpec((1, tn), lambda j: (0, j))],
        out_specs=pl.BlockSpec((bp, tn), lambda j: (0, j)),
        out_shape=jax.ShapeDtypeStruct((bp, n), F32),
        compiler_params=_params(("arbitrary",)),
        name="adaln",
    )(c_pad, w_ada, b_ada.reshape(1, n))


def _norm_mod_kernel(x_ref, g_ref, sc_ref, sh_ref, o_ref, *, per_token, transpose):
    x = x_ref[...]
    y = x * lax.rsqrt(jnp.mean(x * x, axis=-1, keepdims=True) + EPS) * g_ref[...]
    sc = sc_ref[...] if per_token else sc_ref[0]
    sh = sh_ref[...] if per_token else sh_ref[0]
    h = y * (1.0 + sc) + sh
    if transpose:
        o_ref[...] = h.T.astype(o_ref.dtype)
    else:
        o_ref[...] = h.astype(o_ref.dtype)


def _mod_spec(mod, per_token, tm, tn, col_block, rows_per_batch):
    if per_token:
        return pl.BlockSpec((tm, tn), lambda m, n=0, cb=col_block: (m, cb + n))
    tiles_per_batch = rows_per_batch // tm
    return pl.BlockSpec((1, 1, tn), lambda m, n=0, cb=col_block: (m // tiles_per_batch, 0, cb + n))


def _norm_mod(x2, g, mod, *, which, per_token, rows_per_batch, tm, transpose):
    t, d = x2.shape
    sh_blk, sc_blk = (0, 1) if which == 1 else (3, 4)
    out_shape = (d, t) if transpose else (t, d)
    out_spec = pl.BlockSpec((d, tm), lambda m: (0, m)) if transpose else pl.BlockSpec((tm, d), lambda m: (m, 0))
    return pl.pallas_call(
        functools.partial(_norm_mod_kernel, per_token=per_token, transpose=transpose),
        grid=(t // tm,),
        in_specs=[pl.BlockSpec((tm, d), lambda m: (m, 0)),
                  pl.BlockSpec((1, d), lambda m: (0, 0)),
                  _mod_spec(mod, per_token, tm, d, sc_blk, rows_per_batch),
                  _mod_spec(mod, per_token, tm, d, sh_blk, rows_per_batch)],
        out_specs=out_spec,
        out_shape=jax.ShapeDtypeStruct(out_shape, BF16),
        compiler_params=_params(("arbitrary",)),
        name="norm_mod_t" if transpose else "norm_mod",
    )(x2, g.reshape(1, d), mod, mod)


def _rope_tile(acc, cos, sin, n_chunks):
    outs = []
    for j in range(n_chunks):
        xj = acc[:, j * LANES:(j + 1) * LANES]
        outs.append(xj * cos + pltpu.roll(xj, LANES // 2, axis=1) * sin)
    return outs[0] if n_chunks == 1 else jnp.concatenate(outs, axis=1)


def _proj_kernel(*refs, rope_cols, scale, out_f32, out_bf16):
    h_ref, w_ref, cos_ref, sin_ref = refs[:4]
    outs = refs[4:]
    acc = jnp.dot(h_ref[...], w_ref[...], preferred_element_type=F32)
    tn = acc.shape[1]
    if rope_cols:
        roped = _rope_tile(acc[:, :rope_cols], cos_ref[...], sin_ref[...], rope_cols // LANES)
        acc = roped if rope_cols == tn else jnp.concatenate([roped, acc[:, rope_cols:]], axis=1)
    k = 0
    if out_f32:
        outs[k][...] = acc
        k += 1
    if out_bf16:
        outs[k][...] = (acc * scale if scale != 1.0 else acc).astype(BF16)


def _proj(h, w, cos, sin, *, rope_cols, scale=1.0, out_f32, out_bf16, tm, tn, name):
    t, d = h.shape
    n = w.shape[1]
    assert rope_cols in (0, tn) or n == tn
    pos_tiles = cos.shape[0] // tm
    out_shape, out_specs = [], []
    for want, dt in ((out_f32, F32), (out_bf16, BF16)):
        if want:
            out_shape.append(jax.ShapeDtypeStruct((t, n), dt))
            out_specs.append(pl.BlockSpec((tm, tn), lambda m, j: (m, j)))
    return pl.pallas_call(
        functools.partial(_proj_kernel, rope_cols=rope_cols, scale=scale, out_f32=out_f32, out_bf16=out_bf16),
        grid=(t // tm, n // tn),
        in_specs=[pl.BlockSpec((tm, d), lambda m, j: (m, 0)),
                  pl.BlockSpec((d, tn), lambda m, j: (0, j)),
                  pl.BlockSpec((tm, LANES), lambda m, j: (m % pos_tiles, 0)),
                  pl.BlockSpec((tm, LANES), lambda m, j: (m % pos_tiles, 0))],
        out_specs=out_specs,
        out_shape=out_shape,
        compiler_params=_params(("arbitrary", "arbitrary")),
        name=name,
    )(h, w, cos, sin)


def _visible(q_start, n_q, key_start, n_k, n_valid_keys):
    qpos = q_start + lax.broadcasted_iota(jnp.int32, (n_q, 1), 0)
    kpos = key_start + lax.broadcasted_iota(jnp.int32, (1, n_k), 1)
    return ((kpos >> 6) <= (qpos >> 6)) & (kpos < n_valid_keys)


def _num_key_chunks(q_start, n_q, n_valid_keys, n_chunks_total):
    last_visible = (((q_start + n_q - 1) >> 6) + 1) * CHUNK
    last_visible = jnp.minimum(last_visible, n_valid_keys)
    return jnp.minimum((last_visible + KEY_CHUNK - 1) // KEY_CHUNK, n_chunks_total)


def _ordered_key(x):
    b = pltpu.bitcast(x, jnp.int32)
    return b ^ ((b >> 31) & jnp.int32(0x7FFFFFFF))


def _dsa_kernel(q_ref, qi_ref, wi_ref, k_ref, v_ref, ki_ref, o_ref, key_scr, bias_scr, *,
                tq, n_sel, p_len, n_valid_keys, n_idx_heads, n_kv_heads, rep):
    n_chunks_total = key_scr.shape[0]
    q_start = p_len + pl.program_id(1) * tq
    n_ch = _num_key_chunks(q_start, tq, n_valid_keys, n_chunks_total)

    def idx_body(j, carry):
        k0 = pl.multiple_of(j * KEY_CHUNK, KEY_CHUNK)
        kib = ki_ref[pl.ds(k0, KEY_CHUNK), :]
        score = jnp.zeros((tq, KEY_CHUNK), F32)
        wi = wi_ref[:, LANES:2 * LANES] * (n_idx_heads ** -0.5)
        for h in range(n_idx_heads):
            lg = _nt_dot(qi_ref[:, h * LANES:(h + 1) * LANES], kib)
            score = score + jnp.maximum(lg, 0.0) * wi[:, h:h + 1]
        vis = _visible(q_start, tq, k0, KEY_CHUNK, n_valid_keys)
        key_scr[j] = _ordered_key(jnp.where(vis, score, -jnp.inf))
        return carry

    lax.fori_loop(0, n_ch, idx_body, 0)

    def bit_body(b, t):
        cand_u = t | lax.shift_left(jnp.int32(1), 31 - b)
        cand_s = cand_u ^ jnp.int32(INT_MIN)

        def cnt_body(j, cnt):
            ones = jnp.where(key_scr[j] >= cand_s, 1, 0)
            for c in range(KEY_CHUNK // LANES):
                cnt = cnt + ones[:, c * LANES:(c + 1) * LANES]
            return cnt

        cnt = lax.fori_loop(0, n_ch, cnt_body, jnp.zeros((tq, LANES), jnp.int32))
        total = jnp.sum(cnt, axis=1, keepdims=True)
        return jnp.where(total >= n_sel, cand_u, t)

    thr_u = lax.fori_loop(0, 32, bit_body, jnp.zeros((tq, 1), jnp.int32))
    thr = thr_u ^ jnp.int32(INT_MIN)

    def bias_body(j, carry):
        k0 = j * KEY_CHUNK
        vis = _visible(q_start, tq, k0, KEY_CHUNK, n_valid_keys)
        sel = (key_scr[j] >= thr) & vis
        bias_scr[j] = jnp.where(sel, 0.0, NEG_BIG)
        return carry

    lax.fori_loop(0, n_ch, bias_body, 0)

    for g in range(n_kv_heads):
        qs = jnp.concatenate(
            [q_ref[:, (g * rep + r) * LANES:(g * rep + r + 1) * LANES] for r in range(rep)], axis=0)

        def att_body(j, carry, g=g, qs=qs):
            m, l, acc = carry
            k0 = pl.multiple_of(j * KEY_CHUNK, KEY_CHUNK)
            kb = k_ref[pl.ds(k0, KEY_CHUNK), g * LANES:(g + 1) * LANES]
            vb = v_ref[pl.ds(k0, KEY_CHUNK), g * LANES:(g + 1) * LANES]
            bias = bias_scr[j]
            s = _nt_dot(qs, kb) + jnp.concatenate([bias] * rep, axis=0)
            m_new = jnp.maximum(m, jnp.max(s, axis=1, keepdims=True))
            p = jnp.exp(s - m_new)
            alpha = jnp.exp(m - m_new)
            l = alpha * l + jnp.sum(p, axis=1, keepdims=True)
            acc = alpha * acc + jnp.dot(p.astype(BF16), vb, preferred_element_type=F32)
            return m_new, l, acc

        init = (jnp.full((rep * tq, 1), NEG_BIG, F32), jnp.zeros((rep * tq, 1), F32),
                jnp.zeros((rep * tq, LANES), F32))
        _, l, acc = lax.fori_loop(0, n_ch, att_body, init)
        o = acc / l
        for r in range(rep):
            o_ref[:, (g * rep + r) * LANES:(g * rep + r + 1) * LANES] = o[r * tq:(r + 1) * tq].astype(o_ref.dtype)


def _dsa_attention(qs_all, kiw_f32, k_bf, v_bf, ki_bf, *, batch, q_len, k_len, n_valid_keys, p_len, tq,
                   n_heads, n_kv_heads, n_idx_heads, n_sel):
    width = n_heads * HEAD_DIM
    nq = q_len // tq
    n_chunks = k_len // KEY_CHUNK
    rep = n_heads // n_kv_heads
    kv_w = n_kv_heads * HEAD_DIM
    return pl.pallas_call(
        functools.partial(_dsa_kernel, tq=tq, n_sel=n_sel, p_len=p_len, n_valid_keys=n_valid_keys,
                          n_idx_heads=n_idx_heads, n_kv_heads=n_kv_heads, rep=rep),
        grid=(batch, nq),
        in_specs=[pl.BlockSpec((tq, width), lambda b, i: (b * nq + i, 0)),
                  pl.BlockSpec((tq, width), lambda b, i: (b * nq + i, 1)),
                  pl.BlockSpec((tq, 2 * LANES), lambda b, i: (b * nq + i, 0)),
                  pl.BlockSpec((k_len, kv_w), lambda b, i: (b, 0)),
                  pl.BlockSpec((k_len, kv_w), lambda b, i: (b, 0)),
                  pl.BlockSpec((k_len, LANES), lambda b, i: (b, 0))],
        out_specs=pl.BlockSpec((tq, width), lambda b, i: (b * nq + i, 0)),
        out_shape=jax.ShapeDtypeStruct((batch * q_len, width), BF16),
        scratch_shapes=[pltpu.VMEM((n_chunks, tq, KEY_CHUNK), jnp.int32),
                        pltpu.VMEM((n_chunks, tq, KEY_CHUNK), F32)],
        compiler_params=_params(("arbitrary", "arbitrary")),
        name="dsa_attention",
    )(qs_all, qs_all, kiw_f32, k_bf, v_bf, ki_bf)


def _diff_kernel(dq_ref, dk_ref, dv_ref, lq1_ref, lk1_ref, lq2_ref, lk2_ref, g_ref, o_ref, bias_scr, *,
                 tq, p_len, n_valid_keys, lam_init):
    n_chunks_total = bias_scr.shape[0]
    q_start = p_len + pl.program_id(2) * tq
    n_ch = _num_key_chunks(q_start, tq, n_valid_keys, n_chunks_total)
    dv_w = 2 * HEAD_DIM

    lam = (jnp.exp(jnp.sum(lq1_ref[...] * lk1_ref[...], axis=1, keepdims=True))
           - jnp.exp(jnp.sum(lq2_ref[...] * lk2_ref[...], axis=1, keepdims=True)) + lam_init)

    def bias_body(j, carry):
        vis = _visible(q_start, tq, j * KEY_CHUNK, KEY_CHUNK, n_valid_keys)
        bias_scr[j] = jnp.where(vis, 0.0, NEG_BIG)
        return carry

    lax.fori_loop(0, n_ch, bias_body, 0)

    q0 = dq_ref[:, :LANES]
    q1 = dq_ref[:, LANES:]

    def att_body(j, carry):
        k0 = pl.multiple_of(j * KEY_CHUNK, KEY_CHUNK)
        vb = dv_ref[pl.ds(k0, KEY_CHUNK), :]
        bias = bias_scr[j]
        new = []
        for c, qc in ((0, q0), (1, q1)):
            m, l, acc = carry[3 * c:3 * c + 3]
            kb = dk_ref[pl.ds(k0, KEY_CHUNK), c * LANES:(c + 1) * LANES]
            s = _nt_dot(qc, kb) + bias
            m_new = jnp.maximum(m, jnp.max(s, axis=1, keepdims=True))
            p = jnp.exp(s - m_new)
            alpha = jnp.exp(m - m_new)
            l = alpha * l + jnp.sum(p, axis=1, keepdims=True)
            acc = alpha * acc + jnp.dot(p.astype(BF16), vb, preferred_element_type=F32)
            new += [m_new, l, acc]
        return tuple(new)

    one = (jnp.full((tq, 1), NEG_BIG, F32), jnp.zeros((tq, 1), F32), jnp.zeros((tq, dv_w), F32))
    _, l0, a0, _, l1, a1 = lax.fori_loop(0, n_ch, att_body, one + one)
    o = a0 / l0 - lam * (a1 / l1)
    o = o * lax.rsqrt(jnp.mean(o * o, axis=-1, keepdims=True) + EPS) * g_ref[...]
    o_ref[...] = (o * (1.0 - lam_init)).astype(o_ref.dtype)


def _diff_attention(qs_all, dk_bf, dv_bf, lams, g_subln, *, batch, q_len, k_len, n_valid_keys, p_len, tq,
                    n_heads, lam_init):
    hw = 2 * HEAD_DIM
    width = n_heads * hw
    nq = q_len // tq
    n_chunks = k_len // KEY_CHUNK
    assert qs_all.shape[1] == 3 * width
    dq_col0 = 2 * n_heads
    vec = pl.BlockSpec((1, HEAD_DIM), lambda b, h, i: (0, 0))
    return pl.pallas_call(
        functools.partial(_diff_kernel, tq=tq, p_len=p_len, n_valid_keys=n_valid_keys, lam_init=lam_init),
        grid=(batch, n_heads, nq),
        in_specs=[pl.BlockSpec((tq, hw), lambda b, h, i: (b * nq + i, dq_col0 + h)),
                  pl.BlockSpec((k_len, hw), lambda b, h, i: (b, h)),
                  pl.BlockSpec((k_len, hw), lambda b, h, i: (b, h)),
                  vec, vec, vec, vec,
                  pl.BlockSpec((1, hw), lambda b, h, i: (0, 0))],
        out_specs=pl.BlockSpec((tq, hw), lambda b, h, i: (b * nq + i, h)),
        out_shape=jax.ShapeDtypeStruct((batch * q_len, width), BF16),
        scratch_shapes=[pltpu.VMEM((n_chunks, tq, KEY_CHUNK), F32)],
        compiler_params=_params(("arbitrary", "arbitrary", "arbitrary")),
        name="diff_attention",
    )(qs_all, dk_bf, dv_bf, *[v.reshape(1, HEAD_DIM) for v in lams], g_subln.reshape(1, hw))


def _fill_invisible(x, fill, q_start, n_q, key_start, n_valid_keys, all_keys_valid):
    n_k, n_lanes = x.shape
    kpos = key_start + lax.broadcasted_iota(jnp.int32, (n_k, n_lanes), 0)
    qpos = q_start + jnp.minimum(lax.broadcasted_iota(jnp.int32, (1, n_lanes), 1), n_q - 1)
    x = jnp.where((kpos >> 6) <= (qpos >> 6), x, fill)
    return x if all_keys_valid else jnp.where(kpos < n_valid_keys, x, fill)


def _num_full_chunks(q_start, n_valid_keys):
    return jnp.minimum(((q_start >> 6) + 1) * CHUNK, n_valid_keys) // KEY_CHUNK


def _t_bf16(x, n_lanes):
    x = x.astype(F32)
    if x.shape[0] < n_lanes:
        x = jnp.concatenate([x, jnp.zeros((n_lanes - x.shape[0], x.shape[1]), F32)], axis=0)
    return x.T.astype(BF16)


def _col_partial(x, op):
    parts = [x[r * 8:(r + 1) * 8] for r in range(x.shape[0] // 8)]
    while len(parts) > 1:
        parts = [op(parts[i], parts[i + 1]) for i in range(0, len(parts) - 1, 2)] + parts[len(parts) & ~1:]
    return parts[0]


def _col_reduce(x, op):
    return (jnp.max if op is jnp.maximum else jnp.sum)(_col_partial(x, op), axis=0, keepdims=True)


def _row_to_cols(row):
    return jnp.broadcast_to(row, (LANES, row.shape[1])).T


def _dsa_t_kernel(q_ref, qi_ref, wi_ref, k_ref, vaug_ref, ki_ref, o_ref,
                  qit_scr, qt_scr, key_scr, acc_scr, *,
                  tq, tl, n_sel, p_len, n_valid_keys, all_keys_valid, n_idx_heads, n_kv_heads, rep):
    n_chunks_total = key_scr.shape[0]
    q_start = p_len + pl.program_id(1) * tq
    n_ch = _num_key_chunks(q_start, tq, n_valid_keys, n_chunks_total)
    heads_per_dot = 4

    for h in range(n_idx_heads):
        qit_scr[:, h * tl:(h + 1) * tl] = _t_bf16(qi_ref[:, h * LANES:(h + 1) * LANES], tl)
    for g in range(n_kv_heads):
        for r in range(rep):
            hh = g * rep + r
            qt_scr[g, :, r * tl:(r + 1) * tl] = _t_bf16(q_ref[:, hh * LANES:(hh + 1) * LANES], tl)
    wi = wi_ref[:, LANES:2 * LANES] * (n_idx_heads ** -0.5)
    if tq < tl:
        wi = jnp.concatenate([wi, jnp.zeros((tl - tq, LANES), F32)], axis=0)
    wi_t = wi.T

    def idx_body(j, carry):
        k0 = pl.multiple_of(j * KEY_CHUNK, KEY_CHUNK)
        kib = ki_ref[pl.ds(k0, KEY_CHUNK), :]
        score = jnp.zeros((KEY_CHUNK, tl), F32)
        for h0 in range(0, n_idx_heads, heads_per_dot):
            lg = jnp.dot(kib, qit_scr[:, h0 * tl:(h0 + heads_per_dot) * tl], preferred_element_type=F32)
            for h in range(h0, h0 + heads_per_dot):
                score = score + jnp.maximum(lg[:, (h - h0) * tl:(h - h0 + 1) * tl], 0.0) * wi_t[h:h + 1, :]
        score = _fill_invisible(score, -jnp.inf, q_start, tq, k0, n_valid_keys, all_keys_valid)
        key_scr[j] = _ordered_key(score)
        return carry

    lax.fori_loop(0, n_ch, idx_body, 0)

    def bit_body(b, t):
        cand_u = t | lax.shift_left(jnp.int32(1), 31 - b)
        cand_s = cand_u ^ jnp.int32(INT_MIN)

        def cnt_body(j, cnt):
            ones = jnp.where(key_scr[j] >= cand_s, 1, 0)
            return cnt + _col_partial(ones, jnp.add)

        cnt = lax.fori_loop(0, n_ch, cnt_body, jnp.zeros((8, tl), jnp.int32))
        total = jnp.sum(cnt, axis=0, keepdims=True)
        return jnp.where(total >= n_sel, cand_u, t)

    thr_u = lax.fori_loop(0, 32, bit_body, jnp.zeros((1, tl), jnp.int32))
    thr = jnp.maximum(thr_u ^ jnp.int32(INT_MIN), KEY_NEG_INF + 1)

    acc_scr[...] = jnp.zeros_like(acc_scr)

    def att_body(j, ms):
        k0 = pl.multiple_of(j * KEY_CHUNK, KEY_CHUNK)
        bias = jnp.where(key_scr[j] >= thr, 0.0, NEG_BIG)
        bias = jnp.concatenate([bias] * rep, axis=1)
        new = []
        for g in range(n_kv_heads):
            kb = k_ref[pl.ds(k0, KEY_CHUNK), g * LANES:(g + 1) * LANES]
            s = jnp.dot(kb, qt_scr[g], preferred_element_type=F32) + bias
            m_new = jnp.maximum(ms[g], _col_reduce(s, jnp.maximum))
            p = jnp.exp(s - m_new).astype(BF16)
            va = vaug_ref[pl.ds(k0, KEY_CHUNK), g * 2 * LANES:(g + 1) * 2 * LANES]
            alpha = _row_to_cols(jnp.exp(ms[g] - m_new))
            acc_scr[g] = (acc_scr[g] * jnp.concatenate([alpha, alpha], axis=1)
                          + lax.dot_general(p, va, (((0,), (0,)), ((), ())), preferred_element_type=F32))
            new.append(m_new)
        return tuple(new)

    lax.fori_loop(0, n_ch, att_body, tuple(jnp.full((1, rep * tl), NEG_BIG, F32) for _ in range(n_kv_heads)))

    for g in range(n_kv_heads):
        acc = acc_scr[g]
        o = acc[:, :LANES] / acc[:, LANES:]
        for r in range(rep):
            hh = g * rep + r
            o_ref[:, hh * LANES:(hh + 1) * LANES] = o[r * tl:r * tl + tq].astype(o_ref.dtype)


def _dsa_attention_t(qs_all, kiw_f32, k_bf, vaug_bf, ki_bf, *, batch, q_len, k_len, n_valid_keys, p_len, tq,
                     n_heads, n_kv_heads, n_idx_heads, n_sel):
    width = n_heads * HEAD_DIM
    nq = q_len // tq
    tl = max(tq, LANES)
    n_chunks = k_len // KEY_CHUNK
    rep = n_heads // n_kv_heads
    kv_w = n_kv_heads * HEAD_DIM
    return pl.pallas_call(
        functools.partial(_dsa_t_kernel, tq=tq, tl=tl, n_sel=n_sel, p_len=p_len, n_valid_keys=n_valid_keys,
                          all_keys_valid=n_valid_keys == k_len,
                          n_idx_heads=n_idx_heads, n_kv_heads=n_kv_heads, rep=rep),
        grid=(batch, nq),
        in_specs=[pl.BlockSpec((tq, width), lambda b, i: (b * nq + i, 0)),
                  pl.BlockSpec((tq, width), lambda b, i: (b * nq + i, 1)),
                  pl.BlockSpec((tq, 2 * LANES), lambda b, i: (b * nq + i, 0)),
                  pl.BlockSpec((k_len, kv_w), lambda b, i: (b, 0)),
                  pl.BlockSpec((k_len, 2 * kv_w), lambda b, i: (b, 0)),
                  pl.BlockSpec((k_len, LANES), lambda b, i: (b, 0))],
        out_specs=pl.BlockSpec((tq, width), lambda b, i: (b * nq + i, 0)),
        out_shape=jax.ShapeDtypeStruct((batch * q_len, width), BF16),
        scratch_shapes=[pltpu.VMEM((HEAD_DIM, n_idx_heads * tl), BF16),
                        pltpu.VMEM((n_kv_heads, HEAD_DIM, rep * tl), BF16),
                        pltpu.VMEM((n_chunks, KEY_CHUNK, tl), jnp.int32),
                        pltpu.VMEM((n_kv_heads, rep * tl, 2 * LANES), F32)],
        compiler_params=_params(("arbitrary", "arbitrary")),
        name="dsa_attention",
    )(qs_all, qs_all, kiw_f32, k_bf, vaug_bf, ki_bf)


def _diff_t_kernel(dq_ref, dk_ref, dv_ref, lq1_ref, lk1_ref, lq2_ref, lk2_ref, g_ref, o_ref,
                   acc_scr, *, tq, tl, hp, n_chunks_total, p_len, n_valid_keys, all_keys_valid, lam_init):
    q_start = p_len + pl.program_id(2) * tq
    n_ch = _num_key_chunks(q_start, tq, n_valid_keys, n_chunks_total)
    hw = 2 * HEAD_DIM
    n_maps = 2 * hp

    lam = (jnp.exp(jnp.sum(lq1_ref[...] * lk1_ref[...], axis=1, keepdims=True))
           - jnp.exp(jnp.sum(lq2_ref[...] * lk2_ref[...], axis=1, keepdims=True)) + lam_init)
    qt = [_t_bf16(dq_ref[:, c * LANES:(c + 1) * LANES], tl) for c in range(n_maps)]

    acc_scr[...] = jnp.zeros_like(acc_scr)

    def att_body(j, carry, masked):
        k0 = pl.multiple_of(j * KEY_CHUNK, KEY_CHUNK)
        new = []
        for c in range(n_maps):
            m, l = carry[2 * c:2 * c + 2]
            kb = dk_ref[pl.ds(k0, KEY_CHUNK), c * LANES:(c + 1) * LANES]
            vb = dv_ref[pl.ds(k0, KEY_CHUNK), (c // 2) * hw:(c // 2 + 1) * hw]
            s = jnp.dot(kb, qt[c], preferred_element_type=F32)
            if masked:
                s = _fill_invisible(s, NEG_BIG, q_start, tq, k0, n_valid_keys, all_keys_valid)
            m_new = jnp.maximum(m, _col_reduce(s, jnp.maximum))
            p = jnp.exp(s - m_new)
            alpha = jnp.exp(m - m_new)
            alpha_c = _row_to_cols(alpha)
            acc_scr[c] = (acc_scr[c] * jnp.concatenate([alpha_c, alpha_c], axis=1)
                          + lax.dot_general(p.astype(BF16), vb, (((0,), (0,)), ((), ())),
                                            preferred_element_type=F32))
            new += [m_new, alpha * l + _col_reduce(p, jnp.add)]
        return tuple(new)

    one = (jnp.full((1, tl), NEG_BIG, F32), jnp.zeros((1, tl), F32))
    n_full = jnp.minimum(_num_full_chunks(q_start, n_valid_keys), n_ch)
    carry = lax.fori_loop(0, n_full, functools.partial(att_body, masked=False), one * n_maps)
    carry = lax.fori_loop(n_full, n_ch, functools.partial(att_body, masked=True), carry)

    def normalised(c):
        l_cols = _row_to_cols(carry[2 * c + 1])
        return acc_scr[c] / jnp.concatenate([l_cols, l_cols], axis=1)

    for h in range(hp):
        o = (normalised(2 * h) - lam * normalised(2 * h + 1))[:tq]
        o = o * lax.rsqrt(jnp.mean(o * o, axis=-1, keepdims=True) + EPS) * g_ref[...]
        o_ref[:, h * hw:(h + 1) * hw] = (o * (1.0 - lam_init)).astype(o_ref.dtype)


def _diff_attention_t(qs_all, dk_bf, dv_bf, lams, g_subln, *, batch, q_len, k_len, n_valid_keys, p_len, tq,
                      n_heads, lam_init):
    hw = 2 * HEAD_DIM
    hp = 2 if n_heads % 2 == 0 else 1
    width = n_heads * hw
    nq = q_len // tq
    tl = max(tq, LANES)
    n_chunks = k_len // KEY_CHUNK
    assert qs_all.shape[1] == 3 * width
    dq_col0 = 2 * n_heads // hp
    vec = pl.BlockSpec((1, HEAD_DIM), lambda b, h, i: (0, 0))
    return pl.pallas_call(
        functools.partial(_diff_t_kernel, tq=tq, tl=tl, hp=hp, n_chunks_total=n_chunks, p_len=p_len,
                          n_valid_keys=n_valid_keys, all_keys_valid=n_valid_keys == k_len, lam_init=lam_init),
        grid=(batch, n_heads // hp, nq),
        in_specs=[pl.BlockSpec((tq, hp * hw), lambda b, h, i: (b * nq + i, dq_col0 + h)),
                  pl.BlockSpec((k_len, hp * hw), lambda b, h, i: (b, h)),
                  pl.BlockSpec((k_len, hp * hw), lambda b, h, i: (b, h)),
                  vec, vec, vec, vec,
                  pl.BlockSpec((1, hw), lambda b, h, i: (0, 0))],
        out_specs=pl.BlockSpec((tq, hp * hw), lambda b, h, i: (b * nq + i, h)),
        out_shape=jax.ShapeDtypeStruct((batch * q_len, width), BF16),
        scratch_shapes=[pltpu.VMEM((2 * hp, tl, hw), F32)],
        compiler_params=_params(("arbitrary", "arbitrary", "arbitrary")),
        name="diff_attention",
    )(qs_all, dk_bf, dv_bf, *[v.reshape(1, HEAD_DIM) for v in lams], g_subln.reshape(1, hw))


def _outproj_kernel(a_ref, d_ref, wa_ref, wd_ref, x_ref, ga_ref, o_ref, *, per_token):
    mix = (jnp.dot(a_ref[...], wa_ref[...], preferred_element_type=F32)
           + jnp.dot(d_ref[...], wd_ref[...], preferred_element_type=F32))
    ga = ga_ref[...] if per_token else ga_ref[0]
    o_ref[...] = x_ref[...] + ga * mix


def _outproj(a_out, d_out, w_out_bf, x2, mod, *, per_token, rows_per_batch, tm, tn):
    t, d = x2.shape
    wa = a_out.shape[1]
    wd = d_out.shape[1]
    assert wa == wd
    cb = 2 * (d // tn)
    return pl.pallas_call(
        functools.partial(_outproj_kernel, per_token=per_token),
        grid=(t // tm, d // tn),
        in_specs=[pl.BlockSpec((tm, wa), lambda m, n: (m, 0)),
                  pl.BlockSpec((tm, wd), lambda m, n: (m, 0)),
                  pl.BlockSpec((wa, tn), lambda m, n: (0, n)),
                  pl.BlockSpec((wd, tn), lambda m, n: (1, n)),
                  pl.BlockSpec((tm, tn), lambda m, n: (m, n)),
                  _mod_spec(mod, per_token, tm, tn, cb, rows_per_batch)],
        out_specs=pl.BlockSpec((tm, tn), lambda m, n: (m, n)),
        out_shape=jax.ShapeDtypeStruct((t, d), F32),
        compiler_params=_params(("arbitrary", "arbitrary")),
        name="outproj",
    )(a_out, d_out, w_out_bf, w_out_bf, x2, mod)


def _top_rows(x, k, with_rank=False):
    tops = []
    rank = jnp.full(x.shape, float(k), F32) if with_rank else None
    for i in range(k):
        mx = jnp.max(x, axis=0, keepdims=True)
        tops.append(mx)
        hit = x == mx
        if with_rank:
            rank = jnp.where(hit, float(i), rank)
        x = jnp.where(hit, -jnp.inf, x)
    return (tops, rank) if with_rank else tops


def _peer_route_kernel(h_ref, wq_ref, keys_ref, cnt_ref, e1_ref, rank_ref, e2_ref, s1_scr, top_scr):
    c = pl.program_id(1) % 2
    q_t = jnp.dot(wq_ref[...], h_ref[...], preferred_element_type=F32)
    s_t = jnp.dot(keys_ref[0].astype(BF16), q_t.astype(BF16), preferred_element_type=F32)

    @pl.when(c == 0)
    def _():
        s1_scr[...] = s_t
        top_scr[...] = jnp.concatenate(_top_rows(s_t, PEER_TOPK), axis=0)

    @pl.when(c == 1)
    def _():
        tops2, rank2 = _top_rows(s_t, PEER_TOPK, with_rank=True)
        tops = jnp.concatenate(tops2, axis=0)
        top1 = top_scr[...]
        cand = jnp.concatenate([top1[0:1, :] + tops]
                               + [top1[i:i + 1, :] + tops[:PEER_TOPK // 2] for i in range(1, PEER_TOPK)], axis=0)
        best = _top_rows(cand, PEER_TOPK)
        m = best[0]
        z = jnp.zeros_like(m)
        for bk in best:
            z = z + jnp.exp(bk - m)
        thr = best[PEER_TOPK - 1]
        s1 = s1_scr[...]
        cnt = jnp.zeros_like(s1)
        for j in range(PEER_TOPK):
            cnt = cnt + jnp.where(s1 + tops2[j] >= thr, 1.0, 0.0)
        e1 = jnp.exp(s1 - top1[0:1, :]) / z
        e2 = jnp.exp(s_t - tops2[0])
        hn = s_t.shape[0] // 2
        cnt_ref[0] = _pack_bf16_pair(cnt, cnt)
        e1_ref[0] = _pack_bf16_pair(e1, e1)
        rank_ref[0] = _pack_bf16_pair(rank2[:hn], rank2[hn:])
        e2_ref[0] = _pack_bf16_pair(e2[:hn], e2[hn:])


def _pack_bf16_pair(lo, hi):
    def bits(x):
        b = pltpu.bitcast(x, jnp.uint32)
        return (b + jnp.uint32(0x7FFF) + ((b >> 16) & jnp.uint32(1))) >> 16
    return bits(lo) | (bits(hi) << 16)


def _peer_route(h_t, wq_t_bf, sub_keys, *, tm):
    d, t = h_t.shape
    heads, _, n_keys, half = sub_keys.shape
    keys2 = sub_keys.reshape(heads * 2, n_keys, half)
    a_spec = pl.BlockSpec((1, n_keys, tm), lambda m, hc: (hc // 2, 0, m))
    b_spec = pl.BlockSpec((1, n_keys // 2, tm), lambda m, hc: (hc // 2, 0, m))
    a_tab = jax.ShapeDtypeStruct((heads, n_keys, t), jnp.uint32)
    b_tab = jax.ShapeDtypeStruct((heads, n_keys // 2, t), jnp.uint32)
    return pl.pallas_call(
        _peer_route_kernel,
        grid=(t // tm, heads * 2),
        in_specs=[pl.BlockSpec((d, tm), lambda m, hc: (0, m)),
                  pl.BlockSpec((half, d), lambda m, hc: (hc, 0)),
                  pl.BlockSpec((1, n_keys, half), lambda m, hc: (hc, 0, 0))],
        out_specs=[a_spec, a_spec, b_spec, b_spec],
        out_shape=[a_tab, a_tab, b_tab, b_tab],
        scratch_shapes=[pltpu.VMEM((n_keys, tm), F32), pltpu.VMEM((PEER_TOPK, tm), F32)],
        compiler_params=_params(("arbitrary", "arbitrary")),
        name="peer_route",
    )(h_t, wq_t_bf, keys2)


def _gelu_tanh(x):
    return 0.5 * x * (1.0 + jnp.tanh(0.7978845608028654 * (x + 0.044715 * x * x * x)))


def _rows_bf16(row_words, n_rows):
    tile = jnp.broadcast_to(row_words, (8, row_words.shape[1]))
    return pltpu.bitcast(jnp.concatenate([tile] * (n_rows // 16), axis=0), BF16)


def _peer_ffn_kernel(h_ref, u_ref, v_ref, cnt_ref, e1_ref, rank_ref, e2_ref, o_ref, act_scr, *, n_keys, n_blocks):
    e = pl.program_id(1)
    te, tm = act_scr.shape[1:]
    heads = cnt_ref.shape[0]
    a0 = jnp.maximum(e - 1, 0) * (te // n_keys)

    @pl.when(e == 0)
    def _():
        o_ref[...] = jnp.zeros_like(o_ref)
        act_scr[1] = jnp.zeros((te, tm), F32)

    def step(rd, wr):
        act_scr[wr] = jnp.dot(u_ref[...], h_ref[...], preferred_element_type=F32)
        act = act_scr[rd]
        strip = min(tm, LANES)
        hn = n_keys // 2
        w_rows = [[None] * (tm // strip) for _ in range(2 * te // n_keys)]
        for ai in range(te // n_keys):
            cnt_a = [cnt_ref[h, pl.ds(a0 + ai, 1), :] for h in range(heads)]
            e1_a = [e1_ref[h, pl.ds(a0 + ai, 1), :] for h in range(heads)]
            for c in range(tm // strip):
                cols = slice(c * strip, (c + 1) * strip)
                gate = jnp.zeros((n_keys, strip), BF16)
                for h in range(heads):
                    cnt_b = _rows_bf16(cnt_a[h][:, cols], n_keys)
                    e1_b = _rows_bf16(e1_a[h][:, cols], n_keys)
                    keep = pltpu.bitcast(rank_ref[h, :, cols], BF16) < cnt_b
                    gate = gate + jnp.where(keep, pltpu.bitcast(e2_ref[h, :, cols], BF16) * e1_b,
                                            jnp.zeros_like(e1_b))
                words = pltpu.bitcast(gate, jnp.uint32)
                halves = (pltpu.bitcast(words << 16, F32), pltpu.bitcast(words & jnp.uint32(0xFFFF0000), F32))
                for k, g in enumerate(halves):
                    rows = slice(ai * n_keys + k * hn, ai * n_keys + (k + 1) * hn)
                    w_rows[2 * ai + k][c] = (g * _gelu_tanh(act[rows, cols])).astype(BF16)
        w = jnp.concatenate([r[0] if len(r) == 1 else jnp.concatenate(r, axis=1) for r in w_rows], axis=0)
        o_ref[...] += lax.dot_general(w, v_ref[...], (((0,), (0,)), ((), ())),
                                      preferred_element_type=F32)

    @pl.when(e % 2 == 0)
    def _():
        step(1, 0)

    @pl.when(e % 2 == 1)
    def _():
        step(0, 1)


def _peer_ffn(h_t, u_bf, v_bf, cnt, e1, rank, e2, *, tm, te):
    d, t = h_t.shape
    n_exp = u_bf.shape[0]
    n_blocks = n_exp // te
    heads, n_keys, _ = cnt.shape
    once = pl.Buffered(1)
    a_spec = pl.BlockSpec((heads, n_keys, tm), lambda m, e: (0, 0, m), pipeline_mode=once)
    b_spec = pl.BlockSpec((heads, n_keys // 2, tm), lambda m, e: (0, 0, m), pipeline_mode=once)
    return pl.pallas_call(
        functools.partial(_peer_ffn_kernel, n_keys=n_keys, n_blocks=n_blocks),
        grid=(t // tm, n_blocks + 1),
        in_specs=[pl.BlockSpec((d, tm), lambda m, e: (0, m), pipeline_mode=once),
                  pl.BlockSpec((te, d), lambda m, e: (jnp.minimum(e, n_blocks - 1), 0)),
                  pl.BlockSpec((te, d), lambda m, e: (jnp.maximum(e - 1, 0), 0)),
                  a_spec, a_spec, b_spec, b_spec],
        out_specs=pl.BlockSpec((tm, d), lambda m, e: (m, 0)),
        out_shape=jax.ShapeDtypeStruct((t, d), F32),
        scratch_shapes=[pltpu.VMEM((2, te, tm), F32)],
        compiler_params=_params(("arbitrary", "arbitrary")),
        name="peer_ffn",
    )(h_t, u_bf, v_bf, cnt, e1, rank, e2)


def _final_kernel(x_ref, p_ref, ga_ref, g_ref, o_ref, *, per_token, normalize):
    ga = ga_ref[...] if per_token else ga_ref[0]
    x = x_ref[...] + ga * p_ref[...]
    if normalize:
        x = x * lax.rsqrt(jnp.mean(x * x, axis=-1, keepdims=True) + EPS) * g_ref[...]
    o_ref[...] = x


def _final(x1, peer, mod, g_final, *, per_token, rows_per_batch, tm, normalize):
    t, d = x1.shape
    row = pl.BlockSpec((tm, d), lambda m: (m, 0))
    return pl.pallas_call(
        functools.partial(_final_kernel, per_token=per_token, normalize=normalize),
        grid=(t // tm,),
        in_specs=[row, row, _mod_spec(mod, per_token, tm, d, 5, rows_per_batch),
                  pl.BlockSpec((1, d), lambda m: (0, 0))],
        out_specs=row,
        out_shape=jax.ShapeDtypeStruct((t, d), F32),
        compiler_params=_params(("arbitrary",)),
        name="final",
    )(x1, peer, mod, g_final.reshape(1, d))


def _rope_tables(pos):
    half = HEAD_DIM // 2
    inv = ROPE_THETA ** (-jnp.arange(half, dtype=F32) / half)
    ang = pos.astype(F32)[:, None] * inv[None, :]
    cos, sin = jnp.cos(ang), jnp.sin(ang)
    return jnp.concatenate([cos, cos], axis=1), jnp.concatenate([-sin, sin], axis=1)


def _pick_tile(n, prefs):
    for p in prefs:
        if n % p == 0:
            return p
    return n


def _layer(x, mod_rows, past, layer, last_layer, w, dims):
    b, s, d = x.shape
    t = b * s
    n_heads, n_kv, n_idx, n_diff = dims
    p_len = 0 if past is None else past[0].shape[1]
    n_keys_valid = p_len + s
    n_sel = min(DSA_TOPK, n_keys_valid // 4)
    x2 = x.reshape(t, d)

    per_token = s % 256 != 0
    if per_token:
        mod = jnp.repeat(mod_rows, s, axis=0)
    else:
        mod = mod_rows.reshape(b, 1, 6 * d)
    tm_big = _pick_tile(t if per_token else s, (1024, 512, 256, 128))
    tm_mid = _pick_tile(t if per_token else s, (512, 256, 128))
    tm_small = _pick_tile(t if per_token else s, (256, 128))

    pos = p_len + jnp.arange(s)
    cos, sin = _rope_tables(pos)
    if per_token:
        cos, sin = jnp.tile(cos, (b, 1)), jnp.tile(sin, (b, 1))

    h = _norm_mod(x2, w["g_norm_mix"], mod, which=1, per_token=per_token, rows_per_batch=s, tm=tm_mid,
                  transpose=False)

    proj = functools.partial(_proj, h, cos=cos, sin=sin, tm=tm_big)
    wq = n_heads * HEAD_DIM
    (qs_all,) = proj(w["w_qs"], rope_cols=512, scale=HEAD_DIM ** -0.5, out_f32=False, out_bf16=True, tn=512,
                     name="proj_q")
    k_f, k_b = proj(w["w_k"], rope_cols=512, out_f32=True, out_bf16=True, tn=512, name="proj_k")
    v_f, v_b = proj(w["w_v"], rope_cols=0, out_f32=True, out_bf16=True, tn=512, name="proj_v")
    dk_f, dk_b = proj(w["w_dk"], rope_cols=512, out_f32=True, out_bf16=True, tn=512, name="proj_dk")
    dv_f, dv_b = proj(w["w_dv"], rope_cols=0, out_f32=True, out_bf16=True, tn=512, name="proj_dv")
    kiw_f, kiw_b = proj(w["w_kiw"], rope_cols=LANES, out_f32=True, out_bf16=True, tn=2 * LANES, name="proj_kiw")

    ki_f = kiw_f[:, :LANES]
    new_rows = (k_f.reshape(b, s, n_kv, HEAD_DIM), v_f.reshape(b, s, n_kv, HEAD_DIM), ki_f.reshape(b, s, LANES),
                dk_f.reshape(b, s, n_diff, 2, HEAD_DIM), dv_f.reshape(b, s, n_diff, 2 * HEAD_DIM))

    if past is None:
        k_len = s
        k_all, v_all, ki_all, dk_all, dv_all = k_b, v_b, kiw_b, dk_b, dv_b
    else:
        k_len = -(-n_keys_valid // KEY_CHUNK) * KEY_CHUNK

        def join(cache, new, width):
            new = new.reshape(b, s, -1)[:, :, :width]
            both = jnp.concatenate([cache.reshape(b, p_len, width).astype(BF16), new], axis=1)
            both = jnp.pad(both, ((0, 0), (0, k_len - n_keys_valid), (0, 0)))
            return both.reshape(b * k_len, width)

        k_all = join(past[0], k_b, n_kv * HEAD_DIM)
        v_all = join(past[1], v_b, n_kv * HEAD_DIM)
        ki_all = join(past[2], kiw_b, LANES)
        dk_all = join(past[3], dk_b, n_diff * 2 * HEAD_DIM)
        dv_all = join(past[4], dv_b, n_diff * 2 * HEAD_DIM)

    v3 = v_all.reshape(-1, n_kv, HEAD_DIM)
    vaug_all = jnp.concatenate([v3, jnp.ones_like(v3)], axis=2).reshape(-1, n_kv * 2 * HEAD_DIM)
    tq_a = _pick_tile(s, (128, 64, 32))
    a_out = _dsa_attention_t(qs_all, kiw_f, k_all, vaug_all, ki_all, batch=b, q_len=s, k_len=k_len,
                             n_valid_keys=n_keys_valid, p_len=p_len, tq=tq_a, n_heads=n_heads, n_kv_heads=n_kv,
                             n_idx_heads=n_idx, n_sel=n_sel)
    tq_d = _pick_tile(s, (256, 128, 64, 32))
    d_out = _diff_attention_t(qs_all, dk_all, dv_all, w["lams"], w["g_subln"], batch=b, q_len=s, k_len=k_len,
                              n_valid_keys=n_keys_valid, p_len=p_len, tq=tq_d, n_heads=n_diff,
                              lam_init=_lambda_init(layer))

    x1 = _outproj(a_out, d_out, w["w_out"], x2, mod, per_token=per_token, rows_per_batch=s, tm=tm_big, tn=512)

    h2_t = _norm_mod(x1, w["g_norm_ffn"], mod, which=2, per_token=per_token, rows_per_batch=s, tm=tm_mid,
                     transpose=True)
    cnt, e1, rank, e2 = _peer_route(h2_t, w["peer_wq_t"], w["peer_keys"], tm=tm_mid)
    peer = _peer_ffn(h2_t, w["peer_u"], w["peer_v"], cnt, e1, rank, e2, tm=tm_mid, te=512)
    x_out = _final(x1, peer, mod, w["g_final"], per_token=per_token, rows_per_batch=s, tm=tm_small,
                   normalize=last_layer)
    return x_out.reshape(b, s, d), new_rows


def kernel(x_prompt, x_sample, cache_dsa_k, cache_dsa_v, cache_idx_k, cache_diff_k, cache_diff_v, c_prompt, c_sample, w_ada, b_ada, g_norm_mix, g_norm_ffn, w_in, diff_lambda_q1, diff_lambda_k1, diff_lambda_q2, diff_lambda_k2, g_diff_subln, w_out, peer_w_query, peer_sub_keys, peer_u, peer_v, g_final):
    depth = w_in.shape[0]
    d = x_prompt.shape[-1]
    bp, bs = x_prompt.shape[0], x_sample.shape[0]
    n_kv = cache_dsa_k.shape[3]
    n_diff = cache_diff_k.shape[3]
    n_heads, n_idx_heads = DSA_HEADS, IDX_HEADS
    qw, kvw, dw = n_heads * HEAD_DIM, n_kv * HEAD_DIM, n_diff * 2 * HEAD_DIM
    assert w_in.shape[2] == qw + 2 * kvw + n_idx_heads * LANES + LANES + n_idx_heads + 3 * dw
    dims = (n_heads, n_kv, n_idx_heads, n_diff)

    hp, hs = x_prompt, x_sample
    rows_p, rows_s = [], []
    n_c = bp + bs
    c_pad = jnp.pad(jnp.concatenate([c_prompt, c_sample], axis=0), ((0, (-n_c) % 16), (0, 0)))
    for l in range(depth):
        mod_all = _adaln(c_pad, w_ada[l], b_ada[l])
        win = w_in[l]
        o = 0
        seg = {}
        for name, width in (("q", qw), ("k", kvw), ("v", kvw), ("qi", n_idx_heads * LANES), ("ki", LANES),
                            ("wi", n_idx_heads), ("dq", dw), ("dk", dw), ("dv", dw)):
            seg[name] = win[:, o:o + width]
            o += width
        weights = {
            "w_qs": jnp.concatenate([seg["q"], seg["qi"], seg["dq"]], axis=1).astype(BF16),
            "w_k": seg["k"].astype(BF16), "w_v": seg["v"].astype(BF16),
            "w_dk": seg["dk"].astype(BF16), "w_dv": seg["dv"].astype(BF16),
            "w_kiw": jnp.concatenate([seg["ki"], seg["wi"], jnp.zeros((d, LANES - n_idx_heads), F32)],
                                     axis=1).astype(BF16),
            "w_out": w_out[l].astype(BF16),
            "peer_wq_t": peer_w_query[l].T.astype(BF16),
            "peer_keys": peer_sub_keys[l],
            "peer_u": peer_u[l].astype(BF16), "peer_v": peer_v[l].astype(BF16),
            "g_norm_mix": g_norm_mix[l], "g_norm_ffn": g_norm_ffn[l], "g_subln": g_diff_subln[l],
            "lams": (diff_lambda_q1[l], diff_lambda_k1[l], diff_lambda_q2[l], diff_lambda_k2[l]),
            "g_final": g_final,
        }
        last = l == depth - 1
        hp, rp = _layer(hp, mod_all[:bp], None, l, last, weights, dims)
        past = (cache_dsa_k[l], cache_dsa_v[l], cache_idx_k[l], cache_diff_k[l], cache_diff_v[l])
        hs, rs = _layer(hs, mod_all[bp:bp + bs], past, l, last, weights, dims)
        rows_p.append(rp)
        rows_s.append(rs)
    stack = lambda rows, i: jnp.stack([r[i] for r in rows])
    return (hp, hs) + tuple(stack(rows_p, i) for i in range(5)) + tuple(stack(rows_s, i) for i in range(5))
```

```python
import functools
import math

import jax
import jax.numpy as jnp
from jax import lax
from jax.experimental import pallas as pl
from jax.experimental.pallas import tpu as pltpu

CHUNK = 64
HEAD_DIM = 128
ROPE_THETA = 10000.0
EPS = 1e-6
DSA_HEADS = 16
IDX_HEADS = 16
DSA_TOPK = 256
PEER_TOPK = 16
LANES = 128
KEY_CHUNK = 512
NEG_BIG = -1e30
INT_MIN = -(2 ** 31)
KEY_NEG_INF = INT_MIN + 0x7FFFFF
VMEM_LIMIT = 56 * 1024 * 1024

BF16 = jnp.bfloat16
F32 = jnp.float32


def _params(sem, vmem=VMEM_LIMIT):
    return pltpu.CompilerParams(dimension_semantics=sem, vmem_limit_bytes=vmem)


def _lambda_init(layer):
    return 0.8 - 0.6 * math.exp(-0.3 * layer)


def _nt_dot(a, b):
    return lax.dot_general(a, b, (((1,), (1,)), ((), ())), preferred_element_type=F32)


def _adaln_kernel(c_ref, w_ref, b_ref, o_ref):
    c = c_ref[...]
    a = (c * jax.nn.sigmoid(c)).astype(BF16)
    o_ref[...] = jnp.dot(a, w_ref[...].astype(BF16), preferred_element_type=F32) + b_ref[...]


def _adaln(c_pad, w_ada, b_ada):
    bp, d = c_pad.shape
    n = w_ada.shape[1]
    tn = 512
    return pl.pallas_call(
        _adaln_kernel,
        grid=(n // tn,),
        in_specs=[pl.BlockSpec((bp, d), lambda j: (0, 0)),
                  pl.BlockSpec((d, tn), lambda j: (0, j)),
                  pl.BlockSpec((1, tn), lambda j: (0, j))],
        out_specs=pl.BlockSpec((bp, tn), lambda j: (0, j)),
        out_shape=jax.ShapeDtypeStruct((bp, n), F32),
        compiler_params=_params(("arbitrary",)),
        name="adaln",
    )(c_pad, w_ada, b_ada.reshape(1, n))


def _norm_mod_kernel(x_ref, g_ref, sc_ref, sh_ref, o_ref, *, per_token, transpose):
    x = x_ref[...]
    y = x * lax.rsqrt(jnp.mean(x * x, axis=-1, keepdims=True) + EPS) * g_ref[...]
    sc = sc_ref[...] if per_token else sc_ref[0]
    sh = sh_ref[...] if per_token else sh_ref[0]
    h = y * (1.0 + sc) + sh
    if transpose:
        o_ref[...] = h.T.astype(o_ref.dtype)
    else:
        o_ref[...] = h.astype(o_ref.dtype)


def _mod_spec(mod, per_token, tm, tn, col_block, rows_per_batch):
    if per_token:
        return pl.BlockSpec((tm, tn), lambda m, n=0, cb=col_block: (m, cb + n))
    tiles_per_batch = rows_per_batch // tm
    return pl.BlockSpec((1, 1, tn), lambda m, n=0, cb=col_block: (m // tiles_per_batch, 0, cb + n))


def _norm_mod(x2, g, mod, *, which, per_token, rows_per_batch, tm, transpose):
    t, d = x2.shape
    sh_blk, sc_blk = (0, 1) if which == 1 else (3, 4)
    out_shape = (d, t) if transpose else (t, d)
    out_spec = pl.BlockSpec((d, tm), lambda m: (0, m)) if transpose else pl.BlockSpec((tm, d), lambda m: (m, 0))
    return pl.pallas_call(
        functools.partial(_norm_mod_kernel, per_token=per_token, transpose=transpose),
        grid=(t // tm,),
        in_specs=[pl.BlockSpec((tm, d), lambda m: (m, 0)),
                  pl.BlockSpec((1, d), lambda m: (0, 0)),
                  _mod_spec(mod, per_token, tm, d, sc_blk, rows_per_batch),
                  _mod_spec(mod, per_token, tm, d, sh_blk, rows_per_batch)],
        out_specs=out_spec,
        out_shape=jax.ShapeDtypeStruct(out_shape, BF16),
        compiler_params=_params(("arbitrary",)),
        name="norm_mod_t" if transpose else "norm_mod",
    )(x2, g.reshape(1, d), mod, mod)


def _rope_tile(acc, cos, sin, n_chunks):
    outs = []
    for j in range(n_chunks):
        xj = acc[:, j * LANES:(j + 1) * LANES]
        outs.append(xj * cos + pltpu.roll(xj, LANES // 2, axis=1) * sin)
    return outs[0] if n_chunks == 1 else jnp.concatenate(outs, axis=1)


def _proj_kernel(*refs, rope_cols, scale, out_f32, out_bf16):
    h_ref, w_ref, cos_ref, sin_ref = refs[:4]
    outs = refs[4:]
    acc = jnp.dot(h_ref[...], w_ref[...], preferred_element_type=F32)
    tn = acc.shape[1]
    if rope_cols:
        roped = _rope_tile(acc[:, :rope_cols], cos_ref[...], sin_ref[...], rope_cols // LANES)
        acc = roped if rope_cols == tn else jnp.concatenate([roped, acc[:, rope_cols:]], axis=1)
    k = 0
    if out_f32:
        outs[k][...] = acc
        k += 1
    if out_bf16:
        outs[k][...] = (acc * scale if scale != 1.0 else acc).astype(BF16)


def _proj(h, w, cos, sin, *, rope_cols, scale=1.0, out_f32, out_bf16, tm, tn, name):
    t, d = h.shape
    n = w.shape[1]
    assert rope_cols in (0, tn) or n == tn
    pos_tiles = cos.shape[0] // tm
    out_shape, out_specs = [], []
    for want, dt in ((out_f32, F32), (out_bf16, BF16)):
        if want:
            out_shape.append(jax.ShapeDtypeStruct((t, n), dt))
            out_specs.append(pl.BlockSpec((tm, tn), lambda m, j: (m, j)))
    return pl.pallas_call(
        functools.partial(_proj_kernel, rope_cols=rope_cols, scale=scale, out_f32=out_f32, out_bf16=out_bf16),
        grid=(t // tm, n // tn),
        in_specs=[pl.BlockSpec((tm, d), lambda m, j: (m, 0)),
                  pl.BlockSpec((d, tn), lambda m, j: (0, j)),
                  pl.BlockSpec((tm, LANES), lambda m, j: (m % pos_tiles, 0)),
                  pl.BlockSpec((tm, LANES), lambda m, j: (m % pos_tiles, 0))],
        out_specs=out_specs,
        out_shape=out_shape,
        compiler_params=_params(("arbitrary", "arbitrary")),
        name=name,
    )(h, w, cos, sin)


def _visible(q_start, n_q, key_start, n_k, n_valid_keys):
    qpos = q_start + lax.broadcasted_iota(jnp.int32, (n_q, 1), 0)
    kpos = key_start + lax.broadcasted_iota(jnp.int32, (1, n_k), 1)
    return ((kpos >> 6) <= (qpos >> 6)) & (kpos < n_valid_keys)


def _num_key_chunks(q_start, n_q, n_valid_keys, n_chunks_total):
    last_visible = (((q_start + n_q - 1) >> 6) + 1) * CHUNK
    last_visible = jnp.minimum(last_visible, n_valid_keys)
    return jnp.minimum((last_visible + KEY_CHUNK - 1) // KEY_CHUNK, n_chunks_total)


def _ordered_key(x):
    b = pltpu.bitcast(x, jnp.int32)
    return b ^ ((b >> 31) & jnp.int32(0x7FFFFFFF))


def _dsa_kernel(q_ref, qi_ref, wi_ref, k_ref, v_ref, ki_ref, o_ref, key_scr, bias_scr, *,
                tq, n_sel, p_len, n_valid_keys, n_idx_heads, n_kv_heads, rep):
    n_chunks_total = key_scr.shape[0]
    q_start = p_len + pl.program_id(1) * tq
    n_ch = _num_key_chunks(q_start, tq, n_valid_keys, n_chunks_total)

    def idx_body(j, carry):
        k0 = pl.multiple_of(j * KEY_CHUNK, KEY_CHUNK)
        kib = ki_ref[pl.ds(k0, KEY_CHUNK), :]
        score = jnp.zeros((tq, KEY_CHUNK), F32)
        wi = wi_ref[:, LANES:2 * LANES] * (n_idx_heads ** -0.5)
        for h in range(n_idx_heads):
            lg = _nt_dot(qi_ref[:, h * LANES:(h + 1) * LANES], kib)
            score = score + jnp.maximum(lg, 0.0) * wi[:, h:h + 1]
        vis = _visible(q_start, tq, k0, KEY_CHUNK, n_valid_keys)
        key_scr[j] = _ordered_key(jnp.where(vis, score, -jnp.inf))
        return carry

    lax.fori_loop(0, n_ch, idx_body, 0)

    def bit_body(b, t):
        cand_u = t | lax.shift_left(jnp.int32(1), 31 - b)
        cand_s = cand_u ^ jnp.int32(INT_MIN)

        def cnt_body(j, cnt):
            ones = jnp.where(key_scr[j] >= cand_s, 1, 0)
            for c in range(KEY_CHUNK // LANES):
                cnt = cnt + ones[:, c * LANES:(c + 1) * LANES]
            return cnt

        cnt = lax.fori_loop(0, n_ch, cnt_body, jnp.zeros((tq, LANES), jnp.int32))
        total = jnp.sum(cnt, axis=1, keepdims=True)
        return jnp.where(total >= n_sel, cand_u, t)

    thr_u = lax.fori_loop(0, 32, bit_body, jnp.zeros((tq, 1), jnp.int32))
    thr = thr_u ^ jnp.int32(INT_MIN)

    def bias_body(j, carry):
        k0 = j * KEY_CHUNK
        vis = _visible(q_start, tq, k0, KEY_CHUNK, n_valid_keys)
        sel = (key_scr[j] >= thr) & vis
        bias_scr[j] = jnp.where(sel, 0.0, NEG_BIG)
        return carry

    lax.fori_loop(0, n_ch, bias_body, 0)

    for g in range(n_kv_heads):
        qs = jnp.concatenate(
            [q_ref[:, (g * rep + r) * LANES:(g * rep + r + 1) * LANES] for r in range(rep)], axis=0)

        def att_body(j, carry, g=g, qs=qs):
            m, l, acc = carry
            k0 = pl.multiple_of(j * KEY_CHUNK, KEY_CHUNK)
            kb = k_ref[pl.ds(k0, KEY_CHUNK), g * LANES:(g + 1) * LANES]
            vb = v_ref[pl.ds(k0, KEY_CHUNK), g * LANES:(g + 1) * LANES]
            bias = bias_scr[j]
            s = _nt_dot(qs, kb) + jnp.concatenate([bias] * rep, axis=0)
            m_new = jnp.maximum(m, jnp.max(s, axis=1, keepdims=True))
            p = jnp.exp(s - m_new)
            alpha = jnp.exp(m - m_new)
            l = alpha * l + jnp.sum(p, axis=1, keepdims=True)
            acc = alpha * acc + jnp.dot(p.astype(BF16), vb, preferred_element_type=F32)
            return m_new, l, acc

        init = (jnp.full((rep * tq, 1), NEG_BIG, F32), jnp.zeros((rep * tq, 1), F32),
                jnp.zeros((rep * tq, LANES), F32))
        _, l, acc = lax.fori_loop(0, n_ch, att_body, init)
        o = acc / l
        for r in range(rep):
            o_ref[:, (g * rep + r) * LANES:(g * rep + r + 1) * LANES] = o[r * tq:(r + 1) * tq].astype(o_ref.dtype)


def _dsa_attention(qs_all, kiw_f32, k_bf, v_bf, ki_bf, *, batch, q_len, k_len, n_valid_keys, p_len, tq,
                   n_heads, n_kv_heads, n_idx_heads, n_sel):
    width = n_heads * HEAD_DIM
    nq = q_len // tq
    n_chunks = k_len // KEY_CHUNK
    rep = n_heads // n_kv_heads
    kv_w = n_kv_heads * HEAD_DIM
    return pl.pallas_call(
        functools.partial(_dsa_kernel, tq=tq, n_sel=n_sel, p_len=p_len, n_valid_keys=n_valid_keys,
                          n_idx_heads=n_idx_heads, n_kv_heads=n_kv_heads, rep=rep),
        grid=(batch, nq),
        in_specs=[pl.BlockSpec((tq, width), lambda b, i: (b * nq + i, 0)),
                  pl.BlockSpec((tq, width), lambda b, i: (b * nq + i, 1)),
                  pl.BlockSpec((tq, 2 * LANES), lambda b, i: (b * nq + i, 0)),
                  pl.BlockSpec((k_len, kv_w), lambda b, i: (b, 0)),
                  pl.BlockSpec((k_len, kv_w), lambda b, i: (b, 0)),
                  pl.BlockSpec((k_len, LANES), lambda b, i: (b, 0))],
        out_specs=pl.BlockSpec((tq, width), lambda b, i: (b * nq + i, 0)),
        out_shape=jax.ShapeDtypeStruct((batch * q_len, width), BF16),
        scratch_shapes=[pltpu.VMEM((n_chunks, tq, KEY_CHUNK), jnp.int32),
                        pltpu.VMEM((n_chunks, tq, KEY_CHUNK), F32)],
        compiler_params=_params(("arbitrary", "arbitrary")),
        name="dsa_attention",
    )(qs_all, qs_all, kiw_f32, k_bf, v_bf, ki_bf)


def _diff_kernel(dq_ref, dk_ref, dv_ref, lq1_ref, lk1_ref, lq2_ref, lk2_ref, g_ref, o_ref, bias_scr, *,
                 tq, p_len, n_valid_keys, lam_init):
    n_chunks_total = bias_scr.shape[0]
    q_start = p_len + pl.program_id(2) * tq
    n_ch = _num_key_chunks(q_start, tq, n_valid_keys, n_chunks_total)
    dv_w = 2 * HEAD_DIM

    lam = (jnp.exp(jnp.sum(lq1_ref[...] * lk1_ref[...], axis=1, keepdims=True))
           - jnp.exp(jnp.sum(lq2_ref[...] * lk2_ref[...], axis=1, keepdims=True)) + lam_init)

    def bias_body(j, carry):
        vis = _visible(q_start, tq, j * KEY_CHUNK, KEY_CHUNK, n_valid_keys)
        bias_scr[j] = jnp.where(vis, 0.0, NEG_BIG)
        return carry

    lax.fori_loop(0, n_ch, bias_body, 0)

    q0 = dq_ref[:, :LANES]
    q1 = dq_ref[:, LANES:]

    def att_body(j, carry):
        k0 = pl.multiple_of(j * KEY_CHUNK, KEY_CHUNK)
        vb = dv_ref[pl.ds(k0, KEY_CHUNK), :]
        bias = bias_scr[j]
        new = []
        for c, qc in ((0, q0), (1, q1)):
            m, l, acc = carry[3 * c:3 * c + 3]
            kb = dk_ref[pl.ds(k0, KEY_CHUNK), c * LANES:(c + 1) * LANES]
            s = _nt_dot(qc, kb) + bias
            m_new = jnp.maximum(m, jnp.max(s, axis=1, keepdims=True))
            p = jnp.exp(s - m_new)
            alpha = jnp.exp(m - m_new)
            l = alpha * l + jnp.sum(p, axis=1, keepdims=True)
            acc = alpha * acc + jnp.dot(p.astype(BF16), vb, preferred_element_type=F32)
            new += [m_new, l, acc]
        return tuple(new)

    one = (jnp.full((tq, 1), NEG_BIG, F32), jnp.zeros((tq, 1), F32), jnp.zeros((tq, dv_w), F32))
    _, l0, a0, _, l1, a1 = lax.fori_loop(0, n_ch, att_body, one + one)
    o = a0 / l0 - lam * (a1 / l1)
    o = o * lax.rsqrt(jnp.mean(o * o, axis=-1, keepdims=True) + EPS) * g_ref[...]
    o_ref[...] = (o * (1.0 - lam_init)).astype(o_ref.dtype)


def _diff_attention(qs_all, dk_bf, dv_bf, lams, g_subln, *, batch, q_len, k_len, n_valid_keys, p_len, tq,
                    n_heads, lam_init):
    hw = 2 * HEAD_DIM
    width = n_heads * hw
    nq = q_len // tq
    n_chunks = k_len // KEY_CHUNK
    assert qs_all.shape[1] == 3 * width
    dq_col0 = 2 * n_heads
    vec = pl.BlockSpec((1, HEAD_DIM), lambda b, h, i: (0, 0))
    return pl.pallas_call(
        functools.partial(_diff_kernel, tq=tq, p_len=p_len, n_valid_keys=n_valid_keys, lam_init=lam_init),
        grid=(batch, n_heads, nq),
        in_specs=[pl.BlockSpec((tq, hw), lambda b, h, i: (b * nq + i, dq_col0 + h)),
                  pl.BlockSpec((k_len, hw), lambda b, h, i: (b, h)),
                  pl.BlockSpec((k_len, hw), lambda b, h, i: (b, h)),
                  vec, vec, vec, vec,
                  pl.BlockSpec((1, hw), lambda b, h, i: (0, 0))],
        out_specs=pl.BlockSpec((tq, hw), lambda b, h, i: (b * nq + i, h)),
        out_shape=jax.ShapeDtypeStruct((batch * q_len, width), BF16),
        scratch_shapes=[pltpu.VMEM((n_chunks, tq, KEY_CHUNK), F32)],
        compiler_params=_params(("arbitrary", "arbitrary", "arbitrary")),
        name="diff_attention",
    )(qs_all, dk_bf, dv_bf, *[v.reshape(1, HEAD_DIM) for v in lams], g_subln.reshape(1, hw))


def _fill_invisible(x, fill, q_start, n_q, key_start, n_valid_keys, all_keys_valid):
    n_k, n_lanes = x.shape
    kpos = key_start + lax.broadcasted_iota(jnp.int32, (n_k, n_lanes), 0)
    qpos = q_start + jnp.minimum(lax.broadcasted_iota(jnp.int32, (1, n_lanes), 1), n_q - 1)
    x = jnp.where((kpos >> 6) <= (qpos >> 6), x, fill)
    return x if all_keys_valid else jnp.where(kpos < n_valid_keys, x, fill)


def _num_full_chunks(q_start, n_valid_keys):
    return jnp.minimum(((q_start >> 6) + 1) * CHUNK, n_valid_keys) // KEY_CHUNK


def _t_bf16(x, n_lanes):
    x = x.astype(F32)
    if x.shape[0] < n_lanes:
        x = jnp.concatenate([x, jnp.zeros((n_lanes - x.shape[0], x.shape[1]), F32)], axis=0)
    return x.T.astype(BF16)


def _col_partial(x, op):
    parts = [x[r * 8:(r + 1) * 8] for r in range(x.shape[0] // 8)]
    while len(parts) > 1:
        parts = [op(parts[i], parts[i + 1]) for i in range(0, len(parts) - 1, 2)] + parts[len(parts) & ~1:]
    return parts[0]


def _col_reduce(x, op):
    return (jnp.max if op is jnp.maximum else jnp.sum)(_col_partial(x, op), axis=0, keepdims=True)


def _row_to_cols(row):
    return jnp.broadcast_to(row, (LANES, row.shape[1])).T


def _dsa_t_kernel(q_ref, qi_ref, wi_ref, k_ref, vaug_ref, ki_ref, o_ref,
                  qit_scr, qt_scr, key_scr, acc_scr, *,
                  tq, tl, n_sel, p_len, n_valid_keys, all_keys_valid, n_idx_heads, n_kv_heads, rep):
    n_chunks_total = key_scr.shape[0]
    q_start = p_len + pl.program_id(1) * tq
    n_ch = _num_key_chunks(q_start, tq, n_valid_keys, n_chunks_total)
    heads_per_dot = 4

    for h in range(n_idx_heads):
        qit_scr[:, h * tl:(h + 1) * tl] = _t_bf16(qi_ref[:, h * LANES:(h + 1) * LANES], tl)
    for g in range(n_kv_heads):
        for r in range(rep):
            hh = g * rep + r
            qt_scr[g, :, r * tl:(r + 1) * tl] = _t_bf16(q_ref[:, hh * LANES:(hh + 1) * LANES], tl)
    wi = wi_ref[:, LANES:2 * LANES] * (n_idx_heads ** -0.5)
    if tq < tl:
        wi = jnp.concatenate([wi, jnp.zeros((tl - tq, LANES), F32)], axis=0)
    wi_t = wi.T

    def idx_body(j, carry):
        k0 = pl.multiple_of(j * KEY_CHUNK, KEY_CHUNK)
        kib = ki_ref[pl.ds(k0, KEY_CHUNK), :]
        score = jnp.zeros((KEY_CHUNK, tl), F32)
        for h0 in range(0, n_idx_heads, heads_per_dot):
            lg = jnp.dot(kib, qit_scr[:, h0 * tl:(h0 + heads_per_dot) * tl], preferred_element_type=F32)
            for h in range(h0, h0 + heads_per_dot):
                score = score + jnp.maximum(lg[:, (h - h0) * tl:(h - h0 + 1) * tl], 0.0) * wi_t[h:h + 1, :]
        score = _fill_invisible(score, -jnp.inf, q_start, tq, k0, n_valid_keys, all_keys_valid)
        key_scr[j] = _ordered_key(score)
        return carry

    lax.fori_loop(0, n_ch, idx_body, 0)

    def bit_body(b, t):
        cand_u = t | lax.shift_left(jnp.int32(1), 31 - b)
        cand_s = cand_u ^ jnp.int32(INT_MIN)

        def cnt_body(j, cnt):
            ones = jnp.where(key_scr[j] >= cand_s, 1, 0)
            return cnt + _col_partial(ones, jnp.add)

        cnt = lax.fori_loop(0, n_ch, cnt_body, jnp.zeros((8, tl), jnp.int32))
        total = jnp.sum(cnt, axis=0, keepdims=True)
        return jnp.where(total >= n_sel, cand_u, t)

    thr_u = lax.fori_loop(0, 32, bit_body, jnp.zeros((1, tl), jnp.int32))
    thr = jnp.maximum(thr_u ^ jnp.int32(INT_MIN), KEY_NEG_INF + 1)

    acc_scr[...] = jnp.zeros_like(acc_scr)

    def att_body(j, ms):
        k0 = pl.multiple_of(j * KEY_CHUNK, KEY_CHUNK)
        bias = jnp.where(key_scr[j] >= thr, 0.0, NEG_BIG)
        bias = jnp.concatenate([bias] * rep, axis=1)
        new = []
        for g in range(n_kv_heads):
            kb = k_ref[pl.ds(k0, KEY_CHUNK), g * LANES:(g + 1) * LANES]
            s = jnp.dot(kb, qt_scr[g], preferred_element_type=F32) + bias
            m_new = jnp.maximum(ms[g], _col_reduce(s, jnp.maximum))
            p = jnp.exp2(s - m_new).astype(BF16)
            va = vaug_ref[pl.ds(k0, KEY_CHUNK), g * 2 * LANES:(g + 1) * 2 * LANES]
            alpha = _row_to_cols(jnp.exp2(ms[g] - m_new))
            acc_scr[g] = (acc_scr[g] * jnp.concatenate([alpha, alpha], axis=1)
                          + lax.dot_general(p, va, (((0,), (0,)), ((), ())), preferred_element_type=F32))
            new.append(m_new)
        return tuple(new)

    lax.fori_loop(0, n_ch, att_body, tuple(jnp.full((1, rep * tl), NEG_BIG, F32) for _ in range(n_kv_heads)))

    for g in range(n_kv_heads):
        acc = acc_scr[g]
        o = acc[:, :LANES] / acc[:, LANES:]
        for r in range(rep):
            hh = g * rep + r
            o_ref[:, hh * LANES:(hh + 1) * LANES] = o[r * tl:r * tl + tq].astype(o_ref.dtype)


def _dsa_attention_t(qs_all, kiw_f32, k_bf, vaug_bf, ki_bf, *, batch, q_len, k_len, n_valid_keys, p_len, tq,
                     n_heads, n_kv_heads, n_idx_heads, n_sel):
    width = n_heads * HEAD_DIM
    nq = q_len // tq
    tl = max(tq, LANES)
    n_chunks = k_len // KEY_CHUNK
    rep = n_heads // n_kv_heads
    kv_w = n_kv_heads * HEAD_DIM
    return pl.pallas_call(
        functools.partial(_dsa_t_kernel, tq=tq, tl=tl, n_sel=n_sel, p_len=p_len, n_valid_keys=n_valid_keys,
                          all_keys_valid=n_valid_keys == k_len,
                          n_idx_heads=n_idx_heads, n_kv_heads=n_kv_heads, rep=rep),
        grid=(batch, nq),
        in_specs=[pl.BlockSpec((tq, width), lambda b, i: (b * nq + i, 0)),
                  pl.BlockSpec((tq, width), lambda b, i: (b * nq + i, 1)),
                  pl.BlockSpec((tq, 2 * LANES), lambda b, i: (b * nq + i, 0)),
                  pl.BlockSpec((k_len, kv_w), lambda b, i: (b, 0)),
                  pl.BlockSpec((k_len, 2 * kv_w), lambda b, i: (b, 0)),
                  pl.BlockSpec((k_len, LANES), lambda b, i: (b, 0))],
        out_specs=pl.BlockSpec((tq, width), lambda b, i: (b * nq + i, 0)),
        out_shape=jax.ShapeDtypeStruct((batch * q_len, width), BF16),
        scratch_shapes=[pltpu.VMEM((HEAD_DIM, n_idx_heads * tl), BF16),
                        pltpu.VMEM((n_kv_heads, HEAD_DIM, rep * tl), BF16),
                        pltpu.VMEM((n_chunks, KEY_CHUNK, tl), jnp.int32),
                        pltpu.VMEM((n_kv_heads, rep * tl, 2 * LANES), F32)],
        compiler_params=_params(("arbitrary", "arbitrary")),
        name="dsa_attention",
    )(qs_all, qs_all, kiw_f32, k_bf, vaug_bf, ki_bf)


def _diff_t_kernel(dq_ref, dk_ref, dv_ref, lq1_ref, lk1_ref, lq2_ref, lk2_ref, g_ref, o_ref,
                   acc_scr, *, tq, tl, hp, n_chunks_total, p_len, n_valid_keys, all_keys_valid, lam_init):
    q_start = p_len + pl.program_id(2) * tq
    n_ch = _num_key_chunks(q_start, tq, n_valid_keys, n_chunks_total)
    hw = 2 * HEAD_DIM
    n_maps = 2 * hp

    lam = (jnp.exp(jnp.sum(lq1_ref[...] * lk1_ref[...], axis=1, keepdims=True))
           - jnp.exp(jnp.sum(lq2_ref[...] * lk2_ref[...], axis=1, keepdims=True)) + lam_init)
    qt = [_t_bf16(dq_ref[:, c * LANES:(c + 1) * LANES], tl) for c in range(n_maps)]

    acc_scr[...] = jnp.zeros_like(acc_scr)

    def att_body(j, carry, masked):
        k0 = pl.multiple_of(j * KEY_CHUNK, KEY_CHUNK)
        new = []
        for c in range(n_maps):
            m, l = carry[2 * c:2 * c + 2]
            kb = dk_ref[pl.ds(k0, KEY_CHUNK), c * LANES:(c + 1) * LANES]
            vb = dv_ref[pl.ds(k0, KEY_CHUNK), (c // 2) * hw:(c // 2 + 1) * hw]
            s = jnp.dot(kb, qt[c], preferred_element_type=F32)
            if masked:
                s = _fill_invisible(s, NEG_BIG, q_start, tq, k0, n_valid_keys, all_keys_valid)
            m_new = jnp.maximum(m, _col_reduce(s, jnp.maximum))
            p = jnp.exp2(s - m_new)
            alpha = jnp.exp2(m - m_new)
            alpha_c = _row_to_cols(alpha)
            acc_scr[c] = (acc_scr[c] * jnp.concatenate([alpha_c, alpha_c], axis=1)
                          + lax.dot_general(p.astype(BF16), vb, (((0,), (0,)), ((), ())),
                                            preferred_element_type=F32))
            new += [m_new, alpha * l + _col_reduce(p, jnp.add)]
        return tuple(new)

    one = (jnp.full((1, tl), NEG_BIG, F32), jnp.zeros((1, tl), F32))
    n_full = jnp.minimum(_num_full_chunks(q_start, n_valid_keys), n_ch)
    carry = lax.fori_loop(0, n_full, functools.partial(att_body, masked=False), one * n_maps)
    carry = lax.fori_loop(n_full, n_ch, functools.partial(att_body, masked=True), carry)

    def normalised(c):
        l_cols = _row_to_cols(carry[2 * c + 1])
        return acc_scr[c] / jnp.concatenate([l_cols, l_cols], axis=1)

    for h in range(hp):
        o = (normalised(2 * h) - lam * normalised(2 * h + 1))[:tq]
        o = o * lax.rsqrt(jnp.mean(o * o, axis=-1, keepdims=True) + EPS) * g_ref[...]
        o_ref[:, h * hw:(h + 1) * hw] = (o * (1.0 - lam_init)).astype(o_ref.dtype)


def _diff_attention_t(qs_all, dk_bf, dv_bf, lams, g_subln, *, batch, q_len, k_len, n_valid_keys, p_len, tq,
                      n_heads, lam_init):
    hw = 2 * HEAD_DIM
    hp = 2 if n_heads % 2 == 0 else 1
    width = n_heads * hw
    nq = q_len // tq
    tl = max(tq, LANES)
    n_chunks = k_len // KEY_CHUNK
    assert qs_all.shape[1] == 3 * width
    dq_col0 = 2 * n_heads // hp
    vec = pl.BlockSpec((1, HEAD_DIM), lambda b, h, i: (0, 0))
    return pl.pallas_call(
        functools.partial(_diff_t_kernel, tq=tq, tl=tl, hp=hp, n_chunks_total=n_chunks, p_len=p_len,
                          n_valid_keys=n_valid_keys, all_keys_valid=n_valid_keys == k_len, lam_init=lam_init),
        grid=(batch, n_heads // hp, nq),
        in_specs=[pl.BlockSpec((tq, hp * hw), lambda b, h, i: (b * nq + i, dq_col0 + h)),
                  pl.BlockSpec((k_len, hp * hw), lambda b, h, i: (b, h)),
                  pl.BlockSpec((k_len, hp * hw), lambda b, h, i: (b, h)),
                  vec, vec, vec, vec,
                  pl.BlockSpec((1, hw), lambda b, h, i: (0, 0))],
        out_specs=pl.BlockSpec((tq, hp * hw), lambda b, h, i: (b * nq + i, h)),
        out_shape=jax.ShapeDtypeStruct((batch * q_len, width), BF16),
        scratch_shapes=[pltpu.VMEM((2 * hp, tl, hw), F32)],
        compiler_params=_params(("arbitrary", "arbitrary", "arbitrary")),
        name="diff_attention",
    )(qs_all, dk_bf, dv_bf, *[v.reshape(1, HEAD_DIM) for v in lams], g_subln.reshape(1, hw))


def _outproj_kernel(a_ref, d_ref, wa_ref, wd_ref, x_ref, ga_ref, o_ref, *, per_token):
    mix = (jnp.dot(a_ref[...], wa_ref[...], preferred_element_type=F32)
           + jnp.dot(d_ref[...], wd_ref[...], preferred_element_type=F32))
    ga = ga_ref[...] if per_token else ga_ref[0]
    o_ref[...] = x_ref[...] + ga * mix


def _outproj(a_out, d_out, w_out_bf, x2, mod, *, per_token, rows_per_batch, tm, tn):
    t, d = x2.shape
    wa = a_out.shape[1]
    wd = d_out.shape[1]
    assert wa == wd
    cb = 2 * (d // tn)
    return pl.pallas_call(
        functools.partial(_outproj_kernel, per_token=per_token),
        grid=(t // tm, d // tn),
        in_specs=[pl.BlockSpec((tm, wa), lambda m, n: (m, 0)),
                  pl.BlockSpec((tm, wd), lambda m, n: (m, 0)),
                  pl.BlockSpec((wa, tn), lambda m, n: (0, n)),
                  pl.BlockSpec((wd, tn), lambda m, n: (1, n)),
                  pl.BlockSpec((tm, tn), lambda m, n: (m, n)),
                  _mod_spec(mod, per_token, tm, tn, cb, rows_per_batch)],
        out_specs=pl.BlockSpec((tm, tn), lambda m, n: (m, n)),
        out_shape=jax.ShapeDtypeStruct((t, d), F32),
        compiler_params=_params(("arbitrary", "arbitrary")),
        name="outproj",
    )(a_out, d_out, w_out_bf, w_out_bf, x2, mod)


def _top_rows(x, k, with_rank=False):
    tops = []
    rank = jnp.full(x.shape, float(k), F32) if with_rank else None
    for i in range(k):
        mx = jnp.max(x, axis=0, keepdims=True)
        tops.append(mx)
        hit = x == mx
        if with_rank:
            rank = jnp.where(hit, float(i), rank)
        x = jnp.where(hit, -jnp.inf, x)
    return (tops, rank) if with_rank else tops


def _peer_route_kernel(h_ref, wq_ref, keys_ref, cnt_ref, e1_ref, rank_ref, e2_ref, q_scr, s1_scr, top_scr):
    hc = pl.program_id(1)
    c = hc % 2
    half = keys_ref.shape[2]

    @pl.when(hc == 0)
    def _():
        q_scr[...] = jnp.dot(wq_ref[...], h_ref[...], preferred_element_type=F32).astype(BF16)

    q_t = q_scr[pl.ds(pl.multiple_of(hc * half, half), half), :]
    s_t = jnp.dot(keys_ref[0].astype(BF16), q_t, preferred_element_type=F32)

    @pl.when(c == 0)
    def _():
        s1_scr[...] = s_t
        top_scr[...] = jnp.concatenate(_top_rows(s_t, PEER_TOPK), axis=0)

    @pl.when(c == 1)
    def _():
        tops2, rank2 = _top_rows(s_t, PEER_TOPK, with_rank=True)
        tops = jnp.concatenate(tops2, axis=0)
        top1 = top_scr[...]
        cand = jnp.concatenate([top1[0:1, :] + tops]
                               + [top1[i:i + 1, :] + tops[:PEER_TOPK // 2] for i in range(1, PEER_TOPK)], axis=0)
        best = _top_rows(cand, PEER_TOPK)
        m = best[0]
        z = jnp.zeros_like(m)
        for bk in best:
            z = z + jnp.exp(bk - m)
        thr = best[PEER_TOPK - 1]
        s1 = s1_scr[...]
        cnt = jnp.zeros_like(s1)
        for j in range(PEER_TOPK):
            cnt = cnt + jnp.where(s1 + tops2[j] >= thr, 1.0, 0.0)
        e1 = jnp.exp(s1 - top1[0:1, :]) / z
        e2 = jnp.exp(s_t - tops2[0])
        hn = s_t.shape[0] // 2
        cnt_ref[0] = _pack_bf16_pair(cnt, cnt)
        e1_ref[0] = _pack_bf16_pair(e1, e1)
        rank_ref[0] = _pack_bf16_pair(rank2[:hn], rank2[hn:])
        e2_ref[0] = _pack_bf16_pair(e2[:hn], e2[hn:])


def _pack_bf16_pair(lo, hi):
    def bits(x):
        b = pltpu.bitcast(x, jnp.uint32)
        return (b + jnp.uint32(0x7FFF) + ((b >> 16) & jnp.uint32(1))) >> 16
    return bits(lo) | (bits(hi) << 16)


def _peer_route(h_t, wq_t_bf, sub_keys, *, tm):
    d, t = h_t.shape
    heads, _, n_keys, half = sub_keys.shape
    keys2 = sub_keys.reshape(heads * 2, n_keys, half)
    a_spec = pl.BlockSpec((1, n_keys, tm), lambda m, hc: (hc // 2, 0, m))
    b_spec = pl.BlockSpec((1, n_keys // 2, tm), lambda m, hc: (hc // 2, 0, m))
    a_tab = jax.ShapeDtypeStruct((heads, n_keys, t), jnp.uint32)
    b_tab = jax.ShapeDtypeStruct((heads, n_keys // 2, t), jnp.uint32)
    return pl.pallas_call(
        _peer_route_kernel,
        grid=(t // tm, heads * 2),
        in_specs=[pl.BlockSpec((d, tm), lambda m, hc: (0, m)),
                  pl.BlockSpec((heads * 2 * half, d), lambda m, hc: (0, 0), pipeline_mode=pl.Buffered(1)),
                  pl.BlockSpec((1, n_keys, half), lambda m, hc: (hc, 0, 0))],
        out_specs=[a_spec, a_spec, b_spec, b_spec],
        out_shape=[a_tab, a_tab, b_tab, b_tab],
        scratch_shapes=[pltpu.VMEM((heads * 2 * half, tm), BF16), pltpu.VMEM((n_keys, tm), F32),
                        pltpu.VMEM((PEER_TOPK, tm), F32)],
        compiler_params=_params(("arbitrary", "arbitrary")),
        name="peer_route",
    )(h_t, wq_t_bf, keys2)


def _gelu_tanh(x):
    return 0.5 * x * (1.0 + jnp.tanh(0.7978845608028654 * (x + 0.044715 * x * x * x)))


def _rows_bf16(row_words, n_rows):
    tile = jnp.broadcast_to(row_words, (8, row_words.shape[1]))
    return pltpu.bitcast(jnp.concatenate([tile] * (n_rows // 16), axis=0), BF16)


def _peer_ffn_kernel(h_ref, u_ref, v_ref, cnt_ref, e1_ref, rank_ref, e2_ref, o_ref, *, n_keys):
    e = pl.program_id(1)
    te, tm = u_ref.shape[0], h_ref.shape[1]
    heads = cnt_ref.shape[0]
    a0 = e * (te // n_keys)

    @pl.when(e == 0)
    def _():
        o_ref[...] = jnp.zeros_like(o_ref)

    act = jnp.dot(u_ref[...], h_ref[...], preferred_element_type=F32)
    strip = min(tm, LANES)
    hn = n_keys // 2
    w_rows = [[None] * (tm // strip) for _ in range(2 * te // n_keys)]
    for ai in range(te // n_keys):
        cnt_a = [cnt_ref[h, pl.ds(a0 + ai, 1), :] for h in range(heads)]
        e1_a = [e1_ref[h, pl.ds(a0 + ai, 1), :] for h in range(heads)]
        for c in range(tm // strip):
            cols = slice(c * strip, (c + 1) * strip)
            gate = jnp.zeros((n_keys, strip), BF16)
            for h in range(heads):
                cnt_b = _rows_bf16(cnt_a[h][:, cols], n_keys)
                e1_b = _rows_bf16(e1_a[h][:, cols], n_keys)
                keep = pltpu.bitcast(rank_ref[h, :, cols], BF16) < cnt_b
                gate = gate + jnp.where(keep, pltpu.bitcast(e2_ref[h, :, cols], BF16) * e1_b,
                                        jnp.zeros_like(e1_b))
            words = pltpu.bitcast(gate, jnp.uint32)
            halves = (pltpu.bitcast(words << 16, F32), pltpu.bitcast(words & jnp.uint32(0xFFFF0000), F32))
            for k, g in enumerate(halves):
                rows = slice(ai * n_keys + k * hn, ai * n_keys + (k + 1) * hn)
                w_rows[2 * ai + k][c] = (g * _gelu_tanh(act[rows, cols])).astype(BF16)
    w = jnp.concatenate([r[0] if len(r) == 1 else jnp.concatenate(r, axis=1) for r in w_rows], axis=0)
    o_ref[...] += lax.dot_general(w, v_ref[...], (((0,), (0,)), ((), ())),
                                  preferred_element_type=F32)


def _peer_ffn(h_t, u_bf, v_bf, cnt, e1, rank, e2, *, tm, te):
    d, t = h_t.shape
    n_exp = u_bf.shape[0]
    n_blocks = n_exp // te
    heads, n_keys, _ = cnt.shape
    once = pl.Buffered(1)
    a_spec = pl.BlockSpec((heads, n_keys, tm), lambda m, e: (0, 0, m), pipeline_mode=once)
    b_spec = pl.BlockSpec((heads, n_keys // 2, tm), lambda m, e: (0, 0, m), pipeline_mode=once)
    return pl.pallas_call(
        functools.partial(_peer_ffn_kernel, n_keys=n_keys),
        grid=(t // tm, n_blocks),
        in_specs=[pl.BlockSpec((d, tm), lambda m, e: (0, m), pipeline_mode=once),
                  pl.BlockSpec((te, d), lambda m, e: (e, 0)),
                  pl.BlockSpec((te, d), lambda m, e: (e, 0)),
                  a_spec, a_spec, b_spec, b_spec],
        out_specs=pl.BlockSpec((tm, d), lambda m, e: (m, 0)),
        out_shape=jax.ShapeDtypeStruct((t, d), F32),
        compiler_params=_params(("arbitrary", "arbitrary")),
        name="peer_ffn",
    )(h_t, u_bf, v_bf, cnt, e1, rank, e2)


def _final_kernel(x_ref, p_ref, ga_ref, g_ref, o_ref, *, per_token, normalize):
    ga = ga_ref[...] if per_token else ga_ref[0]
    x = x_ref[...] + ga * p_ref[...]
    if normalize:
        x = x * lax.rsqrt(jnp.mean(x * x, axis=-1, keepdims=True) + EPS) * g_ref[...]
    o_ref[...] = x


def _final(x1, peer, mod, g_final, *, per_token, rows_per_batch, tm, normalize):
    t, d = x1.shape
    row = pl.BlockSpec((tm, d), lambda m: (m, 0))
    return pl.pallas_call(
        functools.partial(_final_kernel, per_token=per_token, normalize=normalize),
        grid=(t // tm,),
        in_specs=[row, row, _mod_spec(mod, per_token, tm, d, 5, rows_per_batch),
                  pl.BlockSpec((1, d), lambda m: (0, 0))],
        out_specs=row,
        out_shape=jax.ShapeDtypeStruct((t, d), F32),
        compiler_params=_params(("arbitrary",)),
        name="final",
    )(x1, peer, mod, g_final.reshape(1, d))


def _rope_tables(pos):
    half = HEAD_DIM // 2
    inv = ROPE_THETA ** (-jnp.arange(half, dtype=F32) / half)
    ang = pos.astype(F32)[:, None] * inv[None, :]
    cos, sin = jnp.cos(ang), jnp.sin(ang)
    return jnp.concatenate([cos, cos], axis=1), jnp.concatenate([-sin, sin], axis=1)


def _pick_tile(n, prefs):
    for p in prefs:
        if n % p == 0:
            return p
    return n


def _layer(x, mod_rows, past, layer, last_layer, w, dims):
    b, s, d = x.shape
    t = b * s
    n_heads, n_kv, n_idx, n_diff = dims
    p_len = 0 if past is None else past[0].shape[1]
    n_keys_valid = p_len + s
    n_sel = min(DSA_TOPK, n_keys_valid // 4)
    x2 = x.reshape(t, d)

    per_token = s % 256 != 0
    if per_token:
        mod = jnp.repeat(mod_rows, s, axis=0)
    else:
        mod = mod_rows.reshape(b, 1, 6 * d)
    tm_big = _pick_tile(t if per_token else s, (1024, 512, 256, 128))
    tm_mid = _pick_tile(t if per_token else s, (512, 256, 128))
    tm_small = _pick_tile(t if per_token else s, (256, 128))

    pos = p_len + jnp.arange(s)
    cos, sin = _rope_tables(pos)
    if per_token:
        cos, sin = jnp.tile(cos, (b, 1)), jnp.tile(sin, (b, 1))

    h = _norm_mod(x2, w["g_norm_mix"], mod, which=1, per_token=per_token, rows_per_batch=s, tm=tm_mid,
                  transpose=False)

    proj = functools.partial(_proj, h, cos=cos, sin=sin, tm=tm_big)
    wq = n_heads * HEAD_DIM
    (qs_all,) = proj(w["w_qs"], rope_cols=512, scale=HEAD_DIM ** -0.5 * math.log2(math.e), out_f32=False,
                     out_bf16=True, tn=512, name="proj_q")
    k_f, k_b = proj(w["w_k"], rope_cols=512, out_f32=True, out_bf16=True, tn=512, name="proj_k")
    v_f, v_b = proj(w["w_v"], rope_cols=0, out_f32=True, out_bf16=True, tn=512, name="proj_v")
    dk_f, dk_b = proj(w["w_dk"], rope_cols=512, out_f32=True, out_bf16=True, tn=512, name="proj_dk")
    dv_f, dv_b = proj(w["w_dv"], rope_cols=0, out_f32=True, out_bf16=True, tn=512, name="proj_dv")
    kiw_f, kiw_b = proj(w["w_kiw"], rope_cols=LANES, out_f32=True, out_bf16=True, tn=2 * LANES, name="proj_kiw")

    ki_f = kiw_f[:, :LANES]
    new_rows = (k_f.reshape(b, s, n_kv, HEAD_DIM), v_f.reshape(b, s, n_kv, HEAD_DIM), ki_f.reshape(b, s, LANES),
                dk_f.reshape(b, s, n_diff, 2, HEAD_DIM), dv_f.reshape(b, s, n_diff, 2 * HEAD_DIM))

    if past is None:
        k_len = s
        k_all, v_all, ki_all, dk_all, dv_all = k_b, v_b, kiw_b, dk_b, dv_b
    else:
        k_len = -(-n_keys_valid // KEY_CHUNK) * KEY_CHUNK

        def join(cache, new, width):
            new = new.reshape(b, s, -1)[:, :, :width]
            both = jnp.concatenate([cache.reshape(b, p_len, width).astype(BF16), new], axis=1)
            both = jnp.pad(both, ((0, 0), (0, k_len - n_keys_valid), (0, 0)))
            return both.reshape(b * k_len, width)

        k_all = join(past[0], k_b, n_kv * HEAD_DIM)
        v_all = join(past[1], v_b, n_kv * HEAD_DIM)
        ki_all = join(past[2], kiw_b, LANES)
        dk_all = join(past[3], dk_b, n_diff * 2 * HEAD_DIM)
        dv_all = join(past[4], dv_b, n_diff * 2 * HEAD_DIM)

    v3 = v_all.reshape(-1, n_kv, HEAD_DIM)
    vaug_all = jnp.concatenate([v3, jnp.ones_like(v3)], axis=2).reshape(-1, n_kv * 2 * HEAD_DIM)
    tq_a = _pick_tile(s, (128, 64, 32))
    a_out = _dsa_attention_t(qs_all, kiw_f, k_all, vaug_all, ki_all, batch=b, q_len=s, k_len=k_len,
                             n_valid_keys=n_keys_valid, p_len=p_len, tq=tq_a, n_heads=n_heads, n_kv_heads=n_kv,
                             n_idx_heads=n_idx, n_sel=n_sel)
    tq_d = _pick_tile(s, (256, 128, 64, 32))
    d_out = _diff_attention_t(qs_all, dk_all, dv_all, w["lams"], w["g_subln"], batch=b, q_len=s, k_len=k_len,
                              n_valid_keys=n_keys_valid, p_len=p_len, tq=tq_d, n_heads=n_diff,
                              lam_init=_lambda_init(layer))

    x1 = _outproj(a_out, d_out, w["w_out"], x2, mod, per_token=per_token, rows_per_batch=s, tm=tm_big, tn=512)

    h2_t = _norm_mod(x1, w["g_norm_ffn"], mod, which=2, per_token=per_token, rows_per_batch=s, tm=tm_mid,
                     transpose=True)
    cnt, e1, rank, e2 = _peer_route(h2_t, w["peer_wq_t"], w["peer_keys"], tm=tm_mid)
    peer = _peer_ffn(h2_t, w["peer_u"], w["peer_v"], cnt, e1, rank, e2, tm=tm_mid, te=512)
    x_out = _final(x1, peer, mod, w["g_final"], per_token=per_token, rows_per_batch=s, tm=tm_small,
                   normalize=last_layer)
    return x_out.reshape(b, s, d), new_rows


def kernel(x_prompt, x_sample, cache_dsa_k, cache_dsa_v, cache_idx_k, cache_diff_k, cache_diff_v, c_prompt, c_sample, w_ada, b_ada, g_norm_mix, g_norm_ffn, w_in, diff_lambda_q1, diff_lambda_k1, diff_lambda_q2, diff_lambda_k2, g_diff_subln, w_out, peer_w_query, peer_sub_keys, peer_u, peer_v, g_final):
    depth = w_in.shape[0]
    d = x_prompt.shape[-1]
    bp, bs = x_prompt.shape[0], x_sample.shape[0]
    n_kv = cache_dsa_k.shape[3]
    n_diff = cache_diff_k.shape[3]
    n_heads, n_idx_heads = DSA_HEADS, IDX_HEADS
    qw, kvw, dw = n_heads * HEAD_DIM, n_kv * HEAD_DIM, n_diff * 2 * HEAD_DIM
    assert w_in.shape[2] == qw + 2 * kvw + n_idx_heads * LANES + LANES + n_idx_heads + 3 * dw
    dims = (n_heads, n_kv, n_idx_heads, n_diff)

    hp, hs = x_prompt, x_sample
    rows_p, rows_s = [], []
    n_c = bp + bs
    c_pad = jnp.pad(jnp.concatenate([c_prompt, c_sample], axis=0), ((0, (-n_c) % 16), (0, 0)))
    for l in range(depth):
        mod_all = _adaln(c_pad, w_ada[l], b_ada[l])
        win = w_in[l]
        o = 0
        seg = {}
        for name, width in (("q", qw), ("k", kvw), ("v", kvw), ("qi", n_idx_heads * LANES), ("ki", LANES),
                            ("wi", n_idx_heads), ("dq", dw), ("dk", dw), ("dv", dw)):
            seg[name] = win[:, o:o + width]
            o += width
        weights = {
            "w_qs": jnp.concatenate([seg["q"], seg["qi"], seg["dq"]], axis=1).astype(BF16),
            "w_k": seg["k"].astype(BF16), "w_v": seg["v"].astype(BF16),
            "w_dk": seg["dk"].astype(BF16), "w_dv": seg["dv"].astype(BF16),
            "w_kiw": jnp.concatenate([seg["ki"], seg["wi"], jnp.zeros((d, LANES - n_idx_heads), F32)],
                                     axis=1).astype(BF16),
            "w_out": w_out[l].astype(BF16),
            "peer_wq_t": peer_w_query[l].T.astype(BF16),
            "peer_keys": peer_sub_keys[l],
            "peer_u": peer_u[l].astype(BF16), "peer_v": peer_v[l].astype(BF16),
            "g_norm_mix": g_norm_mix[l], "g_norm_ffn": g_norm_ffn[l], "g_subln": g_diff_subln[l],
            "lams": (diff_lambda_q1[l], diff_lambda_k1[l], diff_lambda_q2[l], diff_lambda_k2[l]),
            "g_final": g_final,
        }
        last = l == depth - 1
        hp, rp = _layer(hp, mod_all[:bp], None, l, last, weights, dims)
        past = (cache_dsa_k[l], cache_dsa_v[l], cache_idx_k[l], cache_diff_k[l], cache_diff_v[l])
        hs, rs = _layer(hs, mod_all[bp:bp + bs], past, l, last, weights, dims)
        rows_p.append(rp)
        rows_s.append(rs)
    stack = lambda rows, i: jnp.stack([r[i] for r in rows])
    return (hp, hs) + tuple(stack(rows_p, i) for i in range(5)) + tuple(stack(rows_s, i) for i in range(5))
```

```python
import functools
import math

import jax
import jax.numpy as jnp
from jax import lax
from jax.experimental import pallas as pl
from jax.experimental.pallas import tpu as pltpu

CHUNK = 64
HEAD_DIM = 128
ROPE_THETA = 10000.0
EPS = 1e-6
DSA_HEADS = 16
IDX_HEADS = 16
DSA_TOPK = 256
PEER_TOPK = 16
LANES = 128
KEY_CHUNK = 512
NEG_BIG = -1e30
INT_MIN = -(2 ** 31)
KEY_NEG_INF = INT_MIN + 0x7FFFFF
VMEM_LIMIT = 56 * 1024 * 1024

BF16 = jnp.bfloat16
F32 = jnp.float32


def _params(sem, vmem=VMEM_LIMIT):
    return pltpu.CompilerParams(dimension_semantics=sem, vmem_limit_bytes=vmem)


def _lambda_init(layer):
    return 0.8 - 0.6 * math.exp(-0.3 * layer)


def _nt_dot(a, b):
    return lax.dot_general(a, b, (((1,), (1,)), ((), ())), preferred_element_type=F32)


def _adaln_kernel(c_ref, w_ref, b_ref, o_ref):
    c = c_ref[...]
    a = (c * jax.nn.sigmoid(c)).astype(BF16)
    o_ref[...] = jnp.dot(a, w_ref[...].astype(BF16), preferred_element_type=F32) + b_ref[...]


def _adaln(c_pad, w_ada, b_ada):
    bp, d = c_pad.shape
    n = w_ada.shape[1]
    tn = 512
    return pl.pallas_call(
        _adaln_kernel,
        grid=(n // tn,),
        in_specs=[pl.BlockSpec((bp, d), lambda j: (0, 0)),
                  pl.BlockSpec((d, tn), lambda j: (0, j)),
                  pl.BlockSpec((1, tn), lambda j: (0, j))],
        out_specs=pl.BlockSpec((bp, tn), lambda j: (0, j)),
        out_shape=jax.ShapeDtypeStruct((bp, n), F32),
        compiler_params=_params(("arbitrary",)),
        name="adaln",
    )(c_pad, w_ada, b_ada.reshape(1, n))


def _norm_mod_kernel(x_ref, g_ref, sc_ref, sh_ref, o_ref, *, per_token, transpose):
    x = x_ref[...]
    y = x * lax.rsqrt(jnp.mean(x * x, axis=-1, keepdims=True) + EPS) * g_ref[...]
    sc = sc_ref[...] if per_token else sc_ref[0]
    sh = sh_ref[...] if per_token else sh_ref[0]
    h = y * (1.0 + sc) + sh
    if transpose:
        o_ref[...] = h.T.astype(o_ref.dtype)
    else:
        o_ref[...] = h.astype(o_ref.dtype)


def _mod_spec(mod, per_token, tm, tn, col_block, rows_per_batch):
    if per_token:
        return pl.BlockSpec((tm, tn), lambda m, n=0, cb=col_block: (m, cb + n))
    tiles_per_batch = rows_per_batch // tm
    return pl.BlockSpec((1, 1, tn), lambda m, n=0, cb=col_block: (m // tiles_per_batch, 0, cb + n))


def _norm_mod(x2, g, mod, *, which, per_token, rows_per_batch, tm, transpose):
    t, d = x2.shape
    sh_blk, sc_blk = (0, 1) if which == 1 else (3, 4)
    out_shape = (d, t) if transpose else (t, d)
    out_spec = pl.BlockSpec((d, tm), lambda m: (0, m)) if transpose else pl.BlockSpec((tm, d), lambda m: (m, 0))
    return pl.pallas_call(
        functools.partial(_norm_mod_kernel, per_token=per_token, transpose=transpose),
        grid=(t // tm,),
        in_specs=[pl.BlockSpec((tm, d), lambda m: (m, 0)),
                  pl.BlockSpec((1, d), lambda m: (0, 0)),
                  _mod_spec(mod, per_token, tm, d, sc_blk, rows_per_batch),
                  _mod_spec(mod, per_token, tm, d, sh_blk, rows_per_batch)],
        out_specs=out_spec,
        out_shape=jax.ShapeDtypeStruct(out_shape, BF16),
        compiler_params=_params(("arbitrary",)),
        name="norm_mod_t" if transpose else "norm_mod",
    )(x2, g.reshape(1, d), mod, mod)


def _rope_tile(acc, cos, sin, n_chunks):
    outs = []
    for j in range(n_chunks):
        xj = acc[:, j * LANES:(j + 1) * LANES]
        outs.append(xj * cos + pltpu.roll(xj, LANES // 2, axis=1) * sin)
    return outs[0] if n_chunks == 1 else jnp.concatenate(outs, axis=1)


def _proj_kernel(*refs, rope_cols, scale, out_f32, out_bf16):
    h_ref, w_ref, cos_ref, sin_ref = refs[:4]
    outs = refs[4:]
    acc = jnp.dot(h_ref[...], w_ref[...], preferred_element_type=F32)
    tn = acc.shape[1]
    if rope_cols:
        roped = _rope_tile(acc[:, :rope_cols], cos_ref[...], sin_ref[...], rope_cols // LANES)
        acc = roped if rope_cols == tn else jnp.concatenate([roped, acc[:, rope_cols:]], axis=1)
    k = 0
    if out_f32:
        outs[k][...] = acc
        k += 1
    if out_bf16:
        outs[k][...] = (acc * scale if scale != 1.0 else acc).astype(BF16)


def _proj(h, w, cos, sin, *, col0, n, rope_cols, scale=1.0, out_f32, out_bf16, tm, tn, name):
    t, d = h.shape
    assert rope_cols in (0, tn) or n == tn
    assert col0 % tn == 0 and n % tn == 0
    j0 = col0 // tn
    pos_tiles = cos.shape[0] // tm
    out_shape, out_specs = [], []
    for want, dt in ((out_f32, F32), (out_bf16, BF16)):
        if want:
            out_shape.append(jax.ShapeDtypeStruct((t, n), dt))
            out_specs.append(pl.BlockSpec((tm, tn), lambda m, j: (m, j)))
    return pl.pallas_call(
        functools.partial(_proj_kernel, rope_cols=rope_cols, scale=scale, out_f32=out_f32, out_bf16=out_bf16),
        grid=(t // tm, n // tn),
        in_specs=[pl.BlockSpec((tm, d), lambda m, j: (m, 0)),
                  pl.BlockSpec((d, tn), lambda m, j: (0, j0 + j)),
                  pl.BlockSpec((tm, LANES), lambda m, j: (m % pos_tiles, 0)),
                  pl.BlockSpec((tm, LANES), lambda m, j: (m % pos_tiles, 0))],
        out_specs=out_specs,
        out_shape=out_shape,
        compiler_params=_params(("arbitrary", "arbitrary")),
        name=name,
    )(h, w, cos, sin)


def _visible(q_start, n_q, key_start, n_k, n_valid_keys):
    qpos = q_start + lax.broadcasted_iota(jnp.int32, (n_q, 1), 0)
    kpos = key_start + lax.broadcasted_iota(jnp.int32, (1, n_k), 1)
    return ((kpos >> 6) <= (qpos >> 6)) & (kpos < n_valid_keys)


def _num_key_chunks(q_start, n_q, n_valid_keys, n_chunks_total):
    last_visible = (((q_start + n_q - 1) >> 6) + 1) * CHUNK
    last_visible = jnp.minimum(last_visible, n_valid_keys)
    return jnp.minimum((last_visible + KEY_CHUNK - 1) // KEY_CHUNK, n_chunks_total)


def _ordered_key(x):
    b = pltpu.bitcast(x, jnp.int32)
    return b ^ ((b >> 31) & jnp.int32(0x7FFFFFFF))


def _dsa_kernel(q_ref, qi_ref, wi_ref, k_ref, v_ref, ki_ref, o_ref, key_scr, bias_scr, *,
                tq, n_sel, p_len, n_valid_keys, n_idx_heads, n_kv_heads, rep):
    n_chunks_total = key_scr.shape[0]
    q_start = p_len + pl.program_id(1) * tq
    n_ch = _num_key_chunks(q_start, tq, n_valid_keys, n_chunks_total)

    def idx_body(j, carry):
        k0 = pl.multiple_of(j * KEY_CHUNK, KEY_CHUNK)
        kib = ki_ref[pl.ds(k0, KEY_CHUNK), :]
        score = jnp.zeros((tq, KEY_CHUNK), F32)
        wi = wi_ref[:, LANES:2 * LANES] * (n_idx_heads ** -0.5)
        for h in range(n_idx_heads):
            lg = _nt_dot(qi_ref[:, h * LANES:(h + 1) * LANES], kib)
            score = score + jnp.maximum(lg, 0.0) * wi[:, h:h + 1]
        vis = _visible(q_start, tq, k0, KEY_CHUNK, n_valid_keys)
        key_scr[j] = _ordered_key(jnp.where(vis, score, -jnp.inf))
        return carry

    lax.fori_loop(0, n_ch, idx_body, 0)

    def bit_body(b, t):
        cand_u = t | lax.shift_left(jnp.int32(1), 31 - b)
        cand_s = cand_u ^ jnp.int32(INT_MIN)

        def cnt_body(j, cnt):
            ones = jnp.where(key_scr[j] >= cand_s, 1, 0)
            for c in range(KEY_CHUNK // LANES):
                cnt = cnt + ones[:, c * LANES:(c + 1) * LANES]
            return cnt

        cnt = lax.fori_loop(0, n_ch, cnt_body, jnp.zeros((tq, LANES), jnp.int32))
        total = jnp.sum(cnt, axis=1, keepdims=True)
        return jnp.where(total >= n_sel, cand_u, t)

    thr_u = lax.fori_loop(0, 32, bit_body, jnp.zeros((tq, 1), jnp.int32))
    thr = thr_u ^ jnp.int32(INT_MIN)

    def bias_body(j, carry):
        k0 = j * KEY_CHUNK
        vis = _visible(q_start, tq, k0, KEY_CHUNK, n_valid_keys)
        sel = (key_scr[j] >= thr) & vis
        bias_scr[j] = jnp.where(sel, 0.0, NEG_BIG)
        return carry

    lax.fori_loop(0, n_ch, bias_body, 0)

    for g in range(n_kv_heads):
        qs = jnp.concatenate(
            [q_ref[:, (g * rep + r) * LANES:(g * rep + r + 1) * LANES] for r in range(rep)], axis=0)

        def att_body(j, carry, g=g, qs=qs):
            m, l, acc = carry
            k0 = pl.multiple_of(j * KEY_CHUNK, KEY_CHUNK)
            kb = k_ref[pl.ds(k0, KEY_CHUNK), g * LANES:(g + 1) * LANES]
            vb = v_ref[pl.ds(k0, KEY_CHUNK), g * LANES:(g + 1) * LANES]
            bias = bias_scr[j]
            s = _nt_dot(qs, kb) + jnp.concatenate([bias] * rep, axis=0)
            m_new = jnp.maximum(m, jnp.max(s, axis=1, keepdims=True))
            p = jnp.exp(s - m_new)
            alpha = jnp.exp(m - m_new)
            l = alpha * l + jnp.sum(p, axis=1, keepdims=True)
            acc = alpha * acc + jnp.dot(p.astype(BF16), vb, preferred_element_type=F32)
            return m_new, l, acc

        init = (jnp.full((rep * tq, 1), NEG_BIG, F32), jnp.zeros((rep * tq, 1), F32),
                jnp.zeros((rep * tq, LANES), F32))
        _, l, acc = lax.fori_loop(0, n_ch, att_body, init)
        o = acc / l
        for r in range(rep):
            o_ref[:, (g * rep + r) * LANES:(g * rep + r + 1) * LANES] = o[r * tq:(r + 1) * tq].astype(o_ref.dtype)


def _dsa_attention(qs_all, kiw_f32, k_bf, v_bf, ki_bf, *, batch, q_len, k_len, n_valid_keys, p_len, tq,
                   n_heads, n_kv_heads, n_idx_heads, n_sel):
    width = n_heads * HEAD_DIM
    nq = q_len // tq
    n_chunks = k_len // KEY_CHUNK
    rep = n_heads // n_kv_heads
    kv_w = n_kv_heads * HEAD_DIM
    return pl.pallas_call(
        functools.partial(_dsa_kernel, tq=tq, n_sel=n_sel, p_len=p_len, n_valid_keys=n_valid_keys,
                          n_idx_heads=n_idx_heads, n_kv_heads=n_kv_heads, rep=rep),
        grid=(batch, nq),
        in_specs=[pl.BlockSpec((tq, width), lambda b, i: (b * nq + i, 0)),
                  pl.BlockSpec((tq, width), lambda b, i: (b * nq + i, 1)),
                  pl.BlockSpec((tq, 2 * LANES), lambda b, i: (b * nq + i, 0)),
                  pl.BlockSpec((k_len, kv_w), lambda b, i: (b, 0)),
                  pl.BlockSpec((k_len, kv_w), lambda b, i: (b, 0)),
                  pl.BlockSpec((k_len, LANES), lambda b, i: (b, 0))],
        out_specs=pl.BlockSpec((tq, width), lambda b, i: (b * nq + i, 0)),
        out_shape=jax.ShapeDtypeStruct((batch * q_len, width), BF16),
        scratch_shapes=[pltpu.VMEM((n_chunks, tq, KEY_CHUNK), jnp.int32),
                        pltpu.VMEM((n_chunks, tq, KEY_CHUNK), F32)],
        compiler_params=_params(("arbitrary", "arbitrary")),
        name="dsa_attention",
    )(qs_all, qs_all, kiw_f32, k_bf, v_bf, ki_bf)


def _diff_kernel(dq_ref, dk_ref, dv_ref, lq1_ref, lk1_ref, lq2_ref, lk2_ref, g_ref, o_ref, bias_scr, *,
                 tq, p_len, n_valid_keys, lam_init):
    n_chunks_total = bias_scr.shape[0]
    q_start = p_len + pl.program_id(2) * tq
    n_ch = _num_key_chunks(q_start, tq, n_valid_keys, n_chunks_total)
    dv_w = 2 * HEAD_DIM

    lam = (jnp.exp(jnp.sum(lq1_ref[...] * lk1_ref[...], axis=1, keepdims=True))
           - jnp.exp(jnp.sum(lq2_ref[...] * lk2_ref[...], axis=1, keepdims=True)) + lam_init)

    def bias_body(j, carry):
        vis = _visible(q_start, tq, j * KEY_CHUNK, KEY_CHUNK, n_valid_keys)
        bias_scr[j] = jnp.where(vis, 0.0, NEG_BIG)
        return carry

    lax.fori_loop(0, n_ch, bias_body, 0)

    q0 = dq_ref[:, :LANES]
    q1 = dq_ref[:, LANES:]

    def att_body(j, carry):
        k0 = pl.multiple_of(j * KEY_CHUNK, KEY_CHUNK)
        vb = dv_ref[pl.ds(k0, KEY_CHUNK), :]
        bias = bias_scr[j]
        new = []
        for c, qc in ((0, q0), (1, q1)):
            m, l, acc = carry[3 * c:3 * c + 3]
            kb = dk_ref[pl.ds(k0, KEY_CHUNK), c * LANES:(c + 1) * LANES]
            s = _nt_dot(qc, kb) + bias
            m_new = jnp.maximum(m, jnp.max(s, axis=1, keepdims=True))
            p = jnp.exp(s - m_new)
            alpha = jnp.exp(m - m_new)
            l = alpha * l + jnp.sum(p, axis=1, keepdims=True)
            acc = alpha * acc + jnp.dot(p.astype(BF16), vb, preferred_element_type=F32)
            new += [m_new, l, acc]
        return tuple(new)

    one = (jnp.full((tq, 1), NEG_BIG, F32), jnp.zeros((tq, 1), F32), jnp.zeros((tq, dv_w), F32))
    _, l0, a0, _, l1, a1 = lax.fori_loop(0, n_ch, att_body, one + one)
    o = a0 / l0 - lam * (a1 / l1)
    o = o * lax.rsqrt(jnp.mean(o * o, axis=-1, keepdims=True) + EPS) * g_ref[...]
    o_ref[...] = (o * (1.0 - lam_init)).astype(o_ref.dtype)


def _diff_attention(qs_all, dk_bf, dv_bf, lams, g_subln, *, batch, q_len, k_len, n_valid_keys, p_len, tq,
                    n_heads, lam_init):
    hw = 2 * HEAD_DIM
    width = n_heads * hw
    nq = q_len // tq
    n_chunks = k_len // KEY_CHUNK
    assert qs_all.shape[1] == 3 * width
    dq_col0 = 2 * n_heads
    vec = pl.BlockSpec((1, HEAD_DIM), lambda b, h, i: (0, 0))
    return pl.pallas_call(
        functools.partial(_diff_kernel, tq=tq, p_len=p_len, n_valid_keys=n_valid_keys, lam_init=lam_init),
        grid=(batch, n_heads, nq),
        in_specs=[pl.BlockSpec((tq, hw), lambda b, h, i: (b * nq + i, dq_col0 + h)),
                  pl.BlockSpec((k_len, hw), lambda b, h, i: (b, h)),
                  pl.BlockSpec((k_len, hw), lambda b, h, i: (b, h)),
                  vec, vec, vec, vec,
                  pl.BlockSpec((1, hw), lambda b, h, i: (0, 0))],
        out_specs=pl.BlockSpec((tq, hw), lambda b, h, i: (b * nq + i, h)),
        out_shape=jax.ShapeDtypeStruct((batch * q_len, width), BF16),
        scratch_shapes=[pltpu.VMEM((n_chunks, tq, KEY_CHUNK), F32)],
        compiler_params=_params(("arbitrary", "arbitrary", "arbitrary")),
        name="diff_attention",
    )(qs_all, dk_bf, dv_bf, *[v.reshape(1, HEAD_DIM) for v in lams], g_subln.reshape(1, hw))


def _fill_invisible(x, fill, q_start, n_q, key_start, n_valid_keys, all_keys_valid):
    n_k, n_lanes = x.shape
    kpos = key_start + lax.broadcasted_iota(jnp.int32, (n_k, n_lanes), 0)
    qpos = q_start + jnp.minimum(lax.broadcasted_iota(jnp.int32, (1, n_lanes), 1), n_q - 1)
    x = jnp.where((kpos >> 6) <= (qpos >> 6), x, fill)
    return x if all_keys_valid else jnp.where(kpos < n_valid_keys, x, fill)


def _num_full_chunks(q_start, n_valid_keys):
    return jnp.minimum(((q_start >> 6) + 1) * CHUNK, n_valid_keys) // KEY_CHUNK


def _t_bf16(x, n_lanes):
    x = x.astype(F32)
    if x.shape[0] < n_lanes:
        x = jnp.concatenate([x, jnp.zeros((n_lanes - x.shape[0], x.shape[1]), F32)], axis=0)
    return x.T.astype(BF16)


def _col_partial(x, op):
    parts = [x[r * 8:(r + 1) * 8] for r in range(x.shape[0] // 8)]
    while len(parts) > 1:
        parts = [op(parts[i], parts[i + 1]) for i in range(0, len(parts) - 1, 2)] + parts[len(parts) & ~1:]
    return parts[0]


def _col_reduce(x, op):
    return (jnp.max if op is jnp.maximum else jnp.sum)(_col_partial(x, op), axis=0, keepdims=True)


def _row_to_cols(row):
    return jnp.broadcast_to(row, (LANES, row.shape[1])).T


def _dsa_t_kernel(q_ref, qi_ref, wi_ref, k_ref, vaug_ref, ki_ref, o_ref,
                  qit_scr, qt_scr, key_scr, acc_scr, *,
                  tq, tl, n_sel, p_len, n_valid_keys, all_keys_valid, n_idx_heads, n_kv_heads, rep):
    n_chunks_total = key_scr.shape[0]
    q_start = p_len + pl.program_id(1) * tq
    n_ch = _num_key_chunks(q_start, tq, n_valid_keys, n_chunks_total)
    heads_per_dot = 4

    for h in range(n_idx_heads):
        qit_scr[:, h * tl:(h + 1) * tl] = _t_bf16(qi_ref[:, h * LANES:(h + 1) * LANES], tl)
    for g in range(n_kv_heads):
        for r in range(rep):
            hh = g * rep + r
            qt_scr[g, :, r * tl:(r + 1) * tl] = _t_bf16(q_ref[:, hh * LANES:(hh + 1) * LANES], tl)
    wi = wi_ref[:, LANES:2 * LANES] * (n_idx_heads ** -0.5)
    if tq < tl:
        wi = jnp.concatenate([wi, jnp.zeros((tl - tq, LANES), F32)], axis=0)
    wi_t = wi.T

    def idx_body(j, carry):
        k0 = pl.multiple_of(j * KEY_CHUNK, KEY_CHUNK)
        kib = ki_ref[pl.ds(k0, KEY_CHUNK), :]
        score = jnp.zeros((KEY_CHUNK, tl), F32)
        for h0 in range(0, n_idx_heads, heads_per_dot):
            lg = jnp.dot(kib, qit_scr[:, h0 * tl:(h0 + heads_per_dot) * tl], preferred_element_type=F32)
            for h in range(h0, h0 + heads_per_dot):
                score = score + jnp.maximum(lg[:, (h - h0) * tl:(h - h0 + 1) * tl], 0.0) * wi_t[h:h + 1, :]
        score = _fill_invisible(score, -jnp.inf, q_start, tq, k0, n_valid_keys, all_keys_valid)
        key_scr[j] = _ordered_key(score)
        return carry

    lax.fori_loop(0, n_ch, idx_body, 0)

    def bit_body(b, t):
        cand_u = t | lax.shift_left(jnp.int32(1), 31 - b)
        cand_s = cand_u ^ jnp.int32(INT_MIN)

        def cnt_body(j, cnt):
            ones = jnp.where(key_scr[j] >= cand_s, 1, 0)
            return cnt + _col_partial(ones, jnp.add)

        cnt = lax.fori_loop(0, n_ch, cnt_body, jnp.zeros((8, tl), jnp.int32))
        total = jnp.sum(cnt, axis=0, keepdims=True)
        return jnp.where(total >= n_sel, cand_u, t)

    thr_u = lax.fori_loop(0, 32, bit_body, jnp.zeros((1, tl), jnp.int32))
    thr = jnp.maximum(thr_u ^ jnp.int32(INT_MIN), KEY_NEG_INF + 1)

    acc_scr[...] = jnp.zeros_like(acc_scr)

    def att_body(j, ms):
        k0 = pl.multiple_of(j * KEY_CHUNK, KEY_CHUNK)
        bias = jnp.where(key_scr[j] >= thr, 0.0, NEG_BIG)
        bias = jnp.concatenate([bias] * rep, axis=1)
        new = []
        for g in range(n_kv_heads):
            kb = k_ref[pl.ds(k0, KEY_CHUNK), g * LANES:(g + 1) * LANES]
            s = jnp.dot(kb, qt_scr[g], preferred_element_type=F32) + bias
            m_new = jnp.maximum(ms[g], _col_reduce(s, jnp.maximum))
            p = jnp.exp2(s - m_new).astype(BF16)
            va = vaug_ref[pl.ds(k0, KEY_CHUNK), g * 2 * LANES:(g + 1) * 2 * LANES]
            alpha = _row_to_cols(jnp.exp2(ms[g] - m_new))
            acc_scr[g] = (acc_scr[g] * jnp.concatenate([alpha, alpha], axis=1)
                          + lax.dot_general(p, va, (((0,), (0,)), ((), ())), preferred_element_type=F32))
            new.append(m_new)
        return tuple(new)

    lax.fori_loop(0, n_ch, att_body, tuple(jnp.full((1, rep * tl), NEG_BIG, F32) for _ in range(n_kv_heads)))

    for g in range(n_kv_heads):
        acc = acc_scr[g]
        o = acc[:, :LANES] / acc[:, LANES:]
        for r in range(rep):
            hh = g * rep + r
            o_ref[:, hh * LANES:(hh + 1) * LANES] = o[r * tl:r * tl + tq].astype(o_ref.dtype)


def _dsa_attention_t(qs_all, kiw_f32, k_bf, vaug_bf, ki_bf, *, batch, q_len, k_len, n_valid_keys, p_len, tq,
                     n_heads, n_kv_heads, n_idx_heads, n_sel):
    width = n_heads * HEAD_DIM
    nq = q_len // tq
    tl = max(tq, LANES)
    n_chunks = k_len // KEY_CHUNK
    rep = n_heads // n_kv_heads
    kv_w = n_kv_heads * HEAD_DIM
    return pl.pallas_call(
        functools.partial(_dsa_t_kernel, tq=tq, tl=tl, n_sel=n_sel, p_len=p_len, n_valid_keys=n_valid_keys,
                          all_keys_valid=n_valid_keys == k_len,
                          n_idx_heads=n_idx_heads, n_kv_heads=n_kv_heads, rep=rep),
        grid=(batch, nq),
        in_specs=[pl.BlockSpec((tq, width), lambda b, i: (b * nq + i, 0)),
                  pl.BlockSpec((tq, width), lambda b, i: (b * nq + i, 1)),
                  pl.BlockSpec((tq, 2 * LANES), lambda b, i: (b * nq + i, 0)),
                  pl.BlockSpec((k_len, kv_w), lambda b, i: (b, 0)),
                  pl.BlockSpec((k_len, 2 * kv_w), lambda b, i: (b, 0)),
                  pl.BlockSpec((k_len, LANES), lambda b, i: (b, 0))],
        out_specs=pl.BlockSpec((tq, width), lambda b, i: (b * nq + i, 0)),
        out_shape=jax.ShapeDtypeStruct((batch * q_len, width), BF16),
        scratch_shapes=[pltpu.VMEM((HEAD_DIM, n_idx_heads * tl), BF16),
                        pltpu.VMEM((n_kv_heads, HEAD_DIM, rep * tl), BF16),
                        pltpu.VMEM((n_chunks, KEY_CHUNK, tl), jnp.int32),
                        pltpu.VMEM((n_kv_heads, rep * tl, 2 * LANES), F32)],
        compiler_params=_params(("arbitrary", "arbitrary")),
        name="dsa_attention",
    )(qs_all, qs_all, kiw_f32, k_bf, vaug_bf, ki_bf)


def _diff_t_kernel(dq_ref, dk_ref, dv_ref, lq1_ref, lk1_ref, lq2_ref, lk2_ref, g_ref, o_ref,
                   acc_scr, *, tq, tl, hp, n_chunks_total, p_len, n_valid_keys, all_keys_valid, lam_init):
    q_start = p_len + pl.program_id(2) * tq
    n_ch = _num_key_chunks(q_start, tq, n_valid_keys, n_chunks_total)
    hw = 2 * HEAD_DIM
    n_maps = 2 * hp

    lam = (jnp.exp(jnp.sum(lq1_ref[...] * lk1_ref[...], axis=1, keepdims=True))
           - jnp.exp(jnp.sum(lq2_ref[...] * lk2_ref[...], axis=1, keepdims=True)) + lam_init)
    qt = [_t_bf16(dq_ref[:, c * LANES:(c + 1) * LANES], tl) for c in range(n_maps)]

    acc_scr[...] = jnp.zeros_like(acc_scr)

    def att_body(j, carry, masked):
        k0 = pl.multiple_of(j * KEY_CHUNK, KEY_CHUNK)
        new = []
        for c in range(n_maps):
            m, l = carry[2 * c:2 * c + 2]
            kb = dk_ref[pl.ds(k0, KEY_CHUNK), c * LANES:(c + 1) * LANES]
            vb = dv_ref[pl.ds(k0, KEY_CHUNK), (c // 2) * hw:(c // 2 + 1) * hw]
            s = jnp.dot(kb, qt[c], preferred_element_type=F32)
            if masked:
                s = _fill_invisible(s, NEG_BIG, q_start, tq, k0, n_valid_keys, all_keys_valid)
            m_new = jnp.maximum(m, _col_reduce(s, jnp.maximum))
            p = jnp.exp2(s - m_new)
            alpha = jnp.exp2(m - m_new)
            alpha_c = _row_to_cols(alpha)
            acc_scr[c] = (acc_scr[c] * jnp.concatenate([alpha_c, alpha_c], axis=1)
                          + lax.dot_general(p.astype(BF16), vb, (((0,), (0,)), ((), ())),
                                            preferred_element_type=F32))
            new += [m_new, alpha * l + _col_reduce(p, jnp.add)]
        return tuple(new)

    one = (jnp.full((1, tl), NEG_BIG, F32), jnp.zeros((1, tl), F32))
    n_full = jnp.minimum(_num_full_chunks(q_start, n_valid_keys), n_ch)
    carry = lax.fori_loop(0, n_full, functools.partial(att_body, masked=False), one * n_maps)
    carry = lax.fori_loop(n_full, n_ch, functools.partial(att_body, masked=True), carry)

    def normalised(c):
        l_cols = _row_to_cols(carry[2 * c + 1])
        return acc_scr[c] / jnp.concatenate([l_cols, l_cols], axis=1)

    for h in range(hp):
        o = (normalised(2 * h) - lam * normalised(2 * h + 1))[:tq]
        o = o * lax.rsqrt(jnp.mean(o * o, axis=-1, keepdims=True) + EPS) * g_ref[...]
        o_ref[:, h * hw:(h + 1) * hw] = (o * (1.0 - lam_init)).astype(o_ref.dtype)


def _diff_attention_t(qs_all, dk_bf, dv_bf, lams, g_subln, *, batch, q_len, k_len, n_valid_keys, p_len, tq,
                      n_heads, lam_init):
    hw = 2 * HEAD_DIM
    hp = 2 if n_heads % 2 == 0 else 1
    width = n_heads * hw
    nq = q_len // tq
    tl = max(tq, LANES)
    n_chunks = k_len // KEY_CHUNK
    assert qs_all.shape[1] == 3 * width
    dq_col0 = 2 * n_heads // hp
    vec = pl.BlockSpec((1, HEAD_DIM), lambda b, h, i: (0, 0))
    return pl.pallas_call(
        functools.partial(_diff_t_kernel, tq=tq, tl=tl, hp=hp, n_chunks_total=n_chunks, p_len=p_len,
                          n_valid_keys=n_valid_keys, all_keys_valid=n_valid_keys == k_len, lam_init=lam_init),
        grid=(batch, n_heads // hp, nq),
        in_specs=[pl.BlockSpec((tq, hp * hw), lambda b, h, i: (b * nq + i, dq_col0 + h)),
                  pl.BlockSpec((k_len, hp * hw), lambda b, h, i: (b, h)),
                  pl.BlockSpec((k_len, hp * hw), lambda b, h, i: (b, h)),
                  vec, vec, vec, vec,
                  pl.BlockSpec((1, hw), lambda b, h, i: (0, 0))],
        out_specs=pl.BlockSpec((tq, hp * hw), lambda b, h, i: (b * nq + i, h)),
        out_shape=jax.ShapeDtypeStruct((batch * q_len, width), BF16),
        scratch_shapes=[pltpu.VMEM((2 * hp, tl, hw), F32)],
        compiler_params=_params(("arbitrary", "arbitrary", "arbitrary")),
        name="diff_attention",
    )(qs_all, dk_bf, dv_bf, *[v.reshape(1, HEAD_DIM) for v in lams], g_subln.reshape(1, hw))


def _outproj_kernel(a_ref, d_ref, wa_ref, wd_ref, x_ref, ga_ref, o_ref, *, per_token):
    mix = (jnp.dot(a_ref[...], wa_ref[...], preferred_element_type=F32)
           + jnp.dot(d_ref[...], wd_ref[...], preferred_element_type=F32))
    ga = ga_ref[...] if per_token else ga_ref[0]
    o_ref[...] = x_ref[...] + ga * mix


def _outproj(a_out, d_out, w_out_bf, x2, mod, *, per_token, rows_per_batch, tm, tn):
    t, d = x2.shape
    wa = a_out.shape[1]
    wd = d_out.shape[1]
    assert wa == wd
    cb = 2 * (d // tn)
    return pl.pallas_call(
        functools.partial(_outproj_kernel, per_token=per_token),
        grid=(t // tm, d // tn),
        in_specs=[pl.BlockSpec((tm, wa), lambda m, n: (m, 0)),
                  pl.BlockSpec((tm, wd), lambda m, n: (m, 0)),
                  pl.BlockSpec((wa, tn), lambda m, n: (0, n)),
                  pl.BlockSpec((wd, tn), lambda m, n: (1, n)),
                  pl.BlockSpec((tm, tn), lambda m, n: (m, n)),
                  _mod_spec(mod, per_token, tm, tn, cb, rows_per_batch)],
        out_specs=pl.BlockSpec((tm, tn), lambda m, n: (m, n)),
        out_shape=jax.ShapeDtypeStruct((t, d), F32),
        compiler_params=_params(("arbitrary", "arbitrary")),
        name="outproj",
    )(a_out, d_out, w_out_bf, w_out_bf, x2, mod)


def _top_rows(x, k, with_rank=False):
    tops = []
    rank = jnp.full(x.shape, float(k), F32) if with_rank else None
    for i in range(k):
        mx = jnp.max(x, axis=0, keepdims=True)
        tops.append(mx)
        hit = x == mx
        if with_rank:
            rank = jnp.where(hit, float(i), rank)
        x = jnp.where(hit, -jnp.inf, x)
    return (tops, rank) if with_rank else tops


def _peer_route_kernel(h_ref, wq_ref, keys_ref, cnt_ref, e1_ref, rank_ref, e2_ref, q_scr, s1_scr, top_scr):
    hc = pl.program_id(1)
    c = hc % 2
    half = keys_ref.shape[2]

    @pl.when(hc == 0)
    def _():
        q_scr[...] = jnp.dot(wq_ref[...], h_ref[...], preferred_element_type=F32).astype(BF16)

    q_t = q_scr[pl.ds(pl.multiple_of(hc * half, half), half), :]
    s_t = jnp.dot(keys_ref[0].astype(BF16), q_t, preferred_element_type=F32)

    @pl.when(c == 0)
    def _():
        s1_scr[...] = s_t
        top_scr[...] = jnp.concatenate(_top_rows(s_t, PEER_TOPK), axis=0)

    @pl.when(c == 1)
    def _():
        tops2, rank2 = _top_rows(s_t, PEER_TOPK, with_rank=True)
        tops = jnp.concatenate(tops2, axis=0)
        top1 = top_scr[...]
        cand = jnp.concatenate([top1[0:1, :] + tops]
                               + [top1[i:i + 1, :] + tops[:PEER_TOPK // 2] for i in range(1, PEER_TOPK)], axis=0)
        best = _top_rows(cand, PEER_TOPK)
        m = best[0]
        z = jnp.zeros_like(m)
        for bk in best:
            z = z + jnp.exp(bk - m)
        thr = best[PEER_TOPK - 1]
        s1 = s1_scr[...]
        cnt = jnp.zeros_like(s1)
        for j in range(PEER_TOPK):
            cnt = cnt + jnp.where(s1 + tops2[j] >= thr, 1.0, 0.0)
        e1 = jnp.exp(s1 - top1[0:1, :]) * (0.5 / z)
        e2 = jnp.exp(s_t - tops2[0])
        hn = s_t.shape[0] // 2
        cnt_ref[0] = _pack_bf16_pair(cnt, cnt)
        e1_ref[0] = _pack_bf16_pair(e1, e1)
        rank_ref[0] = _pack_bf16_pair(rank2[:hn], rank2[hn:])
        e2_ref[0] = _pack_bf16_pair(e2[:hn], e2[hn:])


def _pack_bf16_pair(lo, hi):
    def bits(x):
        b = pltpu.bitcast(x, jnp.uint32)
        return (b + jnp.uint32(0x7FFF) + ((b >> 16) & jnp.uint32(1))) >> 16
    return bits(lo) | (bits(hi) << 16)


def _peer_route(h_t, wq_t_bf, sub_keys, *, tm):
    d, t = h_t.shape
    heads, _, n_keys, half = sub_keys.shape
    keys2 = sub_keys.reshape(heads * 2, n_keys, half)
    a_spec = pl.BlockSpec((1, n_keys, tm), lambda m, hc: (hc // 2, 0, m))
    b_spec = pl.BlockSpec((1, n_keys // 2, tm), lambda m, hc: (hc // 2, 0, m))
    a_tab = jax.ShapeDtypeStruct((heads, n_keys, t), jnp.uint32)
    b_tab = jax.ShapeDtypeStruct((heads, n_keys // 2, t), jnp.uint32)
    return pl.pallas_call(
        _peer_route_kernel,
        grid=(t // tm, heads * 2),
        in_specs=[pl.BlockSpec((d, tm), lambda m, hc: (0, m)),
                  pl.BlockSpec((heads * 2 * half, d), lambda m, hc: (0, 0), pipeline_mode=pl.Buffered(1)),
                  pl.BlockSpec((1, n_keys, half), lambda m, hc: (hc, 0, 0))],
        out_specs=[a_spec, a_spec, b_spec, b_spec],
        out_shape=[a_tab, a_tab, b_tab, b_tab],
        scratch_shapes=[pltpu.VMEM((heads * 2 * half, tm), BF16), pltpu.VMEM((n_keys, tm), F32),
                        pltpu.VMEM((PEER_TOPK, tm), F32)],
        compiler_params=_params(("arbitrary", "arbitrary")),
        name="peer_route",
    )(h_t, wq_t_bf, keys2)


def _gated_gelu(x, half_gate):
    c = 0.7978845608028654
    inner = x * (c + (c * 0.044715) * (x * x))
    return (x * half_gate) * (1.0 + jnp.tanh(inner))


def _rows_bf16(row_words, n_rows):
    tile = jnp.broadcast_to(row_words, (8, row_words.shape[1]))
    return pltpu.bitcast(jnp.concatenate([tile] * (n_rows // 16), axis=0), BF16)


def _peer_ffn_kernel(h_ref, u_ref, v_ref, cnt_ref, e1_ref, rank_ref, e2_ref, o_ref, *, n_keys):
    e = pl.program_id(1)
    te, tm = u_ref.shape[0], h_ref.shape[1]
    heads = cnt_ref.shape[0]
    a0 = e * (te // n_keys)

    @pl.when(e == 0)
    def _():
        o_ref[...] = jnp.zeros_like(o_ref)

    act = jnp.dot(u_ref[...], h_ref[...], preferred_element_type=F32)
    strip = min(tm, LANES)
    hn = n_keys // 2
    w_rows = [[None] * (tm // strip) for _ in range(2 * te // n_keys)]
    for ai in range(te // n_keys):
        cnt_a = [cnt_ref[h, pl.ds(a0 + ai, 1), :] for h in range(heads)]
        e1_a = [e1_ref[h, pl.ds(a0 + ai, 1), :] for h in range(heads)]
        for c in range(tm // strip):
            cols = slice(c * strip, (c + 1) * strip)
            gate = jnp.zeros((n_keys, strip), BF16)
            for h in range(heads):
                cnt_b = _rows_bf16(cnt_a[h][:, cols], n_keys)
                e1_b = _rows_bf16(e1_a[h][:, cols], n_keys)
                keep = pltpu.bitcast(rank_ref[h, :, cols], BF16) < cnt_b
                gate = gate + jnp.where(keep, pltpu.bitcast(e2_ref[h, :, cols], BF16) * e1_b,
                                        jnp.zeros_like(e1_b))
            words = pltpu.bitcast(gate, jnp.uint32)
            halves = (pltpu.bitcast(words << 16, F32), pltpu.bitcast(words & jnp.uint32(0xFFFF0000), F32))
            for k, g in enumerate(halves):
                rows = slice(ai * n_keys + k * hn, ai * n_keys + (k + 1) * hn)
                w_rows[2 * ai + k][c] = _gated_gelu(act[rows, cols], g).astype(BF16)
    w = jnp.concatenate([r[0] if len(r) == 1 else jnp.concatenate(r, axis=1) for r in w_rows], axis=0)
    o_ref[...] += lax.dot_general(w, v_ref[...], (((0,), (0,)), ((), ())),
                                  preferred_element_type=F32)


def _peer_ffn(h_t, u_bf, v_bf, cnt, e1, rank, e2, *, tm, te):
    d, t = h_t.shape
    n_exp = u_bf.shape[0]
    n_blocks = n_exp // te
    heads, n_keys, _ = cnt.shape
    once = pl.Buffered(1)
    a_spec = pl.BlockSpec((heads, n_keys, tm), lambda m, e: (0, 0, m), pipeline_mode=once)
    b_spec = pl.BlockSpec((heads, n_keys // 2, tm), lambda m, e: (0, 0, m), pipeline_mode=once)
    return pl.pallas_call(
        functools.partial(_peer_ffn_kernel, n_keys=n_keys),
        grid=(t // tm, n_blocks),
        in_specs=[pl.BlockSpec((d, tm), lambda m, e: (0, m), pipeline_mode=once),
                  pl.BlockSpec((te, d), lambda m, e: (e, 0)),
                  pl.BlockSpec((te, d), lambda m, e: (e, 0)),
                  a_spec, a_spec, b_spec, b_spec],
        out_specs=pl.BlockSpec((tm, d), lambda m, e: (m, 0)),
        out_shape=jax.ShapeDtypeStruct((t, d), F32),
        compiler_params=_params(("arbitrary", "arbitrary")),
        name="peer_ffn",
    )(h_t, u_bf, v_bf, cnt, e1, rank, e2)


def _final_kernel(x_ref, p_ref, ga_ref, g_ref, o_ref, *, per_token, normalize):
    ga = ga_ref[...] if per_token else ga_ref[0]
    x = x_ref[...] + ga * p_ref[...]
    if normalize:
        x = x * lax.rsqrt(jnp.mean(x * x, axis=-1, keepdims=True) + EPS) * g_ref[...]
    o_ref[...] = x


def _final(x1, peer, mod, g_final, *, per_token, rows_per_batch, tm, normalize):
    t, d = x1.shape
    row = pl.BlockSpec((tm, d), lambda m: (m, 0))
    return pl.pallas_call(
        functools.partial(_final_kernel, per_token=per_token, normalize=normalize),
        grid=(t // tm,),
        in_specs=[row, row, _mod_spec(mod, per_token, tm, d, 5, rows_per_batch),
                  pl.BlockSpec((1, d), lambda m: (0, 0))],
        out_specs=row,
        out_shape=jax.ShapeDtypeStruct((t, d), F32),
        compiler_params=_params(("arbitrary",)),
        name="final",
    )(x1, peer, mod, g_final.reshape(1, d))


def _rope_tables(pos):
    half = HEAD_DIM // 2
    inv = ROPE_THETA ** (-jnp.arange(half, dtype=F32) / half)
    ang = pos.astype(F32)[:, None] * inv[None, :]
    cos, sin = jnp.cos(ang), jnp.sin(ang)
    return jnp.concatenate([cos, cos], axis=1), jnp.concatenate([-sin, sin], axis=1)


def _pick_tile(n, prefs):
    for p in prefs:
        if n % p == 0:
            return p
    return n


def _layer(x, mod_rows, past, layer, last_layer, w, dims):
    b, s, d = x.shape
    t = b * s
    n_heads, n_kv, n_idx, n_diff = dims
    p_len = 0 if past is None else past[0].shape[1]
    n_keys_valid = p_len + s
    n_sel = min(DSA_TOPK, n_keys_valid // 4)
    x2 = x.reshape(t, d)

    per_token = s % 256 != 0
    if per_token:
        mod = jnp.repeat(mod_rows, s, axis=0)
    else:
        mod = mod_rows.reshape(b, 1, 6 * d)
    tm_big = _pick_tile(t if per_token else s, (1024, 512, 256, 128))
    tm_mid = _pick_tile(t if per_token else s, (512, 256, 128))
    tm_small = _pick_tile(t if per_token else s, (256, 128))

    pos = p_len + jnp.arange(s)
    cos, sin = _rope_tables(pos)
    if per_token:
        cos, sin = jnp.tile(cos, (b, 1)), jnp.tile(sin, (b, 1))

    h = _norm_mod(x2, w["g_norm_mix"], mod, which=1, per_token=per_token, rows_per_batch=s, tm=tm_mid,
                  transpose=False)

    proj = functools.partial(_proj, h, w["w_in"], cos=cos, sin=sin, tm=tm_big)
    qw, kvw, dw = n_heads * HEAD_DIM, n_kv * HEAD_DIM, n_diff * 2 * HEAD_DIM
    col = {"qs": 0, "k": 3 * qw, "dk": 3 * qw + kvw, "v": 3 * qw + kvw + dw, "dv": 3 * qw + 2 * kvw + dw,
           "kiw": 3 * qw + 2 * kvw + 2 * dw}
    (qs_all,) = proj(col0=col["qs"], n=3 * qw, rope_cols=512, scale=HEAD_DIM ** -0.5 * math.log2(math.e),
                     out_f32=False, out_bf16=True, tn=512, name="proj_q")
    k_f, k_b = proj(col0=col["k"], n=kvw, rope_cols=512, out_f32=True, out_bf16=True, tn=512, name="proj_k")
    v_f, v_b = proj(col0=col["v"], n=kvw, rope_cols=0, out_f32=True, out_bf16=True, tn=512, name="proj_v")
    dk_f, dk_b = proj(col0=col["dk"], n=dw, rope_cols=512, out_f32=True, out_bf16=True, tn=512, name="proj_dk")
    dv_f, dv_b = proj(col0=col["dv"], n=dw, rope_cols=0, out_f32=True, out_bf16=True, tn=512, name="proj_dv")
    kiw_f, kiw_b = proj(col0=col["kiw"], n=2 * LANES, rope_cols=LANES, out_f32=True, out_bf16=True, tn=2 * LANES,
                        name="proj_kiw")

    ki_f = kiw_f[:, :LANES]
    new_rows = (k_f.reshape(b, s, n_kv, HEAD_DIM), v_f.reshape(b, s, n_kv, HEAD_DIM), ki_f.reshape(b, s, LANES),
                dk_f.reshape(b, s, n_diff, 2, HEAD_DIM), dv_f.reshape(b, s, n_diff, 2 * HEAD_DIM))

    if past is None:
        k_len = s
        k_all, v_all, ki_all, dk_all, dv_all = k_b, v_b, kiw_b, dk_b, dv_b
    else:
        k_len = -(-n_keys_valid // KEY_CHUNK) * KEY_CHUNK

        def join(cache, new, width):
            new = new.reshape(b, s, -1)[:, :, :width]
            old = lax.optimization_barrier(cache.reshape(b, p_len, width))
            both = jnp.concatenate([old.astype(BF16), new], axis=1)
            both = jnp.pad(both, ((0, 0), (0, k_len - n_keys_valid), (0, 0)))
            return both.reshape(b * k_len, width)

        k_all = join(past[0], k_b, n_kv * HEAD_DIM)
        v_all = join(past[1], v_b, n_kv * HEAD_DIM)
        ki_all = join(past[2], kiw_b, LANES)
        dk_all = join(past[3], dk_b, n_diff * 2 * HEAD_DIM)
        dv_all = join(past[4], dv_b, n_diff * 2 * HEAD_DIM)

    ones = jnp.ones((v_all.shape[0], HEAD_DIM), BF16)
    vaug_all = jnp.concatenate(
        [piece for g in range(n_kv) for piece in (v_all[:, g * HEAD_DIM:(g + 1) * HEAD_DIM], ones)], axis=1)
    tq_a = _pick_tile(s, (128, 64, 32))
    a_out = _dsa_attention_t(qs_all, kiw_f, k_all, vaug_all, ki_all, batch=b, q_len=s, k_len=k_len,
                             n_valid_keys=n_keys_valid, p_len=p_len, tq=tq_a, n_heads=n_heads, n_kv_heads=n_kv,
                             n_idx_heads=n_idx, n_sel=n_sel)
    tq_d = _pick_tile(s, (256, 128, 64, 32))
    d_out = _diff_attention_t(qs_all, dk_all, dv_all, w["lams"], w["g_subln"], batch=b, q_len=s, k_len=k_len,
                              n_valid_keys=n_keys_valid, p_len=p_len, tq=tq_d, n_heads=n_diff,
                              lam_init=_lambda_init(layer))

    x1 = _outproj(a_out, d_out, w["w_out"], x2, mod, per_token=per_token, rows_per_batch=s, tm=tm_big, tn=512)

    h2_t = _norm_mod(x1, w["g_norm_ffn"], mod, which=2, per_token=per_token, rows_per_batch=s, tm=tm_mid,
                     transpose=True)
    cnt, e1, rank, e2 = _peer_route(h2_t, w["peer_wq_t"], w["peer_keys"], tm=tm_mid)
    peer = _peer_ffn(h2_t, w["peer_u"], w["peer_v"], cnt, e1, rank, e2, tm=tm_mid, te=512)
    x_out = _final(x1, peer, mod, w["g_final"], per_token=per_token, rows_per_batch=s, tm=tm_small,
                   normalize=last_layer)
    return x_out.reshape(b, s, d), new_rows


def kernel(x_prompt, x_sample, cache_dsa_k, cache_dsa_v, cache_idx_k, cache_diff_k, cache_diff_v, c_prompt, c_sample, w_ada, b_ada, g_norm_mix, g_norm_ffn, w_in, diff_lambda_q1, diff_lambda_k1, diff_lambda_q2, diff_lambda_k2, g_diff_subln, w_out, peer_w_query, peer_sub_keys, peer_u, peer_v, g_final):
    depth = w_in.shape[0]
    d = x_prompt.shape[-1]
    bp, bs = x_prompt.shape[0], x_sample.shape[0]
    n_kv = cache_dsa_k.shape[3]
    n_diff = cache_diff_k.shape[3]
    n_heads, n_idx_heads = DSA_HEADS, IDX_HEADS
    qw, kvw, dw = n_heads * HEAD_DIM, n_kv * HEAD_DIM, n_diff * 2 * HEAD_DIM
    assert w_in.shape[2] == qw + 2 * kvw + n_idx_heads * LANES + LANES + n_idx_heads + 3 * dw
    dims = (n_heads, n_kv, n_idx_heads, n_diff)

    hp, hs = x_prompt, x_sample
    rows_p, rows_s = [], []
    n_c = bp + bs
    c_pad = jnp.pad(jnp.concatenate([c_prompt, c_sample], axis=0), ((0, (-n_c) % 16), (0, 0)))
    for l in range(depth):
        mod_all = _adaln(c_pad, w_ada[l], b_ada[l])
        win = w_in[l]
        o = 0
        seg = {}
        for name, width in (("q", qw), ("k", kvw), ("v", kvw), ("qi", n_idx_heads * LANES), ("ki", LANES),
                            ("wi", n_idx_heads), ("dq", dw), ("dk", dw), ("dv", dw)):
            seg[name] = win[:, o:o + width]
            o += width
        weights = {
            "w_in": jnp.concatenate([seg[k] for k in ("q", "qi", "dq", "k", "dk", "v", "dv", "ki", "wi")]
                                    + [jnp.zeros((d, LANES - n_idx_heads), F32)], axis=1).astype(BF16),
            "w_out": w_out[l].astype(BF16),
            "peer_wq_t": peer_w_query[l].T.astype(BF16),
            "peer_keys": peer_sub_keys[l],
            "peer_u": peer_u[l].astype(BF16), "peer_v": peer_v[l].astype(BF16),
            "g_norm_mix": g_norm_mix[l], "g_norm_ffn": g_norm_ffn[l], "g_subln": g_diff_subln[l],
            "lams": (diff_lambda_q1[l], diff_lambda_k1[l], diff_lambda_q2[l], diff_lambda_k2[l]),
            "g_final": g_final,
        }
        last = l == depth - 1
        hp, rp = _layer(hp, mod_all[:bp], None, l, last, weights, dims)
        past = (cache_dsa_k[l], cache_dsa_v[l], cache_idx_k[l], cache_diff_k[l], cache_diff_v[l])
        hs, rs = _layer(hs, mod_all[bp:bp + bs], past, l, last, weights, dims)
        rows_p.append(rp)
        rows_s.append(rs)
    stack = lambda rows, i: jnp.stack([r[i] for r in rows])
    return (hp, hs) + tuple(stack(rows_p, i) for i in range(5)) + tuple(stack(rows_s, i) for i in range(5))
```

```python
import functools
import math

import jax
import jax.numpy as jnp
from jax import lax
from jax.experimental import pallas as pl
from jax.experimental.pallas import tpu as pltpu

CHUNK = 64
HEAD_DIM = 128
ROPE_THETA = 10000.0
EPS = 1e-6
DSA_HEADS = 16
IDX_HEADS = 16
DSA_TOPK = 256
PEER_TOPK = 16
LANES = 128
KEY_CHUNK = 512
NEG_BIG = -1e30
INT_MIN = -(2 ** 31)
KEY_NEG_INF = INT_MIN + 0x7FFFFF
VMEM_LIMIT = 56 * 1024 * 1024

BF16 = jnp.bfloat16
F32 = jnp.float32


def _params(sem, vmem=VMEM_LIMIT):
    return pltpu.CompilerParams(dimension_semantics=sem, vmem_limit_bytes=vmem)


def _lambda_init(layer):
    return 0.8 - 0.6 * math.exp(-0.3 * layer)


def _adaln_kernel(c_ref, w_ref, b_ref, o_ref):
    c = c_ref[...]
    a = (c * jax.nn.sigmoid(c)).astype(BF16)
    o_ref[...] = jnp.dot(a, w_ref[...].astype(BF16), preferred_element_type=F32) + b_ref[...]


def _adaln(c_pad, w_ada, b_ada):
    bp, d = c_pad.shape
    n = w_ada.shape[1]
    tn = 512
    return pl.pallas_call(
        _adaln_kernel,
        grid=(n // tn,),
        in_specs=[pl.BlockSpec((bp, d), lambda j: (0, 0)),
                  pl.BlockSpec((d, tn), lambda j: (0, j)),
                  pl.BlockSpec((1, tn), lambda j: (0, j))],
        out_specs=pl.BlockSpec((bp, tn), lambda j: (0, j)),
        out_shape=jax.ShapeDtypeStruct((bp, n), F32),
        compiler_params=_params(("arbitrary",)),
        name="adaln",
    )(c_pad, w_ada, b_ada.reshape(1, n))


def _norm_mod_kernel(x_ref, g_ref, sc_ref, sh_ref, o_ref, *, per_token, transpose):
    x = x_ref[...]
    y = x * lax.rsqrt(jnp.mean(x * x, axis=-1, keepdims=True) + EPS) * g_ref[...]
    sc = sc_ref[...] if per_token else sc_ref[0]
    sh = sh_ref[...] if per_token else sh_ref[0]
    h = y * (1.0 + sc) + sh
    if transpose:
        o_ref[...] = h.T.astype(o_ref.dtype)
    else:
        o_ref[...] = h.astype(o_ref.dtype)


def _mod_spec(mod, per_token, tm, tn, col_block, rows_per_batch):
    if per_token:
        return pl.BlockSpec((tm, tn), lambda m, n=0, cb=col_block: (m, cb + n))
    tiles_per_batch = rows_per_batch // tm
    return pl.BlockSpec((1, 1, tn), lambda m, n=0, cb=col_block: (m // tiles_per_batch, 0, cb + n))


def _norm_mod(x2, g, mod, *, which, per_token, rows_per_batch, tm, transpose):
    t, d = x2.shape
    sh_blk, sc_blk = (0, 1) if which == 1 else (3, 4)
    out_shape = (d, t) if transpose else (t, d)
    out_spec = pl.BlockSpec((d, tm), lambda m: (0, m)) if transpose else pl.BlockSpec((tm, d), lambda m: (m, 0))
    return pl.pallas_call(
        functools.partial(_norm_mod_kernel, per_token=per_token, transpose=transpose),
        grid=(t // tm,),
        in_specs=[pl.BlockSpec((tm, d), lambda m: (m, 0)),
                  pl.BlockSpec((1, d), lambda m: (0, 0)),
                  _mod_spec(mod, per_token, tm, d, sc_blk, rows_per_batch),
                  _mod_spec(mod, per_token, tm, d, sh_blk, rows_per_batch)],
        out_specs=out_spec,
        out_shape=jax.ShapeDtypeStruct(out_shape, BF16),
        compiler_params=_params(("arbitrary",)),
        name="norm_mod_t" if transpose else "norm_mod",
    )(x2, g.reshape(1, d), mod, mod)


def _rope_tile(acc, cos, sin, n_chunks):
    outs = []
    for j in range(n_chunks):
        xj = acc[:, j * LANES:(j + 1) * LANES]
        outs.append(xj * cos + pltpu.roll(xj, LANES // 2, axis=1) * sin)
    return outs[0] if n_chunks == 1 else jnp.concatenate(outs, axis=1)


def _proj_kernel(*refs, rope_cols, scale, out_f32, out_bf16):
    h_ref, w_ref, cos_ref, sin_ref = refs[:4]
    outs = refs[4:]
    acc = jnp.dot(h_ref[...], w_ref[...], preferred_element_type=F32)
    tn = acc.shape[1]
    if rope_cols:
        roped = _rope_tile(acc[:, :rope_cols], cos_ref[...], sin_ref[...], rope_cols // LANES)
        acc = roped if rope_cols == tn else jnp.concatenate([roped, acc[:, rope_cols:]], axis=1)
    k = 0
    if out_f32:
        outs[k][...] = acc
        k += 1
    if out_bf16:
        outs[k][...] = (acc * scale if scale != 1.0 else acc).astype(BF16)


def _proj(h, w, cos, sin, *, col0, n, rope_cols, scale=1.0, out_f32, out_bf16, tm, tn, name):
    t, d = h.shape
    assert rope_cols in (0, tn) or n == tn
    assert col0 % tn == 0 and n % tn == 0 and col0 + n <= w.shape[1]
    j0 = col0 // tn
    pos_tiles = cos.shape[0] // tm
    out_shape, out_specs = [], []
    for want, dt in ((out_f32, F32), (out_bf16, BF16)):
        if want:
            out_shape.append(jax.ShapeDtypeStruct((t, n), dt))
            out_specs.append(pl.BlockSpec((tm, tn), lambda m, j: (m, j)))
    return pl.pallas_call(
        functools.partial(_proj_kernel, rope_cols=rope_cols, scale=scale, out_f32=out_f32, out_bf16=out_bf16),
        grid=(t // tm, n // tn),
        in_specs=[pl.BlockSpec((tm, d), lambda m, j: (m, 0)),
                  pl.BlockSpec((d, tn), lambda m, j: (0, j0 + j)),
                  pl.BlockSpec((tm, LANES), lambda m, j: (m % pos_tiles, 0)),
                  pl.BlockSpec((tm, LANES), lambda m, j: (m % pos_tiles, 0))],
        out_specs=out_specs,
        out_shape=out_shape,
        compiler_params=_params(("arbitrary", "arbitrary")),
        name=name,
    )(h, w, cos, sin)


def _num_key_chunks(q_start, n_q, n_valid_keys, n_chunks_total):
    last_visible = (((q_start + n_q - 1) >> 6) + 1) * CHUNK
    last_visible = jnp.minimum(last_visible, n_valid_keys)
    return jnp.minimum((last_visible + KEY_CHUNK - 1) // KEY_CHUNK, n_chunks_total)


def _num_full_chunks(q_start, n_valid_keys):
    return jnp.minimum(((q_start >> 6) + 1) * CHUNK, n_valid_keys) // KEY_CHUNK


def _ordered_key(x):
    b = pltpu.bitcast(x, jnp.int32)
    return b ^ ((b >> 31) & jnp.int32(0x7FFFFFFF))


def _fill_invisible(x, fill, q_start, n_q, key_start, n_valid_keys, all_keys_valid):
    n_k, n_lanes = x.shape
    kpos = key_start + lax.broadcasted_iota(jnp.int32, (n_k, n_lanes), 0)
    qpos = q_start + jnp.minimum(lax.broadcasted_iota(jnp.int32, (1, n_lanes), 1), n_q - 1)
    x = jnp.where((kpos >> 6) <= (qpos >> 6), x, fill)
    return x if all_keys_valid else jnp.where(kpos < n_valid_keys, x, fill)


def _t_bf16(x, n_lanes):
    x = x.astype(F32)
    if x.shape[0] < n_lanes:
        x = jnp.concatenate([x, jnp.zeros((n_lanes - x.shape[0], x.shape[1]), F32)], axis=0)
    return x.T.astype(BF16)


def _col_partial(x, op):
    parts = [x[r * 8:(r + 1) * 8] for r in range(x.shape[0] // 8)]
    while len(parts) > 1:
        parts = [op(parts[i], parts[i + 1]) for i in range(0, len(parts) - 1, 2)] + parts[len(parts) & ~1:]
    return parts[0]


def _col_reduce(x, op):
    return (jnp.max if op is jnp.maximum else jnp.sum)(_col_partial(x, op), axis=0, keepdims=True)


def _row_to_cols(row):
    return jnp.broadcast_to(row, (LANES, row.shape[1])).T


def _dsa_kernel(q_ref, qi_ref, wi_ref, k_ref, vaug_ref, ki_ref, o_ref,
                qit_scr, qt_scr, key_scr, acc_scr, *,
                tq, tl, n_sel, p_len, n_valid_keys, all_keys_valid, n_idx_heads, n_kv_heads, rep):
    n_chunks_total = key_scr.shape[0]
    q_start = p_len + pl.program_id(1) * tq
    n_ch = _num_key_chunks(q_start, tq, n_valid_keys, n_chunks_total)
    heads_per_dot = 4

    for h in range(n_idx_heads):
        qit_scr[:, h * tl:(h + 1) * tl] = _t_bf16(qi_ref[:, h * LANES:(h + 1) * LANES], tl)
    for g in range(n_kv_heads):
        for r in range(rep):
            hh = g * rep + r
            qt_scr[g, :, r * tl:(r + 1) * tl] = _t_bf16(q_ref[:, hh * LANES:(hh + 1) * LANES], tl)
    wi = wi_ref[:, LANES:2 * LANES] * (n_idx_heads ** -0.5)
    if tq < tl:
        wi = jnp.concatenate([wi, jnp.zeros((tl - tq, LANES), F32)], axis=0)
    wi_t = wi.T

    def idx_body(j, carry):
        k0 = pl.multiple_of(j * KEY_CHUNK, KEY_CHUNK)
        kib = ki_ref[pl.ds(k0, KEY_CHUNK), :]
        score = jnp.zeros((KEY_CHUNK, tl), F32)
        for h0 in range(0, n_idx_heads, heads_per_dot):
            lg = jnp.dot(kib, qit_scr[:, h0 * tl:(h0 + heads_per_dot) * tl], preferred_element_type=F32)
            for h in range(h0, h0 + heads_per_dot):
                score = score + jnp.maximum(lg[:, (h - h0) * tl:(h - h0 + 1) * tl], 0.0) * wi_t[h:h + 1, :]
        score = _fill_invisible(score, -jnp.inf, q_start, tq, k0, n_valid_keys, all_keys_valid)
        key_scr[j] = _ordered_key(score)
        return carry

    lax.fori_loop(0, n_ch, idx_body, 0)

    def bit_body(b, t):
        cand_u = t | lax.shift_left(jnp.int32(1), 31 - b)
        cand_s = cand_u ^ jnp.int32(INT_MIN)

        def cnt_body(j, cnt):
            ones = jnp.where(key_scr[j] >= cand_s, 1, 0)
            return cnt + _col_partial(ones, jnp.add)

        cnt = lax.fori_loop(0, n_ch, cnt_body, jnp.zeros((8, tl), jnp.int32))
        total = jnp.sum(cnt, axis=0, keepdims=True)
        return jnp.where(total >= n_sel, cand_u, t)

    thr_u = lax.fori_loop(0, 32, bit_body, jnp.zeros((1, tl), jnp.int32))
    thr = jnp.maximum(thr_u ^ jnp.int32(INT_MIN), KEY_NEG_INF + 1)

    acc_scr[...] = jnp.zeros_like(acc_scr)

    def att_body(j, ms):
        k0 = pl.multiple_of(j * KEY_CHUNK, KEY_CHUNK)
        bias = jnp.where(key_scr[j] >= thr, 0.0, NEG_BIG)
        bias = jnp.concatenate([bias] * rep, axis=1)
        new = []
        for g in range(n_kv_heads):
            kb = k_ref[pl.ds(k0, KEY_CHUNK), g * LANES:(g + 1) * LANES]
            s = jnp.dot(kb, qt_scr[g], preferred_element_type=F32) + bias
            m_new = jnp.maximum(ms[g], _col_reduce(s, jnp.maximum))
            p = jnp.exp2(s - m_new).astype(BF16)
            va = vaug_ref[pl.ds(k0, KEY_CHUNK), g * 2 * LANES:(g + 1) * 2 * LANES]
            alpha = _row_to_cols(jnp.exp2(ms[g] - m_new))
            acc_scr[g] = (acc_scr[g] * jnp.concatenate([alpha, alpha], axis=1)
                          + lax.dot_general(p, va, (((0,), (0,)), ((), ())), preferred_element_type=F32))
            new.append(m_new)
        return tuple(new)

    lax.fori_loop(0, n_ch, att_body, tuple(jnp.full((1, rep * tl), NEG_BIG, F32) for _ in range(n_kv_heads)))

    for g in range(n_kv_heads):
        acc = acc_scr[g]
        o = acc[:, :LANES] / acc[:, LANES:]
        for r in range(rep):
            hh = g * rep + r
            o_ref[:, hh * LANES:(hh + 1) * LANES] = o[r * tl:r * tl + tq].astype(o_ref.dtype)


def _dsa_attention(q_all, qi_all, kiw_f32, k_bf, vaug_bf, ki_bf, *, batch, q_len, k_len, n_valid_keys, p_len, tq,
                   n_heads, n_kv_heads, n_idx_heads, n_sel):
    width = n_heads * HEAD_DIM
    assert qi_all.shape[1] == n_idx_heads * LANES
    nq = q_len // tq
    tl = max(tq, LANES)
    n_chunks = k_len // KEY_CHUNK
    rep = n_heads // n_kv_heads
    kv_w = n_kv_heads * HEAD_DIM
    return pl.pallas_call(
        functools.partial(_dsa_kernel, tq=tq, tl=tl, n_sel=n_sel, p_len=p_len, n_valid_keys=n_valid_keys,
                          all_keys_valid=n_valid_keys == k_len,
                          n_idx_heads=n_idx_heads, n_kv_heads=n_kv_heads, rep=rep),
        grid=(batch, nq),
        in_specs=[pl.BlockSpec((tq, width), lambda b, i: (b * nq + i, 0)),
                  pl.BlockSpec((tq, n_idx_heads * LANES), lambda b, i: (b * nq + i, 0)),
                  pl.BlockSpec((tq, 2 * LANES), lambda b, i: (b * nq + i, 0)),
                  pl.BlockSpec((k_len, kv_w), lambda b, i: (b, 0)),
                  pl.BlockSpec((k_len, 2 * kv_w), lambda b, i: (b, 0)),
                  pl.BlockSpec((k_len, LANES), lambda b, i: (b, 0))],
        out_specs=pl.BlockSpec((tq, width), lambda b, i: (b * nq + i, 0)),
        out_shape=jax.ShapeDtypeStruct((batch * q_len, width), BF16),
        scratch_shapes=[pltpu.VMEM((HEAD_DIM, n_idx_heads * tl), BF16),
                        pltpu.VMEM((n_kv_heads, HEAD_DIM, rep * tl), BF16),
                        pltpu.VMEM((n_chunks, KEY_CHUNK, tl), jnp.int32),
                        pltpu.VMEM((n_kv_heads, rep * tl, 2 * LANES), F32)],
        compiler_params=_params(("arbitrary", "arbitrary")),
        name="dsa_attention",
    )(q_all, qi_all, kiw_f32, k_bf, vaug_bf, ki_bf)


def _diff_kernel(dq_ref, dk_ref, dv_ref, lq1_ref, lk1_ref, lq2_ref, lk2_ref, g_ref, o_ref,
                 acc_scr, *, tq, tl, hp, n_chunks_total, p_len, n_valid_keys, all_keys_valid, lam_init):
    q_start = p_len + pl.program_id(2) * tq
    n_ch = _num_key_chunks(q_start, tq, n_valid_keys, n_chunks_total)
    hw = 2 * HEAD_DIM
    n_maps = 2 * hp

    lam = (jnp.exp(jnp.sum(lq1_ref[...] * lk1_ref[...], axis=1, keepdims=True))
           - jnp.exp(jnp.sum(lq2_ref[...] * lk2_ref[...], axis=1, keepdims=True)) + lam_init)
    qt = [_t_bf16(dq_ref[:, c * LANES:(c + 1) * LANES], tl) for c in range(n_maps)]

    acc_scr[...] = jnp.zeros_like(acc_scr)

    def att_body(j, carry, masked):
        k0 = pl.multiple_of(j * KEY_CHUNK, KEY_CHUNK)
        new = []
        for c in range(n_maps):
            m, l = carry[2 * c:2 * c + 2]
            kb = dk_ref[pl.ds(k0, KEY_CHUNK), c * LANES:(c + 1) * LANES]
            vb = dv_ref[pl.ds(k0, KEY_CHUNK), (c // 2) * hw:(c // 2 + 1) * hw]
            s = jnp.dot(kb, qt[c], preferred_element_type=F32)
            if masked:
                s = _fill_invisible(s, NEG_BIG, q_start, tq, k0, n_valid_keys, all_keys_valid)
            m_new = jnp.maximum(m, _col_reduce(s, jnp.maximum))
            p = jnp.exp2(s - m_new)
            alpha = jnp.exp2(m - m_new)
            alpha_c = _row_to_cols(alpha)
            acc_scr[c] = (acc_scr[c] * jnp.concatenate([alpha_c, alpha_c], axis=1)
                          + lax.dot_general(p.astype(BF16), vb, (((0,), (0,)), ((), ())),
                                            preferred_element_type=F32))
            new += [m_new, alpha * l + _col_reduce(p, jnp.add)]
        return tuple(new)

    one = (jnp.full((1, tl), NEG_BIG, F32), jnp.zeros((1, tl), F32))
    n_full = jnp.minimum(_num_full_chunks(q_start, n_valid_keys), n_ch)
    carry = lax.fori_loop(0, n_full, functools.partial(att_body, masked=False), one * n_maps)
    carry = lax.fori_loop(n_full, n_ch, functools.partial(att_body, masked=True), carry)

    def normalised(c):
        l_cols = _row_to_cols(carry[2 * c + 1])
        return acc_scr[c] / jnp.concatenate([l_cols, l_cols], axis=1)

    for h in range(hp):
        o = (normalised(2 * h) - lam * normalised(2 * h + 1))[:tq]
        o = o * lax.rsqrt(jnp.mean(o * o, axis=-1, keepdims=True) + EPS) * g_ref[...]
        o_ref[:, h * hw:(h + 1) * hw] = (o * (1.0 - lam_init)).astype(o_ref.dtype)


def _diff_attention(dq_all, dk_bf, dv_bf, lams, g_subln, *, batch, q_len, k_len, n_valid_keys, p_len, tq,
                    n_heads, lam_init):
    hw = 2 * HEAD_DIM
    hp = 2 if n_heads % 2 == 0 else 1
    width = n_heads * hw
    assert dq_all.shape[1] == width
    nq = q_len // tq
    tl = max(tq, LANES)
    n_chunks = k_len // KEY_CHUNK
    vec = pl.BlockSpec((1, HEAD_DIM), lambda b, h, i: (0, 0))
    return pl.pallas_call(
        functools.partial(_diff_kernel, tq=tq, tl=tl, hp=hp, n_chunks_total=n_chunks, p_len=p_len,
                          n_valid_keys=n_valid_keys, all_keys_valid=n_valid_keys == k_len, lam_init=lam_init),
        grid=(batch, n_heads // hp, nq),
        in_specs=[pl.BlockSpec((tq, hp * hw), lambda b, h, i: (b * nq + i, h)),
                  pl.BlockSpec((k_len, hp * hw), lambda b, h, i: (b, h)),
                  pl.BlockSpec((k_len, hp * hw), lambda b, h, i: (b, h)),
                  vec, vec, vec, vec,
                  pl.BlockSpec((1, hw), lambda b, h, i: (0, 0))],
        out_specs=pl.BlockSpec((tq, hp * hw), lambda b, h, i: (b * nq + i, h)),
        out_shape=jax.ShapeDtypeStruct((batch * q_len, width), BF16),
        scratch_shapes=[pltpu.VMEM((2 * hp, tl, hw), F32)],
        compiler_params=_params(("arbitrary", "arbitrary", "arbitrary")),
        name="diff_attention",
    )(dq_all, dk_bf, dv_bf, *[v.reshape(1, HEAD_DIM) for v in lams], g_subln.reshape(1, hw))


def _outproj_kernel(a_ref, d_ref, wa_ref, wd_ref, x_ref, ga_ref, o_ref, *, per_token):
    mix = (jnp.dot(a_ref[...], wa_ref[...], preferred_element_type=F32)
           + jnp.dot(d_ref[...], wd_ref[...], preferred_element_type=F32))
    ga = ga_ref[...] if per_token else ga_ref[0]
    o_ref[...] = x_ref[...] + ga * mix


def _outproj(a_out, d_out, w_out_bf, x2, mod, *, per_token, rows_per_batch, tm, tn):
    t, d = x2.shape
    wa = a_out.shape[1]
    wd = d_out.shape[1]
    assert wa == wd
    cb = 2 * (d // tn)
    return pl.pallas_call(
        functools.partial(_outproj_kernel, per_token=per_token),
        grid=(t // tm, d // tn),
        in_specs=[pl.BlockSpec((tm, wa), lambda m, n: (m, 0)),
                  pl.BlockSpec((tm, wd), lambda m, n: (m, 0)),
                  pl.BlockSpec((wa, tn), lambda m, n: (0, n)),
                  pl.BlockSpec((wd, tn), lambda m, n: (1, n)),
                  pl.BlockSpec((tm, tn), lambda m, n: (m, n)),
                  _mod_spec(mod, per_token, tm, tn, cb, rows_per_batch)],
        out_specs=pl.BlockSpec((tm, tn), lambda m, n: (m, n)),
        out_shape=jax.ShapeDtypeStruct((t, d), F32),
        compiler_params=_params(("arbitrary", "arbitrary")),
        name="outproj",
    )(a_out, d_out, w_out_bf, w_out_bf, x2, mod)


def _top_rows(x, k, with_rank=False):
    tops = []
    rank = jnp.full(x.shape, float(k), F32) if with_rank else None
    for i in range(k):
        mx = jnp.max(x, axis=0, keepdims=True)
        tops.append(mx)
        hit = x == mx
        if with_rank:
            rank = jnp.where(hit, float(i), rank)
        x = jnp.where(hit, -jnp.inf, x)
    return (tops, rank) if with_rank else tops


def _pack_bf16_pair(lo, hi):
    def bits(x):
        b = pltpu.bitcast(x, jnp.uint32)
        return (b + jnp.uint32(0x7FFF) + ((b >> 16) & jnp.uint32(1))) >> 16
    return bits(lo) | (bits(hi) << 16)


def _peer_route_kernel(h_ref, wq_ref, keys_ref, cnt_ref, e1_ref, rank_ref, e2_ref, q_scr, s1_scr, top_scr):
    hc = pl.program_id(1)
    c = hc % 2
    half = keys_ref.shape[2]

    @pl.when(hc == 0)
    def _():
        q_scr[...] = jnp.dot(wq_ref[...], h_ref[...], preferred_element_type=F32).astype(BF16)

    q_t = q_scr[pl.ds(pl.multiple_of(hc * half, half), half), :]
    s_t = jnp.dot(keys_ref[0].astype(BF16), q_t, preferred_element_type=F32)

    @pl.when(c == 0)
    def _():
        s1_scr[...] = s_t
        top_scr[...] = jnp.concatenate(_top_rows(s_t, PEER_TOPK), axis=0)

    @pl.when(c == 1)
    def _():
        tops2, rank2 = _top_rows(s_t, PEER_TOPK, with_rank=True)
        tops = jnp.concatenate(tops2, axis=0)
        top1 = top_scr[...]
        cand = jnp.concatenate([top1[0:1, :] + tops]
                               + [top1[i:i + 1, :] + tops[:PEER_TOPK // 2] for i in range(1, PEER_TOPK)], axis=0)
        best = _top_rows(cand, PEER_TOPK)
        m = best[0]
        z = jnp.zeros_like(m)
        for bk in best:
            z = z + jnp.exp(bk - m)
        thr = best[PEER_TOPK - 1]
        s1 = s1_scr[...]
        cnt = jnp.zeros_like(s1)
        for j in range(PEER_TOPK):
            cnt = cnt + jnp.where(s1 + tops2[j] >= thr, 1.0, 0.0)
        e1 = jnp.exp(s1 - top1[0:1, :]) * (0.5 / z)
        e2 = jnp.exp(s_t - tops2[0])
        hn = s_t.shape[0] // 2
        cnt_ref[0] = _pack_bf16_pair(cnt, cnt)
        e1_ref[0] = _pack_bf16_pair(e1, e1)
        rank_ref[0] = _pack_bf16_pair(rank2[:hn], rank2[hn:])
        e2_ref[0] = _pack_bf16_pair(e2[:hn], e2[hn:])


def _peer_route(h_t, wq_t_bf, sub_keys, *, tm):
    d, t = h_t.shape
    heads, _, n_keys, half = sub_keys.shape
    keys2 = sub_keys.reshape(heads * 2, n_keys, half)
    a_spec = pl.BlockSpec((1, n_keys, tm), lambda m, hc: (hc // 2, 0, m))
    b_spec = pl.BlockSpec((1, n_keys // 2, tm), lambda m, hc: (hc // 2, 0, m))
    a_tab = jax.ShapeDtypeStruct((heads, n_keys, t), jnp.uint32)
    b_tab = jax.ShapeDtypeStruct((heads, n_keys // 2, t), jnp.uint32)
    return pl.pallas_call(
        _peer_route_kernel,
        grid=(t // tm, heads * 2),
        in_specs=[pl.BlockSpec((d, tm), lambda m, hc: (0, m)),
                  pl.BlockSpec((heads * 2 * half, d), lambda m, hc: (0, 0), pipeline_mode=pl.Buffered(1)),
                  pl.BlockSpec((1, n_keys, half), lambda m, hc: (hc, 0, 0))],
        out_specs=[a_spec, a_spec, b_spec, b_spec],
        out_shape=[a_tab, a_tab, b_tab, b_tab],
        scratch_shapes=[pltpu.VMEM((heads * 2 * half, tm), BF16), pltpu.VMEM((n_keys, tm), F32),
                        pltpu.VMEM((PEER_TOPK, tm), F32)],
        compiler_params=_params(("arbitrary", "arbitrary")),
        name="peer_route",
    )(h_t, wq_t_bf, keys2)


def _gated_gelu(x, half_gate):
    c = 0.7978845608028654
    inner = x * (c + (c * 0.044715) * (x * x))
    return (x * half_gate) * (1.0 + jnp.tanh(inner))


def _rows_bf16(row_words, n_rows):
    tile = jnp.broadcast_to(row_words, (8, row_words.shape[1]))
    return pltpu.bitcast(jnp.concatenate([tile] * (n_rows // 16), axis=0), BF16)


def _peer_ffn_kernel(h_ref, u_ref, vt_ref, cnt_ref, e1_ref, rank_ref, e2_ref, o_ref, *, n_keys):
    e = pl.program_id(1)
    te, tm = u_ref.shape[0], h_ref.shape[1]
    heads = cnt_ref.shape[0]
    a0 = e * (te // n_keys)

    @pl.when(e == 0)
    def _():
        o_ref[...] = jnp.zeros_like(o_ref)

    act = jnp.dot(u_ref[...], h_ref[...], preferred_element_type=F32)
    strip = min(tm, LANES)
    hn = n_keys // 2
    w_rows = [[None] * (tm // strip) for _ in range(2 * te // n_keys)]
    for ai in range(te // n_keys):
        cnt_a = [cnt_ref[h, pl.ds(a0 + ai, 1), :] for h in range(heads)]
        e1_a = [e1_ref[h, pl.ds(a0 + ai, 1), :] for h in range(heads)]
        for c in range(tm // strip):
            cols = slice(c * strip, (c + 1) * strip)
            gate = jnp.zeros((n_keys, strip), BF16)
            for h in range(heads):
                cnt_b = _rows_bf16(cnt_a[h][:, cols], n_keys)
                e1_b = _rows_bf16(e1_a[h][:, cols], n_keys)
                keep = pltpu.bitcast(rank_ref[h, :, cols], BF16) < cnt_b
                gate = gate + jnp.where(keep, pltpu.bitcast(e2_ref[h, :, cols], BF16) * e1_b,
                                        jnp.zeros_like(e1_b))
            words = pltpu.bitcast(gate, jnp.uint32)
            halves = (pltpu.bitcast(words << 16, F32), pltpu.bitcast(words & jnp.uint32(0xFFFF0000), F32))
            for k, g in enumerate(halves):
                rows = slice(ai * n_keys + k * hn, ai * n_keys + (k + 1) * hn)
                w_rows[2 * ai + k][c] = _gated_gelu(act[rows, cols], g).astype(BF16)
    w = jnp.concatenate([r[0] if len(r) == 1 else jnp.concatenate(r, axis=1) for r in w_rows], axis=0)
    o_ref[...] += jnp.dot(vt_ref[...], w, preferred_element_type=F32)


def _peer_ffn(h_t, u_bf, vt_bf, cnt, e1, rank, e2, *, tm, te):
    d, t = h_t.shape
    n_exp = u_bf.shape[0]
    n_blocks = n_exp // te
    heads, n_keys, _ = cnt.shape
    once = pl.Buffered(1)
    a_spec = pl.BlockSpec((heads, n_keys, tm), lambda m, e: (0, 0, m), pipeline_mode=once)
    b_spec = pl.BlockSpec((heads, n_keys // 2, tm), lambda m, e: (0, 0, m), pipeline_mode=once)
    return pl.pallas_call(
        functools.partial(_peer_ffn_kernel, n_keys=n_keys),
        grid=(t // tm, n_blocks),
        in_specs=[pl.BlockSpec((d, tm), lambda m, e: (0, m), pipeline_mode=once),
                  pl.BlockSpec((te, d), lambda m, e: (e, 0)),
                  pl.BlockSpec((d, te), lambda m, e: (0, e)),
                  a_spec, a_spec, b_spec, b_spec],
        out_specs=pl.BlockSpec((d, tm), lambda m, e: (0, m)),
        out_shape=jax.ShapeDtypeStruct((d, t), F32),
        compiler_params=_params(("arbitrary", "arbitrary")),
        name="peer_ffn",
    )(h_t, u_bf, vt_bf, cnt, e1, rank, e2)


def _final_kernel(x_ref, pt_ref, ga_ref, g_ref, o_ref, *, per_token, normalize):
    ga = ga_ref[...] if per_token else ga_ref[0]
    x = x_ref[...] + ga * pt_ref[...].T
    if normalize:
        x = x * lax.rsqrt(jnp.mean(x * x, axis=-1, keepdims=True) + EPS) * g_ref[...]
    o_ref[...] = x


def _final(x1, peer_t, mod, g_final, *, per_token, rows_per_batch, tm, normalize):
    t, d = x1.shape
    row = pl.BlockSpec((tm, d), lambda m: (m, 0))
    return pl.pallas_call(
        functools.partial(_final_kernel, per_token=per_token, normalize=normalize),
        grid=(t // tm,),
        in_specs=[row, pl.BlockSpec((d, tm), lambda m: (0, m)), _mod_spec(mod, per_token, tm, d, 5, rows_per_batch),
                  pl.BlockSpec((1, d), lambda m: (0, 0))],
        out_specs=row,
        out_shape=jax.ShapeDtypeStruct((t, d), F32),
        compiler_params=_params(("arbitrary",)),
        name="final",
    )(x1, peer_t, mod, g_final.reshape(1, d))


def _rope_tables(pos):
    half = HEAD_DIM // 2
    inv = ROPE_THETA ** (-jnp.arange(half, dtype=F32) / half)
    ang = pos.astype(F32)[:, None] * inv[None, :]
    cos, sin = jnp.cos(ang), jnp.sin(ang)
    return jnp.concatenate([cos, cos], axis=1), jnp.concatenate([-sin, sin], axis=1)


def _pick_tile(n, prefs):
    for p in prefs:
        if n % p == 0:
            return p
    return n


def _layer(x, mod_rows, past, layer, last_layer, w, dims):
    b, s, d = x.shape
    t = b * s
    n_heads, n_kv, n_idx, n_diff = dims
    p_len = 0 if past is None else past[0].shape[1]
    n_keys_valid = p_len + s
    n_sel = min(DSA_TOPK, n_keys_valid // 4)
    x2 = x.reshape(t, d)

    per_token = s % 256 != 0
    if per_token:
        mod = jnp.repeat(mod_rows, s, axis=0)
    else:
        mod = mod_rows.reshape(b, 1, 6 * d)
    tm_big = _pick_tile(t if per_token else s, (1024, 512, 256, 128))
    tm_mid = _pick_tile(t if per_token else s, (512, 256, 128))
    tm_small = _pick_tile(t if per_token else s, (256, 128))

    pos = p_len + jnp.arange(s)
    cos, sin = _rope_tables(pos)
    if per_token:
        cos, sin = jnp.tile(cos, (b, 1)), jnp.tile(sin, (b, 1))

    h = _norm_mod(x2, w["g_norm_mix"], mod, which=1, per_token=per_token, rows_per_batch=s, tm=tm_mid,
                  transpose=False)

    qw, kvw, dw = n_heads * HEAD_DIM, n_kv * HEAD_DIM, n_diff * 2 * HEAD_DIM
    head = functools.partial(_proj, h, w["w_head"], cos=cos, sin=sin, tm=tm_big)
    tail = functools.partial(_proj, h, w["w_tail"], cos=cos, sin=sin, tm=tm_big)
    q_scale = HEAD_DIM ** -0.5 * math.log2(math.e)
    bf_only = dict(out_f32=False, out_bf16=True, tn=512)
    both = dict(out_f32=True, out_bf16=True, tn=512)
    (q_all,) = head(col0=0, n=qw, rope_cols=512, scale=q_scale, name="proj_q", **bf_only)
    (qi_all,) = head(col0=qw + 2 * kvw, n=n_idx * LANES, rope_cols=512, scale=q_scale, name="proj_qi", **bf_only)
    (dq_all,) = tail(col0=0, n=dw, rope_cols=512, scale=q_scale, name="proj_dq", **bf_only)
    k_f, k_b = head(col0=qw, n=kvw, rope_cols=512, name="proj_k", **both)
    v_f, v_b = head(col0=qw + kvw, n=kvw, rope_cols=0, name="proj_v", **both)
    dk_f, dk_b = tail(col0=dw, n=dw, rope_cols=512, name="proj_dk", **both)
    dv_f, dv_b = tail(col0=2 * dw, n=dw, rope_cols=0, name="proj_dv", **both)
    kiw_f, kiw_b = head(col0=qw + 2 * kvw + n_idx * LANES, n=2 * LANES, rope_cols=LANES, out_f32=True,
                        out_bf16=True, tn=2 * LANES, name="proj_kiw")

    ki_f = kiw_f[:, :LANES]
    new_rows = (k_f.reshape(b, s, n_kv, HEAD_DIM), v_f.reshape(b, s, n_kv, HEAD_DIM), ki_f.reshape(b, s, LANES),
                dk_f.reshape(b, s, n_diff, 2, HEAD_DIM), dv_f.reshape(b, s, n_diff, 2 * HEAD_DIM))

    if past is None:
        k_len = s
        k_all, v_all, ki_all, dk_all, dv_all = k_b, v_b, kiw_b, dk_b, dv_b
    else:
        k_len = -(-n_keys_valid // KEY_CHUNK) * KEY_CHUNK

        def join(cache, new, width):
            new = new.reshape(b, s, -1)[:, :, :width]
            old = lax.optimization_barrier(cache.reshape(b, p_len, width))
            both_ = jnp.concatenate([old.astype(BF16), new], axis=1)
            both_ = jnp.pad(both_, ((0, 0), (0, k_len - n_keys_valid), (0, 0)))
            return both_.reshape(b * k_len, width)

        k_all = join(past[0], k_b, n_kv * HEAD_DIM)
        v_all = join(past[1], v_b, n_kv * HEAD_DIM)
        ki_all = join(past[2], kiw_b, LANES)
        dk_all = join(past[3], dk_b, n_diff * 2 * HEAD_DIM)
        dv_all = join(past[4], dv_b, n_diff * 2 * HEAD_DIM)

    ones = jnp.ones((v_all.shape[0], HEAD_DIM), BF16)
    vaug_all = jnp.concatenate(
        [piece for g in range(n_kv) for piece in (v_all[:, g * HEAD_DIM:(g + 1) * HEAD_DIM], ones)], axis=1)
    tq_a = _pick_tile(s, (128, 64, 32))
    a_out = _dsa_attention(q_all, qi_all, kiw_f, k_all, vaug_all, ki_all, batch=b, q_len=s, k_len=k_len,
                           n_valid_keys=n_keys_valid, p_len=p_len, tq=tq_a, n_heads=n_heads, n_kv_heads=n_kv,
                           n_idx_heads=n_idx, n_sel=n_sel)
    tq_d = _pick_tile(s, (256, 128, 64, 32))
    d_out = _diff_attention(dq_all, dk_all, dv_all, w["lams"], w["g_subln"], batch=b, q_len=s, k_len=k_len,
                            n_valid_keys=n_keys_valid, p_len=p_len, tq=tq_d, n_heads=n_diff,
                            lam_init=_lambda_init(layer))

    x1 = _outproj(a_out, d_out, w["w_out"], x2, mod, per_token=per_token, rows_per_batch=s, tm=tm_big, tn=512)

    h2_t = _norm_mod(x1, w["g_norm_ffn"], mod, which=2, per_token=per_token, rows_per_batch=s, tm=tm_mid,
                     transpose=True)
    cnt, e1, rank, e2 = _peer_route(h2_t, w["peer_wq_t"], w["peer_keys"], tm=tm_mid)
    peer_t = _peer_ffn(h2_t, w["peer_u"], w["peer_v_t"], cnt, e1, rank, e2, tm=tm_mid, te=512)
    x_out = _final(x1, peer_t, mod, w["g_final"], per_token=per_token, rows_per_batch=s, tm=tm_small,
                   normalize=last_layer)
    return x_out.reshape(b, s, d), new_rows


def kernel(x_prompt, x_sample, cache_dsa_k, cache_dsa_v, cache_idx_k, cache_diff_k, cache_diff_v, c_prompt, c_sample, w_ada, b_ada, g_norm_mix, g_norm_ffn, w_in, diff_lambda_q1, diff_lambda_k1, diff_lambda_q2, diff_lambda_k2, g_diff_subln, w_out, peer_w_query, peer_sub_keys, peer_u, peer_v, g_final):
    depth = w_in.shape[0]
    bp, bs = x_prompt.shape[0], x_sample.shape[0]
    n_kv = cache_dsa_k.shape[3]
    n_diff = cache_diff_k.shape[3]
    n_heads, n_idx_heads = DSA_HEADS, IDX_HEADS
    qw, kvw, dw = n_heads * HEAD_DIM, n_kv * HEAD_DIM, n_diff * 2 * HEAD_DIM
    tail0 = qw + 2 * kvw + n_idx_heads * LANES + LANES + n_idx_heads
    assert w_in.shape[2] == tail0 + 3 * dw
    head_w = qw + 2 * kvw + n_idx_heads * LANES + 2 * LANES
    dims = (n_heads, n_kv, n_idx_heads, n_diff)

    hp, hs = x_prompt, x_sample
    rows_p, rows_s = [], []
    n_c = bp + bs
    c_pad = jnp.pad(jnp.concatenate([c_prompt, c_sample], axis=0), ((0, (-n_c) % 16), (0, 0)))
    for l in range(depth):
        mod_all = _adaln(c_pad, w_ada[l], b_ada[l])
        weights = {
            "w_head": w_in[l][:, :head_w].astype(BF16),
            "w_tail": w_in[l][:, tail0:].astype(BF16),
            "w_out": w_out[l].astype(BF16),
            "peer_wq_t": peer_w_query[l].T.astype(BF16),
            "peer_keys": peer_sub_keys[l],
            "peer_u": peer_u[l].astype(BF16), "peer_v_t": peer_v[l].T.astype(BF16),
            "g_norm_mix": g_norm_mix[l], "g_norm_ffn": g_norm_ffn[l], "g_subln": g_diff_subln[l],
            "lams": (diff_lambda_q1[l], diff_lambda_k1[l], diff_lambda_q2[l], diff_lambda_k2[l]),
            "g_final": g_final,
        }
        last = l == depth - 1
        hp, rp = _layer(hp, mod_all[:bp], None, l, last, weights, dims)
        past = (cache_dsa_k[l], cache_dsa_v[l], cache_idx_k[l], cache_diff_k[l], cache_diff_v[l])
        hs, rs = _layer(hs, mod_all[bp:bp + bs], past, l, last, weights, dims)
        rows_p.append(rp)
        rows_s.append(rs)
    stack = lambda rows, i: jnp.stack([r[i] for r in rows])
    return (hp, hs) + tuple(stack(rows_p, i) for i in range(5)) + tuple(stack(rows_s, i) for i in range(5))
```

```python
import functools
import math

import jax
import jax.numpy as jnp
from jax import lax
from jax.experimental import pallas as pl
from jax.experimental.pallas import tpu as pltpu

CHUNK = 64
HEAD_DIM = 128
ROPE_THETA = 10000.0
EPS = 1e-6
DSA_HEADS = 16
IDX_HEADS = 16
DSA_TOPK = 256
PEER_TOPK = 16
LANES = 128
KEY_CHUNK = 512
NEG_BIG = -1e30
INT_MIN = -(2 ** 31)
KEY_NEG_INF = INT_MIN + 0x7FFFFF
VMEM_LIMIT = 56 * 1024 * 1024

BF16 = jnp.bfloat16
F32 = jnp.float32


def _params(sem, vmem=VMEM_LIMIT):
    return pltpu.CompilerParams(dimension_semantics=sem, vmem_limit_bytes=vmem)


def _lambda_init(layer):
    return 0.8 - 0.6 * math.exp(-0.3 * layer)


def _adaln_kernel(c_ref, w_ref, b_ref, o_ref):
    c = c_ref[...]
    a = (c * jax.nn.sigmoid(c)).astype(BF16)
    o_ref[...] = jnp.dot(a, w_ref[...].astype(BF16), preferred_element_type=F32) + b_ref[...]


def _adaln(c_pad, w_ada, b_ada):
    bp, d = c_pad.shape
    n = w_ada.shape[1]
    tn = 512
    return pl.pallas_call(
        _adaln_kernel,
        grid=(n // tn,),
        in_specs=[pl.BlockSpec((bp, d), lambda j: (0, 0)),
                  pl.BlockSpec((d, tn), lambda j: (0, j)),
                  pl.BlockSpec((1, tn), lambda j: (0, j))],
        out_specs=pl.BlockSpec((bp, tn), lambda j: (0, j)),
        out_shape=jax.ShapeDtypeStruct((bp, n), F32),
        compiler_params=_params(("arbitrary",)),
        name="adaln",
    )(c_pad, w_ada, b_ada.reshape(1, n))


def _norm_mod_kernel(x_ref, g_ref, sc_ref, sh_ref, o_ref, *, per_token, transpose):
    x = x_ref[...]
    y = x * lax.rsqrt(jnp.mean(x * x, axis=-1, keepdims=True) + EPS) * g_ref[...]
    sc = sc_ref[...] if per_token else sc_ref[0]
    sh = sh_ref[...] if per_token else sh_ref[0]
    h = y * (1.0 + sc) + sh
    if transpose:
        o_ref[...] = h.T.astype(o_ref.dtype)
    else:
        o_ref[...] = h.astype(o_ref.dtype)


def _mod_spec(mod, per_token, tm, tn, col_block, rows_per_batch):
    if per_token:
        return pl.BlockSpec((tm, tn), lambda m, n=0, cb=col_block: (m, cb + n))
    tiles_per_batch = rows_per_batch // tm
    return pl.BlockSpec((1, 1, tn), lambda m, n=0, cb=col_block: (m // tiles_per_batch, 0, cb + n))


def _norm_mod(x2, g, mod, *, which, per_token, rows_per_batch, tm, transpose):
    t, d = x2.shape
    sh_blk, sc_blk = (0, 1) if which == 1 else (3, 4)
    out_shape = (d, t) if transpose else (t, d)
    out_spec = pl.BlockSpec((d, tm), lambda m: (0, m)) if transpose else pl.BlockSpec((tm, d), lambda m: (m, 0))
    return pl.pallas_call(
        functools.partial(_norm_mod_kernel, per_token=per_token, transpose=transpose),
        grid=(t // tm,),
        in_specs=[pl.BlockSpec((tm, d), lambda m: (m, 0)),
                  pl.BlockSpec((1, d), lambda m: (0, 0)),
                  _mod_spec(mod, per_token, tm, d, sc_blk, rows_per_batch),
                  _mod_spec(mod, per_token, tm, d, sh_blk, rows_per_batch)],
        out_specs=out_spec,
        out_shape=jax.ShapeDtypeStruct(out_shape, BF16),
        compiler_params=_params(("arbitrary",)),
        name="norm_mod_t" if transpose else "norm_mod",
    )(x2, g.reshape(1, d), mod, mod)


def _rope_tile(acc, cos, sin, n_chunks):
    outs = []
    for j in range(n_chunks):
        xj = acc[:, j * LANES:(j + 1) * LANES]
        outs.append(xj * cos + pltpu.roll(xj, LANES // 2, axis=1) * sin)
    return outs[0] if n_chunks == 1 else jnp.concatenate(outs, axis=1)


def _proj_kernel(*refs, rope_cols, scale, out_f32, out_bf16):
    h_ref, w_ref, cos_ref, sin_ref = refs[:4]
    outs = refs[4:]
    acc = jnp.dot(h_ref[...], w_ref[...], preferred_element_type=F32)
    tn = acc.shape[1]
    if rope_cols:
        roped = _rope_tile(acc[:, :rope_cols], cos_ref[...], sin_ref[...], rope_cols // LANES)
        acc = roped if rope_cols == tn else jnp.concatenate([roped, acc[:, rope_cols:]], axis=1)
    k = 0
    if out_f32:
        outs[k][...] = acc
        k += 1
    if out_bf16:
        outs[k][...] = (acc * scale if scale != 1.0 else acc).astype(BF16)


def _proj(h, w, cos, sin, *, col0, n, rope_cols, scale=1.0, out_f32, out_bf16, tm, tn, name):
    t, d = h.shape
    assert rope_cols in (0, tn) or n == tn
    assert col0 % tn == 0 and n % tn == 0 and col0 + n <= w.shape[1]
    j0 = col0 // tn
    pos_tiles = cos.shape[0] // tm
    out_shape, out_specs = [], []
    for want, dt in ((out_f32, F32), (out_bf16, BF16)):
        if want:
            out_shape.append(jax.ShapeDtypeStruct((t, n), dt))
            out_specs.append(pl.BlockSpec((tm, tn), lambda m, j: (m, j)))
    return pl.pallas_call(
        functools.partial(_proj_kernel, rope_cols=rope_cols, scale=scale, out_f32=out_f32, out_bf16=out_bf16),
        grid=(t // tm, n // tn),
        in_specs=[pl.BlockSpec((tm, d), lambda m, j: (m, 0)),
                  pl.BlockSpec((d, tn), lambda m, j: (0, j0 + j)),
                  pl.BlockSpec((tm, LANES), lambda m, j: (m % pos_tiles, 0)),
                  pl.BlockSpec((tm, LANES), lambda m, j: (m % pos_tiles, 0))],
        out_specs=out_specs,
        out_shape=out_shape,
        compiler_params=_params(("arbitrary", "arbitrary")),
        name=name,
    )(h, w, cos, sin)


def _num_key_chunks(q_start, n_q, n_valid_keys, n_chunks_total):
    last_visible = (((q_start + n_q - 1) >> 6) + 1) * CHUNK
    last_visible = jnp.minimum(last_visible, n_valid_keys)
    return jnp.minimum((last_visible + KEY_CHUNK - 1) // KEY_CHUNK, n_chunks_total)


def _num_full_chunks(q_start, n_valid_keys):
    return jnp.minimum(((q_start >> 6) + 1) * CHUNK, n_valid_keys) // KEY_CHUNK


def _ordered_key(x):
    b = pltpu.bitcast(x, jnp.int32)
    return b ^ ((b >> 31) & jnp.int32(0x7FFFFFFF))


def _fill_invisible(x, fill, q_start, n_q, key_start, n_valid_keys, all_keys_valid):
    n_k, n_lanes = x.shape
    kpos = key_start + lax.broadcasted_iota(jnp.int32, (n_k, n_lanes), 0)
    qpos = q_start + jnp.minimum(lax.broadcasted_iota(jnp.int32, (1, n_lanes), 1), n_q - 1)
    x = jnp.where((kpos >> 6) <= (qpos >> 6), x, fill)
    return x if all_keys_valid else jnp.where(kpos < n_valid_keys, x, fill)


def _t_bf16(x, n_lanes):
    x = x.astype(F32)
    if x.shape[0] < n_lanes:
        x = jnp.concatenate([x, jnp.zeros((n_lanes - x.shape[0], x.shape[1]), F32)], axis=0)
    return x.T.astype(BF16)


def _rows_i16(words, n_word_rows):
    tile = jnp.broadcast_to(words, (8, words.shape[1]))
    return pltpu.bitcast(jnp.concatenate([tile] * (n_word_rows // 8), axis=0), jnp.int16)


def _col_partial(x, op, group=8):
    parts = [x[r * group:(r + 1) * group] for r in range(x.shape[0] // group)]
    while len(parts) > 1:
        parts = [op(parts[i], parts[i + 1]) for i in range(0, len(parts) - 1, 2)] + parts[len(parts) & ~1:]
    return parts[0]


def _col_reduce(x, op):
    return (jnp.max if op is jnp.maximum else jnp.sum)(_col_partial(x, op), axis=0, keepdims=True)


def _row_to_cols(row):
    return jnp.broadcast_to(row, (LANES, row.shape[1])).T


def _dsa_kernel(q_ref, qi_ref, wi_ref, k_ref, vaug_ref, ki_ref, o_ref,
                qit_scr, qt_scr, key_scr, hi_scr, lo_scr, acc_scr, *,
                tq, tl, n_sel, p_len, n_valid_keys, all_keys_valid, n_idx_heads, n_kv_heads, rep):
    n_chunks_total = key_scr.shape[0]
    q_start = p_len + pl.program_id(1) * tq
    n_ch = _num_key_chunks(q_start, tq, n_valid_keys, n_chunks_total)
    heads_per_dot = 4
    half_chunk = KEY_CHUNK // 2

    for h in range(n_idx_heads):
        qit_scr[:, h * tl:(h + 1) * tl] = _t_bf16(qi_ref[:, h * LANES:(h + 1) * LANES], tl)
    for g in range(n_kv_heads):
        for r in range(rep):
            hh = g * rep + r
            qt_scr[g, :, r * tl:(r + 1) * tl] = _t_bf16(q_ref[:, hh * LANES:(hh + 1) * LANES], tl)
    wi = wi_ref[:, LANES:2 * LANES] * (n_idx_heads ** -0.5)
    if tq < tl:
        wi = jnp.concatenate([wi, jnp.zeros((tl - tq, LANES), F32)], axis=0)
    wi_t = wi.T

    def idx_body(j, carry):
        k0 = pl.multiple_of(j * KEY_CHUNK, KEY_CHUNK)
        kib = ki_ref[pl.ds(k0, KEY_CHUNK), :]
        score = jnp.zeros((KEY_CHUNK, tl), F32)
        for h0 in range(0, n_idx_heads, heads_per_dot):
            lg = jnp.dot(kib, qit_scr[:, h0 * tl:(h0 + heads_per_dot) * tl], preferred_element_type=F32)
            for h in range(h0, h0 + heads_per_dot):
                score = score + jnp.maximum(lg[:, (h - h0) * tl:(h - h0 + 1) * tl], 0.0) * wi_t[h:h + 1, :]
        score = _fill_invisible(score, -jnp.inf, q_start, tq, k0, n_valid_keys, all_keys_valid)
        key = _ordered_key(score)
        key_scr[j] = key
        hi16 = (key >> 16) & 0xFFFF
        lo16 = (key & 0xFFFF) ^ 0x8000
        hi_scr[j] = hi16[:half_chunk] | (hi16[half_chunk:] << 16)
        lo_scr[j] = lo16[:half_chunk] | (lo16[half_chunk:] << 16)
        return carry

    lax.fori_loop(0, n_ch, idx_body, 0)

    def count16(scr, pattern, strict):
        c = _rows_i16(pattern | (pattern << 16), half_chunk)

        def body(j, cnt):
            x = pltpu.bitcast(scr[j], jnp.int16)
            ones = jnp.where((x > c) if strict else (x >= c), jnp.int16(1), jnp.int16(0))
            return cnt + _col_partial(ones, jnp.add, group=16)

        cnt = pltpu.bitcast(lax.fori_loop(0, n_ch, body, jnp.zeros((16, tl), jnp.int16)), jnp.int32)
        return jnp.sum((cnt & 0xFFFF) + ((cnt >> 16) & 0xFFFF), axis=0, keepdims=True)

    def bisect16(scr, need):
        def bit_body(b, t):
            cand = t | lax.shift_left(jnp.int32(1), 15 - b)
            return jnp.where(count16(scr, cand ^ 0x8000, False) >= need, cand, t)
        return lax.fori_loop(0, 16, bit_body, jnp.zeros((1, tl), jnp.int32))

    hi_pat = bisect16(hi_scr, n_sel) ^ 0x8000
    need = n_sel - count16(hi_scr, hi_pat, True)

    def keep_bucket(j, carry):
        hi = pltpu.bitcast(hi_scr[j], jnp.int16)
        lo = pltpu.bitcast(lo_scr[j], jnp.int16)
        same = hi == _rows_i16(hi_pat | (hi_pat << 16), half_chunk)
        lo_scr[j] = pltpu.bitcast(jnp.where(same, lo, jnp.int16(-2 ** 15)), jnp.int32)
        return carry

    lax.fori_loop(0, n_ch, keep_bucket, 0)
    lo_u = bisect16(lo_scr, need)
    thr = (((hi_pat << 16) >> 16) << 16) | lo_u
    thr = jnp.maximum(thr, KEY_NEG_INF + 1)

    acc_scr[...] = jnp.zeros_like(acc_scr)

    def att_body(j, ms):
        k0 = pl.multiple_of(j * KEY_CHUNK, KEY_CHUNK)
        bias = jnp.where(key_scr[j] >= thr, 0.0, NEG_BIG)
        bias = jnp.concatenate([bias] * rep, axis=1)
        new = []
        for g in range(n_kv_heads):
            kb = k_ref[pl.ds(k0, KEY_CHUNK), g * LANES:(g + 1) * LANES]
            s = jnp.dot(kb, qt_scr[g], preferred_element_type=F32) + bias
            m_new = jnp.maximum(ms[g], _col_reduce(s, jnp.maximum))
            p = jnp.exp2(s - m_new).astype(BF16)
            va = vaug_ref[pl.ds(k0, KEY_CHUNK), g * 2 * LANES:(g + 1) * 2 * LANES]
            alpha = _row_to_cols(jnp.exp2(ms[g] - m_new))
            acc_scr[g] = (acc_scr[g] * jnp.concatenate([alpha, alpha], axis=1)
                          + lax.dot_general(p, va, (((0,), (0,)), ((), ())), preferred_element_type=F32))
            new.append(m_new)
        return tuple(new)

    lax.fori_loop(0, n_ch, att_body, tuple(jnp.full((1, rep * tl), NEG_BIG, F32) for _ in range(n_kv_heads)))

    for g in range(n_kv_heads):
        acc = acc_scr[g]
        o = acc[:, :LANES] / acc[:, LANES:]
        for r in range(rep):
            hh = g * rep + r
            o_ref[:, hh * LANES:(hh + 1) * LANES] = o[r * tl:r * tl + tq].astype(o_ref.dtype)


def _dsa_attention(q_all, qi_all, kiw_f32, k_bf, vaug_bf, ki_bf, *, batch, q_len, k_len, n_valid_keys, p_len, tq,
                   n_heads, n_kv_heads, n_idx_heads, n_sel):
    width = n_heads * HEAD_DIM
    assert qi_all.shape[1] == n_idx_heads * LANES
    nq = q_len // tq
    tl = max(tq, LANES)
    n_chunks = k_len // KEY_CHUNK
    rep = n_heads // n_kv_heads
    kv_w = n_kv_heads * HEAD_DIM
    return pl.pallas_call(
        functools.partial(_dsa_kernel, tq=tq, tl=tl, n_sel=n_sel, p_len=p_len, n_valid_keys=n_valid_keys,
                          all_keys_valid=n_valid_keys == k_len,
                          n_idx_heads=n_idx_heads, n_kv_heads=n_kv_heads, rep=rep),
        grid=(batch, nq),
        in_specs=[pl.BlockSpec((tq, width), lambda b, i: (b * nq + i, 0)),
                  pl.BlockSpec((tq, n_idx_heads * LANES), lambda b, i: (b * nq + i, 0)),
                  pl.BlockSpec((tq, 2 * LANES), lambda b, i: (b * nq + i, 0)),
                  pl.BlockSpec((k_len, kv_w), lambda b, i: (b, 0)),
                  pl.BlockSpec((k_len, 2 * kv_w), lambda b, i: (b, 0)),
                  pl.BlockSpec((k_len, LANES), lambda b, i: (b, 0))],
        out_specs=pl.BlockSpec((tq, width), lambda b, i: (b * nq + i, 0)),
        out_shape=jax.ShapeDtypeStruct((batch * q_len, width), BF16),
        scratch_shapes=[pltpu.VMEM((HEAD_DIM, n_idx_heads * tl), BF16),
                        pltpu.VMEM((n_kv_heads, HEAD_DIM, rep * tl), BF16),
                        pltpu.VMEM((n_chunks, KEY_CHUNK, tl), jnp.int32),
                        pltpu.VMEM((n_chunks, KEY_CHUNK // 2, tl), jnp.int32),
                        pltpu.VMEM((n_chunks, KEY_CHUNK // 2, tl), jnp.int32),
                        pltpu.VMEM((n_kv_heads, rep * tl, 2 * LANES), F32)],
        compiler_params=_params(("arbitrary", "arbitrary")),
        name="dsa_attention",
    )(q_all, qi_all, kiw_f32, k_bf, vaug_bf, ki_bf)


def _diff_kernel(dq_ref, dk_ref, dv_ref, lq1_ref, lk1_ref, lq2_ref, lk2_ref, g_ref, o_ref,
                 acc_scr, *, tq, tl, hp, n_chunks_total, p_len, n_valid_keys, all_keys_valid, lam_init):
    q_start = p_len + pl.program_id(2) * tq
    n_ch = _num_key_chunks(q_start, tq, n_valid_keys, n_chunks_total)
    hw = 2 * HEAD_DIM
    n_maps = 2 * hp

    lam = (jnp.exp(jnp.sum(lq1_ref[...] * lk1_ref[...], axis=1, keepdims=True))
           - jnp.exp(jnp.sum(lq2_ref[...] * lk2_ref[...], axis=1, keepdims=True)) + lam_init)
    qt = [_t_bf16(dq_ref[:, c * LANES:(c + 1) * LANES], tl) for c in range(n_maps)]

    acc_scr[...] = jnp.zeros_like(acc_scr)

    def att_body(j, carry, masked):
        k0 = pl.multiple_of(j * KEY_CHUNK, KEY_CHUNK)
        new = []
        for c in range(n_maps):
            m, l = carry[2 * c:2 * c + 2]
            kb = dk_ref[pl.ds(k0, KEY_CHUNK), c * LANES:(c + 1) * LANES]
            vb = dv_ref[pl.ds(k0, KEY_CHUNK), (c // 2) * hw:(c // 2 + 1) * hw]
            s = jnp.dot(kb, qt[c], preferred_element_type=F32)
            if masked:
                s = _fill_invisible(s, NEG_BIG, q_start, tq, k0, n_valid_keys, all_keys_valid)
            m_new = jnp.maximum(m, _col_reduce(s, jnp.maximum))
            p = jnp.exp2(s - m_new)
            alpha = jnp.exp2(m - m_new)
            alpha_c = _row_to_cols(alpha)
            acc_scr[c] = (acc_scr[c] * jnp.concatenate([alpha_c, alpha_c], axis=1)
                          + lax.dot_general(p.astype(BF16), vb, (((0,), (0,)), ((), ())),
                                            preferred_element_type=F32))
            new += [m_new, alpha * l + _col_reduce(p, jnp.add)]
        return tuple(new)

    one = (jnp.full((1, tl), NEG_BIG, F32), jnp.zeros((1, tl), F32))
    n_full = jnp.minimum(_num_full_chunks(q_start, n_valid_keys), n_ch)
    carry = lax.fori_loop(0, n_full, functools.partial(att_body, masked=False), one * n_maps)
    carry = lax.fori_loop(n_full, n_ch, functools.partial(att_body, masked=True), carry)

    def normalised(c):
        l_cols = _row_to_cols(carry[2 * c + 1])
        return acc_scr[c] / jnp.concatenate([l_cols, l_cols], axis=1)

    for h in range(hp):
        o = (normalised(2 * h) - lam * normalised(2 * h + 1))[:tq]
        o = o * lax.rsqrt(jnp.mean(o * o, axis=-1, keepdims=True) + EPS) * g_ref[...]
        o_ref[:, h * hw:(h + 1) * hw] = (o * (1.0 - lam_init)).astype(o_ref.dtype)


def _diff_attention(dq_all, dk_bf, dv_bf, lams, g_subln, *, batch, q_len, k_len, n_valid_keys, p_len, tq,
                    n_heads, lam_init):
    hw = 2 * HEAD_DIM
    hp = 2 if n_heads % 2 == 0 else 1
    width = n_heads * hw
    assert dq_all.shape[1] == width
    nq = q_len // tq
    tl = max(tq, LANES)
    n_chunks = k_len // KEY_CHUNK
    vec = pl.BlockSpec((1, HEAD_DIM), lambda b, h, i: (0, 0))
    return pl.pallas_call(
        functools.partial(_diff_kernel, tq=tq, tl=tl, hp=hp, n_chunks_total=n_chunks, p_len=p_len,
                          n_valid_keys=n_valid_keys, all_keys_valid=n_valid_keys == k_len, lam_init=lam_init),
        grid=(batch, n_heads // hp, nq),
        in_specs=[pl.BlockSpec((tq, hp * hw), lambda b, h, i: (b * nq + i, h)),
                  pl.BlockSpec((k_len, hp * hw), lambda b, h, i: (b, h)),
                  pl.BlockSpec((k_len, hp * hw), lambda b, h, i: (b, h)),
                  vec, vec, vec, vec,
                  pl.BlockSpec((1, hw), lambda b, h, i: (0, 0))],
        out_specs=pl.BlockSpec((tq, hp * hw), lambda b, h, i: (b * nq + i, h)),
        out_shape=jax.ShapeDtypeStruct((batch * q_len, width), BF16),
        scratch_shapes=[pltpu.VMEM((2 * hp, tl, hw), F32)],
        compiler_params=_params(("arbitrary", "arbitrary", "arbitrary")),
        name="diff_attention",
    )(dq_all, dk_bf, dv_bf, *[v.reshape(1, HEAD_DIM) for v in lams], g_subln.reshape(1, hw))


def _outproj_kernel(a_ref, d_ref, wa_ref, wd_ref, x_ref, ga_ref, o_ref, *, per_token):
    mix = (jnp.dot(a_ref[...], wa_ref[...], preferred_element_type=F32)
           + jnp.dot(d_ref[...], wd_ref[...], preferred_element_type=F32))
    ga = ga_ref[...] if per_token else ga_ref[0]
    o_ref[...] = x_ref[...] + ga * mix


def _outproj(a_out, d_out, w_out_bf, x2, mod, *, per_token, rows_per_batch, tm, tn):
    t, d = x2.shape
    wa = a_out.shape[1]
    wd = d_out.shape[1]
    assert wa == wd
    cb = 2 * (d // tn)
    return pl.pallas_call(
        functools.partial(_outproj_kernel, per_token=per_token),
        grid=(t // tm, d // tn),
        in_specs=[pl.BlockSpec((tm, wa), lambda m, n: (m, 0)),
                  pl.BlockSpec((tm, wd), lambda m, n: (m, 0)),
                  pl.BlockSpec((wa, tn), lambda m, n: (0, n)),
                  pl.BlockSpec((wd, tn), lambda m, n: (1, n)),
                  pl.BlockSpec((tm, tn), lambda m, n: (m, n)),
                  _mod_spec(mod, per_token, tm, tn, cb, rows_per_batch)],
        out_specs=pl.BlockSpec((tm, tn), lambda m, n: (m, n)),
        out_shape=jax.ShapeDtypeStruct((t, d), F32),
        compiler_params=_params(("arbitrary", "arbitrary")),
        name="outproj",
    )(a_out, d_out, w_out_bf, w_out_bf, x2, mod)


def _top_rows(x, k, with_rank=False):
    tops = []
    rank = jnp.full(x.shape, float(k), F32) if with_rank else None
    for i in range(k):
        mx = jnp.max(x, axis=0, keepdims=True)
        tops.append(mx)
        hit = x == mx
        if with_rank:
            rank = jnp.where(hit, float(i), rank)
        x = jnp.where(hit, -jnp.inf, x)
    return (tops, rank) if with_rank else tops


def _pack_bf16_pair(lo, hi):
    def bits(x):
        b = pltpu.bitcast(x, jnp.uint32)
        return (b + jnp.uint32(0x7FFF) + ((b >> 16) & jnp.uint32(1))) >> 16
    return bits(lo) | (bits(hi) << 16)


def _peer_route_kernel(h_ref, wq_ref, keys_ref, cnt_ref, e1_ref, rank_ref, e2_ref, q_scr, s1_scr, top_scr):
    hc = pl.program_id(1)
    c = hc % 2
    half = keys_ref.shape[2]

    @pl.when(hc == 0)
    def _():
        q_scr[...] = jnp.dot(wq_ref[...], h_ref[...], preferred_element_type=F32).astype(BF16)

    q_t = q_scr[pl.ds(pl.multiple_of(hc * half, half), half), :]
    s_t = jnp.dot(keys_ref[0].astype(BF16), q_t, preferred_element_type=F32)

    @pl.when(c == 0)
    def _():
        s1_scr[...] = s_t
        top_scr[...] = jnp.concatenate(_top_rows(s_t, PEER_TOPK), axis=0)

    @pl.when(c == 1)
    def _():
        tops2, rank2 = _top_rows(s_t, PEER_TOPK, with_rank=True)
        tops = jnp.concatenate(tops2, axis=0)
        top1 = top_scr[...]
        cand = jnp.concatenate([top1[0:1, :] + tops]
                               + [top1[i:i + 1, :] + tops[:PEER_TOPK // 2] for i in range(1, PEER_TOPK)], axis=0)
        best = _top_rows(cand, PEER_TOPK)
        m = best[0]
        z = jnp.zeros_like(m)
        for bk in best:
            z = z + jnp.exp(bk - m)
        thr = best[PEER_TOPK - 1]
        s1 = s1_scr[...]
        cnt = jnp.zeros_like(s1)
        for j in range(PEER_TOPK):
            cnt = cnt + jnp.where(s1 + tops2[j] >= thr, 1.0, 0.0)
        e1 = jnp.exp(s1 - top1[0:1, :]) * (0.5 / z)
        e2 = jnp.exp(s_t - tops2[0])
        hn = s_t.shape[0] // 2
        cnt_ref[0] = _pack_bf16_pair(cnt, cnt)
        e1_ref[0] = _pack_bf16_pair(e1, e1)
        rank_ref[0] = _pack_bf16_pair(rank2[:hn], rank2[hn:])
        e2_ref[0] = _pack_bf16_pair(e2[:hn], e2[hn:])


def _peer_route(h_t, wq_t_bf, sub_keys, *, tm):
    d, t = h_t.shape
    heads, _, n_keys, half = sub_keys.shape
    keys2 = sub_keys.reshape(heads * 2, n_keys, half)
    a_spec = pl.BlockSpec((1, n_keys, tm), lambda m, hc: (hc // 2, 0, m))
    b_spec = pl.BlockSpec((1, n_keys // 2, tm), lambda m, hc: (hc // 2, 0, m))
    a_tab = jax.ShapeDtypeStruct((heads, n_keys, t), jnp.uint32)
    b_tab = jax.ShapeDtypeStruct((heads, n_keys // 2, t), jnp.uint32)
    return pl.pallas_call(
        _peer_route_kernel,
        grid=(t // tm, heads * 2),
        in_specs=[pl.BlockSpec((d, tm), lambda m, hc: (0, m)),
                  pl.BlockSpec((heads * 2 * half, d), lambda m, hc: (0, 0), pipeline_mode=pl.Buffered(1)),
                  pl.BlockSpec((1, n_keys, half), lambda m, hc: (hc, 0, 0))],
        out_specs=[a_spec, a_spec, b_spec, b_spec],
        out_shape=[a_tab, a_tab, b_tab, b_tab],
        scratch_shapes=[pltpu.VMEM((heads * 2 * half, tm), BF16), pltpu.VMEM((n_keys, tm), F32),
                        pltpu.VMEM((PEER_TOPK, tm), F32)],
        compiler_params=_params(("arbitrary", "arbitrary")),
        name="peer_route",
    )(h_t, wq_t_bf, keys2)


def _gated_gelu(x, half_gate):
    c = 0.7978845608028654
    inner = x * (c + (c * 0.044715) * (x * x))
    return (x * half_gate) * (1.0 + jnp.tanh(inner))


def _rows_bf16(row_words, n_rows):
    tile = jnp.broadcast_to(row_words, (8, row_words.shape[1]))
    return pltpu.bitcast(jnp.concatenate([tile] * (n_rows // 16), axis=0), BF16)


def _peer_ffn_kernel(h_ref, u_ref, vt_ref, cnt_ref, e1_ref, rank_ref, e2_ref, o_ref, *, n_keys):
    e = pl.program_id(1)
    te, tm = u_ref.shape[0], h_ref.shape[1]
    heads = cnt_ref.shape[0]
    a0 = e * (te // n_keys)

    @pl.when(e == 0)
    def _():
        o_ref[...] = jnp.zeros_like(o_ref)

    act = jnp.dot(u_ref[...], h_ref[...], preferred_element_type=F32)
    strip = min(tm, LANES)
    hn = n_keys // 2
    w_rows = [[None] * (tm // strip) for _ in range(2 * te // n_keys)]
    for ai in range(te // n_keys):
        cnt_a = [cnt_ref[h, pl.ds(a0 + ai, 1), :] for h in range(heads)]
        e1_a = [e1_ref[h, pl.ds(a0 + ai, 1), :] for h in range(heads)]
        for c in range(tm // strip):
            cols = slice(c * strip, (c + 1) * strip)
            gate = jnp.zeros((n_keys, strip), BF16)
            for h in range(heads):
                cnt_b = _rows_bf16(cnt_a[h][:, cols], n_keys)
                e1_b = _rows_bf16(e1_a[h][:, cols], n_keys)
                keep = pltpu.bitcast(rank_ref[h, :, cols], BF16) < cnt_b
                gate = gate + jnp.where(keep, pltpu.bitcast(e2_ref[h, :, cols], BF16) * e1_b,
                                        jnp.zeros_like(e1_b))
            words = pltpu.bitcast(gate, jnp.uint32)
            halves = (pltpu.bitcast(words << 16, F32), pltpu.bitcast(words & jnp.uint32(0xFFFF0000), F32))
            for k, g in enumerate(halves):
                rows = slice(ai * n_keys + k * hn, ai * n_keys + (k + 1) * hn)
                w_rows[2 * ai + k][c] = _gated_gelu(act[rows, cols], g).astype(BF16)
    w = jnp.concatenate([r[0] if len(r) == 1 else jnp.concatenate(r, axis=1) for r in w_rows], axis=0)
    o_ref[...] += jnp.dot(vt_ref[...], w, preferred_element_type=F32)


def _peer_ffn(h_t, u_bf, vt_bf, cnt, e1, rank, e2, *, tm, te):
    d, t = h_t.shape
    n_exp = u_bf.shape[0]
    n_blocks = n_exp // te
    heads, n_keys, _ = cnt.shape
    once = pl.Buffered(1)
    a_spec = pl.BlockSpec((heads, n_keys, tm), lambda m, e: (0, 0, m), pipeline_mode=once)
    b_spec = pl.BlockSpec((heads, n_keys // 2, tm), lambda m, e: (0, 0, m), pipeline_mode=once)
    return pl.pallas_call(
        functools.partial(_peer_ffn_kernel, n_keys=n_keys),
        grid=(t // tm, n_blocks),
        in_specs=[pl.BlockSpec((d, tm), lambda m, e: (0, m), pipeline_mode=once),
                  pl.BlockSpec((te, d), lambda m, e: (e, 0)),
                  pl.BlockSpec((d, te), lambda m, e: (0, e)),
                  a_spec, a_spec, b_spec, b_spec],
        out_specs=pl.BlockSpec((d, tm), lambda m, e: (0, m)),
        out_shape=jax.ShapeDtypeStruct((d, t), F32),
        compiler_params=_params(("arbitrary", "arbitrary")),
        name="peer_ffn",
    )(h_t, u_bf, vt_bf, cnt, e1, rank, e2)


def _final_kernel(x_ref, pt_ref, ga_ref, g_ref, o_ref, *, per_token, normalize):
    ga = ga_ref[...] if per_token else ga_ref[0]
    x = x_ref[...] + ga * pt_ref[...].T
    if normalize:
        x = x * lax.rsqrt(jnp.mean(x * x, axis=-1, keepdims=True) + EPS) * g_ref[...]
    o_ref[...] = x


def _final(x1, peer_t, mod, g_final, *, per_token, rows_per_batch, tm, normalize):
    t, d = x1.shape
    row = pl.BlockSpec((tm, d), lambda m: (m, 0))
    return pl.pallas_call(
        functools.partial(_final_kernel, per_token=per_token, normalize=normalize),
        grid=(t // tm,),
        in_specs=[row, pl.BlockSpec((d, tm), lambda m: (0, m)), _mod_spec(mod, per_token, tm, d, 5, rows_per_batch),
                  pl.BlockSpec((1, d), lambda m: (0, 0))],
        out_specs=row,
        out_shape=jax.ShapeDtypeStruct((t, d), F32),
        compiler_params=_params(("arbitrary",)),
        name="final",
    )(x1, peer_t, mod, g_final.reshape(1, d))


def _rope_tables(pos):
    half = HEAD_DIM // 2
    inv = ROPE_THETA ** (-jnp.arange(half, dtype=F32) / half)
    ang = pos.astype(F32)[:, None] * inv[None, :]
    cos, sin = jnp.cos(ang), jnp.sin(ang)
    return jnp.concatenate([cos, cos], axis=1), jnp.concatenate([-sin, sin], axis=1)


def _pick_tile(n, prefs):
    for p in prefs:
        if n % p == 0:
            return p
    return n


def _layer(x, mod_rows, past, layer, last_layer, w, dims):
    b, s, d = x.shape
    t = b * s
    n_heads, n_kv, n_idx, n_diff = dims
    p_len = 0 if past is None else past[0].shape[1]
    n_keys_valid = p_len + s
    n_sel = min(DSA_TOPK, n_keys_valid // 4)
    x2 = x.reshape(t, d)

    per_token = s % 256 != 0
    if per_token:
        mod = jnp.repeat(mod_rows, s, axis=0)
    else:
        mod = mod_rows.reshape(b, 1, 6 * d)
    tm_big = _pick_tile(t if per_token else s, (1024, 512, 256, 128))
    tm_mid = _pick_tile(t if per_token else s, (512, 256, 128))
    tm_small = _pick_tile(t if per_token else s, (256, 128))

    pos = p_len + jnp.arange(s)
    cos, sin = _rope_tables(pos)
    if per_token:
        cos, sin = jnp.tile(cos, (b, 1)), jnp.tile(sin, (b, 1))

    h = _norm_mod(x2, w["g_norm_mix"], mod, which=1, per_token=per_token, rows_per_batch=s, tm=tm_mid,
                  transpose=False)

    qw, kvw, dw = n_heads * HEAD_DIM, n_kv * HEAD_DIM, n_diff * 2 * HEAD_DIM
    head = functools.partial(_proj, h, w["w_head"], cos=cos, sin=sin, tm=tm_big)
    tail = functools.partial(_proj, h, w["w_tail"], cos=cos, sin=sin, tm=tm_big)
    q_scale = HEAD_DIM ** -0.5 * math.log2(math.e)
    bf_only = dict(out_f32=False, out_bf16=True, tn=512)
    both = dict(out_f32=True, out_bf16=True, tn=512)
    (q_all,) = head(col0=0, n=qw, rope_cols=512, scale=q_scale, name="proj_q", **bf_only)
    (qi_all,) = head(col0=qw + 2 * kvw, n=n_idx * LANES, rope_cols=512, scale=q_scale, name="proj_qi", **bf_only)
    (dq_all,) = tail(col0=0, n=dw, rope_cols=512, scale=q_scale, name="proj_dq", **bf_only)
    k_f, k_b = head(col0=qw, n=kvw, rope_cols=512, name="proj_k", **both)
    v_f, v_b = head(col0=qw + kvw, n=kvw, rope_cols=0, name="proj_v", **both)
    dk_f, dk_b = tail(col0=dw, n=dw, rope_cols=512, name="proj_dk", **both)
    dv_f, dv_b = tail(col0=2 * dw, n=dw, rope_cols=0, name="proj_dv", **both)
    kiw_f, kiw_b = head(col0=qw + 2 * kvw + n_idx * LANES, n=2 * LANES, rope_cols=LANES, out_f32=True,
                        out_bf16=True, tn=2 * LANES, name="proj_kiw")

    ki_f = kiw_f[:, :LANES]
    new_rows = (k_f.reshape(b, s, n_kv, HEAD_DIM), v_f.reshape(b, s, n_kv, HEAD_DIM), ki_f.reshape(b, s, LANES),
                dk_f.reshape(b, s, n_diff, 2, HEAD_DIM), dv_f.reshape(b, s, n_diff, 2 * HEAD_DIM))

    if past is None:
        k_len = s
        k_all, v_all, ki_all, dk_all, dv_all = k_b, v_b, kiw_b, dk_b, dv_b
    else:
        k_len = -(-n_keys_valid // KEY_CHUNK) * KEY_CHUNK

        def join(cache, new, width):
            new = new.reshape(b, s, -1)[:, :, :width]
            old = lax.optimization_barrier(cache.reshape(b, p_len, width))
            both_ = jnp.concatenate([old.astype(BF16), new], axis=1)
            both_ = jnp.pad(both_, ((0, 0), (0, k_len - n_keys_valid), (0, 0)))
            return both_.reshape(b * k_len, width)

        k_all = join(past[0], k_b, n_kv * HEAD_DIM)
        v_all = join(past[1], v_b, n_kv * HEAD_DIM)
        ki_all = join(past[2], kiw_b, LANES)
        dk_all = join(past[3], dk_b, n_diff * 2 * HEAD_DIM)
        dv_all = join(past[4], dv_b, n_diff * 2 * HEAD_DIM)

    ones = jnp.ones((v_all.shape[0], HEAD_DIM), BF16)
    vaug_all = jnp.concatenate(
        [piece for g in range(n_kv) for piece in (v_all[:, g * HEAD_DIM:(g + 1) * HEAD_DIM], ones)], axis=1)
    tq_a = _pick_tile(s, (128, 64, 32))
    a_out = _dsa_attention(q_all, qi_all, kiw_f, k_all, vaug_all, ki_all, batch=b, q_len=s, k_len=k_len,
                           n_valid_keys=n_keys_valid, p_len=p_len, tq=tq_a, n_heads=n_heads, n_kv_heads=n_kv,
                           n_idx_heads=n_idx, n_sel=n_sel)
    tq_d = _pick_tile(s, (256, 128, 64, 32))
    d_out = _diff_attention(dq_all, dk_all, dv_all, w["lams"], w["g_subln"], batch=b, q_len=s, k_len=k_len,
                            n_valid_keys=n_keys_valid, p_len=p_len, tq=tq_d, n_heads=n_diff,
                            lam_init=_lambda_init(layer))

    x1 = _outproj(a_out, d_out, w["w_out"], x2, mod, per_token=per_token, rows_per_batch=s, tm=tm_big, tn=512)

    h2_t = _norm_mod(x1, w["g_norm_ffn"], mod, which=2, per_token=per_token, rows_per_batch=s, tm=tm_mid,
                     transpose=True)
    cnt, e1, rank, e2 = _peer_route(h2_t, w["peer_wq_t"], w["peer_keys"], tm=tm_mid)
    peer_t = _peer_ffn(h2_t, w["peer_u"], w["peer_v_t"], cnt, e1, rank, e2, tm=tm_mid, te=512)
    x_out = _final(x1, peer_t, mod, w["g_final"], per_token=per_token, rows_per_batch=s, tm=tm_small,
                   normalize=last_layer)
    return x_out.reshape(b, s, d), new_rows


def kernel(x_prompt, x_sample, cache_dsa_k, cache_dsa_v, cache_idx_k, cache_diff_k, cache_diff_v, c_prompt, c_sample, w_ada, b_ada, g_norm_mix, g_norm_ffn, w_in, diff_lambda_q1, diff_lambda_k1, diff_lambda_q2, diff_lambda_k2, g_diff_subln, w_out, peer_w_query, peer_sub_keys, peer_u, peer_v, g_final):
    depth = w_in.shape[0]
    bp, bs = x_prompt.shape[0], x_sample.shape[0]
    n_kv = cache_dsa_k.shape[3]
    n_diff = cache_diff_k.shape[3]
    n_heads, n_idx_heads = DSA_HEADS, IDX_HEADS
    qw, kvw, dw = n_heads * HEAD_DIM, n_kv * HEAD_DIM, n_diff * 2 * HEAD_DIM
    tail0 = qw + 2 * kvw + n_idx_heads * LANES + LANES + n_idx_heads
    assert w_in.shape[2] == tail0 + 3 * dw
    head_w = qw + 2 * kvw + n_idx_heads * LANES + 2 * LANES
    dims = (n_heads, n_kv, n_idx_heads, n_diff)

    hp, hs = x_prompt, x_sample
    rows_p, rows_s = [], []
    n_c = bp + bs
    c_pad = jnp.pad(jnp.concatenate([c_prompt, c_sample], axis=0), ((0, (-n_c) % 16), (0, 0)))
    for l in range(depth):
        mod_all = _adaln(c_pad, w_ada[l], b_ada[l])
        weights = {
            "w_head": w_in[l][:, :head_w].astype(BF16),
            "w_tail": w_in[l][:, tail0:].astype(BF16),
            "w_out": w_out[l].astype(BF16),
            "peer_wq_t": peer_w_query[l].T.astype(BF16),
            "peer_keys": peer_sub_keys[l],
            "peer_u": peer_u[l].astype(BF16), "peer_v_t": peer_v[l].T.astype(BF16),
            "g_norm_mix": g_norm_mix[l], "g_norm_ffn": g_norm_ffn[l], "g_subln": g_diff_subln[l],
            "lams": (diff_lambda_q1[l], diff_lambda_k1[l], diff_lambda_q2[l], diff_lambda_k2[l]),
            "g_final": g_final,
        }
        last = l == depth - 1
        hp, rp = _layer(hp, mod_all[:bp], None, l, last, weights, dims)
        past = (cache_dsa_k[l], cache_dsa_v[l], cache_idx_k[l], cache_diff_k[l], cache_diff_v[l])
        hs, rs = _layer(hs, mod_all[bp:bp + bs], past, l, last, weights, dims)
        rows_p.append(rp)
        rows_s.append(rs)
    stack = lambda rows, i: jnp.stack([r[i] for r in rows])
    return (hp, hs) + tuple(stack(rows_p, i) for i in range(5)) + tuple(stack(rows_s, i) for i in range(5))
```

```python
import functools
import math

import jax
import jax.numpy as jnp
from jax import lax
from jax.experimental import pallas as pl
from jax.experimental.pallas import tpu as pltpu

CHUNK = 64
HEAD_DIM = 128
ROPE_THETA = 10000.0
EPS = 1e-6
DSA_HEADS = 16
IDX_HEADS = 16
DSA_TOPK = 256
PEER_TOPK = 16
LANES = 128
KEY_CHUNK = 512
NEG_BIG = -1e30
INT_MIN = -(2 ** 31)
KEY_NEG_INF = INT_MIN + 0x7FFFFF
VMEM_LIMIT = 56 * 1024 * 1024

BF16 = jnp.bfloat16
F32 = jnp.float32


def _params(sem, vmem=VMEM_LIMIT):
    return pltpu.CompilerParams(dimension_semantics=sem, vmem_limit_bytes=vmem)


def _lambda_init(layer):
    return 0.8 - 0.6 * math.exp(-0.3 * layer)


def _adaln_kernel(c_ref, w_ref, b_ref, o_ref):
    c = c_ref[...]
    a = (c * jax.nn.sigmoid(c)).astype(BF16)
    o_ref[...] = jnp.dot(a, w_ref[...].astype(BF16), preferred_element_type=F32) + b_ref[...]


def _adaln(c_pad, w_ada, b_ada):
    bp, d = c_pad.shape
    n = w_ada.shape[1]
    tn = 512
    return pl.pallas_call(
        _adaln_kernel,
        grid=(n // tn,),
        in_specs=[pl.BlockSpec((bp, d), lambda j: (0, 0)),
                  pl.BlockSpec((d, tn), lambda j: (0, j)),
                  pl.BlockSpec((1, tn), lambda j: (0, j))],
        out_specs=pl.BlockSpec((bp, tn), lambda j: (0, j)),
        out_shape=jax.ShapeDtypeStruct((bp, n), F32),
        compiler_params=_params(("arbitrary",)),
        name="adaln",
    )(c_pad, w_ada, b_ada.reshape(1, n))


def _norm_mod_kernel(x_ref, g_ref, sc_ref, sh_ref, o_ref, *, per_token, transpose):
    x = x_ref[...]
    y = x * lax.rsqrt(jnp.mean(x * x, axis=-1, keepdims=True) + EPS) * g_ref[...]
    sc = sc_ref[...] if per_token else sc_ref[0]
    sh = sh_ref[...] if per_token else sh_ref[0]
    h = y * (1.0 + sc) + sh
    if transpose:
        o_ref[...] = h.T.astype(o_ref.dtype)
    else:
        o_ref[...] = h.astype(o_ref.dtype)


def _mod_spec(mod, per_token, tm, tn, col_block, rows_per_batch):
    if per_token:
        return pl.BlockSpec((tm, tn), lambda m, n=0, cb=col_block: (m, cb + n))
    tiles_per_batch = rows_per_batch // tm
    return pl.BlockSpec((1, 1, tn), lambda m, n=0, cb=col_block: (m // tiles_per_batch, 0, cb + n))


def _norm_mod(x2, g, mod, *, which, per_token, rows_per_batch, tm, transpose):
    t, d = x2.shape
    sh_blk, sc_blk = (0, 1) if which == 1 else (3, 4)
    out_shape = (d, t) if transpose else (t, d)
    out_spec = pl.BlockSpec((d, tm), lambda m: (0, m)) if transpose else pl.BlockSpec((tm, d), lambda m: (m, 0))
    return pl.pallas_call(
        functools.partial(_norm_mod_kernel, per_token=per_token, transpose=transpose),
        grid=(t // tm,),
        in_specs=[pl.BlockSpec((tm, d), lambda m: (m, 0)),
                  pl.BlockSpec((1, d), lambda m: (0, 0)),
                  _mod_spec(mod, per_token, tm, d, sc_blk, rows_per_batch),
                  _mod_spec(mod, per_token, tm, d, sh_blk, rows_per_batch)],
        out_specs=out_spec,
        out_shape=jax.ShapeDtypeStruct(out_shape, BF16),
        compiler_params=_params(("arbitrary",)),
        name="norm_mod_t" if transpose else "norm_mod",
    )(x2, g.reshape(1, d), mod, mod)


def _rope_tile(acc, cos, sin, n_chunks):
    outs = []
    for j in range(n_chunks):
        xj = acc[:, j * LANES:(j + 1) * LANES]
        outs.append(xj * cos + pltpu.roll(xj, LANES // 2, axis=1) * sin)
    return outs[0] if n_chunks == 1 else jnp.concatenate(outs, axis=1)


def _proj_kernel(*refs, rope_cols, scale, out_f32, out_bf16):
    h_ref, w_ref, cos_ref, sin_ref = refs[:4]
    outs = refs[4:]
    acc = jnp.dot(h_ref[...], w_ref[...], preferred_element_type=F32)
    tn = acc.shape[1]
    if rope_cols:
        roped = _rope_tile(acc[:, :rope_cols], cos_ref[...], sin_ref[...], rope_cols // LANES)
        acc = roped if rope_cols == tn else jnp.concatenate([roped, acc[:, rope_cols:]], axis=1)
    k = 0
    if out_f32:
        outs[k][...] = acc
        k += 1
    if out_bf16:
        outs[k][...] = (acc * scale if scale != 1.0 else acc).astype(BF16)


def _proj(h, w, cos, sin, *, col0, n, rope_cols, scale=1.0, out_f32, out_bf16, tm, tn, name):
    t, d = h.shape
    assert rope_cols in (0, tn) or n == tn
    assert col0 % tn == 0 and n % tn == 0 and col0 + n <= w.shape[1]
    j0 = col0 // tn
    pos_tiles = cos.shape[0] // tm
    out_shape, out_specs = [], []
    for want, dt in ((out_f32, F32), (out_bf16, BF16)):
        if want:
            out_shape.append(jax.ShapeDtypeStruct((t, n), dt))
            out_specs.append(pl.BlockSpec((tm, tn), lambda m, j: (m, j)))
    return pl.pallas_call(
        functools.partial(_proj_kernel, rope_cols=rope_cols, scale=scale, out_f32=out_f32, out_bf16=out_bf16),
        grid=(t // tm, n // tn),
        in_specs=[pl.BlockSpec((tm, d), lambda m, j: (m, 0)),
                  pl.BlockSpec((d, tn), lambda m, j: (0, j0 + j)),
                  pl.BlockSpec((tm, LANES), lambda m, j: (m % pos_tiles, 0)),
                  pl.BlockSpec((tm, LANES), lambda m, j: (m % pos_tiles, 0))],
        out_specs=out_specs,
        out_shape=out_shape,
        compiler_params=_params(("arbitrary", "arbitrary")),
        name=name,
    )(h, w, cos, sin)


def _num_key_chunks(q_start, n_q, n_valid_keys, n_chunks_total):
    last_visible = (((q_start + n_q - 1) >> 6) + 1) * CHUNK
    last_visible = jnp.minimum(last_visible, n_valid_keys)
    return jnp.minimum((last_visible + KEY_CHUNK - 1) // KEY_CHUNK, n_chunks_total)


def _num_full_chunks(q_start, n_valid_keys):
    return jnp.minimum(((q_start >> 6) + 1) * CHUNK, n_valid_keys) // KEY_CHUNK


def _ordered_key(x):
    b = pltpu.bitcast(x, jnp.int32)
    return b ^ ((b >> 31) & jnp.int32(0x7FFFFFFF))


def _fill_invisible(x, fill, q_start, n_q, key_start, n_valid_keys, all_keys_valid):
    n_k, n_lanes = x.shape
    kpos = key_start + lax.broadcasted_iota(jnp.int32, (n_k, n_lanes), 0)
    qpos = q_start + jnp.minimum(lax.broadcasted_iota(jnp.int32, (1, n_lanes), 1), n_q - 1)
    x = jnp.where((kpos >> 6) <= (qpos >> 6), x, fill)
    return x if all_keys_valid else jnp.where(kpos < n_valid_keys, x, fill)


def _t_bf16(x, n_lanes):
    x = x.astype(F32)
    if x.shape[0] < n_lanes:
        x = jnp.concatenate([x, jnp.zeros((n_lanes - x.shape[0], x.shape[1]), F32)], axis=0)
    return x.T.astype(BF16)


def _rows_i16(words, n_word_rows):
    tile = jnp.broadcast_to(words, (8, words.shape[1]))
    return pltpu.bitcast(jnp.concatenate([tile] * (n_word_rows // 8), axis=0), jnp.int16)


def _col_partial(x, op, group=8):
    parts = [x[r * group:(r + 1) * group] for r in range(x.shape[0] // group)]
    while len(parts) > 1:
        parts = [op(parts[i], parts[i + 1]) for i in range(0, len(parts) - 1, 2)] + parts[len(parts) & ~1:]
    return parts[0]


def _col_reduce(x, op):
    return (jnp.max if op is jnp.maximum else jnp.sum)(_col_partial(x, op), axis=0, keepdims=True)


def _row_to_cols(row):
    return jnp.broadcast_to(row, (LANES, row.shape[1])).T


def _dsa_kernel(q_ref, qi_ref, wi_ref, k_ref, vaug_ref, ki_ref, o_ref,
                qit_scr, qt_scr, key_scr, hi_scr, lo_scr, acc_scr, *,
                tq, tl, n_sel, p_len, n_valid_keys, all_keys_valid, n_idx_heads, n_kv_heads, rep):
    n_chunks_total = key_scr.shape[0]
    q_start = p_len + pl.program_id(1) * tq
    n_ch = _num_key_chunks(q_start, tq, n_valid_keys, n_chunks_total)
    heads_per_dot = 4
    half_chunk = KEY_CHUNK // 2

    for h in range(n_idx_heads):
        qit_scr[:, h * tl:(h + 1) * tl] = _t_bf16(qi_ref[:, h * LANES:(h + 1) * LANES], tl)
    for g in range(n_kv_heads):
        for r in range(rep):
            hh = g * rep + r
            qt_scr[g, :, r * tl:(r + 1) * tl] = _t_bf16(q_ref[:, hh * LANES:(hh + 1) * LANES], tl)
    wi = wi_ref[:, LANES:2 * LANES] * (n_idx_heads ** -0.5)
    if tq < tl:
        wi = jnp.concatenate([wi, jnp.zeros((tl - tq, LANES), F32)], axis=0)
    wi_t = wi.T

    def idx_body(j, carry):
        k0 = pl.multiple_of(j * KEY_CHUNK, KEY_CHUNK)
        kib = ki_ref[pl.ds(k0, KEY_CHUNK), :]
        score = jnp.zeros((KEY_CHUNK, tl), F32)
        for h0 in range(0, n_idx_heads, heads_per_dot):
            lg = jnp.dot(kib, qit_scr[:, h0 * tl:(h0 + heads_per_dot) * tl], preferred_element_type=F32)
            for h in range(h0, h0 + heads_per_dot):
                score = score + jnp.maximum(lg[:, (h - h0) * tl:(h - h0 + 1) * tl], 0.0) * wi_t[h:h + 1, :]
        score = _fill_invisible(score, -jnp.inf, q_start, tq, k0, n_valid_keys, all_keys_valid)
        key = _ordered_key(score)
        key_scr[j] = key
        hi16 = (key >> 16) & 0xFFFF
        lo16 = (key & 0xFFFF) ^ 0x8000
        hi_scr[j] = hi16[:half_chunk] | (hi16[half_chunk:] << 16)
        lo_scr[j] = lo16[:half_chunk] | (lo16[half_chunk:] << 16)
        return carry

    lax.fori_loop(0, n_ch, idx_body, 0)

    def count16(scr, pattern, strict):
        c = _rows_i16(pattern | (pattern << 16), half_chunk)

        def body(j, cnt):
            x = pltpu.bitcast(scr[j], jnp.int16)
            ones = jnp.where((x > c) if strict else (x >= c), jnp.int16(1), jnp.int16(0))
            return cnt + _col_partial(ones, jnp.add, group=16)

        cnt = pltpu.bitcast(lax.fori_loop(0, n_ch, body, jnp.zeros((16, tl), jnp.int16)), jnp.int32)
        return jnp.sum((cnt & 0xFFFF) + ((cnt >> 16) & 0xFFFF), axis=0, keepdims=True)

    def bisect16(scr, need):
        def bit_body(b, t):
            cand = t | lax.shift_left(jnp.int32(1), 15 - b)
            return jnp.where(count16(scr, cand ^ 0x8000, False) >= need, cand, t)
        return lax.fori_loop(0, 16, bit_body, jnp.zeros((1, tl), jnp.int32))

    hi_pat = bisect16(hi_scr, n_sel) ^ 0x8000
    need = n_sel - count16(hi_scr, hi_pat, True)

    def keep_bucket(j, carry):
        hi = pltpu.bitcast(hi_scr[j], jnp.int16)
        lo = pltpu.bitcast(lo_scr[j], jnp.int16)
        same = hi == _rows_i16(hi_pat | (hi_pat << 16), half_chunk)
        lo_scr[j] = pltpu.bitcast(jnp.where(same, lo, jnp.int16(-2 ** 15)), jnp.int32)
        return carry

    lax.fori_loop(0, n_ch, keep_bucket, 0)
    lo_u = bisect16(lo_scr, need)
    thr = (((hi_pat << 16) >> 16) << 16) | lo_u
    thr = jnp.maximum(thr, KEY_NEG_INF + 1)

    acc_scr[...] = jnp.zeros_like(acc_scr)

    def att_body(j, ms):
        k0 = pl.multiple_of(j * KEY_CHUNK, KEY_CHUNK)
        bias = jnp.where(key_scr[j] >= thr, 0.0, NEG_BIG)
        bias = jnp.concatenate([bias] * rep, axis=1)
        new = []
        for g in range(n_kv_heads):
            kb = k_ref[pl.ds(k0, KEY_CHUNK), g * LANES:(g + 1) * LANES]
            s = jnp.dot(kb, qt_scr[g], preferred_element_type=F32) + bias
            m_new = jnp.maximum(ms[g], _col_reduce(s, jnp.maximum))
            p = jnp.exp2(s - m_new).astype(BF16)
            va = vaug_ref[pl.ds(k0, KEY_CHUNK), g * 2 * LANES:(g + 1) * 2 * LANES]
            alpha = _row_to_cols(jnp.exp2(ms[g] - m_new))
            acc_scr[g] = (acc_scr[g] * jnp.concatenate([alpha, alpha], axis=1)
                          + lax.dot_general(p, va, (((0,), (0,)), ((), ())), preferred_element_type=F32))
            new.append(m_new)
        return tuple(new)

    lax.fori_loop(0, n_ch, att_body, tuple(jnp.full((1, rep * tl), NEG_BIG, F32) for _ in range(n_kv_heads)))

    for g in range(n_kv_heads):
        acc = acc_scr[g]
        o = acc[:, :LANES] / acc[:, LANES:]
        for r in range(rep):
            hh = g * rep + r
            o_ref[:, hh * LANES:(hh + 1) * LANES] = o[r * tl:r * tl + tq].astype(o_ref.dtype)


def _dsa_attention(q_all, qi_all, kiw_f32, k_bf, vaug_bf, ki_bf, *, batch, q_len, k_len, n_valid_keys, p_len, tq,
                   n_heads, n_kv_heads, n_idx_heads, n_sel):
    width = n_heads * HEAD_DIM
    assert qi_all.shape[1] == n_idx_heads * LANES
    nq = q_len // tq
    tl = max(tq, LANES)
    n_chunks = k_len // KEY_CHUNK
    rep = n_heads // n_kv_heads
    kv_w = n_kv_heads * HEAD_DIM
    return pl.pallas_call(
        functools.partial(_dsa_kernel, tq=tq, tl=tl, n_sel=n_sel, p_len=p_len, n_valid_keys=n_valid_keys,
                          all_keys_valid=n_valid_keys == k_len,
                          n_idx_heads=n_idx_heads, n_kv_heads=n_kv_heads, rep=rep),
        grid=(batch, nq),
        in_specs=[pl.BlockSpec((tq, width), lambda b, i: (b * nq + i, 0)),
                  pl.BlockSpec((tq, n_idx_heads * LANES), lambda b, i: (b * nq + i, 0)),
                  pl.BlockSpec((tq, 2 * LANES), lambda b, i: (b * nq + i, 0)),
                  pl.BlockSpec((k_len, kv_w), lambda b, i: (b, 0)),
                  pl.BlockSpec((k_len, 2 * kv_w), lambda b, i: (b, 0)),
                  pl.BlockSpec((k_len, LANES), lambda b, i: (b, 0))],
        out_specs=pl.BlockSpec((tq, width), lambda b, i: (b * nq + i, 0)),
        out_shape=jax.ShapeDtypeStruct((batch * q_len, width), BF16),
        scratch_shapes=[pltpu.VMEM((HEAD_DIM, n_idx_heads * tl), BF16),
                        pltpu.VMEM((n_kv_heads, HEAD_DIM, rep * tl), BF16),
                        pltpu.VMEM((n_chunks, KEY_CHUNK, tl), jnp.int32),
                        pltpu.VMEM((n_chunks, KEY_CHUNK // 2, tl), jnp.int32),
                        pltpu.VMEM((n_chunks, KEY_CHUNK // 2, tl), jnp.int32),
                        pltpu.VMEM((n_kv_heads, rep * tl, 2 * LANES), F32)],
        compiler_params=_params(("arbitrary", "arbitrary")),
        name="dsa_attention",
    )(q_all, qi_all, kiw_f32, k_bf, vaug_bf, ki_bf)


def _diff_kernel(dq_ref, dk_ref, dv_ref, lq1_ref, lk1_ref, lq2_ref, lk2_ref, g_ref, o_ref,
                 acc_scr, *, tq, tl, hp, n_chunks_total, p_len, n_valid_keys, all_keys_valid, lam_init):
    q_start = p_len + pl.program_id(2) * tq
    n_ch = _num_key_chunks(q_start, tq, n_valid_keys, n_chunks_total)
    hw = 2 * HEAD_DIM
    n_maps = 2 * hp

    lam = (jnp.exp(jnp.sum(lq1_ref[...] * lk1_ref[...], axis=1, keepdims=True))
           - jnp.exp(jnp.sum(lq2_ref[...] * lk2_ref[...], axis=1, keepdims=True)) + lam_init)
    qt = [_t_bf16(dq_ref[:, c * LANES:(c + 1) * LANES], tl) for c in range(n_maps)]

    acc_scr[...] = jnp.zeros_like(acc_scr)

    def att_body(j, carry, masked):
        k0 = pl.multiple_of(j * KEY_CHUNK, KEY_CHUNK)
        new = []
        for c in range(n_maps):
            m, l = carry[2 * c:2 * c + 2]
            kb = dk_ref[pl.ds(k0, KEY_CHUNK), c * LANES:(c + 1) * LANES]
            vb = dv_ref[pl.ds(k0, KEY_CHUNK), (c // 2) * hw:(c // 2 + 1) * hw]
            s = jnp.dot(kb, qt[c], preferred_element_type=F32)
            if masked:
                s = _fill_invisible(s, NEG_BIG, q_start, tq, k0, n_valid_keys, all_keys_valid)
            m_new = jnp.maximum(m, _col_reduce(s, jnp.maximum))
            p = jnp.exp2(s - m_new)
            alpha = jnp.exp2(m - m_new)
            alpha_c = _row_to_cols(alpha)
            acc_scr[c] = (acc_scr[c] * jnp.concatenate([alpha_c, alpha_c], axis=1)
                          + lax.dot_general(p.astype(BF16), vb, (((0,), (0,)), ((), ())),
                                            preferred_element_type=F32))
            new += [m_new, alpha * l + _col_reduce(p, jnp.add)]
        return tuple(new)

    one = (jnp.full((1, tl), NEG_BIG, F32), jnp.zeros((1, tl), F32))
    n_full = jnp.minimum(_num_full_chunks(q_start, n_valid_keys), n_ch)
    carry = lax.fori_loop(0, n_full, functools.partial(att_body, masked=False), one * n_maps)
    carry = lax.fori_loop(n_full, n_ch, functools.partial(att_body, masked=True), carry)

    def normalised(c):
        l_cols = _row_to_cols(carry[2 * c + 1])
        return acc_scr[c] / jnp.concatenate([l_cols, l_cols], axis=1)

    for h in range(hp):
        o = (normalised(2 * h) - lam * normalised(2 * h + 1))[:tq]
        o = o * lax.rsqrt(jnp.mean(o * o, axis=-1, keepdims=True) + EPS) * g_ref[...]
        o_ref[:, h * hw:(h + 1) * hw] = (o * (1.0 - lam_init)).astype(o_ref.dtype)


def _diff_attention(dq_all, dk_bf, dv_bf, lams, g_subln, *, batch, q_len, k_len, n_valid_keys, p_len, tq,
                    n_heads, lam_init):
    hw = 2 * HEAD_DIM
    hp = 2 if n_heads % 2 == 0 else 1
    width = n_heads * hw
    assert dq_all.shape[1] == width
    nq = q_len // tq
    tl = max(tq, LANES)
    n_chunks = k_len // KEY_CHUNK
    vec = pl.BlockSpec((1, HEAD_DIM), lambda b, h, i: (0, 0))
    return pl.pallas_call(
        functools.partial(_diff_kernel, tq=tq, tl=tl, hp=hp, n_chunks_total=n_chunks, p_len=p_len,
                          n_valid_keys=n_valid_keys, all_keys_valid=n_valid_keys == k_len, lam_init=lam_init),
        grid=(batch, n_heads // hp, nq),
        in_specs=[pl.BlockSpec((tq, hp * hw), lambda b, h, i: (b * nq + i, h)),
                  pl.BlockSpec((k_len, hp * hw), lambda b, h, i: (b, h)),
                  pl.BlockSpec((k_len, hp * hw), lambda b, h, i: (b, h)),
                  vec, vec, vec, vec,
                  pl.BlockSpec((1, hw), lambda b, h, i: (0, 0))],
        out_specs=pl.BlockSpec((tq, hp * hw), lambda b, h, i: (b * nq + i, h)),
        out_shape=jax.ShapeDtypeStruct((batch * q_len, width), BF16),
        scratch_shapes=[pltpu.VMEM((2 * hp, tl, hw), F32)],
        compiler_params=_params(("arbitrary", "arbitrary", "arbitrary")),
        name="diff_attention",
    )(dq_all, dk_bf, dv_bf, *[v.reshape(1, HEAD_DIM) for v in lams], g_subln.reshape(1, hw))


def _outproj_kernel(a_ref, d_ref, wa_ref, wd_ref, x_ref, ga_ref, o_ref, *, per_token):
    mix = (jnp.dot(a_ref[...], wa_ref[...], preferred_element_type=F32)
           + jnp.dot(d_ref[...], wd_ref[...], preferred_element_type=F32))
    ga = ga_ref[...] if per_token else ga_ref[0]
    o_ref[...] = x_ref[...] + ga * mix


def _outproj(a_out, d_out, w_out_bf, x2, mod, *, per_token, rows_per_batch, tm, tn):
    t, d = x2.shape
    wa = a_out.shape[1]
    wd = d_out.shape[1]
    assert wa == wd
    cb = 2 * (d // tn)
    return pl.pallas_call(
        functools.partial(_outproj_kernel, per_token=per_token),
        grid=(t // tm, d // tn),
        in_specs=[pl.BlockSpec((tm, wa), lambda m, n: (m, 0)),
                  pl.BlockSpec((tm, wd), lambda m, n: (m, 0)),
                  pl.BlockSpec((wa, tn), lambda m, n: (0, n)),
                  pl.BlockSpec((wd, tn), lambda m, n: (1, n)),
                  pl.BlockSpec((tm, tn), lambda m, n: (m, n)),
                  _mod_spec(mod, per_token, tm, tn, cb, rows_per_batch)],
        out_specs=pl.BlockSpec((tm, tn), lambda m, n: (m, n)),
        out_shape=jax.ShapeDtypeStruct((t, d), F32),
        compiler_params=_params(("arbitrary", "arbitrary")),
        name="outproj",
    )(a_out, d_out, w_out_bf, w_out_bf, x2, mod)


def _top_rows(x, k, with_rank=False):
    tops = []
    rank = jnp.full(x.shape, float(k), F32) if with_rank else None
    for i in range(k):
        mx = jnp.max(x, axis=0, keepdims=True)
        tops.append(mx)
        hit = x == mx
        if with_rank:
            rank = jnp.where(hit, float(i), rank)
        x = jnp.where(hit, -jnp.inf, x)
    return (tops, rank) if with_rank else tops


def _pack_bf16_pair(lo, hi):
    def bits(x):
        b = pltpu.bitcast(x, jnp.uint32)
        return (b + jnp.uint32(0x7FFF) + ((b >> 16) & jnp.uint32(1))) >> 16
    return bits(lo) | (bits(hi) << 16)


def _peer_route_kernel(h_ref, wq_ref, keys_ref, cnt_ref, e1_ref, rank_ref, e2_ref, q_scr, s1_scr, top_scr):
    hc = pl.program_id(1)
    c = hc % 2
    half = keys_ref.shape[2]

    @pl.when(hc == 0)
    def _():
        q_scr[...] = jnp.dot(wq_ref[...], h_ref[...], preferred_element_type=F32).astype(BF16)

    q_t = q_scr[pl.ds(pl.multiple_of(hc * half, half), half), :]
    s_t = jnp.dot(keys_ref[0].astype(BF16), q_t, preferred_element_type=F32)

    @pl.when(c == 0)
    def _():
        s1_scr[...] = s_t
        top_scr[...] = jnp.concatenate(_top_rows(s_t, PEER_TOPK), axis=0)

    @pl.when(c == 1)
    def _():
        tops2, rank2 = _top_rows(s_t, PEER_TOPK, with_rank=True)
        tops = jnp.concatenate(tops2, axis=0)
        top1 = top_scr[...]
        cand = jnp.concatenate([top1[0:1, :] + tops]
                               + [top1[i:i + 1, :] + tops[:PEER_TOPK // 2] for i in range(1, PEER_TOPK)], axis=0)
        best = _top_rows(cand, PEER_TOPK)
        m = best[0]
        z = jnp.zeros_like(m)
        for bk in best:
            z = z + jnp.exp(bk - m)
        thr = best[PEER_TOPK - 1]
        s1 = s1_scr[...]
        cnt = jnp.zeros_like(s1)
        for j in range(PEER_TOPK):
            cnt = cnt + jnp.where(s1 + tops2[j] >= thr, 1.0, 0.0)
        e1 = jnp.exp(s1 - top1[0:1, :]) * (0.5 / z)
        e2 = jnp.exp(s_t - tops2[0])
        hn = s_t.shape[0] // 2
        cnt_ref[0] = _pack_bf16_pair(cnt, cnt)
        e1_ref[0] = _pack_bf16_pair(e1, e1)
        rank_ref[0] = _pack_bf16_pair(rank2[:hn], rank2[hn:])
        e2_ref[0] = _pack_bf16_pair(e2[:hn], e2[hn:])


def _peer_route(h_t, wq_t_bf, sub_keys, *, tm):
    d, t = h_t.shape
    heads, _, n_keys, half = sub_keys.shape
    keys2 = sub_keys.reshape(heads * 2, n_keys, half)
    a_spec = pl.BlockSpec((1, n_keys, tm), lambda m, hc: (hc // 2, 0, m))
    b_spec = pl.BlockSpec((1, n_keys // 2, tm), lambda m, hc: (hc // 2, 0, m))
    a_tab = jax.ShapeDtypeStruct((heads, n_keys, t), jnp.uint32)
    b_tab = jax.ShapeDtypeStruct((heads, n_keys // 2, t), jnp.uint32)
    return pl.pallas_call(
        _peer_route_kernel,
        grid=(t // tm, heads * 2),
        in_specs=[pl.BlockSpec((d, tm), lambda m, hc: (0, m)),
                  pl.BlockSpec((heads * 2 * half, d), lambda m, hc: (0, 0), pipeline_mode=pl.Buffered(1)),
                  pl.BlockSpec((1, n_keys, half), lambda m, hc: (hc, 0, 0))],
        out_specs=[a_spec, a_spec, b_spec, b_spec],
        out_shape=[a_tab, a_tab, b_tab, b_tab],
        scratch_shapes=[pltpu.VMEM((heads * 2 * half, tm), BF16), pltpu.VMEM((n_keys, tm), F32),
                        pltpu.VMEM((PEER_TOPK, tm), F32)],
        compiler_params=_params(("arbitrary", "arbitrary")),
        name="peer_route",
    )(h_t, wq_t_bf, keys2)


def _gated_gelu(x, half_gate):
    c = 0.7978845608028654
    inner = x * (c + (c * 0.044715) * (x * x))
    return (x * half_gate) * (1.0 + jnp.tanh(inner))


def _rows_bf16(row_words, n_rows):
    tile = jnp.broadcast_to(row_words, (8, row_words.shape[1]))
    return pltpu.bitcast(jnp.concatenate([tile] * (n_rows // 16), axis=0), BF16)


def _peer_ffn_kernel(h_ref, u_ref, vt_ref, cnt_ref, e1_ref, rank_ref, e2_ref, o_ref, *, n_keys):
    e = pl.program_id(1)
    te, tm = u_ref.shape[0], h_ref.shape[1]
    heads = cnt_ref.shape[0]
    a0 = e * (te // n_keys)

    @pl.when(e == 0)
    def _():
        o_ref[...] = jnp.zeros_like(o_ref)

    act = jnp.dot(u_ref[...], h_ref[...], preferred_element_type=F32)
    strip = min(tm, LANES)
    hn = n_keys // 2
    w_rows = [[None] * (tm // strip) for _ in range(2 * te // n_keys)]
    for ai in range(te // n_keys):
        cnt_a = [cnt_ref[h, pl.ds(a0 + ai, 1), :] for h in range(heads)]
        e1_a = [e1_ref[h, pl.ds(a0 + ai, 1), :] for h in range(heads)]
        for c in range(tm // strip):
            cols = slice(c * strip, (c + 1) * strip)
            gate = jnp.zeros((n_keys, strip), BF16)
            for h in range(heads):
                cnt_b = _rows_bf16(cnt_a[h][:, cols], n_keys)
                e1_b = _rows_bf16(e1_a[h][:, cols], n_keys)
                keep = pltpu.bitcast(rank_ref[h, :, cols], BF16) < cnt_b
                gate = gate + jnp.where(keep, pltpu.bitcast(e2_ref[h, :, cols], BF16) * e1_b,
                                        jnp.zeros_like(e1_b))
            words = pltpu.bitcast(gate, jnp.uint32)
            halves = (pltpu.bitcast(words << 16, F32), pltpu.bitcast(words & jnp.uint32(0xFFFF0000), F32))
            for k, g in enumerate(halves):
                rows = slice(ai * n_keys + k * hn, ai * n_keys + (k + 1) * hn)
                w_rows[2 * ai + k][c] = _gated_gelu(act[rows, cols], g).astype(BF16)
    w = jnp.concatenate([r[0] if len(r) == 1 else jnp.concatenate(r, axis=1) for r in w_rows], axis=0)
    o_ref[...] += jnp.dot(vt_ref[...], w, preferred_element_type=F32)


def _peer_ffn(h_t, u_bf, vt_bf, cnt, e1, rank, e2, *, tm, te):
    d, t = h_t.shape
    n_exp = u_bf.shape[0]
    n_blocks = n_exp // te
    heads, n_keys, _ = cnt.shape
    once = pl.Buffered(1)
    a_spec = pl.BlockSpec((heads, n_keys, tm), lambda m, e: (0, 0, m), pipeline_mode=once)
    b_spec = pl.BlockSpec((heads, n_keys // 2, tm), lambda m, e: (0, 0, m), pipeline_mode=once)
    return pl.pallas_call(
        functools.partial(_peer_ffn_kernel, n_keys=n_keys),
        grid=(t // tm, n_blocks),
        in_specs=[pl.BlockSpec((d, tm), lambda m, e: (0, m), pipeline_mode=once),
                  pl.BlockSpec((te, d), lambda m, e: (e, 0)),
                  pl.BlockSpec((d, te), lambda m, e: (0, e)),
                  a_spec, a_spec, b_spec, b_spec],
        out_specs=pl.BlockSpec((d, tm), lambda m, e: (0, m), pipeline_mode=once),
        out_shape=jax.ShapeDtypeStruct((d, t), F32),
        compiler_params=_params(("arbitrary", "arbitrary"), vmem=62 * 1024 * 1024),
        name="peer_ffn",
    )(h_t, u_bf, vt_bf, cnt, e1, rank, e2)


def _final_kernel(x_ref, pt_ref, ga_ref, g_ref, o_ref, *, per_token, normalize):
    ga = ga_ref[...] if per_token else ga_ref[0]
    x = x_ref[...] + ga * pt_ref[...].T
    if normalize:
        x = x * lax.rsqrt(jnp.mean(x * x, axis=-1, keepdims=True) + EPS) * g_ref[...]
    o_ref[...] = x


def _final(x1, peer_t, mod, g_final, *, per_token, rows_per_batch, tm, normalize):
    t, d = x1.shape
    row = pl.BlockSpec((tm, d), lambda m: (m, 0))
    return pl.pallas_call(
        functools.partial(_final_kernel, per_token=per_token, normalize=normalize),
        grid=(t // tm,),
        in_specs=[row, pl.BlockSpec((d, tm), lambda m: (0, m)), _mod_spec(mod, per_token, tm, d, 5, rows_per_batch),
                  pl.BlockSpec((1, d), lambda m: (0, 0))],
        out_specs=row,
        out_shape=jax.ShapeDtypeStruct((t, d), F32),
        compiler_params=_params(("arbitrary",)),
        name="final",
    )(x1, peer_t, mod, g_final.reshape(1, d))


def _rope_tables(pos):
    half = HEAD_DIM // 2
    inv = ROPE_THETA ** (-jnp.arange(half, dtype=F32) / half)
    ang = pos.astype(F32)[:, None] * inv[None, :]
    cos, sin = jnp.cos(ang), jnp.sin(ang)
    return jnp.concatenate([cos, cos], axis=1), jnp.concatenate([-sin, sin], axis=1)


def _pick_tile(n, prefs):
    for p in prefs:
        if n % p == 0:
            return p
    return n


def _layer(x, mod_rows, past, layer, last_layer, w, dims):
    b, s, d = x.shape
    t = b * s
    n_heads, n_kv, n_idx, n_diff = dims
    p_len = 0 if past is None else past[0].shape[1]
    n_keys_valid = p_len + s
    n_sel = min(DSA_TOPK, n_keys_valid // 4)
    x2 = x.reshape(t, d)

    per_token = s % 256 != 0
    if per_token:
        mod = jnp.repeat(mod_rows, s, axis=0)
    else:
        mod = mod_rows.reshape(b, 1, 6 * d)
    tm_big = _pick_tile(t if per_token else s, (1024, 512, 256, 128))
    tm_mid = _pick_tile(t if per_token else s, (512, 256, 128))
    tm_small = _pick_tile(t if per_token else s, (256, 128))

    pos = p_len + jnp.arange(s)
    cos, sin = _rope_tables(pos)
    if per_token:
        cos, sin = jnp.tile(cos, (b, 1)), jnp.tile(sin, (b, 1))

    h = _norm_mod(x2, w["g_norm_mix"], mod, which=1, per_token=per_token, rows_per_batch=s, tm=tm_mid,
                  transpose=False)

    qw, kvw, dw = n_heads * HEAD_DIM, n_kv * HEAD_DIM, n_diff * 2 * HEAD_DIM
    head = functools.partial(_proj, h, w["w_head"], cos=cos, sin=sin, tm=tm_big)
    tail = functools.partial(_proj, h, w["w_tail"], cos=cos, sin=sin, tm=tm_big)
    q_scale = HEAD_DIM ** -0.5 * math.log2(math.e)
    bf_only = dict(out_f32=False, out_bf16=True, tn=512)
    both = dict(out_f32=True, out_bf16=True, tn=512)
    (q_all,) = head(col0=0, n=qw, rope_cols=512, scale=q_scale, name="proj_q", **bf_only)
    (qi_all,) = head(col0=qw + 2 * kvw, n=n_idx * LANES, rope_cols=512, scale=q_scale, name="proj_qi", **bf_only)
    (dq_all,) = tail(col0=0, n=dw, rope_cols=512, scale=q_scale, name="proj_dq", **bf_only)
    k_f, k_b = head(col0=qw, n=kvw, rope_cols=512, name="proj_k", **both)
    v_f, v_b = head(col0=qw + kvw, n=kvw, rope_cols=0, name="proj_v", **both)
    dk_f, dk_b = tail(col0=dw, n=dw, rope_cols=512, name="proj_dk", **both)
    dv_f, dv_b = tail(col0=2 * dw, n=dw, rope_cols=0, name="proj_dv", **both)
    kiw_f, kiw_b = head(col0=qw + 2 * kvw + n_idx * LANES, n=2 * LANES, rope_cols=LANES, out_f32=True,
                        out_bf16=True, tn=2 * LANES, name="proj_kiw")

    ki_f = kiw_f[:, :LANES]
    new_rows = (k_f.reshape(b, s, n_kv, HEAD_DIM), v_f.reshape(b, s, n_kv, HEAD_DIM), ki_f.reshape(b, s, LANES),
                dk_f.reshape(b, s, n_diff, 2, HEAD_DIM), dv_f.reshape(b, s, n_diff, 2 * HEAD_DIM))

    if past is None:
        k_len = s
        k_all, v_all, ki_all, dk_all, dv_all = k_b, v_b, kiw_b, dk_b, dv_b
    else:
        k_len = -(-n_keys_valid // KEY_CHUNK) * KEY_CHUNK

        def join(cache, new, width):
            new = new.reshape(b, s, -1)[:, :, :width]
            old = lax.optimization_barrier(cache.reshape(b, p_len, width))
            both_ = jnp.concatenate([old.astype(BF16), new], axis=1)
            both_ = jnp.pad(both_, ((0, 0), (0, k_len - n_keys_valid), (0, 0)))
            return both_.reshape(b * k_len, width)

        k_all = join(past[0], k_b, n_kv * HEAD_DIM)
        v_all = join(past[1], v_b, n_kv * HEAD_DIM)
        ki_all = join(past[2], kiw_b, LANES)
        dk_all = join(past[3], dk_b, n_diff * 2 * HEAD_DIM)
        dv_all = join(past[4], dv_b, n_diff * 2 * HEAD_DIM)

    ones = jnp.ones((v_all.shape[0], HEAD_DIM), BF16)
    vaug_all = jnp.concatenate(
        [piece for g in range(n_kv) for piece in (v_all[:, g * HEAD_DIM:(g + 1) * HEAD_DIM], ones)], axis=1)
    tq_a = _pick_tile(s, (128, 64, 32))
    a_out = _dsa_attention(q_all, qi_all, kiw_f, k_all, vaug_all, ki_all, batch=b, q_len=s, k_len=k_len,
                           n_valid_keys=n_keys_valid, p_len=p_len, tq=tq_a, n_heads=n_heads, n_kv_heads=n_kv,
                           n_idx_heads=n_idx, n_sel=n_sel)
    tq_d = _pick_tile(s, (256, 128, 64, 32))
    d_out = _diff_attention(dq_all, dk_all, dv_all, w["lams"], w["g_subln"], batch=b, q_len=s, k_len=k_len,
                            n_valid_keys=n_keys_valid, p_len=p_len, tq=tq_d, n_heads=n_diff,
                            lam_init=_lambda_init(layer))

    x1 = _outproj(a_out, d_out, w["w_out"], x2, mod, per_token=per_token, rows_per_batch=s, tm=tm_big, tn=512)

    h2_t = _norm_mod(x1, w["g_norm_ffn"], mod, which=2, per_token=per_token, rows_per_batch=s, tm=tm_mid,
                     transpose=True)
    cnt, e1, rank, e2 = _peer_route(h2_t, w["peer_wq_t"], w["peer_keys"], tm=tm_mid)
    peer_t = _peer_ffn(h2_t, w["peer_u"], w["peer_v_t"], cnt, e1, rank, e2, tm=tm_mid, te=1024)
    x_out = _final(x1, peer_t, mod, w["g_final"], per_token=per_token, rows_per_batch=s, tm=tm_small,
                   normalize=last_layer)
    return x_out.reshape(b, s, d), new_rows


def kernel(x_prompt, x_sample, cache_dsa_k, cache_dsa_v, cache_idx_k, cache_diff_k, cache_diff_v, c_prompt, c_sample, w_ada, b_ada, g_norm_mix, g_norm_ffn, w_in, diff_lambda_q1, diff_lambda_k1, diff_lambda_q2, diff_lambda_k2, g_diff_subln, w_out, peer_w_query, peer_sub_keys, peer_u, peer_v, g_final):
    depth = w_in.shape[0]
    bp, bs = x_prompt.shape[0], x_sample.shape[0]
    n_kv = cache_dsa_k.shape[3]
    n_diff = cache_diff_k.shape[3]
    n_heads, n_idx_heads = DSA_HEADS, IDX_HEADS
    qw, kvw, dw = n_heads * HEAD_DIM, n_kv * HEAD_DIM, n_diff * 2 * HEAD_DIM
    tail0 = qw + 2 * kvw + n_idx_heads * LANES + LANES + n_idx_heads
    assert w_in.shape[2] == tail0 + 3 * dw
    head_w = qw + 2 * kvw + n_idx_heads * LANES + 2 * LANES
    dims = (n_heads, n_kv, n_idx_heads, n_diff)

    hp, hs = x_prompt, x_sample
    rows_p, rows_s = [], []
    n_c = bp + bs
    c_pad = jnp.pad(jnp.concatenate([c_prompt, c_sample], axis=0), ((0, (-n_c) % 16), (0, 0)))
    for l in range(depth):
        mod_all = _adaln(c_pad, w_ada[l], b_ada[l])
        weights = {
            "w_head": w_in[l][:, :head_w].astype(BF16),
            "w_tail": w_in[l][:, tail0:].astype(BF16),
            "w_out": w_out[l].astype(BF16),
            "peer_wq_t": peer_w_query[l].T.astype(BF16),
            "peer_keys": peer_sub_keys[l],
            "peer_u": peer_u[l].astype(BF16), "peer_v_t": peer_v[l].T.astype(BF16),
            "g_norm_mix": g_norm_mix[l], "g_norm_ffn": g_norm_ffn[l], "g_subln": g_diff_subln[l],
            "lams": (diff_lambda_q1[l], diff_lambda_k1[l], diff_lambda_q2[l], diff_lambda_k2[l]),
            "g_final": g_final,
        }
        last = l == depth - 1
        hp, rp = _layer(hp, mod_all[:bp], None, l, last, weights, dims)
        past = (cache_dsa_k[l], cache_dsa_v[l], cache_idx_k[l], cache_diff_k[l], cache_diff_v[l])
        hs, rs = _layer(hs, mod_all[bp:bp + bs], past, l, last, weights, dims)
        rows_p.append(rp)
        rows_s.append(rs)
    stack = lambda rows, i: jnp.stack([r[i] for r in rows])
    return (hp, hs) + tuple(stack(rows_p, i) for i in range(5)) + tuple(stack(rows_s, i) for i in range(5))
```

```python
import functools
import math

import jax
import jax.numpy as jnp
from jax import lax
from jax.experimental import pallas as pl
from jax.experimental.pallas import tpu as pltpu

CHUNK = 64
HEAD_DIM = 128
ROPE_THETA = 10000.0
EPS = 1e-6
DSA_HEADS = 16
IDX_HEADS = 16
DSA_TOPK = 256
PEER_TOPK = 16
LANES = 128
KEY_CHUNK = 512
NEG_BIG = -1e30
INT_MIN = -(2 ** 31)
KEY_NEG_INF = INT_MIN + 0x7FFFFF
VMEM_LIMIT = 56 * 1024 * 1024
PEER_FFN_VMEM = 62 * 1024 * 1024

BF16 = jnp.bfloat16
F32 = jnp.float32


def _params(sem, vmem=VMEM_LIMIT):
    return pltpu.CompilerParams(dimension_semantics=sem, vmem_limit_bytes=vmem)


def _lambda_init(layer):
    return 0.8 - 0.6 * math.exp(-0.3 * layer)


def _adaln_kernel(c_ref, w_ref, b_ref, o_ref):
    c = c_ref[...]
    a = (c * jax.nn.sigmoid(c)).astype(BF16)
    o_ref[...] = jnp.dot(a, w_ref[...].astype(BF16), preferred_element_type=F32) + b_ref[...]


def _adaln(c_pad, w_ada, b_ada):
    bp, d = c_pad.shape
    n = w_ada.shape[1]
    tn = 512
    return pl.pallas_call(
        _adaln_kernel,
        grid=(n // tn,),
        in_specs=[pl.BlockSpec((bp, d), lambda j: (0, 0)),
                  pl.BlockSpec((d, tn), lambda j: (0, j)),
                  pl.BlockSpec((1, tn), lambda j: (0, j))],
        out_specs=pl.BlockSpec((bp, tn), lambda j: (0, j)),
        out_shape=jax.ShapeDtypeStruct((bp, n), F32),
        compiler_params=_params(("arbitrary",)),
        name="adaln",
    )(c_pad, w_ada, b_ada.reshape(1, n))


def _norm_mod_kernel(x_ref, g_ref, sc_ref, sh_ref, o_ref, *, per_token, transpose):
    x = x_ref[...]
    y = x * lax.rsqrt(jnp.mean(x * x, axis=-1, keepdims=True) + EPS) * g_ref[...]
    sc = sc_ref[...] if per_token else sc_ref[0]
    sh = sh_ref[...] if per_token else sh_ref[0]
    h = y * (1.0 + sc) + sh
    if transpose:
        o_ref[...] = h.T.astype(o_ref.dtype)
    else:
        o_ref[...] = h.astype(o_ref.dtype)


def _mod_spec(mod, per_token, tm, tn, col_block, rows_per_batch):
    if per_token:
        return pl.BlockSpec((tm, tn), lambda m, n=0, cb=col_block: (m, cb + n))
    tiles_per_batch = rows_per_batch // tm
    return pl.BlockSpec((1, 1, tn), lambda m, n=0, cb=col_block: (m // tiles_per_batch, 0, cb + n))


def _norm_mod(x2, g, mod, *, which, per_token, rows_per_batch, tm, transpose):
    t, d = x2.shape
    sh_blk, sc_blk = (0, 1) if which == 1 else (3, 4)
    out_shape = (d, t) if transpose else (t, d)
    out_spec = pl.BlockSpec((d, tm), lambda m: (0, m)) if transpose else pl.BlockSpec((tm, d), lambda m: (m, 0))
    return pl.pallas_call(
        functools.partial(_norm_mod_kernel, per_token=per_token, transpose=transpose),
        grid=(t // tm,),
        in_specs=[pl.BlockSpec((tm, d), lambda m: (m, 0)),
                  pl.BlockSpec((1, d), lambda m: (0, 0)),
                  _mod_spec(mod, per_token, tm, d, sc_blk, rows_per_batch),
                  _mod_spec(mod, per_token, tm, d, sh_blk, rows_per_batch)],
        out_specs=out_spec,
        out_shape=jax.ShapeDtypeStruct(out_shape, BF16),
        compiler_params=_params(("arbitrary",)),
        name="norm_mod_t" if transpose else "norm_mod",
    )(x2, g.reshape(1, d), mod, mod)


def _rope_tile(acc, cos, sin, n_chunks):
    outs = []
    for j in range(n_chunks):
        xj = acc[:, j * LANES:(j + 1) * LANES]
        outs.append(xj * cos + pltpu.roll(xj, LANES // 2, axis=1) * sin)
    return outs[0] if n_chunks == 1 else jnp.concatenate(outs, axis=1)


def _proj_kernel(*refs, rope_cols, scale, out_f32, out_bf16):
    h_ref, w_ref, cos_ref, sin_ref = refs[:4]
    outs = refs[4:]
    acc = jnp.dot(h_ref[...], w_ref[...], preferred_element_type=F32)
    tn = acc.shape[1]
    if rope_cols:
        roped = _rope_tile(acc[:, :rope_cols], cos_ref[...], sin_ref[...], rope_cols // LANES)
        acc = roped if rope_cols == tn else jnp.concatenate([roped, acc[:, rope_cols:]], axis=1)
    k = 0
    if out_f32:
        outs[k][...] = acc
        k += 1
    if out_bf16:
        outs[k][...] = (acc * scale if scale != 1.0 else acc).astype(BF16)


def _proj(h, w, cos, sin, *, col0, n, rope_cols, scale=1.0, out_f32, out_bf16, tm, tn, name):
    t, d = h.shape
    assert rope_cols in (0, tn) or n == tn
    assert col0 % tn == 0 and n % tn == 0 and col0 + n <= w.shape[1]
    j0 = col0 // tn
    pos_tiles = cos.shape[0] // tm
    out_shape, out_specs = [], []
    for want, dt in ((out_f32, F32), (out_bf16, BF16)):
        if want:
            out_shape.append(jax.ShapeDtypeStruct((t, n), dt))
            out_specs.append(pl.BlockSpec((tm, tn), lambda m, j: (m, j)))
    return pl.pallas_call(
        functools.partial(_proj_kernel, rope_cols=rope_cols, scale=scale, out_f32=out_f32, out_bf16=out_bf16),
        grid=(t // tm, n // tn),
        in_specs=[pl.BlockSpec((tm, d), lambda m, j: (m, 0)),
                  pl.BlockSpec((d, tn), lambda m, j: (0, j0 + j)),
                  pl.BlockSpec((tm, LANES), lambda m, j: (m % pos_tiles, 0)),
                  pl.BlockSpec((tm, LANES), lambda m, j: (m % pos_tiles, 0))],
        out_specs=out_specs,
        out_shape=out_shape,
        compiler_params=_params(("arbitrary", "arbitrary")),
        name=name,
    )(h, w, cos, sin)


def _num_key_chunks(q_start, n_q, n_valid_keys, n_chunks_total):
    last_visible = (((q_start + n_q - 1) >> 6) + 1) * CHUNK
    last_visible = jnp.minimum(last_visible, n_valid_keys)
    return jnp.minimum((last_visible + KEY_CHUNK - 1) // KEY_CHUNK, n_chunks_total)


def _num_full_chunks(q_start, n_valid_keys):
    return jnp.minimum(((q_start >> 6) + 1) * CHUNK, n_valid_keys) // KEY_CHUNK


def _ordered_key(x):
    b = pltpu.bitcast(x, jnp.int32)
    return b ^ ((b >> 31) & jnp.int32(0x7FFFFFFF))


def _fill_invisible(x, fill, q_start, n_q, key_start, n_valid_keys, all_keys_valid):
    n_k, n_lanes = x.shape
    kpos = key_start + lax.broadcasted_iota(jnp.int32, (n_k, n_lanes), 0)
    qpos = q_start + jnp.minimum(lax.broadcasted_iota(jnp.int32, (1, n_lanes), 1), n_q - 1)
    x = jnp.where((kpos >> 6) <= (qpos >> 6), x, fill)
    return x if all_keys_valid else jnp.where(kpos < n_valid_keys, x, fill)


def _t_bf16(x, n_lanes):
    x = x.astype(F32)
    if x.shape[0] < n_lanes:
        x = jnp.concatenate([x, jnp.zeros((n_lanes - x.shape[0], x.shape[1]), F32)], axis=0)
    return x.T.astype(BF16)


def _rows_i16(words, n_word_rows):
    tile = jnp.broadcast_to(words, (8, words.shape[1]))
    return pltpu.bitcast(jnp.concatenate([tile] * (n_word_rows // 8), axis=0), jnp.int16)


def _col_partial(x, op, group=8):
    parts = [x[r * group:(r + 1) * group] for r in range(x.shape[0] // group)]
    while len(parts) > 1:
        parts = [op(parts[i], parts[i + 1]) for i in range(0, len(parts) - 1, 2)] + parts[len(parts) & ~1:]
    return parts[0]


def _col_reduce(x, op):
    return (jnp.max if op is jnp.maximum else jnp.sum)(_col_partial(x, op), axis=0, keepdims=True)


def _row_to_cols(row):
    return jnp.broadcast_to(row, (LANES, row.shape[1])).T


def _dsa_kernel(q_ref, qi_ref, wi_ref, k_ref, vaug_ref, ki_ref, o_ref,
                qit_scr, qt_scr, key_scr, hi_scr, lo_scr, acc_scr, *,
                tq, tl, n_sel, p_len, n_valid_keys, all_keys_valid, n_idx_heads, n_kv_heads, rep):
    n_chunks_total = key_scr.shape[0]
    q_start = p_len + pl.program_id(1) * tq
    n_ch = _num_key_chunks(q_start, tq, n_valid_keys, n_chunks_total)
    heads_per_dot = 4
    half_chunk = KEY_CHUNK // 2

    for h in range(n_idx_heads):
        qit_scr[:, h * tl:(h + 1) * tl] = _t_bf16(qi_ref[:, h * LANES:(h + 1) * LANES], tl)
    for g in range(n_kv_heads):
        for r in range(rep):
            hh = g * rep + r
            qt_scr[g, :, r * tl:(r + 1) * tl] = _t_bf16(q_ref[:, hh * LANES:(hh + 1) * LANES], tl)
    wi = wi_ref[:, LANES:2 * LANES] * (n_idx_heads ** -0.5)
    if tq < tl:
        wi = jnp.concatenate([wi, jnp.zeros((tl - tq, LANES), F32)], axis=0)
    wi_t = wi.T

    def idx_body(j, carry):
        k0 = pl.multiple_of(j * KEY_CHUNK, KEY_CHUNK)
        kib = ki_ref[pl.ds(k0, KEY_CHUNK), :]
        score = jnp.zeros((KEY_CHUNK, tl), F32)
        for h0 in range(0, n_idx_heads, heads_per_dot):
            lg = jnp.dot(kib, qit_scr[:, h0 * tl:(h0 + heads_per_dot) * tl], preferred_element_type=F32)
            for h in range(h0, h0 + heads_per_dot):
                score = score + jnp.maximum(lg[:, (h - h0) * tl:(h - h0 + 1) * tl], 0.0) * wi_t[h:h + 1, :]
        score = _fill_invisible(score, -jnp.inf, q_start, tq, k0, n_valid_keys, all_keys_valid)
        key = _ordered_key(score)
        key_scr[j] = key
        hi16 = (key >> 16) & 0xFFFF
        lo16 = (key & 0xFFFF) ^ 0x8000
        hi_scr[j] = hi16[:half_chunk] | (hi16[half_chunk:] << 16)
        lo_scr[j] = lo16[:half_chunk] | (lo16[half_chunk:] << 16)
        return carry

    lax.fori_loop(0, n_ch, idx_body, 0)

    def count16(scr, pattern, strict):
        c = _rows_i16(pattern | (pattern << 16), half_chunk)

        def body(j, cnt):
            x = pltpu.bitcast(scr[j], jnp.int16)
            ones = jnp.where((x > c) if strict else (x >= c), jnp.int16(1), jnp.int16(0))
            return cnt + _col_partial(ones, jnp.add, group=16)

        cnt = pltpu.bitcast(lax.fori_loop(0, n_ch, body, jnp.zeros((16, tl), jnp.int16)), jnp.int32)
        return jnp.sum((cnt & 0xFFFF) + ((cnt >> 16) & 0xFFFF), axis=0, keepdims=True)

    def bisect16(scr, need):
        def bit_body(b, t):
            cand = t | lax.shift_left(jnp.int32(1), 15 - b)
            return jnp.where(count16(scr, cand ^ 0x8000, False) >= need, cand, t)
        return lax.fori_loop(0, 16, bit_body, jnp.zeros((1, tl), jnp.int32))

    hi_pat = bisect16(hi_scr, n_sel) ^ 0x8000
    need = n_sel - count16(hi_scr, hi_pat, True)

    def keep_bucket(j, carry):
        hi = pltpu.bitcast(hi_scr[j], jnp.int16)
        lo = pltpu.bitcast(lo_scr[j], jnp.int16)
        same = hi == _rows_i16(hi_pat | (hi_pat << 16), half_chunk)
        lo_scr[j] = pltpu.bitcast(jnp.where(same, lo, jnp.int16(-2 ** 15)), jnp.int32)
        return carry

    lax.fori_loop(0, n_ch, keep_bucket, 0)
    lo_u = bisect16(lo_scr, need)
    thr_raw = (((hi_pat << 16) >> 16) << 16) | lo_u
    thr = jnp.maximum(thr_raw, KEY_NEG_INF + 1)

    n_above = n_sel - need
    n_ge = n_above + count16(lo_scr, lo_u ^ 0x8000, False)
    surplus = jnp.where(thr_raw > KEY_NEG_INF, n_ge - n_sel, 0)

    @pl.when(jnp.max(surplus) > 0)
    def _():
        take = n_sel - (n_above + count16(lo_scr, lo_u ^ 0x8000, True))

        def positions(j):
            return j * KEY_CHUNK + lax.broadcasted_iota(jnp.int32, (KEY_CHUNK, tl), 0)

        def tied_before(limit):
            def body(j, cnt):
                ones = jnp.where(key_scr[j] == thr, jnp.where(positions(j) < limit, 1, 0), 0)
                return cnt + _col_partial(ones, jnp.add)
            cnt = lax.fori_loop(0, n_ch, body, jnp.zeros((8, tl), jnp.int32))
            return jnp.sum(cnt, axis=0, keepdims=True)

        n_bits = (n_chunks_total * KEY_CHUNK).bit_length()

        def bit_body(b, t):
            cand = t | lax.shift_left(jnp.int32(1), n_bits - 1 - b)
            return jnp.where(tied_before(cand) < take, cand, t)

        last = lax.fori_loop(0, n_bits, bit_body, jnp.zeros((1, tl), jnp.int32))

        def drop_body(j, carry):
            key = key_scr[j]
            key_scr[j] = jnp.where(key == thr, jnp.where(positions(j) > last, KEY_NEG_INF, key), key)
            return carry

        lax.fori_loop(0, n_ch, drop_body, 0)

    acc_scr[...] = jnp.zeros_like(acc_scr)

    def att_body(j, ms):
        k0 = pl.multiple_of(j * KEY_CHUNK, KEY_CHUNK)
        bias = jnp.where(key_scr[j] >= thr, 0.0, NEG_BIG)
        bias = jnp.concatenate([bias] * rep, axis=1)
        new = []
        for g in range(n_kv_heads):
            kb = k_ref[pl.ds(k0, KEY_CHUNK), g * LANES:(g + 1) * LANES]
            s = jnp.dot(kb, qt_scr[g], preferred_element_type=F32) + bias
            m_new = jnp.maximum(ms[g], _col_reduce(s, jnp.maximum))
            p = jnp.exp2(s - m_new).astype(BF16)
            va = vaug_ref[pl.ds(k0, KEY_CHUNK), g * 2 * LANES:(g + 1) * 2 * LANES]
            alpha = _row_to_cols(jnp.exp2(ms[g] - m_new))
            acc_scr[g] = (acc_scr[g] * jnp.concatenate([alpha, alpha], axis=1)
                          + lax.dot_general(p, va, (((0,), (0,)), ((), ())), preferred_element_type=F32))
            new.append(m_new)
        return tuple(new)

    lax.fori_loop(0, n_ch, att_body, tuple(jnp.full((1, rep * tl), NEG_BIG, F32) for _ in range(n_kv_heads)))

    for g in range(n_kv_heads):
        acc = acc_scr[g]
        o = acc[:, :LANES] / acc[:, LANES:]
        for r in range(rep):
            hh = g * rep + r
            o_ref[:, hh * LANES:(hh + 1) * LANES] = o[r * tl:r * tl + tq].astype(o_ref.dtype)


def _dsa_attention(q_all, qi_all, kiw_f32, k_bf, vaug_bf, ki_bf, *, batch, q_len, k_len, n_valid_keys, p_len, tq,
                   n_heads, n_kv_heads, n_idx_heads, n_sel):
    width = n_heads * HEAD_DIM
    assert qi_all.shape[1] == n_idx_heads * LANES
    nq = q_len // tq
    tl = max(tq, LANES)
    n_chunks = k_len // KEY_CHUNK
    rep = n_heads // n_kv_heads
    kv_w = n_kv_heads * HEAD_DIM
    return pl.pallas_call(
        functools.partial(_dsa_kernel, tq=tq, tl=tl, n_sel=n_sel, p_len=p_len, n_valid_keys=n_valid_keys,
                          all_keys_valid=n_valid_keys == k_len,
                          n_idx_heads=n_idx_heads, n_kv_heads=n_kv_heads, rep=rep),
        grid=(batch, nq),
        in_specs=[pl.BlockSpec((tq, width), lambda b, i: (b * nq + i, 0)),
                  pl.BlockSpec((tq, n_idx_heads * LANES), lambda b, i: (b * nq + i, 0)),
                  pl.BlockSpec((tq, 2 * LANES), lambda b, i: (b * nq + i, 0)),
                  pl.BlockSpec((k_len, kv_w), lambda b, i: (b, 0)),
                  pl.BlockSpec((k_len, 2 * kv_w), lambda b, i: (b, 0)),
                  pl.BlockSpec((k_len, LANES), lambda b, i: (b, 0))],
        out_specs=pl.BlockSpec((tq, width), lambda b, i: (b * nq + i, 0)),
        out_shape=jax.ShapeDtypeStruct((batch * q_len, width), BF16),
        scratch_shapes=[pltpu.VMEM((HEAD_DIM, n_idx_heads * tl), BF16),
                        pltpu.VMEM((n_kv_heads, HEAD_DIM, rep * tl), BF16),
                        pltpu.VMEM((n_chunks, KEY_CHUNK, tl), jnp.int32),
                        pltpu.VMEM((n_chunks, KEY_CHUNK // 2, tl), jnp.int32),
                        pltpu.VMEM((n_chunks, KEY_CHUNK // 2, tl), jnp.int32),
                        pltpu.VMEM((n_kv_heads, rep * tl, 2 * LANES), F32)],
        compiler_params=_params(("arbitrary", "arbitrary")),
        name="dsa_attention",
    )(q_all, qi_all, kiw_f32, k_bf, vaug_bf, ki_bf)


def _diff_kernel(dq_ref, dk_ref, dv_ref, lq1_ref, lk1_ref, lq2_ref, lk2_ref, g_ref, o_ref,
                 acc_scr, *, tq, tl, hp, n_chunks_total, p_len, n_valid_keys, all_keys_valid, lam_init):
    q_start = p_len + pl.program_id(2) * tq
    n_ch = _num_key_chunks(q_start, tq, n_valid_keys, n_chunks_total)
    hw = 2 * HEAD_DIM
    n_maps = 2 * hp

    lam = (jnp.exp(jnp.sum(lq1_ref[...] * lk1_ref[...], axis=1, keepdims=True))
           - jnp.exp(jnp.sum(lq2_ref[...] * lk2_ref[...], axis=1, keepdims=True)) + lam_init)
    qt = [_t_bf16(dq_ref[:, c * LANES:(c + 1) * LANES], tl) for c in range(n_maps)]

    acc_scr[...] = jnp.zeros_like(acc_scr)

    def att_body(j, carry, masked):
        k0 = pl.multiple_of(j * KEY_CHUNK, KEY_CHUNK)
        new = []
        for c in range(n_maps):
            m, l = carry[2 * c:2 * c + 2]
            kb = dk_ref[pl.ds(k0, KEY_CHUNK), c * LANES:(c + 1) * LANES]
            vb = dv_ref[pl.ds(k0, KEY_CHUNK), (c // 2) * hw:(c // 2 + 1) * hw]
            s = jnp.dot(kb, qt[c], preferred_element_type=F32)
            if masked:
                s = _fill_invisible(s, NEG_BIG, q_start, tq, k0, n_valid_keys, all_keys_valid)
            m_new = jnp.maximum(m, _col_reduce(s, jnp.maximum))
            p = jnp.exp2(s - m_new)
            alpha = jnp.exp2(m - m_new)
            alpha_c = _row_to_cols(alpha)
            acc_scr[c] = (acc_scr[c] * jnp.concatenate([alpha_c, alpha_c], axis=1)
                          + lax.dot_general(p.astype(BF16), vb, (((0,), (0,)), ((), ())),
                                            preferred_element_type=F32))
            new += [m_new, alpha * l + _col_reduce(p, jnp.add)]
        return tuple(new)

    one = (jnp.full((1, tl), NEG_BIG, F32), jnp.zeros((1, tl), F32))
    n_full = jnp.minimum(_num_full_chunks(q_start, n_valid_keys), n_ch)
    carry = lax.fori_loop(0, n_full, functools.partial(att_body, masked=False), one * n_maps)
    carry = lax.fori_loop(n_full, n_ch, functools.partial(att_body, masked=True), carry)

    def normalised(c):
        l_cols = _row_to_cols(carry[2 * c + 1])
        return acc_scr[c] / jnp.concatenate([l_cols, l_cols], axis=1)

    for h in range(hp):
        o = (normalised(2 * h) - lam * normalised(2 * h + 1))[:tq]
        o = o * lax.rsqrt(jnp.mean(o * o, axis=-1, keepdims=True) + EPS) * g_ref[...]
        o_ref[:, h * hw:(h + 1) * hw] = (o * (1.0 - lam_init)).astype(o_ref.dtype)


def _diff_attention(dq_all, dk_bf, dv_bf, lams, g_subln, *, batch, q_len, k_len, n_valid_keys, p_len, tq,
                    n_heads, lam_init):
    hw = 2 * HEAD_DIM
    hp = 2 if n_heads % 2 == 0 else 1
    width = n_heads * hw
    assert dq_all.shape[1] == width
    nq = q_len // tq
    tl = max(tq, LANES)
    n_chunks = k_len // KEY_CHUNK
    vec = pl.BlockSpec((1, HEAD_DIM), lambda b, h, i: (0, 0))
    return pl.pallas_call(
        functools.partial(_diff_kernel, tq=tq, tl=tl, hp=hp, n_chunks_total=n_chunks, p_len=p_len,
                          n_valid_keys=n_valid_keys, all_keys_valid=n_valid_keys == k_len, lam_init=lam_init),
        grid=(batch, n_heads // hp, nq),
        in_specs=[pl.BlockSpec((tq, hp * hw), lambda b, h, i: (b * nq + i, h)),
                  pl.BlockSpec((k_len, hp * hw), lambda b, h, i: (b, h)),
                  pl.BlockSpec((k_len, hp * hw), lambda b, h, i: (b, h)),
                  vec, vec, vec, vec,
                  pl.BlockSpec((1, hw), lambda b, h, i: (0, 0))],
        out_specs=pl.BlockSpec((tq, hp * hw), lambda b, h, i: (b * nq + i, h)),
        out_shape=jax.ShapeDtypeStruct((batch * q_len, width), BF16),
        scratch_shapes=[pltpu.VMEM((2 * hp, tl, hw), F32)],
        compiler_params=_params(("arbitrary", "arbitrary", "arbitrary")),
        name="diff_attention",
    )(dq_all, dk_bf, dv_bf, *[v.reshape(1, HEAD_DIM) for v in lams], g_subln.reshape(1, hw))


def _outproj_kernel(a_ref, d_ref, wa_ref, wd_ref, x_ref, ga_ref, o_ref, *, per_token):
    mix = (jnp.dot(a_ref[...], wa_ref[...], preferred_element_type=F32)
           + jnp.dot(d_ref[...], wd_ref[...], preferred_element_type=F32))
    ga = ga_ref[...] if per_token else ga_ref[0]
    o_ref[...] = x_ref[...] + ga * mix


def _outproj(a_out, d_out, w_out_bf, x2, mod, *, per_token, rows_per_batch, tm, tn):
    t, d = x2.shape
    wa = a_out.shape[1]
    wd = d_out.shape[1]
    assert wa == wd
    cb = 2 * (d // tn)
    return pl.pallas_call(
        functools.partial(_outproj_kernel, per_token=per_token),
        grid=(t // tm, d // tn),
        in_specs=[pl.BlockSpec((tm, wa), lambda m, n: (m, 0)),
                  pl.BlockSpec((tm, wd), lambda m, n: (m, 0)),
                  pl.BlockSpec((wa, tn), lambda m, n: (0, n)),
                  pl.BlockSpec((wd, tn), lambda m, n: (1, n)),
                  pl.BlockSpec((tm, tn), lambda m, n: (m, n)),
                  _mod_spec(mod, per_token, tm, tn, cb, rows_per_batch)],
        out_specs=pl.BlockSpec((tm, tn), lambda m, n: (m, n)),
        out_shape=jax.ShapeDtypeStruct((t, d), F32),
        compiler_params=_params(("arbitrary", "arbitrary")),
        name="outproj",
    )(a_out, d_out, w_out_bf, w_out_bf, x2, mod)


def _top_rows(x, k, with_rank=False):
    tops = []
    rank = jnp.full(x.shape, float(k), F32) if with_rank else None
    for i in range(k):
        mx = jnp.max(x, axis=0, keepdims=True)
        tops.append(mx)
        hit = x == mx
        if with_rank:
            rank = jnp.where(hit, float(i), rank)
        x = jnp.where(hit, -jnp.inf, x)
    return (tops, rank) if with_rank else tops


def _pack_bf16_pair(lo, hi):
    def bits(x):
        b = pltpu.bitcast(x, jnp.uint32)
        return (b + jnp.uint32(0x7FFF) + ((b >> 16) & jnp.uint32(1))) >> 16
    return bits(lo) | (bits(hi) << 16)


def _peer_route_kernel(h_ref, wq_ref, keys_ref, cnt_ref, e1_ref, rank_ref, e2_ref, q_scr, s1_scr, top_scr):
    hc = pl.program_id(1)
    c = hc % 2
    half = keys_ref.shape[2]

    @pl.when(hc == 0)
    def _():
        q_scr[...] = jnp.dot(wq_ref[...], h_ref[...], preferred_element_type=F32).astype(BF16)

    q_t = q_scr[pl.ds(pl.multiple_of(hc * half, half), half), :]
    s_t = jnp.dot(keys_ref[0].astype(BF16), q_t, preferred_element_type=F32)

    @pl.when(c == 0)
    def _():
        s1_scr[...] = s_t
        top_scr[...] = jnp.concatenate(_top_rows(s_t, PEER_TOPK), axis=0)

    @pl.when(c == 1)
    def _():
        tops2, rank2 = _top_rows(s_t, PEER_TOPK, with_rank=True)
        tops = jnp.concatenate(tops2, axis=0)
        top1 = top_scr[...]
        cand = jnp.concatenate([top1[0:1, :] + tops]
                               + [top1[i:i + 1, :] + tops[:PEER_TOPK // 2] for i in range(1, PEER_TOPK)], axis=0)
        best = _top_rows(cand, PEER_TOPK)
        m = best[0]
        z = jnp.zeros_like(m)
        for bk in best:
            z = z + jnp.exp(bk - m)
        thr = best[PEER_TOPK - 1]
        s1 = s1_scr[...]
        cnt = jnp.zeros_like(s1)
        for j in range(PEER_TOPK):
            cnt = cnt + jnp.where(s1 + tops2[j] >= thr, 1.0, 0.0)
        e1 = jnp.exp(s1 - top1[0:1, :]) * (0.5 / z)
        e2 = jnp.exp(s_t - tops2[0])
        hn = s_t.shape[0] // 2
        cnt_ref[0] = _pack_bf16_pair(cnt, cnt)
        e1_ref[0] = _pack_bf16_pair(e1, e1)
        rank_ref[0] = _pack_bf16_pair(rank2[:hn], rank2[hn:])
        e2_ref[0] = _pack_bf16_pair(e2[:hn], e2[hn:])


def _peer_route(h_t, wq_t_bf, sub_keys, *, tm):
    d, t = h_t.shape
    heads, _, n_keys, half = sub_keys.shape
    keys2 = sub_keys.reshape(heads * 2, n_keys, half)
    a_spec = pl.BlockSpec((1, n_keys, tm), lambda m, hc: (hc // 2, 0, m))
    b_spec = pl.BlockSpec((1, n_keys // 2, tm), lambda m, hc: (hc // 2, 0, m))
    a_tab = jax.ShapeDtypeStruct((heads, n_keys, t), jnp.uint32)
    b_tab = jax.ShapeDtypeStruct((heads, n_keys // 2, t), jnp.uint32)
    return pl.pallas_call(
        _peer_route_kernel,
        grid=(t // tm, heads * 2),
        in_specs=[pl.BlockSpec((d, tm), lambda m, hc: (0, m)),
                  pl.BlockSpec((heads * 2 * half, d), lambda m, hc: (0, 0), pipeline_mode=pl.Buffered(1)),
                  pl.BlockSpec((1, n_keys, half), lambda m, hc: (hc, 0, 0))],
        out_specs=[a_spec, a_spec, b_spec, b_spec],
        out_shape=[a_tab, a_tab, b_tab, b_tab],
        scratch_shapes=[pltpu.VMEM((heads * 2 * half, tm), BF16), pltpu.VMEM((n_keys, tm), F32),
                        pltpu.VMEM((PEER_TOPK, tm), F32)],
        compiler_params=_params(("arbitrary", "arbitrary")),
        name="peer_route",
    )(h_t, wq_t_bf, keys2)


def _gated_gelu(x, half_gate):
    c = 0.7978845608028654
    inner = x * (c + (c * 0.044715) * (x * x))
    return (x * half_gate) * (1.0 + jnp.tanh(inner))


def _rows_bf16(row_words, n_rows):
    tile = jnp.broadcast_to(row_words, (8, row_words.shape[1]))
    return pltpu.bitcast(jnp.concatenate([tile] * (n_rows // 16), axis=0), BF16)


def _peer_ffn_kernel(h_ref, u_ref, vt_ref, cnt_ref, e1_ref, rank_ref, e2_ref, o_ref, *, n_keys):
    e = pl.program_id(1)
    te, tm = u_ref.shape[0], h_ref.shape[1]
    heads = cnt_ref.shape[0]
    a0 = e * (te // n_keys)

    @pl.when(e == 0)
    def _():
        o_ref[...] = jnp.zeros_like(o_ref)

    act = jnp.dot(u_ref[...], h_ref[...], preferred_element_type=F32)
    strip = min(tm, LANES)
    hn = n_keys // 2
    w_rows = [[None] * (tm // strip) for _ in range(2 * te // n_keys)]
    for ai in range(te // n_keys):
        cnt_a = [cnt_ref[h, pl.ds(a0 + ai, 1), :] for h in range(heads)]
        e1_a = [e1_ref[h, pl.ds(a0 + ai, 1), :] for h in range(heads)]
        for c in range(tm // strip):
            cols = slice(c * strip, (c + 1) * strip)
            gate = jnp.zeros((n_keys, strip), BF16)
            for h in range(heads):
                cnt_b = _rows_bf16(cnt_a[h][:, cols], n_keys)
                e1_b = _rows_bf16(e1_a[h][:, cols], n_keys)
                keep = pltpu.bitcast(rank_ref[h, :, cols], BF16) < cnt_b
                gate = gate + jnp.where(keep, pltpu.bitcast(e2_ref[h, :, cols], BF16) * e1_b,
                                        jnp.zeros_like(e1_b))
            words = pltpu.bitcast(gate, jnp.uint32)
            halves = (pltpu.bitcast(words << 16, F32), pltpu.bitcast(words & jnp.uint32(0xFFFF0000), F32))
            for k, g in enumerate(halves):
                rows = slice(ai * n_keys + k * hn, ai * n_keys + (k + 1) * hn)
                w_rows[2 * ai + k][c] = _gated_gelu(act[rows, cols], g).astype(BF16)
    w = jnp.concatenate([r[0] if len(r) == 1 else jnp.concatenate(r, axis=1) for r in w_rows], axis=0)
    o_ref[...] += jnp.dot(vt_ref[...], w, preferred_element_type=F32)


def _peer_ffn(h_t, u_bf, vt_bf, cnt, e1, rank, e2, *, tm, te):
    d, t = h_t.shape
    n_exp = u_bf.shape[0]
    n_blocks = n_exp // te
    heads, n_keys, _ = cnt.shape
    once = pl.Buffered(1)
    a_spec = pl.BlockSpec((heads, n_keys, tm), lambda m, e: (0, 0, m), pipeline_mode=once)
    b_spec = pl.BlockSpec((heads, n_keys // 2, tm), lambda m, e: (0, 0, m), pipeline_mode=once)
    return pl.pallas_call(
        functools.partial(_peer_ffn_kernel, n_keys=n_keys),
        grid=(t // tm, n_blocks),
        in_specs=[pl.BlockSpec((d, tm), lambda m, e: (0, m), pipeline_mode=once),
                  pl.BlockSpec((te, d), lambda m, e: (e, 0)),
                  pl.BlockSpec((d, te), lambda m, e: (0, e)),
                  a_spec, a_spec, b_spec, b_spec],
        out_specs=pl.BlockSpec((d, tm), lambda m, e: (0, m), pipeline_mode=once),
        out_shape=jax.ShapeDtypeStruct((d, t), F32),
        compiler_params=_params(("arbitrary", "arbitrary"), vmem=PEER_FFN_VMEM),
        name="peer_ffn",
    )(h_t, u_bf, vt_bf, cnt, e1, rank, e2)


def _final_kernel(x_ref, pt_ref, ga_ref, g_ref, o_ref, *, per_token, normalize):
    ga = ga_ref[...] if per_token else ga_ref[0]
    x = x_ref[...] + ga * pt_ref[...].T
    if normalize:
        x = x * lax.rsqrt(jnp.mean(x * x, axis=-1, keepdims=True) + EPS) * g_ref[...]
    o_ref[...] = x


def _final(x1, peer_t, mod, g_final, *, per_token, rows_per_batch, tm, normalize):
    t, d = x1.shape
    row = pl.BlockSpec((tm, d), lambda m: (m, 0))
    return pl.pallas_call(
        functools.partial(_final_kernel, per_token=per_token, normalize=normalize),
        grid=(t // tm,),
        in_specs=[row, pl.BlockSpec((d, tm), lambda m: (0, m)), _mod_spec(mod, per_token, tm, d, 5, rows_per_batch),
                  pl.BlockSpec((1, d), lambda m: (0, 0))],
        out_specs=row,
        out_shape=jax.ShapeDtypeStruct((t, d), F32),
        compiler_params=_params(("arbitrary",)),
        name="final",
    )(x1, peer_t, mod, g_final.reshape(1, d))


def _rope_tables(pos):
    half = HEAD_DIM // 2
    inv = ROPE_THETA ** (-jnp.arange(half, dtype=F32) / half)
    ang = pos.astype(F32)[:, None] * inv[None, :]
    cos, sin = jnp.cos(ang), jnp.sin(ang)
    return jnp.concatenate([cos, cos], axis=1), jnp.concatenate([-sin, sin], axis=1)


def _pick_tile(n, prefs):
    for p in prefs:
        if n % p == 0:
            return p
    return n


def _layer(x, mod_rows, past, layer, last_layer, w, dims):
    b, s, d = x.shape
    t = b * s
    n_heads, n_kv, n_idx, n_diff = dims
    p_len = 0 if past is None else past[0].shape[1]
    n_keys_valid = p_len + s
    n_sel = min(DSA_TOPK, n_keys_valid // 4)
    x2 = x.reshape(t, d)

    per_token = s % 256 != 0
    if per_token:
        mod = jnp.repeat(mod_rows, s, axis=0)
    else:
        mod = mod_rows.reshape(b, 1, 6 * d)
    tm_big = _pick_tile(t if per_token else s, (1024, 512, 256, 128))
    tm_mid = _pick_tile(t if per_token else s, (512, 256, 128))
    tm_small = _pick_tile(t if per_token else s, (256, 128))

    pos = p_len + jnp.arange(s)
    cos, sin = _rope_tables(pos)
    if per_token:
        cos, sin = jnp.tile(cos, (b, 1)), jnp.tile(sin, (b, 1))

    h = _norm_mod(x2, w["g_norm_mix"], mod, which=1, per_token=per_token, rows_per_batch=s, tm=tm_mid,
                  transpose=False)

    qw, kvw, dw = n_heads * HEAD_DIM, n_kv * HEAD_DIM, n_diff * 2 * HEAD_DIM
    head = functools.partial(_proj, h, w["w_head"], cos=cos, sin=sin, tm=tm_big)
    tail = functools.partial(_proj, h, w["w_tail"], cos=cos, sin=sin, tm=tm_big)
    q_scale = HEAD_DIM ** -0.5 * math.log2(math.e)
    bf_only = dict(out_f32=False, out_bf16=True, tn=512)
    both = dict(out_f32=True, out_bf16=True, tn=512)
    (q_all,) = head(col0=0, n=qw, rope_cols=512, scale=q_scale, name="proj_q", **bf_only)
    (qi_all,) = head(col0=qw + 2 * kvw, n=n_idx * LANES, rope_cols=512, scale=q_scale, name="proj_qi", **bf_only)
    (dq_all,) = tail(col0=0, n=dw, rope_cols=512, scale=q_scale, name="proj_dq", **bf_only)
    k_f, k_b = head(col0=qw, n=kvw, rope_cols=512, name="proj_k", **both)
    v_f, v_b = head(col0=qw + kvw, n=kvw, rope_cols=0, name="proj_v", **both)
    dk_f, dk_b = tail(col0=dw, n=dw, rope_cols=512, name="proj_dk", **both)
    dv_f, dv_b = tail(col0=2 * dw, n=dw, rope_cols=0, name="proj_dv", **both)
    kiw_f, kiw_b = head(col0=qw + 2 * kvw + n_idx * LANES, n=2 * LANES, rope_cols=LANES, out_f32=True,
                        out_bf16=True, tn=2 * LANES, name="proj_kiw")

    ki_f = kiw_f[:, :LANES]
    new_rows = (k_f.reshape(b, s, n_kv, HEAD_DIM), v_f.reshape(b, s, n_kv, HEAD_DIM), ki_f.reshape(b, s, LANES),
                dk_f.reshape(b, s, n_diff, 2, HEAD_DIM), dv_f.reshape(b, s, n_diff, 2 * HEAD_DIM))

    if past is None:
        k_len = s
        k_all, v_all, ki_all, dk_all, dv_all = k_b, v_b, kiw_b, dk_b, dv_b
    else:
        k_len = -(-n_keys_valid // KEY_CHUNK) * KEY_CHUNK

        def join(cache, new, width):
            new = new.reshape(b, s, -1)[:, :, :width]
            old = lax.optimization_barrier(cache.reshape(b, p_len, width))
            both_ = jnp.concatenate([old.astype(BF16), new], axis=1)
            both_ = jnp.pad(both_, ((0, 0), (0, k_len - n_keys_valid), (0, 0)))
            return both_.reshape(b * k_len, width)

        k_all = join(past[0], k_b, n_kv * HEAD_DIM)
        v_all = join(past[1], v_b, n_kv * HEAD_DIM)
        ki_all = join(past[2], kiw_b, LANES)
        dk_all = join(past[3], dk_b, n_diff * 2 * HEAD_DIM)
        dv_all = join(past[4], dv_b, n_diff * 2 * HEAD_DIM)

    ones = jnp.ones((v_all.shape[0], HEAD_DIM), BF16)
    vaug_all = jnp.concatenate(
        [piece for g in range(n_kv) for piece in (v_all[:, g * HEAD_DIM:(g + 1) * HEAD_DIM], ones)], axis=1)
    tq_a = _pick_tile(s, (128, 64, 32))
    a_out = _dsa_attention(q_all, qi_all, kiw_f, k_all, vaug_all, ki_all, batch=b, q_len=s, k_len=k_len,
                           n_valid_keys=n_keys_valid, p_len=p_len, tq=tq_a, n_heads=n_heads, n_kv_heads=n_kv,
                           n_idx_heads=n_idx, n_sel=n_sel)
    tq_d = _pick_tile(s, (256, 128, 64, 32))
    d_out = _diff_attention(dq_all, dk_all, dv_all, w["lams"], w["g_subln"], batch=b, q_len=s, k_len=k_len,
                            n_valid_keys=n_keys_valid, p_len=p_len, tq=tq_d, n_heads=n_diff,
                            lam_init=_lambda_init(layer))

    x1 = _outproj(a_out, d_out, w["w_out"], x2, mod, per_token=per_token, rows_per_batch=s, tm=tm_big, tn=512)

    h2_t = _norm_mod(x1, w["g_norm_ffn"], mod, which=2, per_token=per_token, rows_per_batch=s, tm=tm_mid,
                     transpose=True)
    cnt, e1, rank, e2 = _peer_route(h2_t, w["peer_wq_t"], w["peer_keys"], tm=tm_mid)
    peer_t = _peer_ffn(h2_t, w["peer_u"], w["peer_v_t"], cnt, e1, rank, e2, tm=tm_mid, te=1024)
    x_out = _final(x1, peer_t, mod, w["g_final"], per_token=per_token, rows_per_batch=s, tm=tm_small,
                   normalize=last_layer)
    return x_out.reshape(b, s, d), new_rows


def kernel(x_prompt, x_sample, cache_dsa_k, cache_dsa_v, cache_idx_k, cache_diff_k, cache_diff_v, c_prompt, c_sample, w_ada, b_ada, g_norm_mix, g_norm_ffn, w_in, diff_lambda_q1, diff_lambda_k1, diff_lambda_q2, diff_lambda_k2, g_diff_subln, w_out, peer_w_query, peer_sub_keys, peer_u, peer_v, g_final):
    depth = w_in.shape[0]
    bp, bs = x_prompt.shape[0], x_sample.shape[0]
    n_kv = cache_dsa_k.shape[3]
    n_diff = cache_diff_k.shape[3]
    n_heads, n_idx_heads = DSA_HEADS, IDX_HEADS
    qw, kvw, dw = n_heads * HEAD_DIM, n_kv * HEAD_DIM, n_diff * 2 * HEAD_DIM
    tail0 = qw + 2 * kvw + n_idx_heads * LANES + LANES + n_idx_heads
    assert w_in.shape[2] == tail0 + 3 * dw
    head_w = qw + 2 * kvw + n_idx_heads * LANES + 2 * LANES
    dims = (n_heads, n_kv, n_idx_heads, n_diff)

    hp, hs = x_prompt, x_sample
    rows_p, rows_s = [], []
    n_c = bp + bs
    c_pad = jnp.pad(jnp.concatenate([c_prompt, c_sample], axis=0), ((0, (-n_c) % 16), (0, 0)))
    for l in range(depth):
        mod_all = _adaln(c_pad, w_ada[l], b_ada[l])
        weights = {
            "w_head": w_in[l][:, :head_w].astype(BF16),
            "w_tail": w_in[l][:, tail0:].astype(BF16),
            "w_out": w_out[l].astype(BF16),
            "peer_wq_t": peer_w_query[l].T.astype(BF16),
            "peer_keys": peer_sub_keys[l],
            "peer_u": peer_u[l].astype(BF16), "peer_v_t": peer_v[l].T.astype(BF16),
            "g_norm_mix": g_norm_mix[l], "g_norm_ffn": g_norm_ffn[l], "g_subln": g_diff_subln[l],
            "lams": (diff_lambda_q1[l], diff_lambda_k1[l], diff_lambda_q2[l], diff_lambda_k2[l]),
            "g_final": g_final,
        }
        last = l == depth - 1
        hp, rp = _layer(hp, mod_all[:bp], None, l, last, weights, dims)
        past = (cache_dsa_k[l], cache_dsa_v[l], cache_idx_k[l], cache_diff_k[l], cache_diff_v[l])
        hs, rs = _layer(hs, mod_all[bp:bp + bs], past, l, last, weights, dims)
        rows_p.append(rp)
        rows_s.append(rs)
    stack = lambda rows, i: jnp.stack([r[i] for r in rows])
    return (hp, hs) + tuple(stack(rows_p, i) for i in range(5)) + tuple(stack(rows_s, i) for i in range(5))
```

```python
import functools
import math

import jax
import jax.numpy as jnp
from jax import lax
from jax.experimental import pallas as pl
from jax.experimental.pallas import tpu as pltpu

CHUNK = 64
HEAD_DIM = 128
ROPE_THETA = 10000.0
EPS = 1e-6
DSA_HEADS = 16
IDX_HEADS = 16
DSA_TOPK = 256
PEER_TOPK = 16
LANES = 128
KEY_CHUNK = 512
NEG_BIG = -1e30
INT_MIN = -(2 ** 31)
KEY_NEG_INF = INT_MIN + 0x7FFFFF
VMEM_LIMIT = 56 * 1024 * 1024
PEER_FFN_VMEM = 62 * 1024 * 1024

BF16 = jnp.bfloat16
F32 = jnp.float32


def _params(sem, vmem=VMEM_LIMIT):
    return pltpu.CompilerParams(dimension_semantics=sem, vmem_limit_bytes=vmem)


def _lambda_init(layer):
    return 0.8 - 0.6 * math.exp(-0.3 * layer)


def _adaln_kernel(c_ref, w_ref, b_ref, o_ref):
    c = c_ref[...]
    a = (c * jax.nn.sigmoid(c)).astype(BF16)
    o_ref[...] = jnp.dot(a, w_ref[...].astype(BF16), preferred_element_type=F32) + b_ref[...]


def _adaln(c_pad, w_ada, b_ada):
    bp, d = c_pad.shape
    n = w_ada.shape[1]
    tn = 512
    return pl.pallas_call(
        _adaln_kernel,
        grid=(n // tn,),
        in_specs=[pl.BlockSpec((bp, d), lambda j: (0, 0)),
                  pl.BlockSpec((d, tn), lambda j: (0, j)),
                  pl.BlockSpec((1, tn), lambda j: (0, j))],
        out_specs=pl.BlockSpec((bp, tn), lambda j: (0, j)),
        out_shape=jax.ShapeDtypeStruct((bp, n), F32),
        compiler_params=_params(("arbitrary",)),
        name="adaln",
    )(c_pad, w_ada, b_ada.reshape(1, n))


def _norm_mod_kernel(x_ref, g_ref, sc_ref, sh_ref, o_ref, *, per_token, transpose):
    x = x_ref[...]
    y = x * lax.rsqrt(jnp.mean(x * x, axis=-1, keepdims=True) + EPS) * g_ref[...]
    sc = sc_ref[...] if per_token else sc_ref[0]
    sh = sh_ref[...] if per_token else sh_ref[0]
    h = y * (1.0 + sc) + sh
    if transpose:
        o_ref[...] = h.T.astype(o_ref.dtype)
    else:
        o_ref[...] = h.astype(o_ref.dtype)


def _mod_spec(mod, per_token, tm, tn, col_block, rows_per_batch):
    if per_token:
        return pl.BlockSpec((tm, tn), lambda m, n=0, cb=col_block: (m, cb + n))
    tiles_per_batch = rows_per_batch // tm
    return pl.BlockSpec((1, 1, tn), lambda m, n=0, cb=col_block: (m // tiles_per_batch, 0, cb + n))


def _norm_mod(x2, g, mod, *, which, per_token, rows_per_batch, tm, transpose):
    t, d = x2.shape
    sh_blk, sc_blk = (0, 1) if which == 1 else (3, 4)
    out_shape = (d, t) if transpose else (t, d)
    out_spec = pl.BlockSpec((d, tm), lambda m: (0, m)) if transpose else pl.BlockSpec((tm, d), lambda m: (m, 0))
    return pl.pallas_call(
        functools.partial(_norm_mod_kernel, per_token=per_token, transpose=transpose),
        grid=(t // tm,),
        in_specs=[pl.BlockSpec((tm, d), lambda m: (m, 0)),
                  pl.BlockSpec((1, d), lambda m: (0, 0)),
                  _mod_spec(mod, per_token, tm, d, sc_blk, rows_per_batch),
                  _mod_spec(mod, per_token, tm, d, sh_blk, rows_per_batch)],
        out_specs=out_spec,
        out_shape=jax.ShapeDtypeStruct(out_shape, BF16),
        compiler_params=_params(("arbitrary",)),
        name="norm_mod_t" if transpose else "norm_mod",
    )(x2, g.reshape(1, d), mod, mod)


def _rope_tile(acc, cos, sin, n_chunks):
    outs = []
    for j in range(n_chunks):
        xj = acc[:, j * LANES:(j + 1) * LANES]
        outs.append(xj * cos + pltpu.roll(xj, LANES // 2, axis=1) * sin)
    return outs[0] if n_chunks == 1 else jnp.concatenate(outs, axis=1)


def _proj_kernel(*refs, rope_cols, scale, out_f32, out_bf16):
    h_ref, w_ref, cos_ref, sin_ref = refs[:4]
    outs = refs[4:]
    acc = jnp.dot(h_ref[...], w_ref[...], preferred_element_type=F32)
    tn = acc.shape[1]
    if rope_cols:
        roped = _rope_tile(acc[:, :rope_cols], cos_ref[...], sin_ref[...], rope_cols // LANES)
        acc = roped if rope_cols == tn else jnp.concatenate([roped, acc[:, rope_cols:]], axis=1)
    k = 0
    if out_f32:
        outs[k][...] = acc
        k += 1
    if out_bf16:
        outs[k][...] = (acc * scale if scale != 1.0 else acc).astype(BF16)


def _proj(h, w, cos, sin, *, col0, n, rope_cols, scale=1.0, out_f32, out_bf16, tm, tn, name):
    t, d = h.shape
    assert rope_cols in (0, tn) or n == tn
    assert col0 % tn == 0 and n % tn == 0 and col0 + n <= w.shape[1]
    j0 = col0 // tn
    pos_tiles = cos.shape[0] // tm
    out_shape, out_specs = [], []
    for want, dt in ((out_f32, F32), (out_bf16, BF16)):
        if want:
            out_shape.append(jax.ShapeDtypeStruct((t, n), dt))
            out_specs.append(pl.BlockSpec((tm, tn), lambda m, j: (m, j)))
    return pl.pallas_call(
        functools.partial(_proj_kernel, rope_cols=rope_cols, scale=scale, out_f32=out_f32, out_bf16=out_bf16),
        grid=(t // tm, n // tn),
        in_specs=[pl.BlockSpec((tm, d), lambda m, j: (m, 0)),
                  pl.BlockSpec((d, tn), lambda m, j: (0, j0 + j)),
                  pl.BlockSpec((tm, LANES), lambda m, j: (m % pos_tiles, 0)),
                  pl.BlockSpec((tm, LANES), lambda m, j: (m % pos_tiles, 0))],
        out_specs=out_specs,
        out_shape=out_shape,
        compiler_params=_params(("arbitrary", "arbitrary")),
        name=name,
    )(h, w, cos, sin)


def _num_key_chunks(q_start, n_q, n_valid_keys, n_chunks_total):
    last_visible = (((q_start + n_q - 1) >> 6) + 1) * CHUNK
    last_visible = jnp.minimum(last_visible, n_valid_keys)
    return jnp.minimum((last_visible + KEY_CHUNK - 1) // KEY_CHUNK, n_chunks_total)


def _num_full_chunks(q_start, n_valid_keys):
    return jnp.minimum(((q_start >> 6) + 1) * CHUNK, n_valid_keys) // KEY_CHUNK


def _ordered_key(x):
    b = pltpu.bitcast(x, jnp.int32)
    return b ^ ((b >> 31) & jnp.int32(0x7FFFFFFF))


def _fill_invisible(x, fill, q_start, n_q, key_start, n_valid_keys, all_keys_valid):
    n_k, n_lanes = x.shape
    kpos = key_start + lax.broadcasted_iota(jnp.int32, (n_k, n_lanes), 0)
    qpos = q_start + jnp.minimum(lax.broadcasted_iota(jnp.int32, (1, n_lanes), 1), n_q - 1)
    x = jnp.where((kpos >> 6) <= (qpos >> 6), x, fill)
    return x if all_keys_valid else jnp.where(kpos < n_valid_keys, x, fill)


def _t_bf16(x, n_lanes):
    x = x.astype(F32)
    if x.shape[0] < n_lanes:
        x = jnp.concatenate([x, jnp.zeros((n_lanes - x.shape[0], x.shape[1]), F32)], axis=0)
    return x.T.astype(BF16)


def _rows_i16(words, n_word_rows):
    tile = jnp.broadcast_to(words, (8, words.shape[1]))
    return pltpu.bitcast(jnp.concatenate([tile] * (n_word_rows // 8), axis=0), jnp.int16)


def _col_partial(x, op, group=8):
    parts = [x[r * group:(r + 1) * group] for r in range(x.shape[0] // group)]
    while len(parts) > 1:
        parts = [op(parts[i], parts[i + 1]) for i in range(0, len(parts) - 1, 2)] + parts[len(parts) & ~1:]
    return parts[0]


def _col_reduce(x, op):
    return (jnp.max if op is jnp.maximum else jnp.sum)(_col_partial(x, op), axis=0, keepdims=True)


def _row_to_cols(row):
    return jnp.broadcast_to(row, (LANES, row.shape[1])).T


def _dsa_kernel(q_ref, qi_ref, wi_ref, k_ref, vaug_ref, ki_ref, o_ref,
                qit_scr, qt_scr, key_scr, hi_scr, lo_scr, acc_scr, *,
                tq, tl, n_sel, p_len, n_valid_keys, all_keys_valid, n_idx_heads, n_kv_heads, rep):
    n_chunks_total = key_scr.shape[0]
    q_start = p_len + pl.program_id(1) * tq
    n_ch = _num_key_chunks(q_start, tq, n_valid_keys, n_chunks_total)
    heads_per_dot = 4
    half_chunk = KEY_CHUNK // 2

    for h in range(n_idx_heads):
        qit_scr[:, h * tl:(h + 1) * tl] = _t_bf16(qi_ref[:, h * LANES:(h + 1) * LANES], tl)
    for g in range(n_kv_heads):
        for r in range(rep):
            hh = g * rep + r
            qt_scr[g, :, r * tl:(r + 1) * tl] = _t_bf16(q_ref[:, hh * LANES:(hh + 1) * LANES], tl)
    wi = wi_ref[:, LANES:2 * LANES] * (n_idx_heads ** -0.5)
    if tq < tl:
        wi = jnp.concatenate([wi, jnp.zeros((tl - tq, LANES), F32)], axis=0)
    wi_t = wi.T

    def idx_body(j, carry):
        k0 = pl.multiple_of(j * KEY_CHUNK, KEY_CHUNK)
        kib = ki_ref[pl.ds(k0, KEY_CHUNK), :]
        score = jnp.zeros((KEY_CHUNK, tl), F32)
        for h0 in range(0, n_idx_heads, heads_per_dot):
            lg = jnp.dot(kib, qit_scr[:, h0 * tl:(h0 + heads_per_dot) * tl], preferred_element_type=F32)
            for h in range(h0, h0 + heads_per_dot):
                score = score + jnp.maximum(lg[:, (h - h0) * tl:(h - h0 + 1) * tl], 0.0) * wi_t[h:h + 1, :]
        score = _fill_invisible(score, -jnp.inf, q_start, tq, k0, n_valid_keys, all_keys_valid)
        key = _ordered_key(score)
        key_scr[j] = key
        hi16 = (key >> 16) & 0xFFFF
        lo16 = (key & 0xFFFF) ^ 0x8000
        hi_scr[j] = hi16[:half_chunk] | (hi16[half_chunk:] << 16)
        lo_scr[j] = lo16[:half_chunk] | (lo16[half_chunk:] << 16)
        return carry

    lax.fori_loop(0, n_ch, idx_body, 0)

    def count16(scr, pattern, strict):
        c = _rows_i16(pattern | (pattern << 16), half_chunk)

        def body(j, cnt):
            x = pltpu.bitcast(scr[j], jnp.int16)
            ones = jnp.where((x > c) if strict else (x >= c), jnp.int16(1), jnp.int16(0))
            return cnt + _col_partial(ones, jnp.add, group=16)

        cnt = pltpu.bitcast(lax.fori_loop(0, n_ch, body, jnp.zeros((16, tl), jnp.int16)), jnp.int32)
        return jnp.sum((cnt & 0xFFFF) + ((cnt >> 16) & 0xFFFF), axis=0, keepdims=True)

    def bisect16(scr, need):
        def bit_body(b, t):
            cand = t | lax.shift_left(jnp.int32(1), 15 - b)
            return jnp.where(count16(scr, cand ^ 0x8000, False) >= need, cand, t)
        return lax.fori_loop(0, 16, bit_body, jnp.zeros((1, tl), jnp.int32))

    hi_pat = bisect16(hi_scr, n_sel) ^ 0x8000
    need = n_sel - count16(hi_scr, hi_pat, True)

    def keep_bucket(j, carry):
        hi = pltpu.bitcast(hi_scr[j], jnp.int16)
        lo = pltpu.bitcast(lo_scr[j], jnp.int16)
        same = hi == _rows_i16(hi_pat | (hi_pat << 16), half_chunk)
        lo_scr[j] = pltpu.bitcast(jnp.where(same, lo, jnp.int16(-2 ** 15)), jnp.int32)
        return carry

    lax.fori_loop(0, n_ch, keep_bucket, 0)
    lo_u = bisect16(lo_scr, need)
    thr_raw = (((hi_pat << 16) >> 16) << 16) | lo_u
    thr = jnp.maximum(thr_raw, KEY_NEG_INF + 1)

    n_above = n_sel - need
    n_ge = n_above + count16(lo_scr, lo_u ^ 0x8000, False)
    surplus = jnp.where(thr_raw > KEY_NEG_INF, n_ge - n_sel, 0)

    @pl.when(jnp.max(surplus) > 0)
    def _():
        take = n_sel - (n_above + count16(lo_scr, lo_u ^ 0x8000, True))

        def positions(j):
            return j * KEY_CHUNK + lax.broadcasted_iota(jnp.int32, (KEY_CHUNK, tl), 0)

        def tied_before(limit):
            def body(j, cnt):
                ones = jnp.where(key_scr[j] == thr, jnp.where(positions(j) < limit, 1, 0), 0)
                return cnt + _col_partial(ones, jnp.add)
            cnt = lax.fori_loop(0, n_ch, body, jnp.zeros((8, tl), jnp.int32))
            return jnp.sum(cnt, axis=0, keepdims=True)

        n_bits = (n_chunks_total * KEY_CHUNK).bit_length()

        def bit_body(b, t):
            cand = t | lax.shift_left(jnp.int32(1), n_bits - 1 - b)
            return jnp.where(tied_before(cand) < take, cand, t)

        last = lax.fori_loop(0, n_bits, bit_body, jnp.zeros((1, tl), jnp.int32))

        def drop_body(j, carry):
            key = key_scr[j]
            key_scr[j] = jnp.where(key == thr, jnp.where(positions(j) > last, KEY_NEG_INF, key), key)
            return carry

        lax.fori_loop(0, n_ch, drop_body, 0)

    acc_scr[...] = jnp.zeros_like(acc_scr)

    def att_body(j, ms):
        k0 = pl.multiple_of(j * KEY_CHUNK, KEY_CHUNK)
        bias = jnp.where(key_scr[j] >= thr, 0.0, NEG_BIG)
        bias = jnp.concatenate([bias] * rep, axis=1)
        new = []
        for g in range(n_kv_heads):
            kb = k_ref[pl.ds(k0, KEY_CHUNK), g * LANES:(g + 1) * LANES]
            s = jnp.dot(kb, qt_scr[g], preferred_element_type=F32) + bias
            m_new = jnp.maximum(ms[g], _col_reduce(s, jnp.maximum))
            p = jnp.exp2(s - m_new).astype(BF16)
            va = vaug_ref[pl.ds(k0, KEY_CHUNK), g * 2 * LANES:(g + 1) * 2 * LANES]
            alpha = _row_to_cols(jnp.exp2(ms[g] - m_new))
            acc_scr[g] = (acc_scr[g] * jnp.concatenate([alpha, alpha], axis=1)
                          + lax.dot_general(p, va, (((0,), (0,)), ((), ())), preferred_element_type=F32))
            new.append(m_new)
        return tuple(new)

    lax.fori_loop(0, n_ch, att_body, tuple(jnp.full((1, rep * tl), NEG_BIG, F32) for _ in range(n_kv_heads)))

    for g in range(n_kv_heads):
        acc = acc_scr[g]
        o = acc[:, :LANES] / acc[:, LANES:]
        for r in range(rep):
            hh = g * rep + r
            o_ref[:, hh * LANES:(hh + 1) * LANES] = o[r * tl:r * tl + tq].astype(o_ref.dtype)


def _dsa_attention(q_all, qi_all, kiw_f32, k_bf, vaug_bf, ki_bf, *, batch, q_len, k_len, n_valid_keys, p_len, tq,
                   n_heads, n_kv_heads, n_idx_heads, n_sel):
    width = n_heads * HEAD_DIM
    assert qi_all.shape[1] == n_idx_heads * LANES
    nq = q_len // tq
    tl = max(tq, LANES)
    n_chunks = k_len // KEY_CHUNK
    rep = n_heads // n_kv_heads
    kv_w = n_kv_heads * HEAD_DIM
    return pl.pallas_call(
        functools.partial(_dsa_kernel, tq=tq, tl=tl, n_sel=n_sel, p_len=p_len, n_valid_keys=n_valid_keys,
                          all_keys_valid=n_valid_keys == k_len,
                          n_idx_heads=n_idx_heads, n_kv_heads=n_kv_heads, rep=rep),
        grid=(batch, nq),
        in_specs=[pl.BlockSpec((tq, width), lambda b, i: (b * nq + i, 0)),
                  pl.BlockSpec((tq, n_idx_heads * LANES), lambda b, i: (b * nq + i, 0)),
                  pl.BlockSpec((tq, 2 * LANES), lambda b, i: (b * nq + i, 0)),
                  pl.BlockSpec((k_len, kv_w), lambda b, i: (b, 0)),
                  pl.BlockSpec((k_len, 2 * kv_w), lambda b, i: (b, 0)),
                  pl.BlockSpec((k_len, LANES), lambda b, i: (b, 0))],
        out_specs=pl.BlockSpec((tq, width), lambda b, i: (b * nq + i, 0)),
        out_shape=jax.ShapeDtypeStruct((batch * q_len, width), BF16),
        scratch_shapes=[pltpu.VMEM((HEAD_DIM, n_idx_heads * tl), BF16),
                        pltpu.VMEM((n_kv_heads, HEAD_DIM, rep * tl), BF16),
                        pltpu.VMEM((n_chunks, KEY_CHUNK, tl), jnp.int32),
                        pltpu.VMEM((n_chunks, KEY_CHUNK // 2, tl), jnp.int32),
                        pltpu.VMEM((n_chunks, KEY_CHUNK // 2, tl), jnp.int32),
                        pltpu.VMEM((n_kv_heads, rep * tl, 2 * LANES), F32)],
        compiler_params=_params(("arbitrary", "arbitrary")),
        name="dsa_attention",
    )(q_all, qi_all, kiw_f32, k_bf, vaug_bf, ki_bf)


def _diff_kernel(dq_ref, dk_ref, dv_ref, lq1_ref, lk1_ref, lq2_ref, lk2_ref, g_ref, o_ref,
                 acc_scr, *, tq, tl, hp, n_chunks_total, p_len, n_valid_keys, all_keys_valid, lam_init):
    q_start = p_len + pl.program_id(2) * tq
    n_ch = _num_key_chunks(q_start, tq, n_valid_keys, n_chunks_total)
    hw = 2 * HEAD_DIM
    n_maps = 2 * hp

    lam = (jnp.exp(jnp.sum(lq1_ref[...] * lk1_ref[...], axis=1, keepdims=True))
           - jnp.exp(jnp.sum(lq2_ref[...] * lk2_ref[...], axis=1, keepdims=True)) + lam_init)
    qt = [_t_bf16(dq_ref[:, c * LANES:(c + 1) * LANES], tl) for c in range(n_maps)]

    acc_scr[...] = jnp.zeros_like(acc_scr)

    def att_body(j, carry, masked):
        k0 = pl.multiple_of(j * KEY_CHUNK, KEY_CHUNK)
        new = []
        for c in range(n_maps):
            m, l = carry[2 * c:2 * c + 2]
            kb = dk_ref[pl.ds(k0, KEY_CHUNK), c * LANES:(c + 1) * LANES]
            vb = dv_ref[pl.ds(k0, KEY_CHUNK), (c // 2) * hw:(c // 2 + 1) * hw]
            s = jnp.dot(kb, qt[c], preferred_element_type=F32)
            if masked:
                s = _fill_invisible(s, NEG_BIG, q_start, tq, k0, n_valid_keys, all_keys_valid)
            m_new = jnp.maximum(m, _col_reduce(s, jnp.maximum))
            p = jnp.exp2(s - m_new)
            alpha = jnp.exp2(m - m_new)
            alpha_c = _row_to_cols(alpha)
            acc_scr[c] = (acc_scr[c] * jnp.concatenate([alpha_c, alpha_c], axis=1)
                          + lax.dot_general(p.astype(BF16), vb, (((0,), (0,)), ((), ())),
                                            preferred_element_type=F32))
            new += [m_new, alpha * l + _col_reduce(p, jnp.add)]
        return tuple(new)

    one = (jnp.full((1, tl), NEG_BIG, F32), jnp.zeros((1, tl), F32))
    n_full = jnp.minimum(_num_full_chunks(q_start, n_valid_keys), n_ch)
    carry = lax.fori_loop(0, n_full, functools.partial(att_body, masked=False), one * n_maps)
    carry = lax.fori_loop(n_full, n_ch, functools.partial(att_body, masked=True), carry)

    def normalised(c):
        l_cols = _row_to_cols(carry[2 * c + 1])
        return acc_scr[c] / jnp.concatenate([l_cols, l_cols], axis=1)

    for h in range(hp):
        o = (normalised(2 * h) - lam * normalised(2 * h + 1))[:tq]
        o = o * lax.rsqrt(jnp.mean(o * o, axis=-1, keepdims=True) + EPS) * g_ref[...]
        o_ref[:, h * hw:(h + 1) * hw] = (o * (1.0 - lam_init)).astype(o_ref.dtype)


def _diff_attention(dq_all, dk_bf, dv_bf, lams, g_subln, *, batch, q_len, k_len, n_valid_keys, p_len, tq,
                    n_heads, lam_init):
    hw = 2 * HEAD_DIM
    hp = 2 if n_heads % 2 == 0 else 1
    width = n_heads * hw
    assert dq_all.shape[1] == width
    nq = q_len // tq
    tl = max(tq, LANES)
    n_chunks = k_len // KEY_CHUNK
    vec = pl.BlockSpec((1, HEAD_DIM), lambda b, h, i: (0, 0))
    return pl.pallas_call(
        functools.partial(_diff_kernel, tq=tq, tl=tl, hp=hp, n_chunks_total=n_chunks, p_len=p_len,
                          n_valid_keys=n_valid_keys, all_keys_valid=n_valid_keys == k_len, lam_init=lam_init),
        grid=(batch, n_heads // hp, nq),
        in_specs=[pl.BlockSpec((tq, hp * hw), lambda b, h, i: (b * nq + i, h)),
                  pl.BlockSpec((k_len, hp * hw), lambda b, h, i: (b, h)),
                  pl.BlockSpec((k_len, hp * hw), lambda b, h, i: (b, h)),
                  vec, vec, vec, vec,
                  pl.BlockSpec((1, hw), lambda b, h, i: (0, 0))],
        out_specs=pl.BlockSpec((tq, hp * hw), lambda b, h, i: (b * nq + i, h)),
        out_shape=jax.ShapeDtypeStruct((batch * q_len, width), BF16),
        scratch_shapes=[pltpu.VMEM((2 * hp, tl, hw), F32)],
        compiler_params=_params(("arbitrary", "arbitrary", "arbitrary")),
        name="diff_attention",
    )(dq_all, dk_bf, dv_bf, *[v.reshape(1, HEAD_DIM) for v in lams], g_subln.reshape(1, hw))


def _outproj_kernel(a_ref, d_ref, wa_ref, wd_ref, x_ref, ga_ref, o_ref, *, per_token):
    mix = (jnp.dot(a_ref[...], wa_ref[...], preferred_element_type=F32)
           + jnp.dot(d_ref[...], wd_ref[...], preferred_element_type=F32))
    ga = ga_ref[...] if per_token else ga_ref[0]
    o_ref[...] = x_ref[...] + ga * mix


def _outproj(a_out, d_out, w_out_bf, x2, mod, *, per_token, rows_per_batch, tm, tn):
    t, d = x2.shape
    wa = a_out.shape[1]
    wd = d_out.shape[1]
    assert wa == wd
    cb = 2 * (d // tn)
    return pl.pallas_call(
        functools.partial(_outproj_kernel, per_token=per_token),
        grid=(t // tm, d // tn),
        in_specs=[pl.BlockSpec((tm, wa), lambda m, n: (m, 0)),
                  pl.BlockSpec((tm, wd), lambda m, n: (m, 0)),
                  pl.BlockSpec((wa, tn), lambda m, n: (0, n)),
                  pl.BlockSpec((wd, tn), lambda m, n: (1, n)),
                  pl.BlockSpec((tm, tn), lambda m, n: (m, n)),
                  _mod_spec(mod, per_token, tm, tn, cb, rows_per_batch)],
        out_specs=pl.BlockSpec((tm, tn), lambda m, n: (m, n)),
        out_shape=jax.ShapeDtypeStruct((t, d), F32),
        compiler_params=_params(("arbitrary", "arbitrary")),
        name="outproj",
    )(a_out, d_out, w_out_bf, w_out_bf, x2, mod)


def _top_rows(x, k, with_rank=False):
    tops = []
    rank = jnp.full(x.shape, float(k), F32) if with_rank else None
    for i in range(k):
        mx = jnp.max(x, axis=0, keepdims=True)
        tops.append(mx)
        hit = x == mx
        if with_rank:
            rank = jnp.where(hit, float(i), rank)
        x = jnp.where(hit, -jnp.inf, x)
    return (tops, rank) if with_rank else tops


def _pack_bf16_pair(lo, hi):
    def bits(x):
        b = pltpu.bitcast(x, jnp.uint32)
        return (b + jnp.uint32(0x7FFF) + ((b >> 16) & jnp.uint32(1))) >> 16
    return bits(lo) | (bits(hi) << 16)


def _peer_route_kernel(h_ref, wq_ref, keys_ref, cnt_ref, e1_ref, rank_ref, e2_ref, q_scr, s1_scr, top_scr):
    hc = pl.program_id(1)
    c = hc % 2
    half = keys_ref.shape[2]

    @pl.when(hc == 0)
    def _():
        q_scr[...] = jnp.dot(wq_ref[...], h_ref[...], preferred_element_type=F32).astype(BF16)

    q_t = q_scr[pl.ds(pl.multiple_of(hc * half, half), half), :]
    s_t = jnp.dot(keys_ref[0].astype(BF16), q_t, preferred_element_type=F32)

    @pl.when(c == 0)
    def _():
        s1_scr[...] = s_t
        top_scr[...] = jnp.concatenate(_top_rows(s_t, PEER_TOPK), axis=0)

    @pl.when(c == 1)
    def _():
        tops2, rank2 = _top_rows(s_t, PEER_TOPK, with_rank=True)
        tops = jnp.concatenate(tops2, axis=0)
        top1 = top_scr[...]
        cand = jnp.concatenate([top1[0:1, :] + tops]
                               + [top1[i:i + 1, :] + tops[:PEER_TOPK // 2] for i in range(1, PEER_TOPK)], axis=0)
        best = _top_rows(cand, PEER_TOPK)
        m = best[0]
        z = jnp.zeros_like(m)
        for bk in best:
            z = z + jnp.exp(bk - m)
        thr = best[PEER_TOPK - 1]
        s1 = s1_scr[...]
        cnt = jnp.zeros_like(s1)
        for j in range(PEER_TOPK):
            cnt = cnt + jnp.where(s1 + tops2[j] >= thr, 1.0, 0.0)
        e1 = jnp.exp(s1 - top1[0:1, :]) * (0.5 / z)
        e2 = jnp.exp(s_t - tops2[0])
        hn = s_t.shape[0] // 2
        cnt_ref[0] = _pack_bf16_pair(cnt, cnt)
        e1_ref[0] = _pack_bf16_pair(e1, e1)
        rank_ref[0] = _pack_bf16_pair(rank2[:hn], rank2[hn:])
        e2_ref[0] = _pack_bf16_pair(e2[:hn], e2[hn:])


def _peer_route(h_t, wq_t_bf, sub_keys, *, tm):
    d, t = h_t.shape
    heads, _, n_keys, half = sub_keys.shape
    keys2 = sub_keys.reshape(heads * 2, n_keys, half)
    a_spec = pl.BlockSpec((1, n_keys, tm), lambda m, hc: (hc // 2, 0, m))
    b_spec = pl.BlockSpec((1, n_keys // 2, tm), lambda m, hc: (hc // 2, 0, m))
    a_tab = jax.ShapeDtypeStruct((heads, n_keys, t), jnp.uint32)
    b_tab = jax.ShapeDtypeStruct((heads, n_keys // 2, t), jnp.uint32)
    return pl.pallas_call(
        _peer_route_kernel,
        grid=(t // tm, heads * 2),
        in_specs=[pl.BlockSpec((d, tm), lambda m, hc: (0, m)),
                  pl.BlockSpec((heads * 2 * half, d), lambda m, hc: (0, 0), pipeline_mode=pl.Buffered(1)),
                  pl.BlockSpec((1, n_keys, half), lambda m, hc: (hc, 0, 0))],
        out_specs=[a_spec, a_spec, b_spec, b_spec],
        out_shape=[a_tab, a_tab, b_tab, b_tab],
        scratch_shapes=[pltpu.VMEM((heads * 2 * half, tm), BF16), pltpu.VMEM((n_keys, tm), F32),
                        pltpu.VMEM((PEER_TOPK, tm), F32)],
        compiler_params=_params(("arbitrary", "arbitrary")),
        name="peer_route",
    )(h_t, wq_t_bf, keys2)


def _gated_gelu(x, half_gate):
    c = 0.7978845608028654
    inner = x * (c + (c * 0.044715) * (x * x))
    return (x * half_gate) * (1.0 + jnp.tanh(inner))


def _rows_bf16(row_words, n_rows):
    tile = jnp.broadcast_to(row_words, (8, row_words.shape[1]))
    return pltpu.bitcast(jnp.concatenate([tile] * (n_rows // 16), axis=0), BF16)


def _peer_ffn_kernel(h_ref, u_ref, vt_ref, cnt_ref, e1_ref, rank_ref, e2_ref, o_ref, *, n_keys):
    e = pl.program_id(1)
    te, tm = u_ref.shape[0], h_ref.shape[1]
    heads = cnt_ref.shape[0]
    a0 = e * (te // n_keys)

    @pl.when(e == 0)
    def _():
        o_ref[...] = jnp.zeros_like(o_ref)

    act = jnp.dot(u_ref[...], h_ref[...], preferred_element_type=F32)
    strip = min(tm, LANES)
    hn = n_keys // 2
    w_rows = [[None] * (tm // strip) for _ in range(2 * te // n_keys)]
    for ai in range(te // n_keys):
        cnt_a = [cnt_ref[h, pl.ds(a0 + ai, 1), :] for h in range(heads)]
        e1_a = [e1_ref[h, pl.ds(a0 + ai, 1), :] for h in range(heads)]
        for c in range(tm // strip):
            cols = slice(c * strip, (c + 1) * strip)
            gate = jnp.zeros((n_keys, strip), BF16)
            for h in range(heads):
                cnt_b = _rows_bf16(cnt_a[h][:, cols], n_keys)
                e1_b = _rows_bf16(e1_a[h][:, cols], n_keys)
                keep = pltpu.bitcast(rank_ref[h, :, cols], BF16) < cnt_b
                gate = gate + jnp.where(keep, pltpu.bitcast(e2_ref[h, :, cols], BF16) * e1_b,
                                        jnp.zeros_like(e1_b))
            words = pltpu.bitcast(gate, jnp.uint32)
            halves = (pltpu.bitcast(words << 16, F32), pltpu.bitcast(words & jnp.uint32(0xFFFF0000), F32))
            for k, g in enumerate(halves):
                rows = slice(ai * n_keys + k * hn, ai * n_keys + (k + 1) * hn)
                w_rows[2 * ai + k][c] = _gated_gelu(act[rows, cols], g).astype(BF16)
    w = jnp.concatenate([r[0] if len(r) == 1 else jnp.concatenate(r, axis=1) for r in w_rows], axis=0)
    o_ref[...] += jnp.dot(vt_ref[...], w, preferred_element_type=F32)


def _peer_ffn(h_t, u_bf, vt_bf, cnt, e1, rank, e2, *, tm, te):
    d, t = h_t.shape
    n_exp = u_bf.shape[0]
    n_blocks = n_exp // te
    heads, n_keys, _ = cnt.shape
    once = pl.Buffered(1)
    a_spec = pl.BlockSpec((heads, n_keys, tm), lambda m, e: (0, 0, m), pipeline_mode=once)
    b_spec = pl.BlockSpec((heads, n_keys // 2, tm), lambda m, e: (0, 0, m), pipeline_mode=once)
    return pl.pallas_call(
        functools.partial(_peer_ffn_kernel, n_keys=n_keys),
        grid=(t // tm, n_blocks),
        in_specs=[pl.BlockSpec((d, tm), lambda m, e: (0, m), pipeline_mode=once),
                  pl.BlockSpec((te, d), lambda m, e: (e, 0)),
                  pl.BlockSpec((d, te), lambda m, e: (0, e)),
                  a_spec, a_spec, b_spec, b_spec],
        out_specs=pl.BlockSpec((d, tm), lambda m, e: (0, m), pipeline_mode=once),
        out_shape=jax.ShapeDtypeStruct((d, t), F32),
        compiler_params=_params(("arbitrary", "arbitrary"), vmem=PEER_FFN_VMEM),
        name="peer_ffn",
    )(h_t, u_bf, vt_bf, cnt, e1, rank, e2)


def _final_kernel(x_ref, pt_ref, ga_ref, g_ref, o_ref, *, per_token, normalize):
    ga = ga_ref[...] if per_token else ga_ref[0]
    x = x_ref[...] + ga * pt_ref[...].T
    if normalize:
        x = x * lax.rsqrt(jnp.mean(x * x, axis=-1, keepdims=True) + EPS) * g_ref[...]
    o_ref[...] = x


def _final(x1, peer_t, mod, g_final, *, per_token, rows_per_batch, tm, normalize):
    t, d = x1.shape
    row = pl.BlockSpec((tm, d), lambda m: (m, 0))
    return pl.pallas_call(
        functools.partial(_final_kernel, per_token=per_token, normalize=normalize),
        grid=(t // tm,),
        in_specs=[row, pl.BlockSpec((d, tm), lambda m: (0, m)), _mod_spec(mod, per_token, tm, d, 5, rows_per_batch),
                  pl.BlockSpec((1, d), lambda m: (0, 0))],
        out_specs=row,
        out_shape=jax.ShapeDtypeStruct((t, d), F32),
        compiler_params=_params(("arbitrary",)),
        name="final",
    )(x1, peer_t, mod, g_final.reshape(1, d))


def _rope_tables(pos):
    half = HEAD_DIM // 2
    inv = ROPE_THETA ** (-jnp.arange(half, dtype=F32) / half)
    ang = pos.astype(F32)[:, None] * inv[None, :]
    cos, sin = jnp.cos(ang), jnp.sin(ang)
    return jnp.concatenate([cos, cos], axis=1), jnp.concatenate([-sin, sin], axis=1)


def _pick_tile(n, prefs):
    for p in prefs:
        if n % p == 0:
            return p
    return n


def _layer(x, mod_rows, past, layer, last_layer, w, dims):
    b, s, d = x.shape
    t = b * s
    n_heads, n_kv, n_idx, n_diff = dims
    p_len = 0 if past is None else past[0].shape[1]
    n_keys_valid = p_len + s
    n_sel = min(DSA_TOPK, n_keys_valid // 4)
    x2 = x.reshape(t, d)

    per_token = s % 256 != 0
    if per_token:
        mod = jnp.repeat(mod_rows, s, axis=0)
    else:
        mod = mod_rows.reshape(b, 1, 6 * d)
    tm_big = _pick_tile(t if per_token else s, (1024, 512, 256, 128))
    tm_mid = _pick_tile(t if per_token else s, (512, 256, 128))
    tm_small = _pick_tile(t if per_token else s, (256, 128))

    pos = p_len + jnp.arange(s)
    cos, sin = _rope_tables(pos)
    if per_token:
        cos, sin = jnp.tile(cos, (b, 1)), jnp.tile(sin, (b, 1))

    h = _norm_mod(x2, w["g_norm_mix"], mod, which=1, per_token=per_token, rows_per_batch=s, tm=tm_mid,
                  transpose=False)

    qw, kvw, dw = n_heads * HEAD_DIM, n_kv * HEAD_DIM, n_diff * 2 * HEAD_DIM
    head = functools.partial(_proj, h, w["w_head"], cos=cos, sin=sin, tm=tm_big)
    tail = functools.partial(_proj, h, w["w_tail"], cos=cos, sin=sin, tm=tm_big)
    q_scale = HEAD_DIM ** -0.5 * math.log2(math.e)
    wide = 1024
    bf_only = dict(out_f32=False, out_bf16=True, tn=wide)
    both = dict(out_f32=True, out_bf16=True)
    (q_all,) = head(col0=0, n=qw, rope_cols=wide, scale=q_scale, name="proj_q", **bf_only)
    (qi_all,) = head(col0=qw + 2 * kvw, n=n_idx * LANES, rope_cols=wide, scale=q_scale, name="proj_qi", **bf_only)
    (dq_all,) = tail(col0=0, n=dw, rope_cols=wide, scale=q_scale, name="proj_dq", **bf_only)
    k_f, k_b = head(col0=qw, n=kvw, rope_cols=512, tn=512, name="proj_k", **both)
    v_f, v_b = head(col0=qw + kvw, n=kvw, rope_cols=0, tn=512, name="proj_v", **both)
    dk_f, dk_b = tail(col0=dw, n=dw, rope_cols=wide, tn=wide, name="proj_dk", **both)
    dv_f, dv_b = tail(col0=2 * dw, n=dw, rope_cols=0, tn=wide, name="proj_dv", **both)
    kiw_f, kiw_b = head(col0=qw + 2 * kvw + n_idx * LANES, n=2 * LANES, rope_cols=LANES, out_f32=True,
                        out_bf16=True, tn=2 * LANES, name="proj_kiw")

    ki_f = kiw_f[:, :LANES]
    new_rows = (k_f.reshape(b, s, n_kv, HEAD_DIM), v_f.reshape(b, s, n_kv, HEAD_DIM), ki_f.reshape(b, s, LANES),
                dk_f.reshape(b, s, n_diff, 2, HEAD_DIM), dv_f.reshape(b, s, n_diff, 2 * HEAD_DIM))

    if past is None:
        k_len = s
        k_all, v_all, ki_all, dk_all, dv_all = k_b, v_b, kiw_b, dk_b, dv_b
    else:
        k_len = -(-n_keys_valid // KEY_CHUNK) * KEY_CHUNK

        def join(cache, new, width):
            new = new.reshape(b, s, -1)[:, :, :width]
            old = lax.optimization_barrier(cache.reshape(b, p_len, width))
            both_ = jnp.concatenate([old.astype(BF16), new], axis=1)
            both_ = jnp.pad(both_, ((0, 0), (0, k_len - n_keys_valid), (0, 0)))
            return both_.reshape(b * k_len, width)

        k_all = join(past[0], k_b, n_kv * HEAD_DIM)
        v_all = join(past[1], v_b, n_kv * HEAD_DIM)
        ki_all = join(past[2], kiw_b, LANES)
        dk_all = join(past[3], dk_b, n_diff * 2 * HEAD_DIM)
        dv_all = join(past[4], dv_b, n_diff * 2 * HEAD_DIM)

    ones = jnp.ones((v_all.shape[0], HEAD_DIM), BF16)
    vaug_all = jnp.concatenate(
        [piece for g in range(n_kv) for piece in (v_all[:, g * HEAD_DIM:(g + 1) * HEAD_DIM], ones)], axis=1)
    tq_a = _pick_tile(s, (128, 64, 32))
    a_out = _dsa_attention(q_all, qi_all, kiw_f, k_all, vaug_all, ki_all, batch=b, q_len=s, k_len=k_len,
                           n_valid_keys=n_keys_valid, p_len=p_len, tq=tq_a, n_heads=n_heads, n_kv_heads=n_kv,
                           n_idx_heads=n_idx, n_sel=n_sel)
    tq_d = _pick_tile(s, (256, 128, 64, 32))
    d_out = _diff_attention(dq_all, dk_all, dv_all, w["lams"], w["g_subln"], batch=b, q_len=s, k_len=k_len,
                            n_valid_keys=n_keys_valid, p_len=p_len, tq=tq_d, n_heads=n_diff,
                            lam_init=_lambda_init(layer))

    x1 = _outproj(a_out, d_out, w["w_out"], x2, mod, per_token=per_token, rows_per_batch=s, tm=tm_big,
                  tn=_pick_tile(d, (1024, 512)))

    h2_t = _norm_mod(x1, w["g_norm_ffn"], mod, which=2, per_token=per_token, rows_per_batch=s, tm=tm_mid,
                     transpose=True)
    cnt, e1, rank, e2 = _peer_route(h2_t, w["peer_wq_t"], w["peer_keys"], tm=tm_mid)
    peer_t = _peer_ffn(h2_t, w["peer_u"], w["peer_v_t"], cnt, e1, rank, e2, tm=tm_mid, te=1024)
    x_out = _final(x1, peer_t, mod, w["g_final"], per_token=per_token, rows_per_batch=s, tm=tm_small,
                   normalize=last_layer)
    return x_out.reshape(b, s, d), new_rows


def kernel(x_prompt, x_sample, cache_dsa_k, cache_dsa_v, cache_idx_k, cache_diff_k, cache_diff_v, c_prompt, c_sample, w_ada, b_ada, g_norm_mix, g_norm_ffn, w_in, diff_lambda_q1, diff_lambda_k1, diff_lambda_q2, diff_lambda_k2, g_diff_subln, w_out, peer_w_query, peer_sub_keys, peer_u, peer_v, g_final):
    depth = w_in.shape[0]
    bp, bs = x_prompt.shape[0], x_sample.shape[0]
    n_kv = cache_dsa_k.shape[3]
    n_diff = cache_diff_k.shape[3]
    n_heads, n_idx_heads = DSA_HEADS, IDX_HEADS
    qw, kvw, dw = n_heads * HEAD_DIM, n_kv * HEAD_DIM, n_diff * 2 * HEAD_DIM
    tail0 = qw + 2 * kvw + n_idx_heads * LANES + LANES + n_idx_heads
    assert w_in.shape[2] == tail0 + 3 * dw
    head_w = qw + 2 * kvw + n_idx_heads * LANES + 2 * LANES
    dims = (n_heads, n_kv, n_idx_heads, n_diff)

    hp, hs = x_prompt, x_sample
    rows_p, rows_s = [], []
    n_c = bp + bs
    c_pad = jnp.pad(jnp.concatenate([c_prompt, c_sample], axis=0), ((0, (-n_c) % 16), (0, 0)))
    for l in range(depth):
        mod_all = _adaln(c_pad, w_ada[l], b_ada[l])
        weights = {
            "w_head": w_in[l][:, :head_w].astype(BF16),
            "w_tail": w_in[l][:, tail0:].astype(BF16),
            "w_out": w_out[l].astype(BF16),
            "peer_wq_t": peer_w_query[l].T.astype(BF16),
            "peer_keys": peer_sub_keys[l],
            "peer_u": peer_u[l].astype(BF16), "peer_v_t": peer_v[l].T.astype(BF16),
            "g_norm_mix": g_norm_mix[l], "g_norm_ffn": g_norm_ffn[l], "g_subln": g_diff_subln[l],
            "lams": (diff_lambda_q1[l], diff_lambda_k1[l], diff_lambda_q2[l], diff_lambda_k2[l]),
            "g_final": g_final,
        }
        last = l == depth - 1
        hp, rp = _layer(hp, mod_all[:bp], None, l, last, weights, dims)
        past = (cache_dsa_k[l], cache_dsa_v[l], cache_idx_k[l], cache_diff_k[l], cache_diff_v[l])
        hs, rs = _layer(hs, mod_all[bp:bp + bs], past, l, last, weights, dims)
        rows_p.append(rp)
        rows_s.append(rs)
    stack = lambda rows, i: jnp.stack([r[i] for r in rows])
    return (hp, hs) + tuple(stack(rows_p, i) for i in range(5)) + tuple(stack(rows_s, i) for i in range(5))
```

```python
import functools
import math

import jax
import jax.numpy as jnp
from jax import lax
from jax.experimental import pallas as pl
from jax.experimental.pallas import tpu as pltpu

CHUNK = 64
HEAD_DIM = 128
ROPE_THETA = 10000.0
EPS = 1e-6
DSA_HEADS = 16
IDX_HEADS = 16
DSA_TOPK = 256
PEER_TOPK = 16
LANES = 128
KEY_CHUNK = 512
NEG_BIG = -1e30
INT_MIN = -(2 ** 31)
KEY_NEG_INF = INT_MIN + 0x7FFFFF
VMEM_LIMIT = 56 * 1024 * 1024
PEER_FFN_VMEM = 62 * 1024 * 1024

BF16 = jnp.bfloat16
F32 = jnp.float32


def _params(sem, vmem=VMEM_LIMIT):
    return pltpu.CompilerParams(dimension_semantics=sem, vmem_limit_bytes=vmem)


def _lambda_init(layer):
    return 0.8 - 0.6 * math.exp(-0.3 * layer)


def _adaln_kernel(c_ref, w_ref, b_ref, o_ref):
    c = c_ref[...]
    a = (c * jax.nn.sigmoid(c)).astype(BF16)
    o_ref[...] = jnp.dot(a, w_ref[...].astype(BF16), preferred_element_type=F32) + b_ref[...]


def _adaln(c_pad, w_ada, b_ada):
    bp, d = c_pad.shape
    n = w_ada.shape[1]
    tn = 512
    return pl.pallas_call(
        _adaln_kernel,
        grid=(n // tn,),
        in_specs=[pl.BlockSpec((bp, d), lambda j: (0, 0)),
                  pl.BlockSpec((d, tn), lambda j: (0, j)),
                  pl.BlockSpec((1, tn), lambda j: (0, j))],
        out_specs=pl.BlockSpec((bp, tn), lambda j: (0, j)),
        out_shape=jax.ShapeDtypeStruct((bp, n), F32),
        compiler_params=_params(("arbitrary",)),
        name="adaln",
    )(c_pad, w_ada, b_ada.reshape(1, n))


def _norm_mod_kernel(x_ref, g_ref, sc_ref, sh_ref, o_ref, *, per_token, transpose):
    x = x_ref[...]
    y = x * lax.rsqrt(jnp.mean(x * x, axis=-1, keepdims=True) + EPS) * g_ref[...]
    sc = sc_ref[...] if per_token else sc_ref[0]
    sh = sh_ref[...] if per_token else sh_ref[0]
    h = y * (1.0 + sc) + sh
    if transpose:
        o_ref[...] = h.T.astype(o_ref.dtype)
    else:
        o_ref[...] = h.astype(o_ref.dtype)


def _mod_spec(mod, per_token, tm, tn, col_block, rows_per_batch):
    if per_token:
        return pl.BlockSpec((tm, tn), lambda m, n=0, cb=col_block: (m, cb + n))
    tiles_per_batch = rows_per_batch // tm
    return pl.BlockSpec((1, 1, tn), lambda m, n=0, cb=col_block: (m // tiles_per_batch, 0, cb + n))


def _norm_mod(x2, g, mod, *, which, per_token, rows_per_batch, tm, transpose):
    t, d = x2.shape
    sh_blk, sc_blk = (0, 1) if which == 1 else (3, 4)
    out_shape = (d, t) if transpose else (t, d)
    out_spec = pl.BlockSpec((d, tm), lambda m: (0, m)) if transpose else pl.BlockSpec((tm, d), lambda m: (m, 0))
    return pl.pallas_call(
        functools.partial(_norm_mod_kernel, per_token=per_token, transpose=transpose),
        grid=(t // tm,),
        in_specs=[pl.BlockSpec((tm, d), lambda m: (m, 0)),
                  pl.BlockSpec((1, d), lambda m: (0, 0)),
                  _mod_spec(mod, per_token, tm, d, sc_blk, rows_per_batch),
                  _mod_spec(mod, per_token, tm, d, sh_blk, rows_per_batch)],
        out_specs=out_spec,
        out_shape=jax.ShapeDtypeStruct(out_shape, BF16),
        compiler_params=_params(("arbitrary",)),
        name="norm_mod_t" if transpose else "norm_mod",
    )(x2, g.reshape(1, d), mod, mod)


def _rope_tile(acc, cos, sin, n_chunks):
    outs = []
    for j in range(n_chunks):
        xj = acc[:, j * LANES:(j + 1) * LANES]
        outs.append(xj * cos + pltpu.roll(xj, LANES // 2, axis=1) * sin)
    return outs[0] if n_chunks == 1 else jnp.concatenate(outs, axis=1)


def _proj_kernel(*refs, rope_cols, scale, out_f32, out_bf16):
    h_ref, w_ref, cos_ref, sin_ref = refs[:4]
    outs = refs[4:]
    acc = jnp.dot(h_ref[...], w_ref[...], preferred_element_type=F32)
    tn = acc.shape[1]
    if rope_cols:
        roped = _rope_tile(acc[:, :rope_cols], cos_ref[...], sin_ref[...], rope_cols // LANES)
        acc = roped if rope_cols == tn else jnp.concatenate([roped, acc[:, rope_cols:]], axis=1)
    k = 0
    if out_f32:
        outs[k][...] = acc
        k += 1
    if out_bf16:
        outs[k][...] = (acc * scale if scale != 1.0 else acc).astype(BF16)


def _proj(h, w, cos, sin, *, col0, n, rope_cols, scale=1.0, out_f32, out_bf16, tm, tn, name):
    t, d = h.shape
    assert rope_cols in (0, tn) or n == tn
    assert col0 % tn == 0 and n % tn == 0 and col0 + n <= w.shape[1]
    j0 = col0 // tn
    pos_tiles = cos.shape[0] // tm
    out_shape, out_specs = [], []
    for want, dt in ((out_f32, F32), (out_bf16, BF16)):
        if want:
            out_shape.append(jax.ShapeDtypeStruct((t, n), dt))
            out_specs.append(pl.BlockSpec((tm, tn), lambda m, j: (m, j)))
    return pl.pallas_call(
        functools.partial(_proj_kernel, rope_cols=rope_cols, scale=scale, out_f32=out_f32, out_bf16=out_bf16),
        grid=(t // tm, n // tn),
        in_specs=[pl.BlockSpec((tm, d), lambda m, j: (m, 0)),
                  pl.BlockSpec((d, tn), lambda m, j: (0, j0 + j)),
                  pl.BlockSpec((tm, LANES), lambda m, j: (m % pos_tiles, 0)),
                  pl.BlockSpec((tm, LANES), lambda m, j: (m % pos_tiles, 0))],
        out_specs=out_specs,
        out_shape=out_shape,
        compiler_params=_params(("arbitrary", "arbitrary")),
        name=name,
    )(h, w, cos, sin)


def _num_key_chunks(q_start, n_q, n_valid_keys, n_chunks_total):
    last_visible = (((q_start + n_q - 1) >> 6) + 1) * CHUNK
    last_visible = jnp.minimum(last_visible, n_valid_keys)
    return jnp.minimum((last_visible + KEY_CHUNK - 1) // KEY_CHUNK, n_chunks_total)


def _num_full_chunks(q_start, n_valid_keys):
    return jnp.minimum(((q_start >> 6) + 1) * CHUNK, n_valid_keys) // KEY_CHUNK


def _ordered_key(x):
    b = pltpu.bitcast(x, jnp.int32)
    return b ^ ((b >> 31) & jnp.int32(0x7FFFFFFF))


def _fill_invisible(x, fill, q_start, n_q, key_start, n_valid_keys, all_keys_valid):
    n_k, n_lanes = x.shape
    kpos = key_start + lax.broadcasted_iota(jnp.int32, (n_k, n_lanes), 0)
    qpos = q_start + jnp.minimum(lax.broadcasted_iota(jnp.int32, (1, n_lanes), 1), n_q - 1)
    x = jnp.where((kpos >> 6) <= (qpos >> 6), x, fill)
    return x if all_keys_valid else jnp.where(kpos < n_valid_keys, x, fill)


def _t_bf16(x, n_lanes):
    x = x.astype(F32)
    if x.shape[0] < n_lanes:
        x = jnp.concatenate([x, jnp.zeros((n_lanes - x.shape[0], x.shape[1]), F32)], axis=0)
    return x.T.astype(BF16)


def _rows_i16(words, n_word_rows):
    tile = jnp.broadcast_to(words, (8, words.shape[1]))
    return pltpu.bitcast(jnp.concatenate([tile] * (n_word_rows // 8), axis=0), jnp.int16)


def _col_partial(x, op, group=8):
    parts = [x[r * group:(r + 1) * group] for r in range(x.shape[0] // group)]
    while len(parts) > 1:
        parts = [op(parts[i], parts[i + 1]) for i in range(0, len(parts) - 1, 2)] + parts[len(parts) & ~1:]
    return parts[0]


def _col_reduce(x, op):
    return (jnp.max if op is jnp.maximum else jnp.sum)(_col_partial(x, op), axis=0, keepdims=True)


def _row_to_cols(row):
    return jnp.broadcast_to(row, (LANES, row.shape[1])).T


def _dsa_kernel(q_ref, qi_ref, wi_ref, k_ref, vaug_ref, ki_ref, o_ref,
                qit_scr, qt_scr, key_scr, hi_scr, lo_scr, acc_scr, *,
                tq, tl, n_sel, p_len, n_valid_keys, all_keys_valid, n_idx_heads, n_kv_heads, rep):
    n_chunks_total = key_scr.shape[0]
    q_start = p_len + pl.program_id(1) * tq
    n_ch = _num_key_chunks(q_start, tq, n_valid_keys, n_chunks_total)
    heads_per_dot = 4
    half_chunk = KEY_CHUNK // 2

    for h in range(n_idx_heads):
        qit_scr[:, h * tl:(h + 1) * tl] = _t_bf16(qi_ref[:, h * LANES:(h + 1) * LANES], tl)
    for g in range(n_kv_heads):
        for r in range(rep):
            hh = g * rep + r
            qt_scr[g, :, r * tl:(r + 1) * tl] = _t_bf16(q_ref[:, hh * LANES:(hh + 1) * LANES], tl)
    wi = wi_ref[:, LANES:2 * LANES] * (n_idx_heads ** -0.5)
    if tq < tl:
        wi = jnp.concatenate([wi, jnp.zeros((tl - tq, LANES), F32)], axis=0)
    wi_t = wi.T

    def idx_body(j, carry):
        k0 = pl.multiple_of(j * KEY_CHUNK, KEY_CHUNK)
        kib = ki_ref[pl.ds(k0, KEY_CHUNK), :]
        score = jnp.zeros((KEY_CHUNK, tl), F32)
        for h0 in range(0, n_idx_heads, heads_per_dot):
            lg = jnp.dot(kib, qit_scr[:, h0 * tl:(h0 + heads_per_dot) * tl], preferred_element_type=F32)
            for h in range(h0, h0 + heads_per_dot):
                score = score + jnp.maximum(lg[:, (h - h0) * tl:(h - h0 + 1) * tl], 0.0) * wi_t[h:h + 1, :]
        score = _fill_invisible(score, -jnp.inf, q_start, tq, k0, n_valid_keys, all_keys_valid)
        key = _ordered_key(score)
        key_scr[j] = key
        hi16 = (key >> 16) & 0xFFFF
        lo16 = (key & 0xFFFF) ^ 0x8000
        hi_scr[j] = hi16[:half_chunk] | (hi16[half_chunk:] << 16)
        lo_scr[j] = lo16[:half_chunk] | (lo16[half_chunk:] << 16)
        return carry

    lax.fori_loop(0, n_ch, idx_body, 0)

    def count16(scr, pattern, strict):
        c = _rows_i16(pattern | (pattern << 16), half_chunk)

        def body(j, cnt):
            x = pltpu.bitcast(scr[j], jnp.int16)
            ones = jnp.where((x > c) if strict else (x >= c), jnp.int16(1), jnp.int16(0))
            return cnt + _col_partial(ones, jnp.add, group=16)

        cnt = pltpu.bitcast(lax.fori_loop(0, n_ch, body, jnp.zeros((16, tl), jnp.int16)), jnp.int32)
        return jnp.sum((cnt & 0xFFFF) + ((cnt >> 16) & 0xFFFF), axis=0, keepdims=True)

    def bisect16(scr, need):
        def bit_body(b, t):
            cand = t | lax.shift_left(jnp.int32(1), 15 - b)
            return jnp.where(count16(scr, cand ^ 0x8000, False) >= need, cand, t)
        return lax.fori_loop(0, 16, bit_body, jnp.zeros((1, tl), jnp.int32))

    hi_pat = bisect16(hi_scr, n_sel) ^ 0x8000
    need = n_sel - count16(hi_scr, hi_pat, True)

    def keep_bucket(j, carry):
        hi = pltpu.bitcast(hi_scr[j], jnp.int16)
        lo = pltpu.bitcast(lo_scr[j], jnp.int16)
        same = hi == _rows_i16(hi_pat | (hi_pat << 16), half_chunk)
        lo_scr[j] = pltpu.bitcast(jnp.where(same, lo, jnp.int16(-2 ** 15)), jnp.int32)
        return carry

    lax.fori_loop(0, n_ch, keep_bucket, 0)
    lo_u = bisect16(lo_scr, need)
    thr_raw = (((hi_pat << 16) >> 16) << 16) | lo_u
    thr = jnp.maximum(thr_raw, KEY_NEG_INF + 1)

    n_above = n_sel - need
    n_ge = n_above + count16(lo_scr, lo_u ^ 0x8000, False)
    surplus = jnp.where(thr_raw > KEY_NEG_INF, n_ge - n_sel, 0)

    @pl.when(jnp.max(surplus) > 0)
    def _():
        take = n_sel - (n_above + count16(lo_scr, lo_u ^ 0x8000, True))

        def positions(j):
            return j * KEY_CHUNK + lax.broadcasted_iota(jnp.int32, (KEY_CHUNK, tl), 0)

        def tied_before(limit):
            def body(j, cnt):
                ones = jnp.where(key_scr[j] == thr, jnp.where(positions(j) < limit, 1, 0), 0)
                return cnt + _col_partial(ones, jnp.add)
            cnt = lax.fori_loop(0, n_ch, body, jnp.zeros((8, tl), jnp.int32))
            return jnp.sum(cnt, axis=0, keepdims=True)

        n_bits = (n_chunks_total * KEY_CHUNK).bit_length()

        def bit_body(b, t):
            cand = t | lax.shift_left(jnp.int32(1), n_bits - 1 - b)
            return jnp.where(tied_before(cand) < take, cand, t)

        last = lax.fori_loop(0, n_bits, bit_body, jnp.zeros((1, tl), jnp.int32))

        def drop_body(j, carry):
            key = key_scr[j]
            key_scr[j] = jnp.where(key == thr, jnp.where(positions(j) > last, KEY_NEG_INF, key), key)
            return carry

        lax.fori_loop(0, n_ch, drop_body, 0)

    acc_scr[...] = jnp.zeros_like(acc_scr)

    def att_body(j, ms):
        k0 = pl.multiple_of(j * KEY_CHUNK, KEY_CHUNK)
        bias = jnp.where(key_scr[j] >= thr, 0.0, NEG_BIG)
        bias = jnp.concatenate([bias] * rep, axis=1)
        new = []
        for g in range(n_kv_heads):
            kb = k_ref[pl.ds(k0, KEY_CHUNK), g * LANES:(g + 1) * LANES]
            s = jnp.dot(kb, qt_scr[g], preferred_element_type=F32) + bias
            m_new = jnp.maximum(ms[g], _col_reduce(s, jnp.maximum))
            p = jnp.exp2(s - m_new).astype(BF16)
            va = vaug_ref[pl.ds(k0, KEY_CHUNK), g * 2 * LANES:(g + 1) * 2 * LANES]
            alpha = _row_to_cols(jnp.exp2(ms[g] - m_new))
            acc_scr[g] = (acc_scr[g] * jnp.concatenate([alpha, alpha], axis=1)
                          + lax.dot_general(p, va, (((0,), (0,)), ((), ())), preferred_element_type=F32))
            new.append(m_new)
        return tuple(new)

    lax.fori_loop(0, n_ch, att_body, tuple(jnp.full((1, rep * tl), NEG_BIG, F32) for _ in range(n_kv_heads)))

    for g in range(n_kv_heads):
        acc = acc_scr[g]
        o = acc[:, :LANES] / acc[:, LANES:]
        for r in range(rep):
            hh = g * rep + r
            o_ref[:, hh * LANES:(hh + 1) * LANES] = o[r * tl:r * tl + tq].astype(o_ref.dtype)


def _dsa_attention(q_all, qi_all, kiw_f32, k_bf, vaug_bf, ki_bf, *, batch, q_len, k_len, n_valid_keys, p_len, tq,
                   n_heads, n_kv_heads, n_idx_heads, n_sel):
    width = n_heads * HEAD_DIM
    assert qi_all.shape[1] == n_idx_heads * LANES
    nq = q_len // tq
    tl = max(tq, LANES)
    n_chunks = k_len // KEY_CHUNK
    rep = n_heads // n_kv_heads
    kv_w = n_kv_heads * HEAD_DIM
    return pl.pallas_call(
        functools.partial(_dsa_kernel, tq=tq, tl=tl, n_sel=n_sel, p_len=p_len, n_valid_keys=n_valid_keys,
                          all_keys_valid=n_valid_keys == k_len,
                          n_idx_heads=n_idx_heads, n_kv_heads=n_kv_heads, rep=rep),
        grid=(batch, nq),
        in_specs=[pl.BlockSpec((tq, width), lambda b, i: (b * nq + i, 0)),
                  pl.BlockSpec((tq, n_idx_heads * LANES), lambda b, i: (b * nq + i, 0)),
                  pl.BlockSpec((tq, 2 * LANES), lambda b, i: (b * nq + i, 0)),
                  pl.BlockSpec((k_len, kv_w), lambda b, i: (b, 0)),
                  pl.BlockSpec((k_len, 2 * kv_w), lambda b, i: (b, 0)),
                  pl.BlockSpec((k_len, LANES), lambda b, i: (b, 0))],
        out_specs=pl.BlockSpec((tq, width), lambda b, i: (b * nq + i, 0)),
        out_shape=jax.ShapeDtypeStruct((batch * q_len, width), BF16),
        scratch_shapes=[pltpu.VMEM((HEAD_DIM, n_idx_heads * tl), BF16),
                        pltpu.VMEM((n_kv_heads, HEAD_DIM, rep * tl), BF16),
                        pltpu.VMEM((n_chunks, KEY_CHUNK, tl), jnp.int32),
                        pltpu.VMEM((n_chunks, KEY_CHUNK // 2, tl), jnp.int32),
                        pltpu.VMEM((n_chunks, KEY_CHUNK // 2, tl), jnp.int32),
                        pltpu.VMEM((n_kv_heads, rep * tl, 2 * LANES), F32)],
        compiler_params=_params(("arbitrary", "arbitrary")),
        name="dsa_attention",
    )(q_all, qi_all, kiw_f32, k_bf, vaug_bf, ki_bf)


def _diff_kernel(dq_ref, dk_ref, dv_ref, lq1_ref, lk1_ref, lq2_ref, lk2_ref, g_ref, o_ref,
                 acc_scr, *, tq, tl, hp, n_chunks_total, p_len, n_valid_keys, all_keys_valid, lam_init):
    q_start = p_len + pl.program_id(2) * tq
    n_ch = _num_key_chunks(q_start, tq, n_valid_keys, n_chunks_total)
    hw = 2 * HEAD_DIM
    n_maps = 2 * hp

    lam = (jnp.exp(jnp.sum(lq1_ref[...] * lk1_ref[...], axis=1, keepdims=True))
           - jnp.exp(jnp.sum(lq2_ref[...] * lk2_ref[...], axis=1, keepdims=True)) + lam_init)
    qt = [_t_bf16(dq_ref[:, c * LANES:(c + 1) * LANES], tl) for c in range(n_maps)]

    acc_scr[...] = jnp.zeros_like(acc_scr)

    def att_body(j, carry, masked):
        k0 = pl.multiple_of(j * KEY_CHUNK, KEY_CHUNK)
        new = []
        for c in range(n_maps):
            m, l = carry[2 * c:2 * c + 2]
            kb = dk_ref[pl.ds(k0, KEY_CHUNK), c * LANES:(c + 1) * LANES]
            vb = dv_ref[pl.ds(k0, KEY_CHUNK), (c // 2) * hw:(c // 2 + 1) * hw]
            s = jnp.dot(kb, qt[c], preferred_element_type=F32)
            if masked:
                s = _fill_invisible(s, NEG_BIG, q_start, tq, k0, n_valid_keys, all_keys_valid)
            m_new = jnp.maximum(m, _col_reduce(s, jnp.maximum))
            p = jnp.exp2(s - m_new)
            alpha = jnp.exp2(m - m_new)
            alpha_c = _row_to_cols(alpha)
            acc_scr[c] = (acc_scr[c] * jnp.concatenate([alpha_c, alpha_c], axis=1)
                          + lax.dot_general(p.astype(BF16), vb, (((0,), (0,)), ((), ())),
                                            preferred_element_type=F32))
            new += [m_new, alpha * l + _col_reduce(p, jnp.add)]
        return tuple(new)

    one = (jnp.full((1, tl), NEG_BIG, F32), jnp.zeros((1, tl), F32))
    n_full = jnp.minimum(_num_full_chunks(q_start, n_valid_keys), n_ch)
    carry = lax.fori_loop(0, n_full, functools.partial(att_body, masked=False), one * n_maps)
    carry = lax.fori_loop(n_full, n_ch, functools.partial(att_body, masked=True), carry)

    def normalised(c):
        l_cols = _row_to_cols(carry[2 * c + 1])
        return acc_scr[c] / jnp.concatenate([l_cols, l_cols], axis=1)

    for h in range(hp):
        o = (normalised(2 * h) - lam * normalised(2 * h + 1))[:tq]
        o = o * lax.rsqrt(jnp.mean(o * o, axis=-1, keepdims=True) + EPS) * g_ref[...]
        o_ref[:, h * hw:(h + 1) * hw] = (o * (1.0 - lam_init)).astype(o_ref.dtype)


def _diff_attention(dq_all, dk_bf, dv_bf, lams, g_subln, *, batch, q_len, k_len, n_valid_keys, p_len, tq,
                    n_heads, lam_init):
    hw = 2 * HEAD_DIM
    hp = 2 if n_heads % 2 == 0 else 1
    width = n_heads * hw
    assert dq_all.shape[1] == width
    nq = q_len // tq
    tl = max(tq, LANES)
    n_chunks = k_len // KEY_CHUNK
    vec = pl.BlockSpec((1, HEAD_DIM), lambda b, h, i: (0, 0))
    return pl.pallas_call(
        functools.partial(_diff_kernel, tq=tq, tl=tl, hp=hp, n_chunks_total=n_chunks, p_len=p_len,
                          n_valid_keys=n_valid_keys, all_keys_valid=n_valid_keys == k_len, lam_init=lam_init),
        grid=(batch, n_heads // hp, nq),
        in_specs=[pl.BlockSpec((tq, hp * hw), lambda b, h, i: (b * nq + i, h)),
                  pl.BlockSpec((k_len, hp * hw), lambda b, h, i: (b, h)),
                  pl.BlockSpec((k_len, hp * hw), lambda b, h, i: (b, h)),
                  vec, vec, vec, vec,
                  pl.BlockSpec((1, hw), lambda b, h, i: (0, 0))],
        out_specs=pl.BlockSpec((tq, hp * hw), lambda b, h, i: (b * nq + i, h)),
        out_shape=jax.ShapeDtypeStruct((batch * q_len, width), BF16),
        scratch_shapes=[pltpu.VMEM((2 * hp, tl, hw), F32)],
        compiler_params=_params(("arbitrary", "arbitrary", "arbitrary")),
        name="diff_attention",
    )(dq_all, dk_bf, dv_bf, *[v.reshape(1, HEAD_DIM) for v in lams], g_subln.reshape(1, hw))


def _outproj_kernel(a_ref, d_ref, wa_ref, wd_ref, x_ref, ga_ref, o_ref, *, per_token):
    mix = (jnp.dot(a_ref[...], wa_ref[...], preferred_element_type=F32)
           + jnp.dot(d_ref[...], wd_ref[...], preferred_element_type=F32))
    ga = ga_ref[...] if per_token else ga_ref[0]
    o_ref[...] = x_ref[...] + ga * mix


def _outproj(a_out, d_out, w_out_bf, x2, mod, *, per_token, rows_per_batch, tm, tn):
    t, d = x2.shape
    wa = a_out.shape[1]
    wd = d_out.shape[1]
    assert wa == wd
    cb = 2 * (d // tn)
    return pl.pallas_call(
        functools.partial(_outproj_kernel, per_token=per_token),
        grid=(t // tm, d // tn),
        in_specs=[pl.BlockSpec((tm, wa), lambda m, n: (m, 0)),
                  pl.BlockSpec((tm, wd), lambda m, n: (m, 0)),
                  pl.BlockSpec((wa, tn), lambda m, n: (0, n)),
                  pl.BlockSpec((wd, tn), lambda m, n: (1, n)),
                  pl.BlockSpec((tm, tn), lambda m, n: (m, n)),
                  _mod_spec(mod, per_token, tm, tn, cb, rows_per_batch)],
        out_specs=pl.BlockSpec((tm, tn), lambda m, n: (m, n)),
        out_shape=jax.ShapeDtypeStruct((t, d), F32),
        compiler_params=_params(("arbitrary", "arbitrary")),
        name="outproj",
    )(a_out, d_out, w_out_bf, w_out_bf, x2, mod)


def _top_rows(x, k, with_rank=False):
    tops = []
    rank = jnp.full(x.shape, float(k), F32) if with_rank else None
    for i in range(k):
        mx = jnp.max(x, axis=0, keepdims=True)
        tops.append(mx)
        hit = x == mx
        if with_rank:
            rank = jnp.where(hit, float(i), rank)
        x = jnp.where(hit, -jnp.inf, x)
    return (tops, rank) if with_rank else tops


def _pack_bf16_pair(lo, hi):
    def bits(x):
        b = pltpu.bitcast(x, jnp.uint32)
        return (b + jnp.uint32(0x7FFF) + ((b >> 16) & jnp.uint32(1))) >> 16
    return bits(lo) | (bits(hi) << 16)


def _peer_route_kernel(h_ref, wq_ref, keys_ref, cnt_ref, e1_ref, rank_ref, e2_ref, q_scr, s1_scr, top_scr):
    hc = pl.program_id(1)
    c = hc % 2
    half = keys_ref.shape[2]

    @pl.when(hc == 0)
    def _():
        q_scr[...] = jnp.dot(wq_ref[...], h_ref[...], preferred_element_type=F32).astype(BF16)

    q_t = q_scr[pl.ds(pl.multiple_of(hc * half, half), half), :]
    s_t = jnp.dot(keys_ref[0].astype(BF16), q_t, preferred_element_type=F32)

    @pl.when(c == 0)
    def _():
        s1_scr[...] = s_t
        top_scr[...] = jnp.concatenate(_top_rows(s_t, PEER_TOPK), axis=0)

    @pl.when(c == 1)
    def _():
        tops2, rank2 = _top_rows(s_t, PEER_TOPK, with_rank=True)
        tops = jnp.concatenate(tops2, axis=0)
        top1 = top_scr[...]
        cand = jnp.concatenate([top1[0:1, :] + tops]
                               + [top1[i:i + 1, :] + tops[:PEER_TOPK // 2] for i in range(1, PEER_TOPK)], axis=0)
        best = _top_rows(cand, PEER_TOPK)
        m = best[0]
        z = jnp.zeros_like(m)
        for bk in best:
            z = z + jnp.exp(bk - m)
        thr = best[PEER_TOPK - 1]
        s1 = s1_scr[...]
        cnt = jnp.zeros_like(s1)
        for j in range(PEER_TOPK):
            cnt = cnt + jnp.where(s1 + tops2[j] >= thr, 1.0, 0.0)
        e1 = jnp.exp(s1 - top1[0:1, :]) * (0.5 / z)
        e2 = jnp.exp(s_t - tops2[0])
        hn = s_t.shape[0] // 2
        cnt_ref[0] = _pack_bf16_pair(cnt, cnt)
        e1_ref[0] = _pack_bf16_pair(e1, e1)
        rank_ref[0] = _pack_bf16_pair(rank2[:hn], rank2[hn:])
        e2_ref[0] = _pack_bf16_pair(e2[:hn], e2[hn:])


def _peer_route(h_t, wq_t_bf, sub_keys, *, tm):
    d, t = h_t.shape
    heads, _, n_keys, half = sub_keys.shape
    keys2 = sub_keys.reshape(heads * 2, n_keys, half)
    a_spec = pl.BlockSpec((1, n_keys, tm), lambda m, hc: (hc // 2, 0, m))
    b_spec = pl.BlockSpec((1, n_keys // 2, tm), lambda m, hc: (hc // 2, 0, m))
    a_tab = jax.ShapeDtypeStruct((heads, n_keys, t), jnp.uint32)
    b_tab = jax.ShapeDtypeStruct((heads, n_keys // 2, t), jnp.uint32)
    return pl.pallas_call(
        _peer_route_kernel,
        grid=(t // tm, heads * 2),
        in_specs=[pl.BlockSpec((d, tm), lambda m, hc: (0, m)),
                  pl.BlockSpec((heads * 2 * half, d), lambda m, hc: (0, 0), pipeline_mode=pl.Buffered(1)),
                  pl.BlockSpec((1, n_keys, half), lambda m, hc: (hc, 0, 0))],
        out_specs=[a_spec, a_spec, b_spec, b_spec],
        out_shape=[a_tab, a_tab, b_tab, b_tab],
        scratch_shapes=[pltpu.VMEM((heads * 2 * half, tm), BF16), pltpu.VMEM((n_keys, tm), F32),
                        pltpu.VMEM((PEER_TOPK, tm), F32)],
        compiler_params=_params(("arbitrary", "arbitrary")),
        name="peer_route",
    )(h_t, wq_t_bf, keys2)


def _gated_gelu(x, half_gate):
    c = 0.7978845608028654
    inner = x * (c + (c * 0.044715) * (x * x))
    return (x * half_gate) * (1.0 + jnp.tanh(inner))


def _rows_bf16(row_words, n_rows):
    tile = jnp.broadcast_to(row_words, (8, row_words.shape[1]))
    return pltpu.bitcast(jnp.concatenate([tile] * (n_rows // 16), axis=0), BF16)


def _peer_ffn_kernel(h_ref, u_ref, vt_ref, cnt_ref, e1_ref, rank_ref, e2_ref, o_ref, *, n_keys):
    e = pl.program_id(1)
    te, tm = u_ref.shape[0], h_ref.shape[1]
    heads = cnt_ref.shape[0]
    a0 = e * (te // n_keys)

    @pl.when(e == 0)
    def _():
        o_ref[...] = jnp.zeros_like(o_ref)

    act = jnp.dot(u_ref[...], h_ref[...], preferred_element_type=F32)
    strip = min(tm, LANES)
    hn = n_keys // 2
    w_rows = [[None] * (tm // strip) for _ in range(2 * te // n_keys)]
    for ai in range(te // n_keys):
        cnt_a = [cnt_ref[h, pl.ds(a0 + ai, 1), :] for h in range(heads)]
        e1_a = [e1_ref[h, pl.ds(a0 + ai, 1), :] for h in range(heads)]
        for c in range(tm // strip):
            cols = slice(c * strip, (c + 1) * strip)
            gate = jnp.zeros((n_keys, strip), BF16)
            for h in range(heads):
                cnt_b = _rows_bf16(cnt_a[h][:, cols], n_keys)
                e1_b = _rows_bf16(e1_a[h][:, cols], n_keys)
                keep = pltpu.bitcast(rank_ref[h, :, cols], BF16) < cnt_b
                gate = gate + jnp.where(keep, pltpu.bitcast(e2_ref[h, :, cols], BF16) * e1_b,
                                        jnp.zeros_like(e1_b))
            words = pltpu.bitcast(gate, jnp.uint32)
            halves = (pltpu.bitcast(words << 16, F32), pltpu.bitcast(words & jnp.uint32(0xFFFF0000), F32))
            for k, g in enumerate(halves):
                rows = slice(ai * n_keys + k * hn, ai * n_keys + (k + 1) * hn)
                w_rows[2 * ai + k][c] = _gated_gelu(act[rows, cols], g).astype(BF16)
    w = jnp.concatenate([r[0] if len(r) == 1 else jnp.concatenate(r, axis=1) for r in w_rows], axis=0)
    o_ref[...] += jnp.dot(vt_ref[...], w, preferred_element_type=F32)


def _peer_ffn(h_t, u_bf, vt_bf, cnt, e1, rank, e2, *, tm, te):
    d, t = h_t.shape
    n_exp = u_bf.shape[0]
    n_blocks = n_exp // te
    heads, n_keys, _ = cnt.shape
    once = pl.Buffered(1)
    a_spec = pl.BlockSpec((heads, n_keys, tm), lambda m, e: (0, 0, m), pipeline_mode=once)
    b_spec = pl.BlockSpec((heads, n_keys // 2, tm), lambda m, e: (0, 0, m), pipeline_mode=once)
    return pl.pallas_call(
        functools.partial(_peer_ffn_kernel, n_keys=n_keys),
        grid=(t // tm, n_blocks),
        in_specs=[pl.BlockSpec((d, tm), lambda m, e: (0, m), pipeline_mode=once),
                  pl.BlockSpec((te, d), lambda m, e: (e, 0)),
                  pl.BlockSpec((d, te), lambda m, e: (0, e)),
                  a_spec, a_spec, b_spec, b_spec],
        out_specs=pl.BlockSpec((d, tm), lambda m, e: (0, m), pipeline_mode=once),
        out_shape=jax.ShapeDtypeStruct((d, t), F32),
        compiler_params=_params(("arbitrary", "arbitrary"), vmem=PEER_FFN_VMEM),
        name="peer_ffn",
    )(h_t, u_bf, vt_bf, cnt, e1, rank, e2)


def _final_kernel(x_ref, pt_ref, ga_ref, g_ref, o_ref, *, per_token, normalize):
    ga = ga_ref[...] if per_token else ga_ref[0]
    x = x_ref[...] + ga * pt_ref[...].T
    if normalize:
        x = x * lax.rsqrt(jnp.mean(x * x, axis=-1, keepdims=True) + EPS) * g_ref[...]
    o_ref[...] = x


def _final(x1, peer_t, mod, g_final, *, per_token, rows_per_batch, tm, normalize):
    t, d = x1.shape
    row = pl.BlockSpec((tm, d), lambda m: (m, 0))
    return pl.pallas_call(
        functools.partial(_final_kernel, per_token=per_token, normalize=normalize),
        grid=(t // tm,),
        in_specs=[row, pl.BlockSpec((d, tm), lambda m: (0, m)), _mod_spec(mod, per_token, tm, d, 5, rows_per_batch),
                  pl.BlockSpec((1, d), lambda m: (0, 0))],
        out_specs=row,
        out_shape=jax.ShapeDtypeStruct((t, d), F32),
        compiler_params=_params(("arbitrary",)),
        name="final",
    )(x1, peer_t, mod, g_final.reshape(1, d))


def _rope_tables(pos):
    half = HEAD_DIM // 2
    inv = ROPE_THETA ** (-jnp.arange(half, dtype=F32) / half)
    ang = pos.astype(F32)[:, None] * inv[None, :]
    cos, sin = jnp.cos(ang), jnp.sin(ang)
    return jnp.concatenate([cos, cos], axis=1), jnp.concatenate([-sin, sin], axis=1)


def _pick_tile(n, prefs):
    for p in prefs:
        if n % p == 0:
            return p
    return n


def _layer(x, mod_rows, past, layer, last_layer, w, dims):
    b, s, d = x.shape
    t = b * s
    n_heads, n_kv, n_idx, n_diff = dims
    p_len = 0 if past is None else past[0].shape[1]
    n_keys_valid = p_len + s
    n_sel = min(DSA_TOPK, n_keys_valid // 4)
    x2 = x.reshape(t, d)

    per_token = s % 256 != 0
    if per_token:
        mod = jnp.repeat(mod_rows, s, axis=0)
    else:
        mod = mod_rows.reshape(b, 1, 6 * d)
    tm_big = _pick_tile(t if per_token else s, (1024, 512, 256, 128))
    tm_mid = _pick_tile(t if per_token else s, (512, 256, 128))
    tm_small = _pick_tile(t if per_token else s, (256, 128))

    pos = p_len + jnp.arange(s)
    cos, sin = _rope_tables(pos)
    if per_token:
        cos, sin = jnp.tile(cos, (b, 1)), jnp.tile(sin, (b, 1))

    h = _norm_mod(x2, w["g_norm_mix"], mod, which=1, per_token=per_token, rows_per_batch=s, tm=tm_mid,
                  transpose=False)

    qw, kvw, dw = n_heads * HEAD_DIM, n_kv * HEAD_DIM, n_diff * 2 * HEAD_DIM
    head = functools.partial(_proj, h, w["w_head"], cos=cos, sin=sin, tm=tm_big)
    tail = functools.partial(_proj, h, w["w_tail"], cos=cos, sin=sin, tm=tm_big)
    q_scale = HEAD_DIM ** -0.5 * math.log2(math.e)
    wide = 1024
    bf_only = dict(out_f32=False, out_bf16=True, tn=wide)
    both = dict(out_f32=True, out_bf16=True)
    (q_all,) = head(col0=0, n=qw, rope_cols=wide, scale=q_scale, name="proj_q", **bf_only)
    (qi_all,) = head(col0=qw + 2 * kvw, n=n_idx * LANES, rope_cols=wide, scale=q_scale, name="proj_qi", **bf_only)
    (dq_all,) = tail(col0=0, n=dw, rope_cols=wide, scale=q_scale, name="proj_dq", **bf_only)
    k_f, k_b = head(col0=qw, n=kvw, rope_cols=512, tn=512, name="proj_k", **both)
    v_f, v_b = head(col0=qw + kvw, n=kvw, rope_cols=0, tn=512, name="proj_v", **both)
    dk_f, dk_b = tail(col0=dw, n=dw, rope_cols=wide, tn=wide, name="proj_dk", **both)
    dv_f, dv_b = tail(col0=2 * dw, n=dw, rope_cols=0, tn=wide, name="proj_dv", **both)
    kiw_f, kiw_b = head(col0=qw + 2 * kvw + n_idx * LANES, n=2 * LANES, rope_cols=LANES, out_f32=True,
                        out_bf16=True, tn=2 * LANES, name="proj_kiw")

    ki_f = kiw_f[:, :LANES]
    new_rows = (k_f.reshape(b, s, n_kv, HEAD_DIM), v_f.reshape(b, s, n_kv, HEAD_DIM), ki_f.reshape(b, s, LANES),
                dk_f.reshape(b, s, n_diff, 2, HEAD_DIM), dv_f.reshape(b, s, n_diff, 2 * HEAD_DIM))

    if past is None:
        k_len = s
        k_all, v_all, ki_all, dk_all, dv_all = k_b, v_b, kiw_b, dk_b, dv_b
    else:
        k_len = -(-n_keys_valid // KEY_CHUNK) * KEY_CHUNK

        def join(cache, new, width):
            new = new.reshape(b, s, -1)[:, :, :width]
            old = lax.optimization_barrier(cache.reshape(b, p_len, width))
            both_ = jnp.concatenate([old.astype(BF16), new], axis=1)
            both_ = jnp.pad(both_, ((0, 0), (0, k_len - n_keys_valid), (0, 0)))
            return both_.reshape(b * k_len, width)

        k_all = join(past[0], k_b, n_kv * HEAD_DIM)
        v_all = join(past[1], v_b, n_kv * HEAD_DIM)
        ki_all = join(past[2], kiw_b, LANES)
        dk_all = join(past[3], dk_b, n_diff * 2 * HEAD_DIM)
        dv_all = join(past[4], dv_b, n_diff * 2 * HEAD_DIM)

    ones = jnp.ones((v_all.shape[0], HEAD_DIM), BF16)
    vaug_all = jnp.concatenate(
        [piece for g in range(n_kv) for piece in (v_all[:, g * HEAD_DIM:(g + 1) * HEAD_DIM], ones)], axis=1)
    tq_a = _pick_tile(s, (128, 64, 32))
    a_out = _dsa_attention(q_all, qi_all, kiw_f, k_all, vaug_all, ki_all, batch=b, q_len=s, k_len=k_len,
                           n_valid_keys=n_keys_valid, p_len=p_len, tq=tq_a, n_heads=n_heads, n_kv_heads=n_kv,
                           n_idx_heads=n_idx, n_sel=n_sel)
    tq_d = _pick_tile(s, (256, 128, 64, 32))
    d_out = _diff_attention(dq_all, dk_all, dv_all, w["lams"], w["g_subln"], batch=b, q_len=s, k_len=k_len,
                            n_valid_keys=n_keys_valid, p_len=p_len, tq=tq_d, n_heads=n_diff,
                            lam_init=_lambda_init(layer))

    x1 = _outproj(a_out, d_out, w["w_out"], x2, mod, per_token=per_token, rows_per_batch=s, tm=tm_big,
                  tn=_pick_tile(d, (1024, 512)))

    h2_t = _norm_mod(x1, w["g_norm_ffn"], mod, which=2, per_token=per_token, rows_per_batch=s, tm=tm_mid,
                     transpose=True)
    cnt, e1, rank, e2 = _peer_route(h2_t, w["peer_wq_t"], w["peer_keys"], tm=tm_mid)
    peer_t = _peer_ffn(h2_t, w["peer_u"], w["peer_v_t"], cnt, e1, rank, e2, tm=tm_mid, te=1024)
    x_out = _final(x1, peer_t, mod, w["g_final"], per_token=per_token, rows_per_batch=s, tm=tm_small,
                   normalize=last_layer)
    return x_out.reshape(b, s, d), new_rows


def kernel(x_prompt, x_sample, cache_dsa_k, cache_dsa_v, cache_idx_k, cache_diff_k, cache_diff_v, c_prompt, c_sample, w_ada, b_ada, g_norm_mix, g_norm_ffn, w_in, diff_lambda_q1, diff_lambda_k1, diff_lambda_q2, diff_lambda_k2, g_diff_subln, w_out, peer_w_query, peer_sub_keys, peer_u, peer_v, g_final):
    depth = w_in.shape[0]
    bp, bs = x_prompt.shape[0], x_sample.shape[0]
    n_kv = cache_dsa_k.shape[3]
    n_diff = cache_diff_k.shape[3]
    n_heads, n_idx_heads = DSA_HEADS, IDX_HEADS
    qw, kvw, dw = n_heads * HEAD_DIM, n_kv * HEAD_DIM, n_diff * 2 * HEAD_DIM
    tail0 = qw + 2 * kvw + n_idx_heads * LANES + LANES + n_idx_heads
    assert w_in.shape[2] == tail0 + 3 * dw
    dims = (n_heads, n_kv, n_idx_heads, n_diff)

    hp, hs = x_prompt, x_sample
    rows_p, rows_s = [], []
    n_c = bp + bs
    c_pad = jnp.pad(jnp.concatenate([c_prompt, c_sample], axis=0), ((0, (-n_c) % 16), (0, 0)))
    for l in range(depth):
        mod_all = _adaln(c_pad, w_ada[l], b_ada[l])
        w_in_bf = w_in[l].astype(BF16)
        weights = {
            "w_head": w_in_bf,
            "w_tail": w_in_bf[:, tail0:],
            "w_out": w_out[l].astype(BF16),
            "peer_wq_t": peer_w_query[l].T.astype(BF16),
            "peer_keys": peer_sub_keys[l],
            "peer_u": peer_u[l].astype(BF16), "peer_v_t": peer_v[l].T.astype(BF16),
            "g_norm_mix": g_norm_mix[l], "g_norm_ffn": g_norm_ffn[l], "g_subln": g_diff_subln[l],
            "lams": (diff_lambda_q1[l], diff_lambda_k1[l], diff_lambda_q2[l], diff_lambda_k2[l]),
            "g_final": g_final,
        }
        last = l == depth - 1
        hp, rp = _layer(hp, mod_all[:bp], None, l, last, weights, dims)
        past = (cache_dsa_k[l], cache_dsa_v[l], cache_idx_k[l], cache_diff_k[l], cache_diff_v[l])
        hs, rs = _layer(hs, mod_all[bp:bp + bs], past, l, last, weights, dims)
        rows_p.append(rp)
        rows_s.append(rs)
    stack = lambda rows, i: jnp.stack([r[i] for r in rows])
    return (hp, hs) + tuple(stack(rows_p, i) for i in range(5)) + tuple(stack(rows_s, i) for i in range(5))
```

```python
import functools
import math

import jax
import jax.numpy as jnp
from jax import lax
from jax.experimental import pallas as pl
from jax.experimental.pallas import tpu as pltpu

CHUNK = 64
HEAD_DIM = 128
ROPE_THETA = 10000.0
EPS = 1e-6
DSA_HEADS = 16
IDX_HEADS = 16
DSA_TOPK = 256
PEER_TOPK = 16
LANES = 128
KEY_CHUNK = 512
NEG_BIG = -1e30
INT_MIN = -(2 ** 31)
KEY_NEG_INF = INT_MIN + 0x7FFFFF
VMEM_LIMIT = 56 * 1024 * 1024
PEER_FFN_VMEM = 62 * 1024 * 1024

BF16 = jnp.bfloat16
F32 = jnp.float32


def _params(sem, vmem=VMEM_LIMIT):
    return pltpu.CompilerParams(dimension_semantics=sem, vmem_limit_bytes=vmem)


def _lambda_init(layer):
    return 0.8 - 0.6 * math.exp(-0.3 * layer)


def _adaln_kernel(c_ref, w_ref, b_ref, o_ref):
    c = c_ref[...]
    a = (c * jax.nn.sigmoid(c)).astype(BF16)
    o_ref[...] = jnp.dot(a, w_ref[...].astype(BF16), preferred_element_type=F32) + b_ref[...]


def _adaln(c_pad, w_ada, b_ada):
    bp, d = c_pad.shape
    n = w_ada.shape[1]
    tn = 512
    return pl.pallas_call(
        _adaln_kernel,
        grid=(n // tn,),
        in_specs=[pl.BlockSpec((bp, d), lambda j: (0, 0)),
                  pl.BlockSpec((d, tn), lambda j: (0, j)),
                  pl.BlockSpec((1, tn), lambda j: (0, j))],
        out_specs=pl.BlockSpec((bp, tn), lambda j: (0, j)),
        out_shape=jax.ShapeDtypeStruct((bp, n), F32),
        compiler_params=_params(("arbitrary",)),
        name="adaln",
    )(c_pad, w_ada, b_ada.reshape(1, n))


def _norm_mod_kernel(x_ref, g_ref, sc_ref, sh_ref, o_ref, *, per_token, transpose):
    x = x_ref[...]
    y = x * lax.rsqrt(jnp.mean(x * x, axis=-1, keepdims=True) + EPS) * g_ref[...]
    sc = sc_ref[...] if per_token else sc_ref[0]
    sh = sh_ref[...] if per_token else sh_ref[0]
    h = y * (1.0 + sc) + sh
    if transpose:
        o_ref[...] = h.T.astype(o_ref.dtype)
    else:
        o_ref[...] = h.astype(o_ref.dtype)


def _mod_spec(mod, per_token, tm, tn, col_block, rows_per_batch):
    if per_token:
        return pl.BlockSpec((tm, tn), lambda m, n=0, cb=col_block: (m, cb + n))
    tiles_per_batch = rows_per_batch // tm
    return pl.BlockSpec((1, 1, tn), lambda m, n=0, cb=col_block: (m // tiles_per_batch, 0, cb + n))


def _norm_mod(x2, g, mod, *, which, per_token, rows_per_batch, tm, transpose):
    t, d = x2.shape
    sh_blk, sc_blk = (0, 1) if which == 1 else (3, 4)
    out_shape = (d, t) if transpose else (t, d)
    out_spec = pl.BlockSpec((d, tm), lambda m: (0, m)) if transpose else pl.BlockSpec((tm, d), lambda m: (m, 0))
    return pl.pallas_call(
        functools.partial(_norm_mod_kernel, per_token=per_token, transpose=transpose),
        grid=(t // tm,),
        in_specs=[pl.BlockSpec((tm, d), lambda m: (m, 0)),
                  pl.BlockSpec((1, d), lambda m: (0, 0)),
                  _mod_spec(mod, per_token, tm, d, sc_blk, rows_per_batch),
                  _mod_spec(mod, per_token, tm, d, sh_blk, rows_per_batch)],
        out_specs=out_spec,
        out_shape=jax.ShapeDtypeStruct(out_shape, BF16),
        compiler_params=_params(("arbitrary",)),
        name="norm_mod_t" if transpose else "norm_mod",
    )(x2, g.reshape(1, d), mod, mod)


def _rope_tile(acc, cos, sin, n_chunks):
    outs = []
    for j in range(n_chunks):
        xj = acc[:, j * LANES:(j + 1) * LANES]
        outs.append(xj * cos + pltpu.roll(xj, LANES // 2, axis=1) * sin)
    return outs[0] if n_chunks == 1 else jnp.concatenate(outs, axis=1)


def _proj_kernel(*refs, rope_cols, scale, out_f32, out_bf16):
    h_ref, w_ref, cos_ref, sin_ref = refs[:4]
    outs = refs[4:]
    acc = jnp.dot(h_ref[...], w_ref[...], preferred_element_type=F32)
    tn = acc.shape[1]
    if rope_cols:
        roped = _rope_tile(acc[:, :rope_cols], cos_ref[...], sin_ref[...], rope_cols // LANES)
        acc = roped if rope_cols == tn else jnp.concatenate([roped, acc[:, rope_cols:]], axis=1)
    k = 0
    if out_f32:
        outs[k][...] = acc
        k += 1
    if out_bf16:
        outs[k][...] = (acc * scale if scale != 1.0 else acc).astype(BF16)


def _proj(h, w, cos, sin, *, col0, n, rope_cols, scale=1.0, out_f32, out_bf16, tm, tn, name):
    t, d = h.shape
    assert rope_cols in (0, tn) or n == tn
    assert col0 % tn == 0 and n % tn == 0 and col0 + n <= w.shape[1]
    j0 = col0 // tn
    pos_tiles = cos.shape[0] // tm
    out_shape, out_specs = [], []
    for want, dt in ((out_f32, F32), (out_bf16, BF16)):
        if want:
            out_shape.append(jax.ShapeDtypeStruct((t, n), dt))
            out_specs.append(pl.BlockSpec((tm, tn), lambda m, j: (m, j)))
    return pl.pallas_call(
        functools.partial(_proj_kernel, rope_cols=rope_cols, scale=scale, out_f32=out_f32, out_bf16=out_bf16),
        grid=(t // tm, n // tn),
        in_specs=[pl.BlockSpec((tm, d), lambda m, j: (m, 0)),
                  pl.BlockSpec((d, tn), lambda m, j: (0, j0 + j)),
                  pl.BlockSpec((tm, LANES), lambda m, j: (m % pos_tiles, 0)),
                  pl.BlockSpec((tm, LANES), lambda m, j: (m % pos_tiles, 0))],
        out_specs=out_specs,
        out_shape=out_shape,
        compiler_params=_params(("arbitrary", "arbitrary")),
        name=name,
    )(h, w, cos, sin)


def _num_key_chunks(q_start, n_q, n_valid_keys, n_chunks_total):
    last_visible = (((q_start + n_q - 1) >> 6) + 1) * CHUNK
    last_visible = jnp.minimum(last_visible, n_valid_keys)
    return jnp.minimum((last_visible + KEY_CHUNK - 1) // KEY_CHUNK, n_chunks_total)


def _num_full_chunks(q_start, n_valid_keys):
    return jnp.minimum(((q_start >> 6) + 1) * CHUNK, n_valid_keys) // KEY_CHUNK


def _ordered_key(x):
    b = pltpu.bitcast(x, jnp.int32)
    return b ^ ((b >> 31) & jnp.int32(0x7FFFFFFF))


def _fill_invisible(x, fill, q_start, n_q, key_start, n_valid_keys, all_keys_valid):
    n_k, n_lanes = x.shape
    kpos = key_start + lax.broadcasted_iota(jnp.int32, (n_k, n_lanes), 0)
    qpos = q_start + jnp.minimum(lax.broadcasted_iota(jnp.int32, (1, n_lanes), 1), n_q - 1)
    x = jnp.where((kpos >> 6) <= (qpos >> 6), x, fill)
    return x if all_keys_valid else jnp.where(kpos < n_valid_keys, x, fill)


def _t_bf16(x, n_lanes):
    x = x.astype(F32)
    if x.shape[0] < n_lanes:
        x = jnp.concatenate([x, jnp.zeros((n_lanes - x.shape[0], x.shape[1]), F32)], axis=0)
    return x.T.astype(BF16)


def _rows_i16(words, n_word_rows):
    tile = jnp.broadcast_to(words, (8, words.shape[1]))
    return pltpu.bitcast(jnp.concatenate([tile] * (n_word_rows // 8), axis=0), jnp.int16)


def _col_partial(x, op, group=8):
    parts = [x[r * group:(r + 1) * group] for r in range(x.shape[0] // group)]
    while len(parts) > 1:
        parts = [op(parts[i], parts[i + 1]) for i in range(0, len(parts) - 1, 2)] + parts[len(parts) & ~1:]
    return parts[0]


def _col_reduce(x, op):
    return (jnp.max if op is jnp.maximum else jnp.sum)(_col_partial(x, op), axis=0, keepdims=True)


def _row_to_cols(row):
    return jnp.broadcast_to(row, (LANES, row.shape[1])).T


def _dsa_kernel(q_ref, qi_ref, wi_ref, k_ref, vaug_ref, ki_ref, o_ref,
                qit_scr, qt_scr, key_scr, hi_scr, lo_scr, acc_scr, *,
                tq, tl, n_sel, p_len, n_valid_keys, all_keys_valid, n_idx_heads, n_kv_heads, rep):
    n_chunks_total = key_scr.shape[0]
    q_start = p_len + pl.program_id(1) * tq
    n_ch = _num_key_chunks(q_start, tq, n_valid_keys, n_chunks_total)
    heads_per_dot = 4
    half_chunk = KEY_CHUNK // 2

    for h in range(n_idx_heads):
        qit_scr[:, h * tl:(h + 1) * tl] = _t_bf16(qi_ref[:, h * LANES:(h + 1) * LANES], tl)
    for g in range(n_kv_heads):
        for r in range(rep):
            hh = g * rep + r
            qt_scr[g, :, r * tl:(r + 1) * tl] = _t_bf16(q_ref[:, hh * LANES:(hh + 1) * LANES], tl)
    wi = wi_ref[:, LANES:2 * LANES] * (n_idx_heads ** -0.5)
    if tq < tl:
        wi = jnp.concatenate([wi, jnp.zeros((tl - tq, LANES), F32)], axis=0)
    wi_t = wi.T

    def idx_body(j, carry):
        k0 = pl.multiple_of(j * KEY_CHUNK, KEY_CHUNK)
        kib = ki_ref[pl.ds(k0, KEY_CHUNK), :]
        score = jnp.zeros((KEY_CHUNK, tl), F32)
        for h0 in range(0, n_idx_heads, heads_per_dot):
            lg = jnp.dot(kib, qit_scr[:, h0 * tl:(h0 + heads_per_dot) * tl], preferred_element_type=F32)
            for h in range(h0, h0 + heads_per_dot):
                score = score + jnp.maximum(lg[:, (h - h0) * tl:(h - h0 + 1) * tl], 0.0) * wi_t[h:h + 1, :]
        score = _fill_invisible(score, -jnp.inf, q_start, tq, k0, n_valid_keys, all_keys_valid)
        key = _ordered_key(score)
        key_scr[j] = key
        hi16 = (key >> 16) & 0xFFFF
        lo16 = (key & 0xFFFF) ^ 0x8000
        hi_scr[j] = hi16[:half_chunk] | (hi16[half_chunk:] << 16)
        lo_scr[j] = lo16[:half_chunk] | (lo16[half_chunk:] << 16)
        return carry

    lax.fori_loop(0, n_ch, idx_body, 0)

    def count16(scr, pattern, strict):
        c = _rows_i16(pattern | (pattern << 16), half_chunk)

        def body(j, cnt):
            x = pltpu.bitcast(scr[j], jnp.int16)
            ones = jnp.where((x > c) if strict else (x >= c), jnp.int16(1), jnp.int16(0))
            return cnt + _col_partial(ones, jnp.add, group=16)

        cnt = pltpu.bitcast(lax.fori_loop(0, n_ch, body, jnp.zeros((16, tl), jnp.int16)), jnp.int32)
        return jnp.sum((cnt & 0xFFFF) + ((cnt >> 16) & 0xFFFF), axis=0, keepdims=True)

    def bisect16(scr, need):
        def bit_body(b, t):
            cand = t | lax.shift_left(jnp.int32(1), 15 - b)
            return jnp.where(count16(scr, cand ^ 0x8000, False) >= need, cand, t)
        return lax.fori_loop(0, 16, bit_body, jnp.zeros((1, tl), jnp.int32))

    hi_pat = bisect16(hi_scr, n_sel) ^ 0x8000
    need = n_sel - count16(hi_scr, hi_pat, True)

    def keep_bucket(j, carry):
        hi = pltpu.bitcast(hi_scr[j], jnp.int16)
        lo = pltpu.bitcast(lo_scr[j], jnp.int16)
        same = hi == _rows_i16(hi_pat | (hi_pat << 16), half_chunk)
        lo_scr[j] = pltpu.bitcast(jnp.where(same, lo, jnp.int16(-2 ** 15)), jnp.int32)
        return carry

    lax.fori_loop(0, n_ch, keep_bucket, 0)
    lo_u = bisect16(lo_scr, need)
    thr_raw = (((hi_pat << 16) >> 16) << 16) | lo_u
    thr = jnp.maximum(thr_raw, KEY_NEG_INF + 1)

    n_above = n_sel - need
    n_ge = n_above + count16(lo_scr, lo_u ^ 0x8000, False)
    surplus = jnp.where(thr_raw > KEY_NEG_INF, n_ge - n_sel, 0)

    @pl.when(jnp.max(surplus) > 0)
    def _():
        take = n_sel - (n_above + count16(lo_scr, lo_u ^ 0x8000, True))

        def positions(j):
            return j * KEY_CHUNK + lax.broadcasted_iota(jnp.int32, (KEY_CHUNK, tl), 0)

        def tied_before(limit):
            def body(j, cnt):
                ones = jnp.where(key_scr[j] == thr, jnp.where(positions(j) < limit, 1, 0), 0)
                return cnt + _col_partial(ones, jnp.add)
            cnt = lax.fori_loop(0, n_ch, body, jnp.zeros((8, tl), jnp.int32))
            return jnp.sum(cnt, axis=0, keepdims=True)

        n_bits = (n_chunks_total * KEY_CHUNK).bit_length()

        def bit_body(b, t):
            cand = t | lax.shift_left(jnp.int32(1), n_bits - 1 - b)
            return jnp.where(tied_before(cand) < take, cand, t)

        last = lax.fori_loop(0, n_bits, bit_body, jnp.zeros((1, tl), jnp.int32))

        def drop_body(j, carry):
            key = key_scr[j]
            key_scr[j] = jnp.where(key == thr, jnp.where(positions(j) > last, KEY_NEG_INF, key), key)
            return carry

        lax.fori_loop(0, n_ch, drop_body, 0)

    acc_scr[...] = jnp.zeros_like(acc_scr)

    def att_body(j, ms):
        k0 = pl.multiple_of(j * KEY_CHUNK, KEY_CHUNK)
        bias = jnp.where(key_scr[j] >= thr, 0.0, NEG_BIG)
        bias = jnp.concatenate([bias] * rep, axis=1)
        new = []
        for g in range(n_kv_heads):
            kb = k_ref[pl.ds(k0, KEY_CHUNK), g * LANES:(g + 1) * LANES]
            s = jnp.dot(kb, qt_scr[g], preferred_element_type=F32) + bias
            m_new = jnp.maximum(ms[g], _col_reduce(s, jnp.maximum))
            p = jnp.exp2(s - m_new).astype(BF16)
            va = vaug_ref[pl.ds(k0, KEY_CHUNK), g * 2 * LANES:(g + 1) * 2 * LANES]
            alpha = _row_to_cols(jnp.exp2(ms[g] - m_new))
            acc_scr[g] = (acc_scr[g] * jnp.concatenate([alpha, alpha], axis=1)
                          + lax.dot_general(p, va, (((0,), (0,)), ((), ())), preferred_element_type=F32))
            new.append(m_new)
        return tuple(new)

    lax.fori_loop(0, n_ch, att_body, tuple(jnp.full((1, rep * tl), NEG_BIG, F32) for _ in range(n_kv_heads)))

    for g in range(n_kv_heads):
        acc = acc_scr[g]
        o = acc[:, :LANES] / acc[:, LANES:]
        for r in range(rep):
            hh = g * rep + r
            o_ref[:, hh * LANES:(hh + 1) * LANES] = o[r * tl:r * tl + tq].astype(o_ref.dtype)


def _dsa_attention(q_all, qi_all, kiw_f32, k_bf, vaug_bf, ki_bf, *, batch, q_len, k_len, n_valid_keys, p_len, tq,
                   n_heads, n_kv_heads, n_idx_heads, n_sel):
    width = n_heads * HEAD_DIM
    assert qi_all.shape[1] == n_idx_heads * LANES
    nq = q_len // tq
    tl = max(tq, LANES)
    n_chunks = k_len // KEY_CHUNK
    rep = n_heads // n_kv_heads
    kv_w = n_kv_heads * HEAD_DIM
    return pl.pallas_call(
        functools.partial(_dsa_kernel, tq=tq, tl=tl, n_sel=n_sel, p_len=p_len, n_valid_keys=n_valid_keys,
                          all_keys_valid=n_valid_keys == k_len,
                          n_idx_heads=n_idx_heads, n_kv_heads=n_kv_heads, rep=rep),
        grid=(batch, nq),
        in_specs=[pl.BlockSpec((tq, width), lambda b, i: (b * nq + i, 0)),
                  pl.BlockSpec((tq, n_idx_heads * LANES), lambda b, i: (b * nq + i, 0)),
                  pl.BlockSpec((tq, 2 * LANES), lambda b, i: (b * nq + i, 0)),
                  pl.BlockSpec((k_len, kv_w), lambda b, i: (b, 0)),
                  pl.BlockSpec((k_len, 2 * kv_w), lambda b, i: (b, 0)),
                  pl.BlockSpec((k_len, LANES), lambda b, i: (b, 0))],
        out_specs=pl.BlockSpec((tq, width), lambda b, i: (b * nq + i, 0)),
        out_shape=jax.ShapeDtypeStruct((batch * q_len, width), BF16),
        scratch_shapes=[pltpu.VMEM((HEAD_DIM, n_idx_heads * tl), BF16),
                        pltpu.VMEM((n_kv_heads, HEAD_DIM, rep * tl), BF16),
                        pltpu.VMEM((n_chunks, KEY_CHUNK, tl), jnp.int32),
                        pltpu.VMEM((n_chunks, KEY_CHUNK // 2, tl), jnp.int32),
                        pltpu.VMEM((n_chunks, KEY_CHUNK // 2, tl), jnp.int32),
                        pltpu.VMEM((n_kv_heads, rep * tl, 2 * LANES), F32)],
        compiler_params=_params(("arbitrary", "arbitrary")),
        name="dsa_attention",
    )(q_all, qi_all, kiw_f32, k_bf, vaug_bf, ki_bf)


def _diff_kernel(dq_ref, dk_ref, dv_ref, *rest, tq, tl, hp, n_chunks_total, n_tail, p_len, n_valid_keys,
                 all_keys_valid, lam_init):
    if n_tail:
        dkt_ref, dvt_ref = rest[:2]
        rest = rest[2:]
    lq1_ref, lk1_ref, lq2_ref, lk2_ref, g_ref, o_ref, acc_scr = rest
    q_start = p_len + pl.program_id(2) * tq
    n_ch = _num_key_chunks(q_start, tq, n_valid_keys, n_chunks_total)
    hw = 2 * HEAD_DIM
    n_maps = 2 * hp

    lam = (jnp.exp(jnp.sum(lq1_ref[...] * lk1_ref[...], axis=1, keepdims=True))
           - jnp.exp(jnp.sum(lq2_ref[...] * lk2_ref[...], axis=1, keepdims=True)) + lam_init)
    qt = [_t_bf16(dq_ref[:, c * LANES:(c + 1) * LANES], tl) for c in range(n_maps)]

    acc_scr[...] = jnp.zeros_like(acc_scr)

    def update(carry, key_blocks, value_blocks, key_start, n_valid, all_valid, masked):
        new = []
        for c in range(n_maps):
            m, l = carry[2 * c:2 * c + 2]
            s = jnp.dot(key_blocks[c], qt[c], preferred_element_type=F32)
            if masked:
                s = _fill_invisible(s, NEG_BIG, q_start, tq, key_start, n_valid, all_valid)
            m_new = jnp.maximum(m, _col_reduce(s, jnp.maximum))
            p = jnp.exp2(s - m_new)
            alpha = jnp.exp2(m - m_new)
            alpha_c = _row_to_cols(alpha)
            acc_scr[c] = (acc_scr[c] * jnp.concatenate([alpha_c, alpha_c], axis=1)
                          + lax.dot_general(p.astype(BF16), value_blocks[c // 2], (((0,), (0,)), ((), ())),
                                            preferred_element_type=F32))
            new += [m_new, alpha * l + _col_reduce(p, jnp.add)]
        return tuple(new)

    def att_body(j, carry, masked):
        k0 = pl.multiple_of(j * KEY_CHUNK, KEY_CHUNK)
        keys = [dk_ref[pl.ds(k0, KEY_CHUNK), c * LANES:(c + 1) * LANES] for c in range(n_maps)]
        values = [dv_ref[pl.ds(k0, KEY_CHUNK), h * hw:(h + 1) * hw] for h in range(hp)]
        return update(carry, keys, values, k0, n_valid_keys, all_keys_valid, masked)

    one = (jnp.full((1, tl), NEG_BIG, F32), jnp.zeros((1, tl), F32))
    n_full = jnp.minimum(_num_full_chunks(q_start, n_valid_keys), n_ch)
    carry = lax.fori_loop(0, n_full, functools.partial(att_body, masked=False), one * n_maps)
    carry = lax.fori_loop(n_full, n_ch, functools.partial(att_body, masked=True), carry)
    if n_tail:
        k_main = n_chunks_total * KEY_CHUNK
        keys = [dkt_ref[:, c * LANES:(c + 1) * LANES] for c in range(n_maps)]
        values = [dvt_ref[:, h * hw:(h + 1) * hw] for h in range(hp)]
        carry = update(carry, keys, values, k_main, k_main + n_tail, True, True)

    def normalised(c):
        l_cols = _row_to_cols(carry[2 * c + 1])
        return acc_scr[c] / jnp.concatenate([l_cols, l_cols], axis=1)

    for h in range(hp):
        o = (normalised(2 * h) - lam * normalised(2 * h + 1))[:tq]
        o = o * lax.rsqrt(jnp.mean(o * o, axis=-1, keepdims=True) + EPS) * g_ref[...]
        o_ref[:, h * hw:(h + 1) * hw] = (o * (1.0 - lam_init)).astype(o_ref.dtype)


def _diff_attention(dq_all, dk_bf, dv_bf, lams, g_subln, *, batch, q_len, k_len, n_valid_keys, p_len, tq,
                    n_heads, lam_init, tail=None):
    hw = 2 * HEAD_DIM
    hp = 2 if n_heads % 2 == 0 else 1
    width = n_heads * hw
    assert dq_all.shape[1] == width
    nq = q_len // tq
    tl = max(tq, LANES)
    n_chunks = k_len // KEY_CHUNK
    n_tail = 0 if tail is None else q_len
    assert tail is None or (n_valid_keys == k_len and nq == 1)
    vec = pl.BlockSpec((1, HEAD_DIM), lambda b, h, i: (0, 0))
    kv_spec = pl.BlockSpec((k_len, hp * hw), lambda b, h, i: (b, h))
    tail_specs = [] if tail is None else [pl.BlockSpec((n_tail, hp * hw), lambda b, h, i: (b, h))] * 2
    return pl.pallas_call(
        functools.partial(_diff_kernel, tq=tq, tl=tl, hp=hp, n_chunks_total=n_chunks, n_tail=n_tail, p_len=p_len,
                          n_valid_keys=n_valid_keys, all_keys_valid=n_valid_keys == k_len, lam_init=lam_init),
        grid=(batch, n_heads // hp, nq),
        in_specs=[pl.BlockSpec((tq, hp * hw), lambda b, h, i: (b * nq + i, h)), kv_spec, kv_spec] + tail_specs
                 + [vec, vec, vec, vec, pl.BlockSpec((1, hw), lambda b, h, i: (0, 0))],
        out_specs=pl.BlockSpec((tq, hp * hw), lambda b, h, i: (b * nq + i, h)),
        out_shape=jax.ShapeDtypeStruct((batch * q_len, width), BF16),
        scratch_shapes=[pltpu.VMEM((2 * hp, tl, hw), F32)],
        compiler_params=_params(("arbitrary", "arbitrary", "arbitrary")),
        name="diff_attention",
    )(dq_all, dk_bf, dv_bf, *(tail or ()), *[v.reshape(1, HEAD_DIM) for v in lams], g_subln.reshape(1, hw))


def _outproj_kernel(a_ref, d_ref, wa_ref, wd_ref, x_ref, ga_ref, o_ref, *, per_token):
    mix = (jnp.dot(a_ref[...], wa_ref[...], preferred_element_type=F32)
           + jnp.dot(d_ref[...], wd_ref[...], preferred_element_type=F32))
    ga = ga_ref[...] if per_token else ga_ref[0]
    o_ref[...] = x_ref[...] + ga * mix


def _outproj(a_out, d_out, w_out_bf, x2, mod, *, per_token, rows_per_batch, tm, tn):
    t, d = x2.shape
    wa = a_out.shape[1]
    wd = d_out.shape[1]
    assert wa == wd
    cb = 2 * (d // tn)
    return pl.pallas_call(
        functools.partial(_outproj_kernel, per_token=per_token),
        grid=(t // tm, d // tn),
        in_specs=[pl.BlockSpec((tm, wa), lambda m, n: (m, 0)),
                  pl.BlockSpec((tm, wd), lambda m, n: (m, 0)),
                  pl.BlockSpec((wa, tn), lambda m, n: (0, n)),
                  pl.BlockSpec((wd, tn), lambda m, n: (1, n)),
                  pl.BlockSpec((tm, tn), lambda m, n: (m, n)),
                  _mod_spec(mod, per_token, tm, tn, cb, rows_per_batch)],
        out_specs=pl.BlockSpec((tm, tn), lambda m, n: (m, n)),
        out_shape=jax.ShapeDtypeStruct((t, d), F32),
        compiler_params=_params(("arbitrary", "arbitrary")),
        name="outproj",
    )(a_out, d_out, w_out_bf, w_out_bf, x2, mod)


def _top_rows(x, k, with_rank=False):
    tops = []
    rank = jnp.full(x.shape, float(k), F32) if with_rank else None
    for i in range(k):
        mx = jnp.max(x, axis=0, keepdims=True)
        tops.append(mx)
        hit = x == mx
        if with_rank:
            rank = jnp.where(hit, float(i), rank)
        x = jnp.where(hit, -jnp.inf, x)
    return (tops, rank) if with_rank else tops


def _pack_bf16_pair(lo, hi):
    def bits(x):
        b = pltpu.bitcast(x, jnp.uint32)
        return (b + jnp.uint32(0x7FFF) + ((b >> 16) & jnp.uint32(1))) >> 16
    return bits(lo) | (bits(hi) << 16)


def _peer_route_kernel(h_ref, wq_ref, keys_ref, cnt_ref, e1_ref, rank_ref, e2_ref, q_scr, s1_scr, top_scr):
    hc = pl.program_id(1)
    c = hc % 2
    half = keys_ref.shape[2]

    @pl.when(hc == 0)
    def _():
        q_scr[...] = jnp.dot(wq_ref[...], h_ref[...], preferred_element_type=F32).astype(BF16)

    q_t = q_scr[pl.ds(pl.multiple_of(hc * half, half), half), :]
    s_t = jnp.dot(keys_ref[0].astype(BF16), q_t, preferred_element_type=F32)

    @pl.when(c == 0)
    def _():
        s1_scr[...] = s_t
        top_scr[...] = jnp.concatenate(_top_rows(s_t, PEER_TOPK), axis=0)

    @pl.when(c == 1)
    def _():
        tops2, rank2 = _top_rows(s_t, PEER_TOPK, with_rank=True)
        tops = jnp.concatenate(tops2, axis=0)
        top1 = top_scr[...]
        cand = jnp.concatenate([top1[0:1, :] + tops]
                               + [top1[i:i + 1, :] + tops[:PEER_TOPK // 2] for i in range(1, PEER_TOPK)], axis=0)
        best = _top_rows(cand, PEER_TOPK)
        m = best[0]
        z = jnp.zeros_like(m)
        for bk in best:
            z = z + jnp.exp(bk - m)
        thr = best[PEER_TOPK - 1]
        s1 = s1_scr[...]
        cnt = jnp.zeros_like(s1)
        for j in range(PEER_TOPK):
            cnt = cnt + jnp.where(s1 + tops2[j] >= thr, 1.0, 0.0)
        e1 = jnp.exp(s1 - top1[0:1, :]) * (0.5 / z)
        e2 = jnp.exp(s_t - tops2[0])
        hn = s_t.shape[0] // 2
        cnt_ref[0] = _pack_bf16_pair(cnt, cnt)
        e1_ref[0] = _pack_bf16_pair(e1, e1)
        rank_ref[0] = _pack_bf16_pair(rank2[:hn], rank2[hn:])
        e2_ref[0] = _pack_bf16_pair(e2[:hn], e2[hn:])


def _peer_route(h_t, wq_t_bf, sub_keys, *, tm):
    d, t = h_t.shape
    heads, _, n_keys, half = sub_keys.shape
    keys2 = sub_keys.reshape(heads * 2, n_keys, half)
    a_spec = pl.BlockSpec((1, n_keys, tm), lambda m, hc: (hc // 2, 0, m))
    b_spec = pl.BlockSpec((1, n_keys // 2, tm), lambda m, hc: (hc // 2, 0, m))
    a_tab = jax.ShapeDtypeStruct((heads, n_keys, t), jnp.uint32)
    b_tab = jax.ShapeDtypeStruct((heads, n_keys // 2, t), jnp.uint32)
    return pl.pallas_call(
        _peer_route_kernel,
        grid=(t // tm, heads * 2),
        in_specs=[pl.BlockSpec((d, tm), lambda m, hc: (0, m)),
                  pl.BlockSpec((heads * 2 * half, d), lambda m, hc: (0, 0), pipeline_mode=pl.Buffered(1)),
                  pl.BlockSpec((1, n_keys, half), lambda m, hc: (hc, 0, 0))],
        out_specs=[a_spec, a_spec, b_spec, b_spec],
        out_shape=[a_tab, a_tab, b_tab, b_tab],
        scratch_shapes=[pltpu.VMEM((heads * 2 * half, tm), BF16), pltpu.VMEM((n_keys, tm), F32),
                        pltpu.VMEM((PEER_TOPK, tm), F32)],
        compiler_params=_params(("arbitrary", "arbitrary")),
        name="peer_route",
    )(h_t, wq_t_bf, keys2)


def _gated_gelu(x, half_gate):
    c = 0.7978845608028654
    inner = x * (c + (c * 0.044715) * (x * x))
    return (x * half_gate) * (1.0 + jnp.tanh(inner))


def _rows_bf16(row_words, n_rows):
    tile = jnp.broadcast_to(row_words, (8, row_words.shape[1]))
    return pltpu.bitcast(jnp.concatenate([tile] * (n_rows // 16), axis=0), BF16)


def _peer_ffn_kernel(h_ref, u_ref, vt_ref, cnt_ref, e1_ref, rank_ref, e2_ref, o_ref, *, n_keys):
    e = pl.program_id(1)
    te, tm = u_ref.shape[0], h_ref.shape[1]
    heads = cnt_ref.shape[0]
    a0 = e * (te // n_keys)

    @pl.when(e == 0)
    def _():
        o_ref[...] = jnp.zeros_like(o_ref)

    act = jnp.dot(u_ref[...], h_ref[...], preferred_element_type=F32)
    strip = min(tm, LANES)
    hn = n_keys // 2
    w_rows = [[None] * (tm // strip) for _ in range(2 * te // n_keys)]
    for ai in range(te // n_keys):
        cnt_a = [cnt_ref[h, pl.ds(a0 + ai, 1), :] for h in range(heads)]
        e1_a = [e1_ref[h, pl.ds(a0 + ai, 1), :] for h in range(heads)]
        for c in range(tm // strip):
            cols = slice(c * strip, (c + 1) * strip)
            gate = jnp.zeros((n_keys, strip), BF16)
            for h in range(heads):
                cnt_b = _rows_bf16(cnt_a[h][:, cols], n_keys)
                e1_b = _rows_bf16(e1_a[h][:, cols], n_keys)
                keep = pltpu.bitcast(rank_ref[h, :, cols], BF16) < cnt_b
                gate = gate + jnp.where(keep, pltpu.bitcast(e2_ref[h, :, cols], BF16) * e1_b,
                                        jnp.zeros_like(e1_b))
            words = pltpu.bitcast(gate, jnp.uint32)
            halves = (pltpu.bitcast(words << 16, F32), pltpu.bitcast(words & jnp.uint32(0xFFFF0000), F32))
            for k, g in enumerate(halves):
                rows = slice(ai * n_keys + k * hn, ai * n_keys + (k + 1) * hn)
                w_rows[2 * ai + k][c] = _gated_gelu(act[rows, cols], g).astype(BF16)
    w = jnp.concatenate([r[0] if len(r) == 1 else jnp.concatenate(r, axis=1) for r in w_rows], axis=0)
    o_ref[...] += jnp.dot(vt_ref[...], w, preferred_element_type=F32)


def _peer_ffn(h_t, u_bf, vt_bf, cnt, e1, rank, e2, *, tm, te):
    d, t = h_t.shape
    n_exp = u_bf.shape[0]
    n_blocks = n_exp // te
    heads, n_keys, _ = cnt.shape
    once = pl.Buffered(1)
    a_spec = pl.BlockSpec((heads, n_keys, tm), lambda m, e: (0, 0, m), pipeline_mode=once)
    b_spec = pl.BlockSpec((heads, n_keys // 2, tm), lambda m, e: (0, 0, m), pipeline_mode=once)
    return pl.pallas_call(
        functools.partial(_peer_ffn_kernel, n_keys=n_keys),
        grid=(t // tm, n_blocks),
        in_specs=[pl.BlockSpec((d, tm), lambda m, e: (0, m), pipeline_mode=once),
                  pl.BlockSpec((te, d), lambda m, e: (e, 0)),
                  pl.BlockSpec((d, te), lambda m, e: (0, e)),
                  a_spec, a_spec, b_spec, b_spec],
        out_specs=pl.BlockSpec((d, tm), lambda m, e: (0, m), pipeline_mode=once),
        out_shape=jax.ShapeDtypeStruct((d, t), F32),
        compiler_params=_params(("arbitrary", "arbitrary"), vmem=PEER_FFN_VMEM),
        name="peer_ffn",
    )(h_t, u_bf, vt_bf, cnt, e1, rank, e2)


def _final_kernel(x_ref, pt_ref, ga_ref, g_ref, o_ref, *, per_token, normalize):
    ga = ga_ref[...] if per_token else ga_ref[0]
    x = x_ref[...] + ga * pt_ref[...].T
    if normalize:
        x = x * lax.rsqrt(jnp.mean(x * x, axis=-1, keepdims=True) + EPS) * g_ref[...]
    o_ref[...] = x


def _final(x1, peer_t, mod, g_final, *, per_token, rows_per_batch, tm, normalize):
    t, d = x1.shape
    row = pl.BlockSpec((tm, d), lambda m: (m, 0))
    return pl.pallas_call(
        functools.partial(_final_kernel, per_token=per_token, normalize=normalize),
        grid=(t // tm,),
        in_specs=[row, pl.BlockSpec((d, tm), lambda m: (0, m)), _mod_spec(mod, per_token, tm, d, 5, rows_per_batch),
                  pl.BlockSpec((1, d), lambda m: (0, 0))],
        out_specs=row,
        out_shape=jax.ShapeDtypeStruct((t, d), F32),
        compiler_params=_params(("arbitrary",)),
        name="final",
    )(x1, peer_t, mod, g_final.reshape(1, d))


def _rope_tables(pos):
    half = HEAD_DIM // 2
    inv = ROPE_THETA ** (-jnp.arange(half, dtype=F32) / half)
    ang = pos.astype(F32)[:, None] * inv[None, :]
    cos, sin = jnp.cos(ang), jnp.sin(ang)
    return jnp.concatenate([cos, cos], axis=1), jnp.concatenate([-sin, sin], axis=1)


def _pick_tile(n, prefs):
    for p in prefs:
        if n % p == 0:
            return p
    return n


def _layer(x, mod_rows, past, layer, last_layer, w, dims):
    b, s, d = x.shape
    t = b * s
    n_heads, n_kv, n_idx, n_diff = dims
    p_len = 0 if past is None else past[0].shape[1]
    n_keys_valid = p_len + s
    n_sel = min(DSA_TOPK, n_keys_valid // 4)
    x2 = x.reshape(t, d)

    per_token = s % 256 != 0
    if per_token:
        mod = jnp.repeat(mod_rows, s, axis=0)
    else:
        mod = mod_rows.reshape(b, 1, 6 * d)
    tm_big = _pick_tile(t if per_token else s, (1024, 512, 256, 128))
    tm_mid = _pick_tile(t if per_token else s, (512, 256, 128))
    tm_small = _pick_tile(t if per_token else s, (256, 128))

    pos = p_len + jnp.arange(s)
    cos, sin = _rope_tables(pos)
    if per_token:
        cos, sin = jnp.tile(cos, (b, 1)), jnp.tile(sin, (b, 1))

    h = _norm_mod(x2, w["g_norm_mix"], mod, which=1, per_token=per_token, rows_per_batch=s, tm=tm_mid,
                  transpose=False)

    qw, kvw, dw = n_heads * HEAD_DIM, n_kv * HEAD_DIM, n_diff * 2 * HEAD_DIM
    head = functools.partial(_proj, h, w["w_head"], cos=cos, sin=sin, tm=tm_big)
    tail = functools.partial(_proj, h, w["w_tail"], cos=cos, sin=sin, tm=tm_big)
    q_scale = HEAD_DIM ** -0.5 * math.log2(math.e)
    wide = 1024
    bf_only = dict(out_f32=False, out_bf16=True, tn=wide)
    both = dict(out_f32=True, out_bf16=True)
    (q_all,) = head(col0=0, n=qw, rope_cols=wide, scale=q_scale, name="proj_q", **bf_only)
    (qi_all,) = head(col0=qw + 2 * kvw, n=n_idx * LANES, rope_cols=wide, scale=q_scale, name="proj_qi", **bf_only)
    (dq_all,) = tail(col0=0, n=dw, rope_cols=wide, scale=q_scale, name="proj_dq", **bf_only)
    k_f, k_b = head(col0=qw, n=kvw, rope_cols=512, tn=512, name="proj_k", **both)
    v_f, v_b = head(col0=qw + kvw, n=kvw, rope_cols=0, tn=512, name="proj_v", **both)
    dk_f, dk_b = tail(col0=dw, n=dw, rope_cols=wide, tn=wide, name="proj_dk", **both)
    dv_f, dv_b = tail(col0=2 * dw, n=dw, rope_cols=0, tn=wide, name="proj_dv", **both)
    kiw_f, kiw_b = head(col0=qw + 2 * kvw + n_idx * LANES, n=2 * LANES, rope_cols=LANES, out_f32=True,
                        out_bf16=True, tn=2 * LANES, name="proj_kiw")

    ki_f = kiw_f[:, :LANES]
    new_rows = (k_f.reshape(b, s, n_kv, HEAD_DIM), v_f.reshape(b, s, n_kv, HEAD_DIM), ki_f.reshape(b, s, LANES),
                dk_f.reshape(b, s, n_diff, 2, HEAD_DIM), dv_f.reshape(b, s, n_diff, 2 * HEAD_DIM))

    diff_keys = None
    if past is None:
        k_len = s
        k_all, v_all, ki_all, dk_all, dv_all = k_b, v_b, kiw_b, dk_b, dv_b
    else:
        k_len = -(-n_keys_valid // KEY_CHUNK) * KEY_CHUNK

        def join(cache, new, width):
            new = new.reshape(b, s, -1)[:, :, :width]
            old = lax.optimization_barrier(cache.reshape(b, p_len, width))
            both_ = jnp.concatenate([old.astype(BF16), new], axis=1)
            both_ = jnp.pad(both_, ((0, 0), (0, k_len - n_keys_valid), (0, 0)))
            return both_.reshape(b * k_len, width)

        k_all = join(past[0], k_b, n_kv * HEAD_DIM)
        v_all = join(past[1], v_b, n_kv * HEAD_DIM)
        ki_all = join(past[2], kiw_b, LANES)
        if p_len % KEY_CHUNK == 0 and s % 16 == 0 and s <= 256:
            def flat(cache):
                return lax.optimization_barrier(cache.reshape(b, p_len, dw)).astype(BF16).reshape(b * p_len, dw)
            diff_keys = dict(dk=flat(past[3]), dv=flat(past[4]), k_len=p_len, n_valid_keys=p_len,
                             tail=(dk_b, dv_b))
            dk_all = dv_all = None
        else:
            dk_all = join(past[3], dk_b, dw)
            dv_all = join(past[4], dv_b, dw)

    ones = jnp.ones((v_all.shape[0], HEAD_DIM), BF16)
    vaug_all = jnp.concatenate(
        [piece for g in range(n_kv) for piece in (v_all[:, g * HEAD_DIM:(g + 1) * HEAD_DIM], ones)], axis=1)
    tq_a = _pick_tile(s, (128, 64, 32))
    a_out = _dsa_attention(q_all, qi_all, kiw_f, k_all, vaug_all, ki_all, batch=b, q_len=s, k_len=k_len,
                           n_valid_keys=n_keys_valid, p_len=p_len, tq=tq_a, n_heads=n_heads, n_kv_heads=n_kv,
                           n_idx_heads=n_idx, n_sel=n_sel)
    tq_d = _pick_tile(s, (256, 128, 64, 32))
    if diff_keys is None:
        diff_keys = dict(dk=dk_all, dv=dv_all, k_len=k_len, n_valid_keys=n_keys_valid, tail=None)
    d_out = _diff_attention(dq_all, diff_keys["dk"], diff_keys["dv"], w["lams"], w["g_subln"], batch=b, q_len=s,
                            k_len=diff_keys["k_len"], n_valid_keys=diff_keys["n_valid_keys"], p_len=p_len,
                            tq=tq_d, n_heads=n_diff, lam_init=_lambda_init(layer), tail=diff_keys["tail"])

    x1 = _outproj(a_out, d_out, w["w_out"], x2, mod, per_token=per_token, rows_per_batch=s, tm=tm_big,
                  tn=_pick_tile(d, (1024, 512)))

    h2_t = _norm_mod(x1, w["g_norm_ffn"], mod, which=2, per_token=per_token, rows_per_batch=s, tm=tm_mid,
                     transpose=True)
    cnt, e1, rank, e2 = _peer_route(h2_t, w["peer_wq_t"], w["peer_keys"], tm=tm_mid)
    peer_t = _peer_ffn(h2_t, w["peer_u"], w["peer_v_t"], cnt, e1, rank, e2, tm=tm_mid, te=1024)
    x_out = _final(x1, peer_t, mod, w["g_final"], per_token=per_token, rows_per_batch=s, tm=tm_small,
                   normalize=last_layer)
    return x_out.reshape(b, s, d), new_rows


def kernel(x_prompt, x_sample, cache_dsa_k, cache_dsa_v, cache_idx_k, cache_diff_k, cache_diff_v, c_prompt, c_sample, w_ada, b_ada, g_norm_mix, g_norm_ffn, w_in, diff_lambda_q1, diff_lambda_k1, diff_lambda_q2, diff_lambda_k2, g_diff_subln, w_out, peer_w_query, peer_sub_keys, peer_u, peer_v, g_final):
    depth = w_in.shape[0]
    bp, bs = x_prompt.shape[0], x_sample.shape[0]
    n_kv = cache_dsa_k.shape[3]
    n_diff = cache_diff_k.shape[3]
    n_heads, n_idx_heads = DSA_HEADS, IDX_HEADS
    qw, kvw, dw = n_heads * HEAD_DIM, n_kv * HEAD_DIM, n_diff * 2 * HEAD_DIM
    tail0 = qw + 2 * kvw + n_idx_heads * LANES + LANES + n_idx_heads
    assert w_in.shape[2] == tail0 + 3 * dw
    dims = (n_heads, n_kv, n_idx_heads, n_diff)

    hp, hs = x_prompt, x_sample
    rows_p, rows_s = [], []
    n_c = bp + bs
    c_pad = jnp.pad(jnp.concatenate([c_prompt, c_sample], axis=0), ((0, (-n_c) % 16), (0, 0)))
    for l in range(depth):
        mod_all = _adaln(c_pad, w_ada[l], b_ada[l])
        w_in_bf = w_in[l].astype(BF16)
        weights = {
            "w_head": w_in_bf,
            "w_tail": w_in_bf[:, tail0:],
            "w_out": w_out[l].astype(BF16),
            "peer_wq_t": peer_w_query[l].T.astype(BF16),
            "peer_keys": peer_sub_keys[l],
            "peer_u": peer_u[l].astype(BF16), "peer_v_t": peer_v[l].T.astype(BF16),
            "g_norm_mix": g_norm_mix[l], "g_norm_ffn": g_norm_ffn[l], "g_subln": g_diff_subln[l],
            "lams": (diff_lambda_q1[l], diff_lambda_k1[l], diff_lambda_q2[l], diff_lambda_k2[l]),
            "g_final": g_final,
        }
        last = l == depth - 1
        hp, rp = _layer(hp, mod_all[:bp], None, l, last, weights, dims)
        past = (cache_dsa_k[l], cache_dsa_v[l], cache_idx_k[l], cache_diff_k[l], cache_diff_v[l])
        hs, rs = _layer(hs, mod_all[bp:bp + bs], past, l, last, weights, dims)
        rows_p.append(rp)
        rows_s.append(rs)
    stack = lambda rows, i: jnp.stack([r[i] for r in rows])
    return (hp, hs) + tuple(stack(rows_p, i) for i in range(5)) + tuple(stack(rows_s, i) for i in range(5))
```

```python
import functools
import math

import jax
import jax.numpy as jnp
from jax import lax
from jax.experimental import pallas as pl
from jax.experimental.pallas import tpu as pltpu

CHUNK = 64
HEAD_DIM = 128
ROPE_THETA = 10000.0
EPS = 1e-6
DSA_HEADS = 16
IDX_HEADS = 16
DSA_TOPK = 256
PEER_TOPK = 16
LANES = 128
KEY_CHUNK = 512
NEG_BIG = -1e30
INT_MIN = -(2 ** 31)
KEY_NEG_INF = INT_MIN + 0x7FFFFF
VMEM_LIMIT = 56 * 1024 * 1024
PEER_FFN_VMEM = 62 * 1024 * 1024

BF16 = jnp.bfloat16
F32 = jnp.float32


def _params(sem, vmem=VMEM_LIMIT):
    return pltpu.CompilerParams(dimension_semantics=sem, vmem_limit_bytes=vmem)


def _lambda_init(layer):
    return 0.8 - 0.6 * math.exp(-0.3 * layer)


def _adaln_kernel(c_ref, w_ref, b_ref, o_ref):
    c = c_ref[...]
    a = (c * jax.nn.sigmoid(c)).astype(BF16)
    o_ref[...] = jnp.dot(a, w_ref[...].astype(BF16), preferred_element_type=F32) + b_ref[...]


def _adaln(c_pad, w_ada, b_ada):
    bp, d = c_pad.shape
    n = w_ada.shape[1]
    tn = 512
    return pl.pallas_call(
        _adaln_kernel,
        grid=(n // tn,),
        in_specs=[pl.BlockSpec((bp, d), lambda j: (0, 0)),
                  pl.BlockSpec((d, tn), lambda j: (0, j)),
                  pl.BlockSpec((1, tn), lambda j: (0, j))],
        out_specs=pl.BlockSpec((bp, tn), lambda j: (0, j)),
        out_shape=jax.ShapeDtypeStruct((bp, n), F32),
        compiler_params=_params(("arbitrary",)),
        name="adaln",
    )(c_pad, w_ada, b_ada.reshape(1, n))


def _norm_mod_kernel(x_ref, g_ref, sc_ref, sh_ref, o_ref, *, per_token, transpose):
    x = x_ref[...]
    y = x * lax.rsqrt(jnp.mean(x * x, axis=-1, keepdims=True) + EPS) * g_ref[...]
    sc = sc_ref[...] if per_token else sc_ref[0]
    sh = sh_ref[...] if per_token else sh_ref[0]
    h = y * (1.0 + sc) + sh
    if transpose:
        o_ref[...] = h.T.astype(o_ref.dtype)
    else:
        o_ref[...] = h.astype(o_ref.dtype)


def _mod_spec(mod, per_token, tm, tn, col_block, rows_per_batch):
    if per_token:
        return pl.BlockSpec((tm, tn), lambda m, n=0, cb=col_block: (m, cb + n))
    tiles_per_batch = rows_per_batch // tm
    return pl.BlockSpec((1, 1, tn), lambda m, n=0, cb=col_block: (m // tiles_per_batch, 0, cb + n))


def _norm_mod(x2, g, mod, *, which, per_token, rows_per_batch, tm, transpose):
    t, d = x2.shape
    sh_blk, sc_blk = (0, 1) if which == 1 else (3, 4)
    out_shape = (d, t) if transpose else (t, d)
    out_spec = pl.BlockSpec((d, tm), lambda m: (0, m)) if transpose else pl.BlockSpec((tm, d), lambda m: (m, 0))
    return pl.pallas_call(
        functools.partial(_norm_mod_kernel, per_token=per_token, transpose=transpose),
        grid=(t // tm,),
        in_specs=[pl.BlockSpec((tm, d), lambda m: (m, 0)),
                  pl.BlockSpec((1, d), lambda m: (0, 0)),
                  _mod_spec(mod, per_token, tm, d, sc_blk, rows_per_batch),
                  _mod_spec(mod, per_token, tm, d, sh_blk, rows_per_batch)],
        out_specs=out_spec,
        out_shape=jax.ShapeDtypeStruct(out_shape, BF16),
        compiler_params=_params(("arbitrary",)),
        name="norm_mod_t" if transpose else "norm_mod",
    )(x2, g.reshape(1, d), mod, mod)


def _rope_tile(acc, cos, sin, n_chunks):
    outs = []
    for j in range(n_chunks):
        xj = acc[:, j * LANES:(j + 1) * LANES]
        outs.append(xj * cos + pltpu.roll(xj, LANES // 2, axis=1) * sin)
    return outs[0] if n_chunks == 1 else jnp.concatenate(outs, axis=1)


def _proj_kernel(*refs, rope_cols, scale, out_f32, out_bf16):
    h_ref, w_ref, cos_ref, sin_ref = refs[:4]
    outs = refs[4:]
    acc = jnp.dot(h_ref[...], w_ref[...], preferred_element_type=F32)
    tn = acc.shape[1]
    if rope_cols:
        roped = _rope_tile(acc[:, :rope_cols], cos_ref[...], sin_ref[...], rope_cols // LANES)
        acc = roped if rope_cols == tn else jnp.concatenate([roped, acc[:, rope_cols:]], axis=1)
    k = 0
    if out_f32:
        outs[k][...] = acc
        k += 1
    if out_bf16:
        outs[k][...] = (acc * scale if scale != 1.0 else acc).astype(BF16)


def _proj(h, w, cos, sin, *, col0, n, rope_cols, scale=1.0, out_f32, out_bf16, tm, tn, name):
    t, d = h.shape
    assert rope_cols in (0, tn) or n == tn
    assert col0 % tn == 0 and n % tn == 0 and col0 + n <= w.shape[1]
    j0 = col0 // tn
    pos_tiles = cos.shape[0] // tm
    out_shape, out_specs = [], []
    for want, dt in ((out_f32, F32), (out_bf16, BF16)):
        if want:
            out_shape.append(jax.ShapeDtypeStruct((t, n), dt))
            out_specs.append(pl.BlockSpec((tm, tn), lambda m, j: (m, j)))
    return pl.pallas_call(
        functools.partial(_proj_kernel, rope_cols=rope_cols, scale=scale, out_f32=out_f32, out_bf16=out_bf16),
        grid=(t // tm, n // tn),
        in_specs=[pl.BlockSpec((tm, d), lambda m, j: (m, 0)),
                  pl.BlockSpec((d, tn), lambda m, j: (0, j0 + j)),
                  pl.BlockSpec((tm, LANES), lambda m, j: (m % pos_tiles, 0)),
                  pl.BlockSpec((tm, LANES), lambda m, j: (m % pos_tiles, 0))],
        out_specs=out_specs,
        out_shape=out_shape,
        compiler_params=_params(("arbitrary", "arbitrary")),
        name=name,
    )(h, w, cos, sin)


def _num_key_chunks(q_start, n_q, n_valid_keys, n_chunks_total):
    last_visible = (((q_start + n_q - 1) >> 6) + 1) * CHUNK
    last_visible = jnp.minimum(last_visible, n_valid_keys)
    return jnp.minimum((last_visible + KEY_CHUNK - 1) // KEY_CHUNK, n_chunks_total)


def _num_full_chunks(q_start, n_valid_keys):
    return jnp.minimum(((q_start >> 6) + 1) * CHUNK, n_valid_keys) // KEY_CHUNK


def _ordered_key(x):
    b = pltpu.bitcast(x, jnp.int32)
    return b ^ ((b >> 31) & jnp.int32(0x7FFFFFFF))


def _fill_invisible(x, fill, q_start, n_q, key_start, n_valid_keys, all_keys_valid):
    n_k, n_lanes = x.shape
    kpos = key_start + lax.broadcasted_iota(jnp.int32, (n_k, n_lanes), 0)
    qpos = q_start + jnp.minimum(lax.broadcasted_iota(jnp.int32, (1, n_lanes), 1), n_q - 1)
    x = jnp.where((kpos >> 6) <= (qpos >> 6), x, fill)
    return x if all_keys_valid else jnp.where(kpos < n_valid_keys, x, fill)


def _t_bf16(x, n_lanes):
    x = x.astype(F32)
    if x.shape[0] < n_lanes:
        x = jnp.concatenate([x, jnp.zeros((n_lanes - x.shape[0], x.shape[1]), F32)], axis=0)
    return x.T.astype(BF16)


def _rows_i16(words, n_word_rows):
    tile = jnp.broadcast_to(words, (8, words.shape[1]))
    return pltpu.bitcast(jnp.concatenate([tile] * (n_word_rows // 8), axis=0), jnp.int16)


def _col_partial(x, op, group=8):
    parts = [x[r * group:(r + 1) * group] for r in range(x.shape[0] // group)]
    while len(parts) > 1:
        parts = [op(parts[i], parts[i + 1]) for i in range(0, len(parts) - 1, 2)] + parts[len(parts) & ~1:]
    return parts[0]


def _col_reduce(x, op):
    return (jnp.max if op is jnp.maximum else jnp.sum)(_col_partial(x, op), axis=0, keepdims=True)


def _row_to_cols(row):
    return jnp.broadcast_to(row, (LANES, row.shape[1])).T


def _dsa_kernel(q_ref, qi_ref, wi_ref, k_ref, vaug_ref, ki_ref, o_ref,
                qit_scr, qt_scr, key_scr, hi_scr, lo_scr, acc_scr, *,
                tq, tl, n_sel, p_len, n_valid_keys, all_keys_valid, n_idx_heads, n_kv_heads, rep):
    n_chunks_total = key_scr.shape[0]
    q_start = p_len + pl.program_id(1) * tq
    n_ch = _num_key_chunks(q_start, tq, n_valid_keys, n_chunks_total)
    heads_per_dot = 4
    half_chunk = KEY_CHUNK // 2

    for h in range(n_idx_heads):
        qit_scr[:, h * tl:(h + 1) * tl] = _t_bf16(qi_ref[:, h * LANES:(h + 1) * LANES], tl)
    for g in range(n_kv_heads):
        for r in range(rep):
            hh = g * rep + r
            qt_scr[g, :, r * tl:(r + 1) * tl] = _t_bf16(q_ref[:, hh * LANES:(hh + 1) * LANES], tl)
    wi = wi_ref[:, LANES:2 * LANES] * (n_idx_heads ** -0.5)
    if tq < tl:
        wi = jnp.concatenate([wi, jnp.zeros((tl - tq, LANES), F32)], axis=0)
    wi_t = wi.T

    def idx_body(j, carry):
        k0 = pl.multiple_of(j * KEY_CHUNK, KEY_CHUNK)
        kib = ki_ref[pl.ds(k0, KEY_CHUNK), :]
        score = jnp.zeros((KEY_CHUNK, tl), F32)
        for h0 in range(0, n_idx_heads, heads_per_dot):
            lg = jnp.dot(kib, qit_scr[:, h0 * tl:(h0 + heads_per_dot) * tl], preferred_element_type=F32)
            for h in range(h0, h0 + heads_per_dot):
                score = score + jnp.maximum(lg[:, (h - h0) * tl:(h - h0 + 1) * tl], 0.0) * wi_t[h:h + 1, :]
        score = _fill_invisible(score, -jnp.inf, q_start, tq, k0, n_valid_keys, all_keys_valid)
        key = _ordered_key(score)
        key_scr[j] = key
        hi16 = (key >> 16) & 0xFFFF
        lo16 = (key & 0xFFFF) ^ 0x8000
        hi_scr[j] = hi16[:half_chunk] | (hi16[half_chunk:] << 16)
        lo_scr[j] = lo16[:half_chunk] | (lo16[half_chunk:] << 16)
        return carry

    lax.fori_loop(0, n_ch, idx_body, 0)

    def count16(scr, pattern, strict):
        c = _rows_i16(pattern | (pattern << 16), half_chunk)

        def body(j, cnt):
            x = pltpu.bitcast(scr[j], jnp.int16)
            ones = jnp.where((x > c) if strict else (x >= c), jnp.int16(1), jnp.int16(0))
            return cnt + _col_partial(ones, jnp.add, group=16)

        cnt = pltpu.bitcast(lax.fori_loop(0, n_ch, body, jnp.zeros((16, tl), jnp.int16)), jnp.int32)
        return jnp.sum((cnt & 0xFFFF) + ((cnt >> 16) & 0xFFFF), axis=0, keepdims=True)

    def bisect16(scr, need):
        def bit_body(b, t):
            cand = t | lax.shift_left(jnp.int32(1), 15 - b)
            return jnp.where(count16(scr, cand ^ 0x8000, False) >= need, cand, t)
        return lax.fori_loop(0, 16, bit_body, jnp.zeros((1, tl), jnp.int32))

    hi_pat = bisect16(hi_scr, n_sel) ^ 0x8000
    need = n_sel - count16(hi_scr, hi_pat, True)

    def keep_bucket(j, carry):
        hi = pltpu.bitcast(hi_scr[j], jnp.int16)
        lo = pltpu.bitcast(lo_scr[j], jnp.int16)
        same = hi == _rows_i16(hi_pat | (hi_pat << 16), half_chunk)
        lo_scr[j] = pltpu.bitcast(jnp.where(same, lo, jnp.int16(-2 ** 15)), jnp.int32)
        return carry

    lax.fori_loop(0, n_ch, keep_bucket, 0)
    lo_u = bisect16(lo_scr, need)
    thr_raw = (((hi_pat << 16) >> 16) << 16) | lo_u
    thr = jnp.maximum(thr_raw, KEY_NEG_INF + 1)

    n_above = n_sel - need
    n_ge = n_above + count16(lo_scr, lo_u ^ 0x8000, False)
    surplus = jnp.where(thr_raw > KEY_NEG_INF, n_ge - n_sel, 0)

    @pl.when(jnp.max(surplus) > 0)
    def _():
        take = n_sel - (n_above + count16(lo_scr, lo_u ^ 0x8000, True))

        def positions(j):
            return j * KEY_CHUNK + lax.broadcasted_iota(jnp.int32, (KEY_CHUNK, tl), 0)

        def tied_before(limit):
            def body(j, cnt):
                ones = jnp.where(key_scr[j] == thr, jnp.where(positions(j) < limit, 1, 0), 0)
                return cnt + _col_partial(ones, jnp.add)
            cnt = lax.fori_loop(0, n_ch, body, jnp.zeros((8, tl), jnp.int32))
            return jnp.sum(cnt, axis=0, keepdims=True)

        n_bits = (n_chunks_total * KEY_CHUNK).bit_length()

        def bit_body(b, t):
            cand = t | lax.shift_left(jnp.int32(1), n_bits - 1 - b)
            return jnp.where(tied_before(cand) < take, cand, t)

        last = lax.fori_loop(0, n_bits, bit_body, jnp.zeros((1, tl), jnp.int32))

        def drop_body(j, carry):
            key = key_scr[j]
            key_scr[j] = jnp.where(key == thr, jnp.where(positions(j) > last, KEY_NEG_INF, key), key)
            return carry

        lax.fori_loop(0, n_ch, drop_body, 0)

    acc_scr[...] = jnp.zeros_like(acc_scr)

    def att_body(j, ms):
        k0 = pl.multiple_of(j * KEY_CHUNK, KEY_CHUNK)
        bias = jnp.where(key_scr[j] >= thr, 0.0, NEG_BIG)
        bias = jnp.concatenate([bias] * rep, axis=1)
        new = []
        for g in range(n_kv_heads):
            kb = k_ref[pl.ds(k0, KEY_CHUNK), g * LANES:(g + 1) * LANES]
            s = jnp.dot(kb, qt_scr[g], preferred_element_type=F32) + bias
            m_new = jnp.maximum(ms[g], _col_reduce(s, jnp.maximum))
            p = jnp.exp2(s - m_new).astype(BF16)
            va = vaug_ref[pl.ds(k0, KEY_CHUNK), g * 2 * LANES:(g + 1) * 2 * LANES]
            alpha = _row_to_cols(jnp.exp2(ms[g] - m_new))
            acc_scr[g] = (acc_scr[g] * jnp.concatenate([alpha, alpha], axis=1)
                          + lax.dot_general(p, va, (((0,), (0,)), ((), ())), preferred_element_type=F32))
            new.append(m_new)
        return tuple(new)

    lax.fori_loop(0, n_ch, att_body, tuple(jnp.full((1, rep * tl), NEG_BIG, F32) for _ in range(n_kv_heads)))

    for g in range(n_kv_heads):
        acc = acc_scr[g]
        o = acc[:, :LANES] / acc[:, LANES:]
        for r in range(rep):
            hh = g * rep + r
            o_ref[:, hh * LANES:(hh + 1) * LANES] = o[r * tl:r * tl + tq].astype(o_ref.dtype)


def _dsa_attention(q_all, qi_all, kiw_f32, k_bf, vaug_bf, ki_bf, *, batch, q_len, k_len, n_valid_keys, p_len, tq,
                   n_heads, n_kv_heads, n_idx_heads, n_sel):
    width = n_heads * HEAD_DIM
    assert qi_all.shape[1] == n_idx_heads * LANES
    nq = q_len // tq
    tl = max(tq, LANES)
    n_chunks = k_len // KEY_CHUNK
    rep = n_heads // n_kv_heads
    kv_w = n_kv_heads * HEAD_DIM
    return pl.pallas_call(
        functools.partial(_dsa_kernel, tq=tq, tl=tl, n_sel=n_sel, p_len=p_len, n_valid_keys=n_valid_keys,
                          all_keys_valid=n_valid_keys == k_len,
                          n_idx_heads=n_idx_heads, n_kv_heads=n_kv_heads, rep=rep),
        grid=(batch, nq),
        in_specs=[pl.BlockSpec((tq, width), lambda b, i: (b * nq + i, 0)),
                  pl.BlockSpec((tq, n_idx_heads * LANES), lambda b, i: (b * nq + i, 0)),
                  pl.BlockSpec((tq, 2 * LANES), lambda b, i: (b * nq + i, 0)),
                  pl.BlockSpec((k_len, kv_w), lambda b, i: (b, 0)),
                  pl.BlockSpec((k_len, 2 * kv_w), lambda b, i: (b, 0)),
                  pl.BlockSpec((k_len, LANES), lambda b, i: (b, 0))],
        out_specs=pl.BlockSpec((tq, width), lambda b, i: (b * nq + i, 0)),
        out_shape=jax.ShapeDtypeStruct((batch * q_len, width), BF16),
        scratch_shapes=[pltpu.VMEM((HEAD_DIM, n_idx_heads * tl), BF16),
                        pltpu.VMEM((n_kv_heads, HEAD_DIM, rep * tl), BF16),
                        pltpu.VMEM((n_chunks, KEY_CHUNK, tl), jnp.int32),
                        pltpu.VMEM((n_chunks, KEY_CHUNK // 2, tl), jnp.int32),
                        pltpu.VMEM((n_chunks, KEY_CHUNK // 2, tl), jnp.int32),
                        pltpu.VMEM((n_kv_heads, rep * tl, 2 * LANES), F32)],
        compiler_params=_params(("arbitrary", "arbitrary")),
        name="dsa_attention",
    )(q_all, qi_all, kiw_f32, k_bf, vaug_bf, ki_bf)


def _diff_kernel(dq_ref, dk_ref, dv_ref, *rest, tq, tl, hp, n_chunks_total, n_tail, p_len, n_valid_keys,
                 all_keys_valid, lam_init):
    if n_tail:
        dkt_ref, dvt_ref = rest[:2]
        rest = rest[2:]
    lq1_ref, lk1_ref, lq2_ref, lk2_ref, g_ref, o_ref, acc_scr = rest
    q_start = p_len + pl.program_id(2) * tq
    n_ch = _num_key_chunks(q_start, tq, n_valid_keys, n_chunks_total)
    hw = 2 * HEAD_DIM
    n_maps = 2 * hp

    lam = (jnp.exp(jnp.sum(lq1_ref[...] * lk1_ref[...], axis=1, keepdims=True))
           - jnp.exp(jnp.sum(lq2_ref[...] * lk2_ref[...], axis=1, keepdims=True)) + lam_init)
    qt = [_t_bf16(dq_ref[:, c * LANES:(c + 1) * LANES], tl) for c in range(n_maps)]

    acc_scr[...] = jnp.zeros_like(acc_scr)

    def update(carry, key_blocks, value_blocks, key_start, n_valid, all_valid, masked):
        new = []
        for c in range(n_maps):
            m, l = carry[2 * c:2 * c + 2]
            s = jnp.dot(key_blocks[c], qt[c], preferred_element_type=F32)
            if masked:
                s = _fill_invisible(s, NEG_BIG, q_start, tq, key_start, n_valid, all_valid)
            m_new = jnp.maximum(m, _col_reduce(s, jnp.maximum))
            p = jnp.exp2(s - m_new)
            alpha = jnp.exp2(m - m_new)
            alpha_c = _row_to_cols(alpha)
            acc_scr[c] = (acc_scr[c] * jnp.concatenate([alpha_c, alpha_c], axis=1)
                          + lax.dot_general(p.astype(BF16), value_blocks[c // 2], (((0,), (0,)), ((), ())),
                                            preferred_element_type=F32))
            new += [m_new, alpha * l + _col_reduce(p, jnp.add)]
        return tuple(new)

    def att_body(j, carry, masked):
        k0 = pl.multiple_of(j * KEY_CHUNK, KEY_CHUNK)
        keys = [dk_ref[pl.ds(k0, KEY_CHUNK), c * LANES:(c + 1) * LANES] for c in range(n_maps)]
        values = [dv_ref[pl.ds(k0, KEY_CHUNK), h * hw:(h + 1) * hw] for h in range(hp)]
        return update(carry, keys, values, k0, n_valid_keys, all_keys_valid, masked)

    one = (jnp.full((1, tl), NEG_BIG, F32), jnp.zeros((1, tl), F32))
    n_full = jnp.minimum(_num_full_chunks(q_start, n_valid_keys), n_ch)
    carry = lax.fori_loop(0, n_full, functools.partial(att_body, masked=False), one * n_maps)
    carry = lax.fori_loop(n_full, n_ch, functools.partial(att_body, masked=True), carry)
    if n_tail:
        k_main = n_chunks_total * KEY_CHUNK
        keys = [dkt_ref[:, c * LANES:(c + 1) * LANES] for c in range(n_maps)]
        values = [dvt_ref[:, h * hw:(h + 1) * hw] for h in range(hp)]
        carry = update(carry, keys, values, k_main, k_main + n_tail, True, True)

    def normalised(c):
        l_cols = _row_to_cols(carry[2 * c + 1])
        return acc_scr[c] / jnp.concatenate([l_cols, l_cols], axis=1)

    for h in range(hp):
        o = (normalised(2 * h) - lam * normalised(2 * h + 1))[:tq]
        o = o * lax.rsqrt(jnp.mean(o * o, axis=-1, keepdims=True) + EPS) * g_ref[...]
        o_ref[:, h * hw:(h + 1) * hw] = (o * (1.0 - lam_init)).astype(o_ref.dtype)


def _diff_attention(dq_all, dk_bf, dv_bf, lams, g_subln, *, batch, q_len, k_len, n_valid_keys, p_len, tq,
                    n_heads, lam_init, tail=None):
    hw = 2 * HEAD_DIM
    hp = 2 if n_heads % 2 == 0 else 1
    width = n_heads * hw
    assert dq_all.shape[1] == width
    nq = q_len // tq
    tl = max(tq, LANES)
    n_chunks = k_len // KEY_CHUNK
    n_tail = 0 if tail is None else q_len
    assert tail is None or (n_valid_keys == k_len and nq == 1)
    vec = pl.BlockSpec((1, HEAD_DIM), lambda b, h, i: (0, 0))
    kv_spec = pl.BlockSpec((k_len, hp * hw), lambda b, h, i: (b, h))
    tail_specs = [] if tail is None else [pl.BlockSpec((n_tail, hp * hw), lambda b, h, i: (b, h))] * 2
    return pl.pallas_call(
        functools.partial(_diff_kernel, tq=tq, tl=tl, hp=hp, n_chunks_total=n_chunks, n_tail=n_tail, p_len=p_len,
                          n_valid_keys=n_valid_keys, all_keys_valid=n_valid_keys == k_len, lam_init=lam_init),
        grid=(batch, n_heads // hp, nq),
        in_specs=[pl.BlockSpec((tq, hp * hw), lambda b, h, i: (b * nq + i, h)), kv_spec, kv_spec] + tail_specs
                 + [vec, vec, vec, vec, pl.BlockSpec((1, hw), lambda b, h, i: (0, 0))],
        out_specs=pl.BlockSpec((tq, hp * hw), lambda b, h, i: (b * nq + i, h)),
        out_shape=jax.ShapeDtypeStruct((batch * q_len, width), BF16),
        scratch_shapes=[pltpu.VMEM((2 * hp, tl, hw), F32)],
        compiler_params=_params(("arbitrary", "arbitrary", "arbitrary")),
        name="diff_attention",
    )(dq_all, dk_bf, dv_bf, *(tail or ()), *[v.reshape(1, HEAD_DIM) for v in lams], g_subln.reshape(1, hw))


def _outproj_kernel(a_ref, d_ref, wa_ref, wd_ref, x_ref, ga_ref, o_ref, *, per_token):
    mix = (jnp.dot(a_ref[...], wa_ref[...], preferred_element_type=F32)
           + jnp.dot(d_ref[...], wd_ref[...], preferred_element_type=F32))
    ga = ga_ref[...] if per_token else ga_ref[0]
    o_ref[...] = x_ref[...] + ga * mix


def _outproj(a_out, d_out, w_out_bf, x2, mod, *, per_token, rows_per_batch, tm, tn):
    t, d = x2.shape
    wa = a_out.shape[1]
    wd = d_out.shape[1]
    assert wa == wd
    cb = 2 * (d // tn)
    return pl.pallas_call(
        functools.partial(_outproj_kernel, per_token=per_token),
        grid=(t // tm, d // tn),
        in_specs=[pl.BlockSpec((tm, wa), lambda m, n: (m, 0)),
                  pl.BlockSpec((tm, wd), lambda m, n: (m, 0)),
                  pl.BlockSpec((wa, tn), lambda m, n: (0, n)),
                  pl.BlockSpec((wd, tn), lambda m, n: (1, n)),
                  pl.BlockSpec((tm, tn), lambda m, n: (m, n)),
                  _mod_spec(mod, per_token, tm, tn, cb, rows_per_batch)],
        out_specs=pl.BlockSpec((tm, tn), lambda m, n: (m, n)),
        out_shape=jax.ShapeDtypeStruct((t, d), F32),
        compiler_params=_params(("arbitrary", "arbitrary")),
        name="outproj",
    )(a_out, d_out, w_out_bf, w_out_bf, x2, mod)


def _top_rows(x, k, with_rank=False):
    tops = []
    rank = jnp.full(x.shape, float(k), F32) if with_rank else None
    for i in range(k):
        mx = jnp.max(x, axis=0, keepdims=True)
        tops.append(mx)
        hit = x == mx
        if with_rank:
            rank = jnp.where(hit, float(i), rank)
        x = jnp.where(hit, -jnp.inf, x)
    return (tops, rank) if with_rank else tops


def _pack_bf16_pair(lo, hi):
    def bits(x):
        b = pltpu.bitcast(x, jnp.uint32)
        return (b + jnp.uint32(0x7FFF) + ((b >> 16) & jnp.uint32(1))) >> 16
    return bits(lo) | (bits(hi) << 16)


def _peer_route_kernel(h_ref, wq_ref, keys_ref, cnt_ref, e1_ref, rank_ref, e2_ref, q_scr, s1_scr, top_scr):
    hc = pl.program_id(1)
    c = hc % 2
    half = keys_ref.shape[2]

    @pl.when(hc == 0)
    def _():
        q_scr[...] = jnp.dot(wq_ref[...], h_ref[...], preferred_element_type=F32).astype(BF16)

    q_t = q_scr[pl.ds(pl.multiple_of(hc * half, half), half), :]
    s_t = jnp.dot(keys_ref[0].astype(BF16), q_t, preferred_element_type=F32)

    @pl.when(c == 0)
    def _():
        s1_scr[...] = s_t
        top_scr[...] = jnp.concatenate(_top_rows(s_t, PEER_TOPK), axis=0)

    @pl.when(c == 1)
    def _():
        tops2, rank2 = _top_rows(s_t, PEER_TOPK, with_rank=True)
        tops = jnp.concatenate(tops2, axis=0)
        top1 = top_scr[...]
        cand = jnp.concatenate([top1[0:1, :] + tops]
                               + [top1[i:i + 1, :] + tops[:PEER_TOPK // 2] for i in range(1, PEER_TOPK)], axis=0)
        best = _top_rows(cand, PEER_TOPK)
        m = best[0]
        z = jnp.zeros_like(m)
        for bk in best:
            z = z + jnp.exp(bk - m)
        thr = best[PEER_TOPK - 1]
        s1 = s1_scr[...]
        cnt = jnp.zeros_like(s1)
        for j in range(PEER_TOPK):
            cnt = cnt + jnp.where(s1 + tops2[j] >= thr, 1.0, 0.0)
        e1 = jnp.exp(s1 - top1[0:1, :]) * (0.5 / z)
        e2 = jnp.exp(s_t - tops2[0])
        hn = s_t.shape[0] // 2
        cnt_ref[0] = _pack_bf16_pair(cnt, cnt)
        e1_ref[0] = _pack_bf16_pair(e1, e1)
        rank_ref[0] = _pack_bf16_pair(rank2[:hn], rank2[hn:])
        e2_ref[0] = _pack_bf16_pair(e2[:hn], e2[hn:])


def _peer_route(h_t, wq_t_bf, sub_keys, *, tm):
    d, t = h_t.shape
    heads, _, n_keys, half = sub_keys.shape
    keys2 = sub_keys.reshape(heads * 2, n_keys, half)
    a_spec = pl.BlockSpec((1, n_keys, tm), lambda m, hc: (hc // 2, 0, m))
    b_spec = pl.BlockSpec((1, n_keys // 2, tm), lambda m, hc: (hc // 2, 0, m))
    a_tab = jax.ShapeDtypeStruct((heads, n_keys, t), jnp.uint32)
    b_tab = jax.ShapeDtypeStruct((heads, n_keys // 2, t), jnp.uint32)
    return pl.pallas_call(
        _peer_route_kernel,
        grid=(t // tm, heads * 2),
        in_specs=[pl.BlockSpec((d, tm), lambda m, hc: (0, m)),
                  pl.BlockSpec((heads * 2 * half, d), lambda m, hc: (0, 0), pipeline_mode=pl.Buffered(1)),
                  pl.BlockSpec((1, n_keys, half), lambda m, hc: (hc, 0, 0))],
        out_specs=[a_spec, a_spec, b_spec, b_spec],
        out_shape=[a_tab, a_tab, b_tab, b_tab],
        scratch_shapes=[pltpu.VMEM((heads * 2 * half, tm), BF16), pltpu.VMEM((n_keys, tm), F32),
                        pltpu.VMEM((PEER_TOPK, tm), F32)],
        compiler_params=_params(("arbitrary", "arbitrary")),
        name="peer_route",
    )(h_t, wq_t_bf, keys2)


def _gated_gelu(x, half_gate):
    c = 0.7978845608028654
    inner = x * (c + (c * 0.044715) * (x * x))
    return (x * half_gate) * (1.0 + jnp.tanh(inner))


def _rows_bf16(row_words, n_rows):
    tile = jnp.broadcast_to(row_words, (8, row_words.shape[1]))
    return pltpu.bitcast(jnp.concatenate([tile] * (n_rows // 16), axis=0), BF16)


def _peer_ffn_kernel(h_ref, u_ref, vt_ref, cnt_ref, e1_ref, rank_ref, e2_ref, o_ref, *, n_keys):
    e = pl.program_id(1)
    te, tm = u_ref.shape[0], h_ref.shape[1]
    heads = cnt_ref.shape[0]
    a0 = e * (te // n_keys)

    @pl.when(e == 0)
    def _():
        o_ref[...] = jnp.zeros_like(o_ref)

    act = jnp.dot(u_ref[...], h_ref[...], preferred_element_type=F32)
    strip = min(tm, LANES)
    hn = n_keys // 2
    w_rows = [[None] * (tm // strip) for _ in range(2 * te // n_keys)]
    for ai in range(te // n_keys):
        cnt_a = [cnt_ref[h, pl.ds(a0 + ai, 1), :] for h in range(heads)]
        e1_a = [e1_ref[h, pl.ds(a0 + ai, 1), :] for h in range(heads)]
        for c in range(tm // strip):
            cols = slice(c * strip, (c + 1) * strip)
            gate = jnp.zeros((n_keys, strip), BF16)
            for h in range(heads):
                cnt_b = _rows_bf16(cnt_a[h][:, cols], n_keys)
                e1_b = _rows_bf16(e1_a[h][:, cols], n_keys)
                keep = pltpu.bitcast(rank_ref[h, :, cols], BF16) < cnt_b
                gate = gate + jnp.where(keep, pltpu.bitcast(e2_ref[h, :, cols], BF16) * e1_b,
                                        jnp.zeros_like(e1_b))
            words = pltpu.bitcast(gate, jnp.uint32)
            halves = (pltpu.bitcast(words << 16, F32), pltpu.bitcast(words & jnp.uint32(0xFFFF0000), F32))
            for k, g in enumerate(halves):
                rows = slice(ai * n_keys + k * hn, ai * n_keys + (k + 1) * hn)
                w_rows[2 * ai + k][c] = _gated_gelu(act[rows, cols], g).astype(BF16)
    w = jnp.concatenate([r[0] if len(r) == 1 else jnp.concatenate(r, axis=1) for r in w_rows], axis=0)
    o_ref[...] += jnp.dot(vt_ref[...], w, preferred_element_type=F32)


def _peer_ffn(h_t, u_bf, vt_bf, cnt, e1, rank, e2, *, tm, te):
    d, t = h_t.shape
    n_exp = u_bf.shape[0]
    n_blocks = n_exp // te
    heads, n_keys, _ = cnt.shape
    once = pl.Buffered(1)
    a_spec = pl.BlockSpec((heads, n_keys, tm), lambda m, e: (0, 0, m), pipeline_mode=once)
    b_spec = pl.BlockSpec((heads, n_keys // 2, tm), lambda m, e: (0, 0, m), pipeline_mode=once)
    return pl.pallas_call(
        functools.partial(_peer_ffn_kernel, n_keys=n_keys),
        grid=(t // tm, n_blocks),
        in_specs=[pl.BlockSpec((d, tm), lambda m, e: (0, m), pipeline_mode=once),
                  pl.BlockSpec((te, d), lambda m, e: (e, 0)),
                  pl.BlockSpec((d, te), lambda m, e: (0, e)),
                  a_spec, a_spec, b_spec, b_spec],
        out_specs=pl.BlockSpec((d, tm), lambda m, e: (0, m), pipeline_mode=once),
        out_shape=jax.ShapeDtypeStruct((d, t), F32),
        compiler_params=_params(("arbitrary", "arbitrary"), vmem=PEER_FFN_VMEM),
        name="peer_ffn",
    )(h_t, u_bf, vt_bf, cnt, e1, rank, e2)


def _transpose_cast_kernel(x_ref, o_ref):
    o_ref[...] = x_ref[...].T.astype(o_ref.dtype)


def _transpose_cast(x, dtype):
    r, c = x.shape
    tr, tc = _pick_tile(r, (2048, 1024, 512, 256, 128)), _pick_tile(c, (1024, 512, 256, 128))
    return pl.pallas_call(
        _transpose_cast_kernel,
        grid=(r // tr, c // tc),
        in_specs=[pl.BlockSpec((tr, tc), lambda i, j: (i, j))],
        out_specs=pl.BlockSpec((tc, tr), lambda i, j: (j, i)),
        out_shape=jax.ShapeDtypeStruct((c, r), dtype),
        compiler_params=_params(("arbitrary", "arbitrary")),
        name="transpose_cast",
    )(x)


def _final_kernel(x_ref, pt_ref, ga_ref, g_ref, o_ref, *, per_token, normalize):
    ga = ga_ref[...] if per_token else ga_ref[0]
    x = x_ref[...] + ga * pt_ref[...].T
    if normalize:
        x = x * lax.rsqrt(jnp.mean(x * x, axis=-1, keepdims=True) + EPS) * g_ref[...]
    o_ref[...] = x


def _final(x1, peer_t, mod, g_final, *, per_token, rows_per_batch, tm, normalize):
    t, d = x1.shape
    row = pl.BlockSpec((tm, d), lambda m: (m, 0))
    return pl.pallas_call(
        functools.partial(_final_kernel, per_token=per_token, normalize=normalize),
        grid=(t // tm,),
        in_specs=[row, pl.BlockSpec((d, tm), lambda m: (0, m)), _mod_spec(mod, per_token, tm, d, 5, rows_per_batch),
                  pl.BlockSpec((1, d), lambda m: (0, 0))],
        out_specs=row,
        out_shape=jax.ShapeDtypeStruct((t, d), F32),
        compiler_params=_params(("arbitrary",)),
        name="final",
    )(x1, peer_t, mod, g_final.reshape(1, d))


def _rope_tables(pos):
    half = HEAD_DIM // 2
    inv = ROPE_THETA ** (-jnp.arange(half, dtype=F32) / half)
    ang = pos.astype(F32)[:, None] * inv[None, :]
    cos, sin = jnp.cos(ang), jnp.sin(ang)
    return jnp.concatenate([cos, cos], axis=1), jnp.concatenate([-sin, sin], axis=1)


def _pick_tile(n, prefs):
    for p in prefs:
        if n % p == 0:
            return p
    return n


def _layer(x, mod_rows, past, layer, last_layer, w, dims):
    b, s, d = x.shape
    t = b * s
    n_heads, n_kv, n_idx, n_diff = dims
    p_len = 0 if past is None else past[0].shape[1]
    n_keys_valid = p_len + s
    n_sel = min(DSA_TOPK, n_keys_valid // 4)
    x2 = x.reshape(t, d)

    per_token = s % 256 != 0
    if per_token:
        mod = jnp.repeat(mod_rows, s, axis=0)
    else:
        mod = mod_rows.reshape(b, 1, 6 * d)
    tm_big = _pick_tile(t if per_token else s, (1024, 512, 256, 128))
    tm_mid = _pick_tile(t if per_token else s, (512, 256, 128))
    tm_small = _pick_tile(t if per_token else s, (256, 128))

    pos = p_len + jnp.arange(s)
    cos, sin = _rope_tables(pos)
    if per_token:
        cos, sin = jnp.tile(cos, (b, 1)), jnp.tile(sin, (b, 1))

    h = _norm_mod(x2, w["g_norm_mix"], mod, which=1, per_token=per_token, rows_per_batch=s, tm=tm_mid,
                  transpose=False)

    qw, kvw, dw = n_heads * HEAD_DIM, n_kv * HEAD_DIM, n_diff * 2 * HEAD_DIM
    head = functools.partial(_proj, h, w["w_head"], cos=cos, sin=sin, tm=tm_big)
    tail = functools.partial(_proj, h, w["w_tail"], cos=cos, sin=sin, tm=tm_big)
    q_scale = HEAD_DIM ** -0.5 * math.log2(math.e)
    wide = 1024
    bf_only = dict(out_f32=False, out_bf16=True, tn=wide)
    both = dict(out_f32=True, out_bf16=True)
    (q_all,) = head(col0=0, n=qw, rope_cols=wide, scale=q_scale, name="proj_q", **bf_only)
    (qi_all,) = head(col0=qw + 2 * kvw, n=n_idx * LANES, rope_cols=wide, scale=q_scale, name="proj_qi", **bf_only)
    (dq_all,) = tail(col0=0, n=dw, rope_cols=wide, scale=q_scale, name="proj_dq", **bf_only)
    k_f, k_b = head(col0=qw, n=kvw, rope_cols=512, tn=512, name="proj_k", **both)
    v_f, v_b = head(col0=qw + kvw, n=kvw, rope_cols=0, tn=512, name="proj_v", **both)
    dk_f, dk_b = tail(col0=dw, n=dw, rope_cols=wide, tn=wide, name="proj_dk", **both)
    dv_f, dv_b = tail(col0=2 * dw, n=dw, rope_cols=0, tn=wide, name="proj_dv", **both)
    kiw_f, kiw_b = head(col0=qw + 2 * kvw + n_idx * LANES, n=2 * LANES, rope_cols=LANES, out_f32=True,
                        out_bf16=True, tn=2 * LANES, name="proj_kiw")

    ki_f = kiw_f[:, :LANES]
    new_rows = (k_f.reshape(b, s, n_kv, HEAD_DIM), v_f.reshape(b, s, n_kv, HEAD_DIM), ki_f.reshape(b, s, LANES),
                dk_f.reshape(b, s, n_diff, 2, HEAD_DIM), dv_f.reshape(b, s, n_diff, 2 * HEAD_DIM))

    diff_keys = None
    if past is None:
        k_len = s
        k_all, v_all, ki_all, dk_all, dv_all = k_b, v_b, kiw_b, dk_b, dv_b
    else:
        k_len = -(-n_keys_valid // KEY_CHUNK) * KEY_CHUNK

        def join(cache, new, width):
            new = new.reshape(b, s, -1)[:, :, :width]
            old = lax.optimization_barrier(cache.reshape(b, p_len, width))
            both_ = jnp.concatenate([old.astype(BF16), new], axis=1)
            both_ = jnp.pad(both_, ((0, 0), (0, k_len - n_keys_valid), (0, 0)))
            return both_.reshape(b * k_len, width)

        k_all = join(past[0], k_b, n_kv * HEAD_DIM)
        v_all = join(past[1], v_b, n_kv * HEAD_DIM)
        ki_all = join(past[2], kiw_b, LANES)
        if p_len % KEY_CHUNK == 0 and s % 16 == 0 and s <= 256:
            def flat(cache):
                return lax.optimization_barrier(cache.reshape(b, p_len, dw)).astype(BF16).reshape(b * p_len, dw)
            diff_keys = dict(dk=flat(past[3]), dv=flat(past[4]), k_len=p_len, n_valid_keys=p_len,
                             tail=(dk_b, dv_b))
            dk_all = dv_all = None
        else:
            dk_all = join(past[3], dk_b, dw)
            dv_all = join(past[4], dv_b, dw)

    ones = jnp.ones((v_all.shape[0], HEAD_DIM), BF16)
    vaug_all = jnp.concatenate(
        [piece for g in range(n_kv) for piece in (v_all[:, g * HEAD_DIM:(g + 1) * HEAD_DIM], ones)], axis=1)
    tq_a = _pick_tile(s, (128, 64, 32))
    a_out = _dsa_attention(q_all, qi_all, kiw_f, k_all, vaug_all, ki_all, batch=b, q_len=s, k_len=k_len,
                           n_valid_keys=n_keys_valid, p_len=p_len, tq=tq_a, n_heads=n_heads, n_kv_heads=n_kv,
                           n_idx_heads=n_idx, n_sel=n_sel)
    tq_d = _pick_tile(s, (256, 128, 64, 32))
    if diff_keys is None:
        diff_keys = dict(dk=dk_all, dv=dv_all, k_len=k_len, n_valid_keys=n_keys_valid, tail=None)
    d_out = _diff_attention(dq_all, diff_keys["dk"], diff_keys["dv"], w["lams"], w["g_subln"], batch=b, q_len=s,
                            k_len=diff_keys["k_len"], n_valid_keys=diff_keys["n_valid_keys"], p_len=p_len,
                            tq=tq_d, n_heads=n_diff, lam_init=_lambda_init(layer), tail=diff_keys["tail"])

    x1 = _outproj(a_out, d_out, w["w_out"], x2, mod, per_token=per_token, rows_per_batch=s, tm=tm_big,
                  tn=_pick_tile(d, (1024, 512)))

    h2_t = _norm_mod(x1, w["g_norm_ffn"], mod, which=2, per_token=per_token, rows_per_batch=s, tm=tm_mid,
                     transpose=True)
    cnt, e1, rank, e2 = _peer_route(h2_t, w["peer_wq_t"], w["peer_keys"], tm=tm_mid)
    peer_t = _peer_ffn(h2_t, w["peer_u"], w["peer_v_t"], cnt, e1, rank, e2, tm=tm_mid, te=1024)
    x_out = _final(x1, peer_t, mod, w["g_final"], per_token=per_token, rows_per_batch=s, tm=tm_small,
                   normalize=last_layer)
    return x_out.reshape(b, s, d), new_rows


def kernel(x_prompt, x_sample, cache_dsa_k, cache_dsa_v, cache_idx_k, cache_diff_k, cache_diff_v, c_prompt, c_sample, w_ada, b_ada, g_norm_mix, g_norm_ffn, w_in, diff_lambda_q1, diff_lambda_k1, diff_lambda_q2, diff_lambda_k2, g_diff_subln, w_out, peer_w_query, peer_sub_keys, peer_u, peer_v, g_final):
    depth = w_in.shape[0]
    bp, bs = x_prompt.shape[0], x_sample.shape[0]
    n_kv = cache_dsa_k.shape[3]
    n_diff = cache_diff_k.shape[3]
    n_heads, n_idx_heads = DSA_HEADS, IDX_HEADS
    qw, kvw, dw = n_heads * HEAD_DIM, n_kv * HEAD_DIM, n_diff * 2 * HEAD_DIM
    tail0 = qw + 2 * kvw + n_idx_heads * LANES + LANES + n_idx_heads
    assert w_in.shape[2] == tail0 + 3 * dw
    dims = (n_heads, n_kv, n_idx_heads, n_diff)

    hp, hs = x_prompt, x_sample
    rows_p, rows_s = [], []
    n_c = bp + bs
    c_pad = jnp.pad(jnp.concatenate([c_prompt, c_sample], axis=0), ((0, (-n_c) % 16), (0, 0)))
    for l in range(depth):
        mod_all = _adaln(c_pad, w_ada[l], b_ada[l])
        w_in_bf = w_in[l].astype(BF16)
        weights = {
            "w_head": w_in_bf,
            "w_tail": w_in_bf[:, tail0:],
            "w_out": w_out[l].astype(BF16),
            "peer_wq_t": _transpose_cast(peer_w_query[l], BF16),
            "peer_keys": peer_sub_keys[l],
            "peer_u": peer_u[l].astype(BF16), "peer_v_t": _transpose_cast(peer_v[l], BF16),
            "g_norm_mix": g_norm_mix[l], "g_norm_ffn": g_norm_ffn[l], "g_subln": g_diff_subln[l],
            "lams": (diff_lambda_q1[l], diff_lambda_k1[l], diff_lambda_q2[l], diff_lambda_k2[l]),
            "g_final": g_final,
        }
        last = l == depth - 1
        hp, rp = _layer(hp, mod_all[:bp], None, l, last, weights, dims)
        past = (cache_dsa_k[l], cache_dsa_v[l], cache_idx_k[l], cache_diff_k[l], cache_diff_v[l])
        hs, rs = _layer(hs, mod_all[bp:bp + bs], past, l, last, weights, dims)
        rows_p.append(rp)
        rows_s.append(rs)
    stack = lambda rows, i: jnp.stack([r[i] for r in rows])
    return (hp, hs) + tuple(stack(rows_p, i) for i in range(5)) + tuple(stack(rows_s, i) for i in range(5))
```

```python
import functools
import math

import jax
import jax.numpy as jnp
from jax import lax
from jax.experimental import pallas as pl
from jax.experimental.pallas import tpu as pltpu

CHUNK = 64
HEAD_DIM = 128
ROPE_THETA = 10000.0
EPS = 1e-6
DSA_HEADS = 16
IDX_HEADS = 16
DSA_TOPK = 256
PEER_TOPK = 16
LANES = 128
KEY_CHUNK = 512
NEG_BIG = -1e30
INT_MIN = -(2 ** 31)
KEY_NEG_INF = INT_MIN + 0x7FFFFF
VMEM_LIMIT = 56 * 1024 * 1024
PEER_FFN_VMEM = 62 * 1024 * 1024

BF16 = jnp.bfloat16
F32 = jnp.float32


def _params(sem, vmem=VMEM_LIMIT):
    return pltpu.CompilerParams(dimension_semantics=sem, vmem_limit_bytes=vmem)


def _lambda_init(layer):
    return 0.8 - 0.6 * math.exp(-0.3 * layer)


def _adaln_kernel(c_ref, w_ref, b_ref, o_ref):
    c = c_ref[...]
    a = (c * jax.nn.sigmoid(c)).astype(BF16)
    o_ref[...] = jnp.dot(a, w_ref[...].astype(BF16), preferred_element_type=F32) + b_ref[...]


def _adaln(c_pad, w_ada, b_ada):
    bp, d = c_pad.shape
    n = w_ada.shape[1]
    tn = 512
    return pl.pallas_call(
        _adaln_kernel,
        grid=(n // tn,),
        in_specs=[pl.BlockSpec((bp, d), lambda j: (0, 0)),
                  pl.BlockSpec((d, tn), lambda j: (0, j)),
                  pl.BlockSpec((1, tn), lambda j: (0, j))],
        out_specs=pl.BlockSpec((bp, tn), lambda j: (0, j)),
        out_shape=jax.ShapeDtypeStruct((bp, n), F32),
        compiler_params=_params(("arbitrary",)),
        name="adaln",
    )(c_pad, w_ada, b_ada.reshape(1, n))


def _mod_rows(ref, per_token):
    if not per_token:
        return ref[0]
    v = ref[...]
    return jnp.broadcast_to(v[:, None, :], (v.shape[0], per_token, v.shape[1])).reshape(-1, v.shape[1])


def _norm_mod_kernel(x_ref, g_ref, sc_ref, sh_ref, o_ref, *, per_token, transpose):
    x = x_ref[...]
    y = x * lax.rsqrt(jnp.mean(x * x, axis=-1, keepdims=True) + EPS) * g_ref[...]
    h = y * (1.0 + _mod_rows(sc_ref, per_token)) + _mod_rows(sh_ref, per_token)
    if transpose:
        o_ref[...] = h.T.astype(o_ref.dtype)
    else:
        o_ref[...] = h.astype(o_ref.dtype)


def _mod_spec(mod, per_token, tm, tn, col_block, rows_per_batch):
    if per_token:
        return pl.BlockSpec((tm // per_token, tn), lambda m, n=0, cb=col_block: (m, cb + n))
    tiles_per_batch = rows_per_batch // tm
    return pl.BlockSpec((1, 1, tn), lambda m, n=0, cb=col_block: (m // tiles_per_batch, 0, cb + n))


def _norm_mod(x2, g, mod, *, which, per_token, rows_per_batch, tm, transpose):
    t, d = x2.shape
    sh_blk, sc_blk = (0, 1) if which == 1 else (3, 4)
    out_shape = (d, t) if transpose else (t, d)
    out_spec = pl.BlockSpec((d, tm), lambda m: (0, m)) if transpose else pl.BlockSpec((tm, d), lambda m: (m, 0))
    return pl.pallas_call(
        functools.partial(_norm_mod_kernel, per_token=per_token, transpose=transpose),
        grid=(t // tm,),
        in_specs=[pl.BlockSpec((tm, d), lambda m: (m, 0)),
                  pl.BlockSpec((1, d), lambda m: (0, 0)),
                  _mod_spec(mod, per_token, tm, d, sc_blk, rows_per_batch),
                  _mod_spec(mod, per_token, tm, d, sh_blk, rows_per_batch)],
        out_specs=out_spec,
        out_shape=jax.ShapeDtypeStruct(out_shape, BF16),
        compiler_params=_params(("arbitrary",)),
        name="norm_mod_t" if transpose else "norm_mod",
    )(x2, g.reshape(1, d), mod, mod)


def _rope_tile(acc, cos, sin, n_chunks):
    outs = []
    for j in range(n_chunks):
        xj = acc[:, j * LANES:(j + 1) * LANES]
        outs.append(xj * cos + pltpu.roll(xj, LANES // 2, axis=1) * sin)
    return outs[0] if n_chunks == 1 else jnp.concatenate(outs, axis=1)


def _proj_kernel(*refs, rope_cols, scale, out_f32, out_bf16):
    h_ref, w_ref, cos_ref, sin_ref = refs[:4]
    outs = refs[4:]
    acc = jnp.dot(h_ref[...], w_ref[...], preferred_element_type=F32)
    tn = acc.shape[1]
    if rope_cols:
        roped = _rope_tile(acc[:, :rope_cols], cos_ref[...], sin_ref[...], rope_cols // LANES)
        acc = roped if rope_cols == tn else jnp.concatenate([roped, acc[:, rope_cols:]], axis=1)
    k = 0
    if out_f32:
        outs[k][...] = acc
        k += 1
    if out_bf16:
        outs[k][...] = (acc * scale if scale != 1.0 else acc).astype(BF16)


def _proj(h, w, cos, sin, *, col0, n, rope_cols, scale=1.0, out_f32, out_bf16, tm, tn, name):
    t, d = h.shape
    assert rope_cols in (0, tn) or n == tn
    assert col0 % tn == 0 and n % tn == 0 and col0 + n <= w.shape[1]
    j0 = col0 // tn
    pos_tiles = cos.shape[0] // tm
    out_shape, out_specs = [], []
    for want, dt in ((out_f32, F32), (out_bf16, BF16)):
        if want:
            out_shape.append(jax.ShapeDtypeStruct((t, n), dt))
            out_specs.append(pl.BlockSpec((tm, tn), lambda m, j: (m, j)))
    return pl.pallas_call(
        functools.partial(_proj_kernel, rope_cols=rope_cols, scale=scale, out_f32=out_f32, out_bf16=out_bf16),
        grid=(t // tm, n // tn),
        in_specs=[pl.BlockSpec((tm, d), lambda m, j: (m, 0)),
                  pl.BlockSpec((d, tn), lambda m, j: (0, j0 + j)),
                  pl.BlockSpec((tm, LANES), lambda m, j: (m % pos_tiles, 0)),
                  pl.BlockSpec((tm, LANES), lambda m, j: (m % pos_tiles, 0))],
        out_specs=out_specs,
        out_shape=out_shape,
        compiler_params=_params(("arbitrary", "arbitrary")),
        name=name,
    )(h, w, cos, sin)


def _num_key_chunks(q_start, n_q, n_valid_keys, n_chunks_total):
    last_visible = (((q_start + n_q - 1) >> 6) + 1) * CHUNK
    last_visible = jnp.minimum(last_visible, n_valid_keys)
    return jnp.minimum((last_visible + KEY_CHUNK - 1) // KEY_CHUNK, n_chunks_total)


def _num_full_chunks(q_start, n_valid_keys):
    return jnp.minimum(((q_start >> 6) + 1) * CHUNK, n_valid_keys) // KEY_CHUNK


def _ordered_key(x):
    b = pltpu.bitcast(x, jnp.int32)
    return b ^ ((b >> 31) & jnp.int32(0x7FFFFFFF))


def _fill_invisible(x, fill, q_start, n_q, key_start, n_valid_keys, all_keys_valid):
    n_k, n_lanes = x.shape
    kpos = key_start + lax.broadcasted_iota(jnp.int32, (n_k, n_lanes), 0)
    qpos = q_start + jnp.minimum(lax.broadcasted_iota(jnp.int32, (1, n_lanes), 1), n_q - 1)
    x = jnp.where((kpos >> 6) <= (qpos >> 6), x, fill)
    return x if all_keys_valid else jnp.where(kpos < n_valid_keys, x, fill)


def _t_bf16(x, n_lanes):
    x = x.astype(F32)
    if x.shape[0] < n_lanes:
        x = jnp.concatenate([x, jnp.zeros((n_lanes - x.shape[0], x.shape[1]), F32)], axis=0)
    return x.T.astype(BF16)


def _rows_i16(words, n_word_rows):
    tile = jnp.broadcast_to(words, (8, words.shape[1]))
    return pltpu.bitcast(jnp.concatenate([tile] * (n_word_rows // 8), axis=0), jnp.int16)


def _col_partial(x, op, group=8):
    parts = [x[r * group:(r + 1) * group] for r in range(x.shape[0] // group)]
    while len(parts) > 1:
        parts = [op(parts[i], parts[i + 1]) for i in range(0, len(parts) - 1, 2)] + parts[len(parts) & ~1:]
    return parts[0]


def _col_reduce(x, op):
    return (jnp.max if op is jnp.maximum else jnp.sum)(_col_partial(x, op), axis=0, keepdims=True)


def _row_to_cols(row):
    return jnp.broadcast_to(row, (LANES, row.shape[1])).T


def _dsa_kernel(q_ref, qi_ref, wi_ref, k_ref, vaug_ref, ki_ref, o_ref,
                qit_scr, qt_scr, key_scr, hi_scr, lo_scr, acc_scr, *,
                tq, tl, n_sel, p_len, n_valid_keys, all_keys_valid, n_idx_heads, n_kv_heads, rep):
    n_chunks_total = key_scr.shape[0]
    q_start = p_len + pl.program_id(1) * tq
    n_ch = _num_key_chunks(q_start, tq, n_valid_keys, n_chunks_total)
    heads_per_dot = 4
    half_chunk = KEY_CHUNK // 2

    for h in range(n_idx_heads):
        qit_scr[:, h * tl:(h + 1) * tl] = _t_bf16(qi_ref[:, h * LANES:(h + 1) * LANES], tl)
    for g in range(n_kv_heads):
        for r in range(rep):
            hh = g * rep + r
            qt_scr[g, :, r * tl:(r + 1) * tl] = _t_bf16(q_ref[:, hh * LANES:(hh + 1) * LANES], tl)
    wi = wi_ref[:, LANES:2 * LANES] * (n_idx_heads ** -0.5)
    if tq < tl:
        wi = jnp.concatenate([wi, jnp.zeros((tl - tq, LANES), F32)], axis=0)
    wi_t = wi.T

    def idx_body(j, carry):
        k0 = pl.multiple_of(j * KEY_CHUNK, KEY_CHUNK)
        kib = ki_ref[pl.ds(k0, KEY_CHUNK), :]
        score = jnp.zeros((KEY_CHUNK, tl), F32)
        for h0 in range(0, n_idx_heads, heads_per_dot):
            lg = jnp.dot(kib, qit_scr[:, h0 * tl:(h0 + heads_per_dot) * tl], preferred_element_type=F32)
            for h in range(h0, h0 + heads_per_dot):
                score = score + jnp.maximum(lg[:, (h - h0) * tl:(h - h0 + 1) * tl], 0.0) * wi_t[h:h + 1, :]
        score = _fill_invisible(score, -jnp.inf, q_start, tq, k0, n_valid_keys, all_keys_valid)
        key = _ordered_key(score)
        key_scr[j] = key
        hi16 = (key >> 16) & 0xFFFF
        lo16 = (key & 0xFFFF) ^ 0x8000
        hi_scr[j] = hi16[:half_chunk] | (hi16[half_chunk:] << 16)
        lo_scr[j] = lo16[:half_chunk] | (lo16[half_chunk:] << 16)
        return carry

    lax.fori_loop(0, n_ch, idx_body, 0)

    def count16(scr, pattern, strict):
        c = _rows_i16(pattern | (pattern << 16), half_chunk)

        def body(j, cnt):
            x = pltpu.bitcast(scr[j], jnp.int16)
            ones = jnp.where((x > c) if strict else (x >= c), jnp.int16(1), jnp.int16(0))
            return cnt + _col_partial(ones, jnp.add, group=16)

        cnt = pltpu.bitcast(lax.fori_loop(0, n_ch, body, jnp.zeros((16, tl), jnp.int16)), jnp.int32)
        return jnp.sum((cnt & 0xFFFF) + ((cnt >> 16) & 0xFFFF), axis=0, keepdims=True)

    def bisect16(scr, need):
        def bit_body(b, t):
            cand = t | lax.shift_left(jnp.int32(1), 15 - b)
            return jnp.where(count16(scr, cand ^ 0x8000, False) >= need, cand, t)
        return lax.fori_loop(0, 16, bit_body, jnp.zeros((1, tl), jnp.int32))

    hi_pat = bisect16(hi_scr, n_sel) ^ 0x8000
    need = n_sel - count16(hi_scr, hi_pat, True)

    def keep_bucket(j, carry):
        hi = pltpu.bitcast(hi_scr[j], jnp.int16)
        lo = pltpu.bitcast(lo_scr[j], jnp.int16)
        same = hi == _rows_i16(hi_pat | (hi_pat << 16), half_chunk)
        lo_scr[j] = pltpu.bitcast(jnp.where(same, lo, jnp.int16(-2 ** 15)), jnp.int32)
        return carry

    lax.fori_loop(0, n_ch, keep_bucket, 0)
    lo_u = bisect16(lo_scr, need)
    thr_raw = (((hi_pat << 16) >> 16) << 16) | lo_u
    thr = jnp.maximum(thr_raw, KEY_NEG_INF + 1)

    n_above = n_sel - need
    n_ge = n_above + count16(lo_scr, lo_u ^ 0x8000, False)
    surplus = jnp.where(thr_raw > KEY_NEG_INF, n_ge - n_sel, 0)

    @pl.when(jnp.max(surplus) > 0)
    def _():
        take = n_sel - (n_above + count16(lo_scr, lo_u ^ 0x8000, True))

        def positions(j):
            return j * KEY_CHUNK + lax.broadcasted_iota(jnp.int32, (KEY_CHUNK, tl), 0)

        def tied_before(limit):
            def body(j, cnt):
                ones = jnp.where(key_scr[j] == thr, jnp.where(positions(j) < limit, 1, 0), 0)
                return cnt + _col_partial(ones, jnp.add)
            cnt = lax.fori_loop(0, n_ch, body, jnp.zeros((8, tl), jnp.int32))
            return jnp.sum(cnt, axis=0, keepdims=True)

        n_bits = (n_chunks_total * KEY_CHUNK).bit_length()

        def bit_body(b, t):
            cand = t | lax.shift_left(jnp.int32(1), n_bits - 1 - b)
            return jnp.where(tied_before(cand) < take, cand, t)

        last = lax.fori_loop(0, n_bits, bit_body, jnp.zeros((1, tl), jnp.int32))

        def drop_body(j, carry):
            key = key_scr[j]
            key_scr[j] = jnp.where(key == thr, jnp.where(positions(j) > last, KEY_NEG_INF, key), key)
            return carry

        lax.fori_loop(0, n_ch, drop_body, 0)

    acc_scr[...] = jnp.zeros_like(acc_scr)

    def att_body(j, ms):
        k0 = pl.multiple_of(j * KEY_CHUNK, KEY_CHUNK)
        bias = jnp.where(key_scr[j] >= thr, 0.0, NEG_BIG)
        bias = jnp.concatenate([bias] * rep, axis=1)
        new = []
        for g in range(n_kv_heads):
            kb = k_ref[pl.ds(k0, KEY_CHUNK), g * LANES:(g + 1) * LANES]
            s = jnp.dot(kb, qt_scr[g], preferred_element_type=F32) + bias
            m_new = jnp.maximum(ms[g], _col_reduce(s, jnp.maximum))
            p = jnp.exp2(s - m_new).astype(BF16)
            va = vaug_ref[pl.ds(k0, KEY_CHUNK), g * 2 * LANES:(g + 1) * 2 * LANES]
            alpha = _row_to_cols(jnp.exp2(ms[g] - m_new))
            acc_scr[g] = (acc_scr[g] * jnp.concatenate([alpha, alpha], axis=1)
                          + lax.dot_general(p, va, (((0,), (0,)), ((), ())), preferred_element_type=F32))
            new.append(m_new)
        return tuple(new)

    lax.fori_loop(0, n_ch, att_body, tuple(jnp.full((1, rep * tl), NEG_BIG, F32) for _ in range(n_kv_heads)))

    for g in range(n_kv_heads):
        acc = acc_scr[g]
        o = acc[:, :LANES] / acc[:, LANES:]
        for r in range(rep):
            hh = g * rep + r
            o_ref[:, hh * LANES:(hh + 1) * LANES] = o[r * tl:r * tl + tq].astype(o_ref.dtype)


def _dsa_attention(q_all, qi_all, kiw_f32, k_bf, vaug_bf, ki_bf, *, batch, q_len, k_len, n_valid_keys, p_len, tq,
                   n_heads, n_kv_heads, n_idx_heads, n_sel):
    width = n_heads * HEAD_DIM
    assert qi_all.shape[1] == n_idx_heads * LANES
    nq = q_len // tq
    tl = max(tq, LANES)
    n_chunks = k_len // KEY_CHUNK
    rep = n_heads // n_kv_heads
    kv_w = n_kv_heads * HEAD_DIM
    return pl.pallas_call(
        functools.partial(_dsa_kernel, tq=tq, tl=tl, n_sel=n_sel, p_len=p_len, n_valid_keys=n_valid_keys,
                          all_keys_valid=n_valid_keys == k_len,
                          n_idx_heads=n_idx_heads, n_kv_heads=n_kv_heads, rep=rep),
        grid=(batch, nq),
        in_specs=[pl.BlockSpec((tq, width), lambda b, i: (b * nq + i, 0)),
                  pl.BlockSpec((tq, n_idx_heads * LANES), lambda b, i: (b * nq + i, 0)),
                  pl.BlockSpec((tq, 2 * LANES), lambda b, i: (b * nq + i, 0)),
                  pl.BlockSpec((k_len, kv_w), lambda b, i: (b, 0)),
                  pl.BlockSpec((k_len, 2 * kv_w), lambda b, i: (b, 0)),
                  pl.BlockSpec((k_len, LANES), lambda b, i: (b, 0))],
        out_specs=pl.BlockSpec((tq, width), lambda b, i: (b * nq + i, 0)),
        out_shape=jax.ShapeDtypeStruct((batch * q_len, width), BF16),
        scratch_shapes=[pltpu.VMEM((HEAD_DIM, n_idx_heads * tl), BF16),
                        pltpu.VMEM((n_kv_heads, HEAD_DIM, rep * tl), BF16),
                        pltpu.VMEM((n_chunks, KEY_CHUNK, tl), jnp.int32),
                        pltpu.VMEM((n_chunks, KEY_CHUNK // 2, tl), jnp.int32),
                        pltpu.VMEM((n_chunks, KEY_CHUNK // 2, tl), jnp.int32),
                        pltpu.VMEM((n_kv_heads, rep * tl, 2 * LANES), F32)],
        compiler_params=_params(("arbitrary", "arbitrary")),
        name="dsa_attention",
    )(q_all, qi_all, kiw_f32, k_bf, vaug_bf, ki_bf)


def _diff_kernel(dq_ref, dk_ref, dv_ref, *rest, tq, tl, hp, n_chunks_total, n_tail, p_len, n_valid_keys,
                 all_keys_valid, lam_init):
    if n_tail:
        dkt_ref, dvt_ref = rest[:2]
        rest = rest[2:]
    lq1_ref, lk1_ref, lq2_ref, lk2_ref, g_ref, o_ref, acc_scr = rest
    q_start = p_len + pl.program_id(2) * tq
    n_ch = _num_key_chunks(q_start, tq, n_valid_keys, n_chunks_total)
    hw = 2 * HEAD_DIM
    n_maps = 2 * hp

    lam = (jnp.exp(jnp.sum(lq1_ref[...] * lk1_ref[...], axis=1, keepdims=True))
           - jnp.exp(jnp.sum(lq2_ref[...] * lk2_ref[...], axis=1, keepdims=True)) + lam_init)
    qt = [_t_bf16(dq_ref[:, c * LANES:(c + 1) * LANES], tl) for c in range(n_maps)]

    acc_scr[...] = jnp.zeros_like(acc_scr)

    def update(carry, key_blocks, value_blocks, key_start, n_valid, all_valid, masked):
        new = []
        for c in range(n_maps):
            m, l = carry[2 * c:2 * c + 2]
            s = jnp.dot(key_blocks[c], qt[c], preferred_element_type=F32)
            if masked:
                s = _fill_invisible(s, NEG_BIG, q_start, tq, key_start, n_valid, all_valid)
            m_new = jnp.maximum(m, _col_reduce(s, jnp.maximum))
            p = jnp.exp2(s - m_new)
            alpha = jnp.exp2(m - m_new)
            alpha_c = _row_to_cols(alpha)
            acc_scr[c] = (acc_scr[c] * jnp.concatenate([alpha_c, alpha_c], axis=1)
                          + lax.dot_general(p.astype(BF16), value_blocks[c // 2], (((0,), (0,)), ((), ())),
                                            preferred_element_type=F32))
            new += [m_new, alpha * l + _col_reduce(p, jnp.add)]
        return tuple(new)

    def att_body(j, carry, masked):
        k0 = pl.multiple_of(j * KEY_CHUNK, KEY_CHUNK)
        keys = [dk_ref[pl.ds(k0, KEY_CHUNK), c * LANES:(c + 1) * LANES] for c in range(n_maps)]
        values = [dv_ref[pl.ds(k0, KEY_CHUNK), h * hw:(h + 1) * hw] for h in range(hp)]
        return update(carry, keys, values, k0, n_valid_keys, all_keys_valid, masked)

    one = (jnp.full((1, tl), NEG_BIG, F32), jnp.zeros((1, tl), F32))
    n_full = jnp.minimum(_num_full_chunks(q_start, n_valid_keys), n_ch)
    carry = lax.fori_loop(0, n_full, functools.partial(att_body, masked=False), one * n_maps)
    carry = lax.fori_loop(n_full, n_ch, functools.partial(att_body, masked=True), carry)
    if n_tail:
        k_main = n_chunks_total * KEY_CHUNK
        keys = [dkt_ref[:, c * LANES:(c + 1) * LANES] for c in range(n_maps)]
        values = [dvt_ref[:, h * hw:(h + 1) * hw] for h in range(hp)]
        carry = update(carry, keys, values, k_main, k_main + n_tail, True, True)

    def normalised(c):
        l_cols = _row_to_cols(carry[2 * c + 1])
        return acc_scr[c] / jnp.concatenate([l_cols, l_cols], axis=1)

    for h in range(hp):
        o = (normalised(2 * h) - lam * normalised(2 * h + 1))[:tq]
        o = o * lax.rsqrt(jnp.mean(o * o, axis=-1, keepdims=True) + EPS) * g_ref[...]
        o_ref[:, h * hw:(h + 1) * hw] = (o * (1.0 - lam_init)).astype(o_ref.dtype)


def _diff_attention(dq_all, dk_bf, dv_bf, lams, g_subln, *, batch, q_len, k_len, n_valid_keys, p_len, tq,
                    n_heads, lam_init, tail=None):
    hw = 2 * HEAD_DIM
    hp = 2 if n_heads % 2 == 0 else 1
    width = n_heads * hw
    assert dq_all.shape[1] == width
    nq = q_len // tq
    tl = max(tq, LANES)
    n_chunks = k_len // KEY_CHUNK
    n_tail = 0 if tail is None else q_len
    assert tail is None or (n_valid_keys == k_len and nq == 1)
    vec = pl.BlockSpec((1, HEAD_DIM), lambda b, h, i: (0, 0))
    kv_spec = pl.BlockSpec((k_len, hp * hw), lambda b, h, i: (b, h))
    tail_specs = [] if tail is None else [pl.BlockSpec((n_tail, hp * hw), lambda b, h, i: (b, h))] * 2
    return pl.pallas_call(
        functools.partial(_diff_kernel, tq=tq, tl=tl, hp=hp, n_chunks_total=n_chunks, n_tail=n_tail, p_len=p_len,
                          n_valid_keys=n_valid_keys, all_keys_valid=n_valid_keys == k_len, lam_init=lam_init),
        grid=(batch, n_heads // hp, nq),
        in_specs=[pl.BlockSpec((tq, hp * hw), lambda b, h, i: (b * nq + i, h)), kv_spec, kv_spec] + tail_specs
                 + [vec, vec, vec, vec, pl.BlockSpec((1, hw), lambda b, h, i: (0, 0))],
        out_specs=pl.BlockSpec((tq, hp * hw), lambda b, h, i: (b * nq + i, h)),
        out_shape=jax.ShapeDtypeStruct((batch * q_len, width), BF16),
        scratch_shapes=[pltpu.VMEM((2 * hp, tl, hw), F32)],
        compiler_params=_params(("arbitrary", "arbitrary", "arbitrary")),
        name="diff_attention",
    )(dq_all, dk_bf, dv_bf, *(tail or ()), *[v.reshape(1, HEAD_DIM) for v in lams], g_subln.reshape(1, hw))


def _outproj_kernel(a_ref, d_ref, wa_ref, wd_ref, x_ref, ga_ref, o_ref, *, per_token):
    mix = (jnp.dot(a_ref[...], wa_ref[...], preferred_element_type=F32)
           + jnp.dot(d_ref[...], wd_ref[...], preferred_element_type=F32))
    o_ref[...] = x_ref[...] + _mod_rows(ga_ref, per_token) * mix


def _outproj(a_out, d_out, w_out_bf, x2, mod, *, per_token, rows_per_batch, tm, tn):
    t, d = x2.shape
    wa = a_out.shape[1]
    wd = d_out.shape[1]
    assert wa == wd
    cb = 2 * (d // tn)
    return pl.pallas_call(
        functools.partial(_outproj_kernel, per_token=per_token),
        grid=(t // tm, d // tn),
        in_specs=[pl.BlockSpec((tm, wa), lambda m, n: (m, 0)),
                  pl.BlockSpec((tm, wd), lambda m, n: (m, 0)),
                  pl.BlockSpec((wa, tn), lambda m, n: (0, n)),
                  pl.BlockSpec((wd, tn), lambda m, n: (1, n)),
                  pl.BlockSpec((tm, tn), lambda m, n: (m, n)),
                  _mod_spec(mod, per_token, tm, tn, cb, rows_per_batch)],
        out_specs=pl.BlockSpec((tm, tn), lambda m, n: (m, n)),
        out_shape=jax.ShapeDtypeStruct((t, d), F32),
        compiler_params=_params(("arbitrary", "arbitrary")),
        name="outproj",
    )(a_out, d_out, w_out_bf, w_out_bf, x2, mod)


def _top_rows(x, k, with_rank=False):
    tops = []
    rank = jnp.full(x.shape, float(k), F32) if with_rank else None
    for i in range(k):
        mx = jnp.max(x, axis=0, keepdims=True)
        tops.append(mx)
        hit = x == mx
        if with_rank:
            rank = jnp.where(hit, float(i), rank)
        x = jnp.where(hit, -jnp.inf, x)
    return (tops, rank) if with_rank else tops


def _pack_bf16_pair(lo, hi):
    def bits(x):
        b = pltpu.bitcast(x, jnp.uint32)
        return (b + jnp.uint32(0x7FFF) + ((b >> 16) & jnp.uint32(1))) >> 16
    return bits(lo) | (bits(hi) << 16)


def _peer_route_kernel(h_ref, wq_ref, keys_ref, cnt_ref, e1_ref, rank_ref, e2_ref, q_scr, s1_scr, top_scr):
    hc = pl.program_id(1)
    c = hc % 2
    half = keys_ref.shape[2]

    @pl.when(hc == 0)
    def _():
        q_scr[...] = jnp.dot(wq_ref[...], h_ref[...], preferred_element_type=F32).astype(BF16)

    q_t = q_scr[pl.ds(pl.multiple_of(hc * half, half), half), :]
    s_t = jnp.dot(keys_ref[0].astype(BF16), q_t, preferred_element_type=F32)

    @pl.when(c == 0)
    def _():
        s1_scr[...] = s_t
        top_scr[...] = jnp.concatenate(_top_rows(s_t, PEER_TOPK), axis=0)

    @pl.when(c == 1)
    def _():
        tops2, rank2 = _top_rows(s_t, PEER_TOPK, with_rank=True)
        tops = jnp.concatenate(tops2, axis=0)
        top1 = top_scr[...]
        cand = jnp.concatenate([top1[0:1, :] + tops]
                               + [top1[i:i + 1, :] + tops[:PEER_TOPK // 2] for i in range(1, PEER_TOPK)], axis=0)
        best = _top_rows(cand, PEER_TOPK)
        m = best[0]
        z = jnp.zeros_like(m)
        for bk in best:
            z = z + jnp.exp(bk - m)
        thr = best[PEER_TOPK - 1]
        s1 = s1_scr[...]
        cnt = jnp.zeros_like(s1)
        for j in range(PEER_TOPK):
            cnt = cnt + jnp.where(s1 + tops2[j] >= thr, 1.0, 0.0)
        e1 = jnp.exp(s1 - top1[0:1, :]) * (0.5 / z)
        e2 = jnp.exp(s_t - tops2[0])
        hn = s_t.shape[0] // 2
        cnt_ref[0] = _pack_bf16_pair(cnt, cnt)
        e1_ref[0] = _pack_bf16_pair(e1, e1)
        rank_ref[0] = _pack_bf16_pair(rank2[:hn], rank2[hn:])
        e2_ref[0] = _pack_bf16_pair(e2[:hn], e2[hn:])


def _peer_route(h_t, wq_t_bf, sub_keys, *, tm):
    d, t = h_t.shape
    heads, _, n_keys, half = sub_keys.shape
    keys2 = sub_keys.reshape(heads * 2, n_keys, half)
    a_spec = pl.BlockSpec((1, n_keys, tm), lambda m, hc: (hc // 2, 0, m))
    b_spec = pl.BlockSpec((1, n_keys // 2, tm), lambda m, hc: (hc // 2, 0, m))
    a_tab = jax.ShapeDtypeStruct((heads, n_keys, t), jnp.uint32)
    b_tab = jax.ShapeDtypeStruct((heads, n_keys // 2, t), jnp.uint32)
    return pl.pallas_call(
        _peer_route_kernel,
        grid=(t // tm, heads * 2),
        in_specs=[pl.BlockSpec((d, tm), lambda m, hc: (0, m)),
                  pl.BlockSpec((heads * 2 * half, d), lambda m, hc: (0, 0), pipeline_mode=pl.Buffered(1)),
                  pl.BlockSpec((1, n_keys, half), lambda m, hc: (hc, 0, 0))],
        out_specs=[a_spec, a_spec, b_spec, b_spec],
        out_shape=[a_tab, a_tab, b_tab, b_tab],
        scratch_shapes=[pltpu.VMEM((heads * 2 * half, tm), BF16), pltpu.VMEM((n_keys, tm), F32),
                        pltpu.VMEM((PEER_TOPK, tm), F32)],
        compiler_params=_params(("arbitrary", "arbitrary")),
        name="peer_route",
    )(h_t, wq_t_bf, keys2)


def _gated_gelu(x, half_gate):
    c = 0.7978845608028654
    inner = x * (c + (c * 0.044715) * (x * x))
    return (x * half_gate) * (1.0 + jnp.tanh(inner))


def _rows_bf16(row_words, n_rows):
    tile = jnp.broadcast_to(row_words, (8, row_words.shape[1]))
    return pltpu.bitcast(jnp.concatenate([tile] * (n_rows // 16), axis=0), BF16)


def _peer_ffn_kernel(h_ref, u_ref, vt_ref, cnt_ref, e1_ref, rank_ref, e2_ref, o_ref, *, n_keys):
    e = pl.program_id(1)
    te, tm = u_ref.shape[0], h_ref.shape[1]
    heads = cnt_ref.shape[0]
    a0 = e * (te // n_keys)

    @pl.when(e == 0)
    def _():
        o_ref[...] = jnp.zeros_like(o_ref)

    act = jnp.dot(u_ref[...], h_ref[...], preferred_element_type=F32)
    strip = min(tm, LANES)
    hn = n_keys // 2
    w_rows = [[None] * (tm // strip) for _ in range(2 * te // n_keys)]
    for ai in range(te // n_keys):
        cnt_a = [cnt_ref[h, pl.ds(a0 + ai, 1), :] for h in range(heads)]
        e1_a = [e1_ref[h, pl.ds(a0 + ai, 1), :] for h in range(heads)]
        for c in range(tm // strip):
            cols = slice(c * strip, (c + 1) * strip)
            gate = jnp.zeros((n_keys, strip), BF16)
            for h in range(heads):
                cnt_b = _rows_bf16(cnt_a[h][:, cols], n_keys)
                e1_b = _rows_bf16(e1_a[h][:, cols], n_keys)
                keep = pltpu.bitcast(rank_ref[h, :, cols], BF16) < cnt_b
                gate = gate + jnp.where(keep, pltpu.bitcast(e2_ref[h, :, cols], BF16) * e1_b,
                                        jnp.zeros_like(e1_b))
            words = pltpu.bitcast(gate, jnp.uint32)
            halves = (pltpu.bitcast(words << 16, F32), pltpu.bitcast(words & jnp.uint32(0xFFFF0000), F32))
            for k, g in enumerate(halves):
                rows = slice(ai * n_keys + k * hn, ai * n_keys + (k + 1) * hn)
                w_rows[2 * ai + k][c] = _gated_gelu(act[rows, cols], g).astype(BF16)
    w = jnp.concatenate([r[0] if len(r) == 1 else jnp.concatenate(r, axis=1) for r in w_rows], axis=0)
    o_ref[...] += jnp.dot(vt_ref[...], w, preferred_element_type=F32)


def _peer_ffn(h_t, u_bf, vt_bf, cnt, e1, rank, e2, *, tm, te):
    d, t = h_t.shape
    n_exp = u_bf.shape[0]
    n_blocks = n_exp // te
    heads, n_keys, _ = cnt.shape
    once = pl.Buffered(1)
    a_spec = pl.BlockSpec((heads, n_keys, tm), lambda m, e: (0, 0, m), pipeline_mode=once)
    b_spec = pl.BlockSpec((heads, n_keys // 2, tm), lambda m, e: (0, 0, m), pipeline_mode=once)
    return pl.pallas_call(
        functools.partial(_peer_ffn_kernel, n_keys=n_keys),
        grid=(t // tm, n_blocks),
        in_specs=[pl.BlockSpec((d, tm), lambda m, e: (0, m), pipeline_mode=once),
                  pl.BlockSpec((te, d), lambda m, e: (e, 0)),
                  pl.BlockSpec((d, te), lambda m, e: (0, e)),
                  a_spec, a_spec, b_spec, b_spec],
        out_specs=pl.BlockSpec((d, tm), lambda m, e: (0, m), pipeline_mode=once),
        out_shape=jax.ShapeDtypeStruct((d, t), F32),
        compiler_params=_params(("arbitrary", "arbitrary"), vmem=PEER_FFN_VMEM),
        name="peer_ffn",
    )(h_t, u_bf, vt_bf, cnt, e1, rank, e2)


def _transpose_cast_kernel(x_ref, o_ref):
    o_ref[...] = x_ref[...].T.astype(o_ref.dtype)


def _transpose_cast(x, dtype):
    r, c = x.shape
    tr, tc = _pick_tile(r, (2048, 1024, 512, 256, 128)), _pick_tile(c, (1024, 512, 256, 128))
    return pl.pallas_call(
        _transpose_cast_kernel,
        grid=(r // tr, c // tc),
        in_specs=[pl.BlockSpec((tr, tc), lambda i, j: (i, j))],
        out_specs=pl.BlockSpec((tc, tr), lambda i, j: (j, i)),
        out_shape=jax.ShapeDtypeStruct((c, r), dtype),
        compiler_params=_params(("arbitrary", "arbitrary")),
        name="transpose_cast",
    )(x)


def _final_kernel(x_ref, pt_ref, ga_ref, g_ref, o_ref, *, per_token, normalize):
    x = x_ref[...] + _mod_rows(ga_ref, per_token) * pt_ref[...].T
    if normalize:
        x = x * lax.rsqrt(jnp.mean(x * x, axis=-1, keepdims=True) + EPS) * g_ref[...]
    o_ref[...] = x


def _final(x1, peer_t, mod, g_final, *, per_token, rows_per_batch, tm, normalize):
    t, d = x1.shape
    row = pl.BlockSpec((tm, d), lambda m: (m, 0))
    return pl.pallas_call(
        functools.partial(_final_kernel, per_token=per_token, normalize=normalize),
        grid=(t // tm,),
        in_specs=[row, pl.BlockSpec((d, tm), lambda m: (0, m)), _mod_spec(mod, per_token, tm, d, 5, rows_per_batch),
                  pl.BlockSpec((1, d), lambda m: (0, 0))],
        out_specs=row,
        out_shape=jax.ShapeDtypeStruct((t, d), F32),
        compiler_params=_params(("arbitrary",)),
        name="final",
    )(x1, peer_t, mod, g_final.reshape(1, d))


def _rope_tables(pos):
    half = HEAD_DIM // 2
    inv = ROPE_THETA ** (-jnp.arange(half, dtype=F32) / half)
    ang = pos.astype(F32)[:, None] * inv[None, :]
    cos, sin = jnp.cos(ang), jnp.sin(ang)
    return jnp.concatenate([cos, cos], axis=1), jnp.concatenate([-sin, sin], axis=1)


def _pick_tile(n, prefs):
    for p in prefs:
        if n % p == 0:
            return p
    return n


def _layer(x, mod_rows, past, layer, last_layer, w, dims):
    b, s, d = x.shape
    t = b * s
    n_heads, n_kv, n_idx, n_diff = dims
    p_len = 0 if past is None else past[0].shape[1]
    n_keys_valid = p_len + s
    n_sel = min(DSA_TOPK, n_keys_valid // 4)
    x2 = x.reshape(t, d)

    per_token = s if s % 256 != 0 else 0
    mod = mod_rows if per_token else mod_rows.reshape(b, 1, 6 * d)
    tm_big = _pick_tile(t if per_token else s, (1024, 512, 256, 128))
    tm_mid = _pick_tile(t if per_token else s, (512, 256, 128))
    tm_small = _pick_tile(t if per_token else s, (256, 128))

    pos = p_len + jnp.arange(s)
    cos, sin = _rope_tables(pos)
    if per_token:
        cos, sin = jnp.tile(cos, (b, 1)), jnp.tile(sin, (b, 1))

    h = _norm_mod(x2, w["g_norm_mix"], mod, which=1, per_token=per_token, rows_per_batch=s, tm=tm_mid,
                  transpose=False)

    qw, kvw, dw = n_heads * HEAD_DIM, n_kv * HEAD_DIM, n_diff * 2 * HEAD_DIM
    head = functools.partial(_proj, h, w["w_head"], cos=cos, sin=sin, tm=tm_big)
    tail = functools.partial(_proj, h, w["w_tail"], cos=cos, sin=sin, tm=tm_big)
    q_scale = HEAD_DIM ** -0.5 * math.log2(math.e)
    wide = 1024
    bf_only = dict(out_f32=False, out_bf16=True, tn=wide)
    both = dict(out_f32=True, out_bf16=True)
    (q_all,) = head(col0=0, n=qw, rope_cols=wide, scale=q_scale, name="proj_q", **bf_only)
    (qi_all,) = head(col0=qw + 2 * kvw, n=n_idx * LANES, rope_cols=wide, scale=q_scale, name="proj_qi", **bf_only)
    (dq_all,) = tail(col0=0, n=dw, rope_cols=wide, scale=q_scale, name="proj_dq", **bf_only)
    k_f, k_b = head(col0=qw, n=kvw, rope_cols=512, tn=512, name="proj_k", **both)
    v_f, v_b = head(col0=qw + kvw, n=kvw, rope_cols=0, tn=512, name="proj_v", **both)
    dk_f, dk_b = tail(col0=dw, n=dw, rope_cols=wide, tn=wide, name="proj_dk", **both)
    dv_f, dv_b = tail(col0=2 * dw, n=dw, rope_cols=0, tn=wide, name="proj_dv", **both)
    kiw_f, kiw_b = head(col0=qw + 2 * kvw + n_idx * LANES, n=2 * LANES, rope_cols=LANES, out_f32=True,
                        out_bf16=True, tn=2 * LANES, name="proj_kiw")

    ki_f = kiw_f[:, :LANES]
    new_rows = (k_f.reshape(b, s, n_kv, HEAD_DIM), v_f.reshape(b, s, n_kv, HEAD_DIM), ki_f.reshape(b, s, LANES),
                dk_f.reshape(b, s, n_diff, 2, HEAD_DIM), dv_f.reshape(b, s, n_diff, 2 * HEAD_DIM))

    diff_keys = None
    if past is None:
        k_len = s
        k_all, v_all, ki_all, dk_all, dv_all = k_b, v_b, kiw_b, dk_b, dv_b
    else:
        k_len = -(-n_keys_valid // KEY_CHUNK) * KEY_CHUNK

        def join(cache, new, width):
            new = new.reshape(b, s, -1)[:, :, :width]
            old = lax.optimization_barrier(cache.reshape(b, p_len, width))
            both_ = jnp.concatenate([old.astype(BF16), new], axis=1)
            both_ = jnp.pad(both_, ((0, 0), (0, k_len - n_keys_valid), (0, 0)))
            return both_.reshape(b * k_len, width)

        k_all = join(past[0], k_b, n_kv * HEAD_DIM)
        v_all = join(past[1], v_b, n_kv * HEAD_DIM)
        ki_all = join(past[2], kiw_b, LANES)
        if p_len % KEY_CHUNK == 0 and s % 16 == 0 and s <= 256:
            def flat(cache):
                return lax.optimization_barrier(cache.reshape(b, p_len, dw)).astype(BF16).reshape(b * p_len, dw)
            diff_keys = dict(dk=flat(past[3]), dv=flat(past[4]), k_len=p_len, n_valid_keys=p_len,
                             tail=(dk_b, dv_b))
            dk_all = dv_all = None
        else:
            dk_all = join(past[3], dk_b, dw)
            dv_all = join(past[4], dv_b, dw)

    ones = jnp.ones((v_all.shape[0], HEAD_DIM), BF16)
    vaug_all = jnp.concatenate(
        [piece for g in range(n_kv) for piece in (v_all[:, g * HEAD_DIM:(g + 1) * HEAD_DIM], ones)], axis=1)
    tq_a = _pick_tile(s, (128, 64, 32))
    a_out = _dsa_attention(q_all, qi_all, kiw_f, k_all, vaug_all, ki_all, batch=b, q_len=s, k_len=k_len,
                           n_valid_keys=n_keys_valid, p_len=p_len, tq=tq_a, n_heads=n_heads, n_kv_heads=n_kv,
                           n_idx_heads=n_idx, n_sel=n_sel)
    tq_d = _pick_tile(s, (256, 128, 64, 32))
    if diff_keys is None:
        diff_keys = dict(dk=dk_all, dv=dv_all, k_len=k_len, n_valid_keys=n_keys_valid, tail=None)
    d_out = _diff_attention(dq_all, diff_keys["dk"], diff_keys["dv"], w["lams"], w["g_subln"], batch=b, q_len=s,
                            k_len=diff_keys["k_len"], n_valid_keys=diff_keys["n_valid_keys"], p_len=p_len,
                            tq=tq_d, n_heads=n_diff, lam_init=_lambda_init(layer), tail=diff_keys["tail"])

    x1 = _outproj(a_out, d_out, w["w_out"], x2, mod, per_token=per_token, rows_per_batch=s, tm=tm_big,
                  tn=_pick_tile(d, (1024, 512)))

    h2_t = _norm_mod(x1, w["g_norm_ffn"], mod, which=2, per_token=per_token, rows_per_batch=s, tm=tm_mid,
                     transpose=True)
    cnt, e1, rank, e2 = _peer_route(h2_t, w["peer_wq_t"], w["peer_keys"], tm=tm_mid)
    peer_t = _peer_ffn(h2_t, w["peer_u"], w["peer_v_t"], cnt, e1, rank, e2, tm=tm_mid, te=1024)
    x_out = _final(x1, peer_t, mod, w["g_final"], per_token=per_token, rows_per_batch=s, tm=tm_small,
                   normalize=last_layer)
    return x_out.reshape(b, s, d), new_rows


def kernel(x_prompt, x_sample, cache_dsa_k, cache_dsa_v, cache_idx_k, cache_diff_k, cache_diff_v, c_prompt, c_sample, w_ada, b_ada, g_norm_mix, g_norm_ffn, w_in, diff_lambda_q1, diff_lambda_k1, diff_lambda_q2, diff_lambda_k2, g_diff_subln, w_out, peer_w_query, peer_sub_keys, peer_u, peer_v, g_final):
    depth = w_in.shape[0]
    bp, bs = x_prompt.shape[0], x_sample.shape[0]
    n_kv = cache_dsa_k.shape[3]
    n_diff = cache_diff_k.shape[3]
    n_heads, n_idx_heads = DSA_HEADS, IDX_HEADS
    qw, kvw, dw = n_heads * HEAD_DIM, n_kv * HEAD_DIM, n_diff * 2 * HEAD_DIM
    tail0 = qw + 2 * kvw + n_idx_heads * LANES + LANES + n_idx_heads
    assert w_in.shape[2] == tail0 + 3 * dw
    dims = (n_heads, n_kv, n_idx_heads, n_diff)

    hp, hs = x_prompt, x_sample
    rows_p, rows_s = [], []
    n_c = bp + bs
    c_pad = jnp.pad(jnp.concatenate([c_prompt, c_sample], axis=0), ((0, (-n_c) % 16), (0, 0)))
    for l in range(depth):
        mod_all = _adaln(c_pad, w_ada[l], b_ada[l])
        w_in_bf = w_in[l].astype(BF16)
        weights = {
            "w_head": w_in_bf,
            "w_tail": w_in_bf[:, tail0:],
            "w_out": w_out[l].astype(BF16),
            "peer_wq_t": _transpose_cast(peer_w_query[l], BF16),
            "peer_keys": peer_sub_keys[l],
            "peer_u": peer_u[l].astype(BF16), "peer_v_t": _transpose_cast(peer_v[l], BF16),
            "g_norm_mix": g_norm_mix[l], "g_norm_ffn": g_norm_ffn[l], "g_subln": g_diff_subln[l],
            "lams": (diff_lambda_q1[l], diff_lambda_k1[l], diff_lambda_q2[l], diff_lambda_k2[l]),
            "g_final": g_final,
        }
        last = l == depth - 1
        hp, rp = _layer(hp, mod_all[:bp], None, l, last, weights, dims)
        past = (cache_dsa_k[l], cache_dsa_v[l], cache_idx_k[l], cache_diff_k[l], cache_diff_v[l])
        hs, rs = _layer(hs, mod_all[bp:bp + bs], past, l, last, weights, dims)
        rows_p.append(rp)
        rows_s.append(rs)
    stack = lambda rows, i: jnp.stack([r[i] for r in rows])
    return (hp, hs) + tuple(stack(rows_p, i) for i in range(5)) + tuple(stack(rows_s, i) for i in range(5))
```

```python
import functools
import math

import jax
import jax.numpy as jnp
from jax import lax
from jax.experimental import pallas as pl
from jax.experimental.pallas import tpu as pltpu

CHUNK = 64
HEAD_DIM = 128
ROPE_THETA = 10000.0
EPS = 1e-6
DSA_HEADS = 16
IDX_HEADS = 16
DSA_TOPK = 256
PEER_TOPK = 16
LANES = 128
KEY_CHUNK = 512
NEG_BIG = -1e30
INT_MIN = -(2 ** 31)
KEY_NEG_INF = INT_MIN + 0x7FFFFF
VMEM_LIMIT = 56 * 1024 * 1024
PEER_FFN_VMEM = 62 * 1024 * 1024

BF16 = jnp.bfloat16
F32 = jnp.float32


def _params(sem, vmem=VMEM_LIMIT):
    return pltpu.CompilerParams(dimension_semantics=sem, vmem_limit_bytes=vmem)


def _lambda_init(layer):
    return 0.8 - 0.6 * math.exp(-0.3 * layer)


def _adaln_kernel(c_ref, w_ref, b_ref, o_ref):
    c = c_ref[...]
    a = (c * jax.nn.sigmoid(c)).astype(BF16)
    o_ref[...] = jnp.dot(a, w_ref[...].astype(BF16), preferred_element_type=F32) + b_ref[...]


def _adaln(c_pad, w_ada, b_ada):
    bp, d = c_pad.shape
    n = w_ada.shape[1]
    tn = 512
    return pl.pallas_call(
        _adaln_kernel,
        grid=(n // tn,),
        in_specs=[pl.BlockSpec((bp, d), lambda j: (0, 0)),
                  pl.BlockSpec((d, tn), lambda j: (0, j)),
                  pl.BlockSpec((1, tn), lambda j: (0, j))],
        out_specs=pl.BlockSpec((bp, tn), lambda j: (0, j)),
        out_shape=jax.ShapeDtypeStruct((bp, n), F32),
        compiler_params=_params(("arbitrary",)),
        name="adaln",
    )(c_pad, w_ada, b_ada.reshape(1, n))


def _mod_rows(ref, per_token):
    if not per_token:
        return ref[0]
    v = ref[...]
    return jnp.broadcast_to(v[:, None, :], (v.shape[0], per_token, v.shape[1])).reshape(-1, v.shape[1])


def _norm_mod_kernel(x_ref, g_ref, sc_ref, sh_ref, o_ref, *, per_token, transpose):
    x = x_ref[...]
    y = x * lax.rsqrt(jnp.mean(x * x, axis=-1, keepdims=True) + EPS) * g_ref[...]
    h = y * (1.0 + _mod_rows(sc_ref, per_token)) + _mod_rows(sh_ref, per_token)
    if transpose:
        o_ref[...] = h.T.astype(o_ref.dtype)
    else:
        o_ref[...] = h.astype(o_ref.dtype)


def _mod_spec(mod, per_token, tm, tn, col_block, rows_per_batch):
    if per_token:
        return pl.BlockSpec((tm // per_token, tn), lambda m, n=0, cb=col_block: (m, cb + n))
    tiles_per_batch = rows_per_batch // tm
    return pl.BlockSpec((1, 1, tn), lambda m, n=0, cb=col_block: (m // tiles_per_batch, 0, cb + n))


def _norm_mod(x2, g, mod, *, which, per_token, rows_per_batch, tm, transpose):
    t, d = x2.shape
    sh_blk, sc_blk = (0, 1) if which == 1 else (3, 4)
    out_shape = (d, t) if transpose else (t, d)
    out_spec = pl.BlockSpec((d, tm), lambda m: (0, m)) if transpose else pl.BlockSpec((tm, d), lambda m: (m, 0))
    return pl.pallas_call(
        functools.partial(_norm_mod_kernel, per_token=per_token, transpose=transpose),
        grid=(t // tm,),
        in_specs=[pl.BlockSpec((tm, d), lambda m: (m, 0)),
                  pl.BlockSpec((1, d), lambda m: (0, 0)),
                  _mod_spec(mod, per_token, tm, d, sc_blk, rows_per_batch),
                  _mod_spec(mod, per_token, tm, d, sh_blk, rows_per_batch)],
        out_specs=out_spec,
        out_shape=jax.ShapeDtypeStruct(out_shape, BF16),
        compiler_params=_params(("arbitrary",)),
        name="norm_mod_t" if transpose else "norm_mod",
    )(x2, g.reshape(1, d), mod, mod)


def _rope_tile(acc, cos, sin, n_chunks):
    outs = []
    for j in range(n_chunks):
        xj = acc[:, j * LANES:(j + 1) * LANES]
        outs.append(xj * cos + pltpu.roll(xj, LANES // 2, axis=1) * sin)
    return outs[0] if n_chunks == 1 else jnp.concatenate(outs, axis=1)


def _proj_kernel(*refs, rope_cols, scale, out_f32, out_bf16, out_aug=False):
    h_ref, w_ref, cos_ref, sin_ref = refs[:4]
    outs = refs[4:]
    if out_aug:
        aug_ref, outs = outs[-1], outs[:-1]
    acc = jnp.dot(h_ref[...], w_ref[...], preferred_element_type=F32)
    tn = acc.shape[1]
    if rope_cols:
        roped = _rope_tile(acc[:, :rope_cols], cos_ref[...], sin_ref[...], rope_cols // LANES)
        acc = roped if rope_cols == tn else jnp.concatenate([roped, acc[:, rope_cols:]], axis=1)
    k = 0
    if out_f32:
        outs[k][...] = acc
        k += 1
    if out_bf16:
        outs[k][...] = (acc * scale if scale != 1.0 else acc).astype(BF16)
    if out_aug:
        ones = jnp.ones((acc.shape[0], LANES), BF16)
        for j in range(tn // LANES):
            aug_ref[:, 2 * j * LANES:(2 * j + 1) * LANES] = acc[:, j * LANES:(j + 1) * LANES].astype(BF16)
            aug_ref[:, (2 * j + 1) * LANES:(2 * j + 2) * LANES] = ones


def _proj(h, w, cos, sin, *, col0, n, rope_cols, scale=1.0, out_f32, out_bf16, tm, tn, name, out_aug=False):
    t, d = h.shape
    assert rope_cols in (0, tn) or n == tn
    assert col0 % tn == 0 and n % tn == 0 and col0 + n <= w.shape[1]
    j0 = col0 // tn
    pos_tiles = cos.shape[0] // tm
    out_shape, out_specs = [], []
    for want, dt in ((out_f32, F32), (out_bf16, BF16)):
        if want:
            out_shape.append(jax.ShapeDtypeStruct((t, n), dt))
            out_specs.append(pl.BlockSpec((tm, tn), lambda m, j: (m, j)))
    if out_aug:
        out_shape.append(jax.ShapeDtypeStruct((t, 2 * n), BF16))
        out_specs.append(pl.BlockSpec((tm, 2 * tn), lambda m, j: (m, j)))
    return pl.pallas_call(
        functools.partial(_proj_kernel, rope_cols=rope_cols, scale=scale, out_f32=out_f32, out_bf16=out_bf16,
                          out_aug=out_aug),
        grid=(t // tm, n // tn),
        in_specs=[pl.BlockSpec((tm, d), lambda m, j: (m, 0)),
                  pl.BlockSpec((d, tn), lambda m, j: (0, j0 + j)),
                  pl.BlockSpec((tm, LANES), lambda m, j: (m % pos_tiles, 0)),
                  pl.BlockSpec((tm, LANES), lambda m, j: (m % pos_tiles, 0))],
        out_specs=out_specs,
        out_shape=out_shape,
        compiler_params=_params(("arbitrary", "arbitrary")),
        name=name,
    )(h, w, cos, sin)


def _num_key_chunks(q_start, n_q, n_valid_keys, n_chunks_total):
    last_visible = (((q_start + n_q - 1) >> 6) + 1) * CHUNK
    last_visible = jnp.minimum(last_visible, n_valid_keys)
    return jnp.minimum((last_visible + KEY_CHUNK - 1) // KEY_CHUNK, n_chunks_total)


def _num_full_chunks(q_start, n_valid_keys):
    return jnp.minimum(((q_start >> 6) + 1) * CHUNK, n_valid_keys) // KEY_CHUNK


def _ordered_key(x):
    b = pltpu.bitcast(x, jnp.int32)
    return b ^ ((b >> 31) & jnp.int32(0x7FFFFFFF))


def _fill_invisible(x, fill, q_start, n_q, key_start, n_valid_keys, all_keys_valid):
    n_k, n_lanes = x.shape
    kpos = key_start + lax.broadcasted_iota(jnp.int32, (n_k, n_lanes), 0)
    qpos = q_start + jnp.minimum(lax.broadcasted_iota(jnp.int32, (1, n_lanes), 1), n_q - 1)
    x = jnp.where((kpos >> 6) <= (qpos >> 6), x, fill)
    return x if all_keys_valid else jnp.where(kpos < n_valid_keys, x, fill)


def _t_bf16(x, n_lanes):
    x = x.astype(F32)
    if x.shape[0] < n_lanes:
        x = jnp.concatenate([x, jnp.zeros((n_lanes - x.shape[0], x.shape[1]), F32)], axis=0)
    return x.T.astype(BF16)


def _rows_i16(words, n_word_rows):
    tile = jnp.broadcast_to(words, (8, words.shape[1]))
    return pltpu.bitcast(jnp.concatenate([tile] * (n_word_rows // 8), axis=0), jnp.int16)


def _col_partial(x, op, group=8):
    parts = [x[r * group:(r + 1) * group] for r in range(x.shape[0] // group)]
    while len(parts) > 1:
        parts = [op(parts[i], parts[i + 1]) for i in range(0, len(parts) - 1, 2)] + parts[len(parts) & ~1:]
    return parts[0]


def _col_reduce(x, op):
    return (jnp.max if op is jnp.maximum else jnp.sum)(_col_partial(x, op), axis=0, keepdims=True)


def _row_to_cols(row):
    return jnp.broadcast_to(row, (LANES, row.shape[1])).T


def _dsa_kernel(q_ref, qi_ref, wi_ref, k_ref, vaug_ref, ki_ref, o_ref,
                qit_scr, qt_scr, key_scr, hi_scr, lo_scr, acc_scr, *,
                tq, tl, n_sel, p_len, n_valid_keys, all_keys_valid, n_idx_heads, n_kv_heads, rep):
    n_chunks_total = key_scr.shape[0]
    q_start = p_len + pl.program_id(1) * tq
    n_ch = _num_key_chunks(q_start, tq, n_valid_keys, n_chunks_total)
    heads_per_dot = 4
    half_chunk = KEY_CHUNK // 2

    for h in range(n_idx_heads):
        qit_scr[:, h * tl:(h + 1) * tl] = _t_bf16(qi_ref[:, h * LANES:(h + 1) * LANES], tl)
    for g in range(n_kv_heads):
        for r in range(rep):
            hh = g * rep + r
            qt_scr[g, :, r * tl:(r + 1) * tl] = _t_bf16(q_ref[:, hh * LANES:(hh + 1) * LANES], tl)
    wi = wi_ref[:, LANES:2 * LANES] * (n_idx_heads ** -0.5)
    if tq < tl:
        wi = jnp.concatenate([wi, jnp.zeros((tl - tq, LANES), F32)], axis=0)
    wi_t = wi.T

    def idx_body(j, carry):
        k0 = pl.multiple_of(j * KEY_CHUNK, KEY_CHUNK)
        kib = ki_ref[pl.ds(k0, KEY_CHUNK), :]
        score = jnp.zeros((KEY_CHUNK, tl), F32)
        for h0 in range(0, n_idx_heads, heads_per_dot):
            lg = jnp.dot(kib, qit_scr[:, h0 * tl:(h0 + heads_per_dot) * tl], preferred_element_type=F32)
            for h in range(h0, h0 + heads_per_dot):
                score = score + jnp.maximum(lg[:, (h - h0) * tl:(h - h0 + 1) * tl], 0.0) * wi_t[h:h + 1, :]
        score = _fill_invisible(score, -jnp.inf, q_start, tq, k0, n_valid_keys, all_keys_valid)
        key = _ordered_key(score)
        key_scr[j] = key
        hi16 = (key >> 16) & 0xFFFF
        lo16 = (key & 0xFFFF) ^ 0x8000
        hi_scr[j] = hi16[:half_chunk] | (hi16[half_chunk:] << 16)
        lo_scr[j] = lo16[:half_chunk] | (lo16[half_chunk:] << 16)
        return carry

    lax.fori_loop(0, n_ch, idx_body, 0)

    def count16(scr, pattern, strict):
        c = _rows_i16(pattern | (pattern << 16), half_chunk)

        def body(j, cnt):
            x = pltpu.bitcast(scr[j], jnp.int16)
            ones = jnp.where((x > c) if strict else (x >= c), jnp.int16(1), jnp.int16(0))
            return cnt + _col_partial(ones, jnp.add, group=16)

        cnt = pltpu.bitcast(lax.fori_loop(0, n_ch, body, jnp.zeros((16, tl), jnp.int16)), jnp.int32)
        return jnp.sum((cnt & 0xFFFF) + ((cnt >> 16) & 0xFFFF), axis=0, keepdims=True)

    def bisect16(scr, need):
        def bit_body(b, t):
            cand = t | lax.shift_left(jnp.int32(1), 15 - b)
            return jnp.where(count16(scr, cand ^ 0x8000, False) >= need, cand, t)
        return lax.fori_loop(0, 16, bit_body, jnp.zeros((1, tl), jnp.int32))

    hi_pat = bisect16(hi_scr, n_sel) ^ 0x8000
    need = n_sel - count16(hi_scr, hi_pat, True)

    def keep_bucket(j, carry):
        hi = pltpu.bitcast(hi_scr[j], jnp.int16)
        lo = pltpu.bitcast(lo_scr[j], jnp.int16)
        same = hi == _rows_i16(hi_pat | (hi_pat << 16), half_chunk)
        lo_scr[j] = pltpu.bitcast(jnp.where(same, lo, jnp.int16(-2 ** 15)), jnp.int32)
        return carry

    lax.fori_loop(0, n_ch, keep_bucket, 0)
    lo_u = bisect16(lo_scr, need)
    thr_raw = (((hi_pat << 16) >> 16) << 16) | lo_u
    thr = jnp.maximum(thr_raw, KEY_NEG_INF + 1)

    n_above = n_sel - need
    n_ge = n_above + count16(lo_scr, lo_u ^ 0x8000, False)
    surplus = jnp.where(thr_raw > KEY_NEG_INF, n_ge - n_sel, 0)

    @pl.when(jnp.max(surplus) > 0)
    def _():
        take = n_sel - (n_above + count16(lo_scr, lo_u ^ 0x8000, True))

        def positions(j):
            return j * KEY_CHUNK + lax.broadcasted_iota(jnp.int32, (KEY_CHUNK, tl), 0)

        def tied_before(limit):
            def body(j, cnt):
                ones = jnp.where(key_scr[j] == thr, jnp.where(positions(j) < limit, 1, 0), 0)
                return cnt + _col_partial(ones, jnp.add)
            cnt = lax.fori_loop(0, n_ch, body, jnp.zeros((8, tl), jnp.int32))
            return jnp.sum(cnt, axis=0, keepdims=True)

        n_bits = (n_chunks_total * KEY_CHUNK).bit_length()

        def bit_body(b, t):
            cand = t | lax.shift_left(jnp.int32(1), n_bits - 1 - b)
            return jnp.where(tied_before(cand) < take, cand, t)

        last = lax.fori_loop(0, n_bits, bit_body, jnp.zeros((1, tl), jnp.int32))

        def drop_body(j, carry):
            key = key_scr[j]
            key_scr[j] = jnp.where(key == thr, jnp.where(positions(j) > last, KEY_NEG_INF, key), key)
            return carry

        lax.fori_loop(0, n_ch, drop_body, 0)

    acc_scr[...] = jnp.zeros_like(acc_scr)

    def att_body(j, ms):
        k0 = pl.multiple_of(j * KEY_CHUNK, KEY_CHUNK)
        bias = jnp.where(key_scr[j] >= thr, 0.0, NEG_BIG)
        bias = jnp.concatenate([bias] * rep, axis=1)
        new = []
        for g in range(n_kv_heads):
            kb = k_ref[pl.ds(k0, KEY_CHUNK), g * LANES:(g + 1) * LANES]
            s = jnp.dot(kb, qt_scr[g], preferred_element_type=F32) + bias
            m_new = jnp.maximum(ms[g], _col_reduce(s, jnp.maximum))
            p = jnp.exp2(s - m_new).astype(BF16)
            va = vaug_ref[pl.ds(k0, KEY_CHUNK), g * 2 * LANES:(g + 1) * 2 * LANES]
            alpha = _row_to_cols(jnp.exp2(ms[g] - m_new))
            acc_scr[g] = (acc_scr[g] * jnp.concatenate([alpha, alpha], axis=1)
                          + lax.dot_general(p, va, (((0,), (0,)), ((), ())), preferred_element_type=F32))
            new.append(m_new)
        return tuple(new)

    lax.fori_loop(0, n_ch, att_body, tuple(jnp.full((1, rep * tl), NEG_BIG, F32) for _ in range(n_kv_heads)))

    for g in range(n_kv_heads):
        acc = acc_scr[g]
        o = acc[:, :LANES] / acc[:, LANES:]
        for r in range(rep):
            hh = g * rep + r
            o_ref[:, hh * LANES:(hh + 1) * LANES] = o[r * tl:r * tl + tq].astype(o_ref.dtype)


def _dsa_attention(q_all, qi_all, kiw_f32, k_bf, vaug_bf, ki_bf, *, batch, q_len, k_len, n_valid_keys, p_len, tq,
                   n_heads, n_kv_heads, n_idx_heads, n_sel):
    width = n_heads * HEAD_DIM
    assert qi_all.shape[1] == n_idx_heads * LANES
    nq = q_len // tq
    tl = max(tq, LANES)
    n_chunks = k_len // KEY_CHUNK
    rep = n_heads // n_kv_heads
    kv_w = n_kv_heads * HEAD_DIM
    return pl.pallas_call(
        functools.partial(_dsa_kernel, tq=tq, tl=tl, n_sel=n_sel, p_len=p_len, n_valid_keys=n_valid_keys,
                          all_keys_valid=n_valid_keys == k_len,
                          n_idx_heads=n_idx_heads, n_kv_heads=n_kv_heads, rep=rep),
        grid=(batch, nq),
        in_specs=[pl.BlockSpec((tq, width), lambda b, i: (b * nq + i, 0)),
                  pl.BlockSpec((tq, n_idx_heads * LANES), lambda b, i: (b * nq + i, 0)),
                  pl.BlockSpec((tq, 2 * LANES), lambda b, i: (b * nq + i, 0)),
                  pl.BlockSpec((k_len, kv_w), lambda b, i: (b, 0)),
                  pl.BlockSpec((k_len, 2 * kv_w), lambda b, i: (b, 0)),
                  pl.BlockSpec((k_len, LANES), lambda b, i: (b, 0))],
        out_specs=pl.BlockSpec((tq, width), lambda b, i: (b * nq + i, 0)),
        out_shape=jax.ShapeDtypeStruct((batch * q_len, width), BF16),
        scratch_shapes=[pltpu.VMEM((HEAD_DIM, n_idx_heads * tl), BF16),
                        pltpu.VMEM((n_kv_heads, HEAD_DIM, rep * tl), BF16),
                        pltpu.VMEM((n_chunks, KEY_CHUNK, tl), jnp.int32),
                        pltpu.VMEM((n_chunks, KEY_CHUNK // 2, tl), jnp.int32),
                        pltpu.VMEM((n_chunks, KEY_CHUNK // 2, tl), jnp.int32),
                        pltpu.VMEM((n_kv_heads, rep * tl, 2 * LANES), F32)],
        compiler_params=_params(("arbitrary", "arbitrary")),
        name="dsa_attention",
    )(q_all, qi_all, kiw_f32, k_bf, vaug_bf, ki_bf)


def _diff_kernel(dq_ref, dk_ref, dv_ref, *rest, tq, tl, hp, n_chunks_total, n_tail, p_len, n_valid_keys,
                 all_keys_valid, lam_init):
    if n_tail:
        dkt_ref, dvt_ref = rest[:2]
        rest = rest[2:]
    lq1_ref, lk1_ref, lq2_ref, lk2_ref, g_ref, o_ref, acc_scr = rest
    q_start = p_len + pl.program_id(2) * tq
    n_ch = _num_key_chunks(q_start, tq, n_valid_keys, n_chunks_total)
    hw = 2 * HEAD_DIM
    n_maps = 2 * hp

    lam = (jnp.exp(jnp.sum(lq1_ref[...] * lk1_ref[...], axis=1, keepdims=True))
           - jnp.exp(jnp.sum(lq2_ref[...] * lk2_ref[...], axis=1, keepdims=True)) + lam_init)
    qt = [_t_bf16(dq_ref[:, c * LANES:(c + 1) * LANES], tl) for c in range(n_maps)]

    acc_scr[...] = jnp.zeros_like(acc_scr)

    def update(carry, key_blocks, value_blocks, key_start, n_valid, all_valid, masked):
        new = []
        for c in range(n_maps):
            m, l = carry[2 * c:2 * c + 2]
            s = jnp.dot(key_blocks[c], qt[c], preferred_element_type=F32)
            if masked:
                s = _fill_invisible(s, NEG_BIG, q_start, tq, key_start, n_valid, all_valid)
            m_new = jnp.maximum(m, _col_reduce(s, jnp.maximum))
            p = jnp.exp2(s - m_new)
            alpha = jnp.exp2(m - m_new)
            alpha_c = _row_to_cols(alpha)
            acc_scr[c] = (acc_scr[c] * jnp.concatenate([alpha_c, alpha_c], axis=1)
                          + lax.dot_general(p.astype(BF16), value_blocks[c // 2], (((0,), (0,)), ((), ())),
                                            preferred_element_type=F32))
            new += [m_new, alpha * l + _col_reduce(p, jnp.add)]
        return tuple(new)

    def att_body(j, carry, masked):
        k0 = pl.multiple_of(j * KEY_CHUNK, KEY_CHUNK)
        keys = [dk_ref[pl.ds(k0, KEY_CHUNK), c * LANES:(c + 1) * LANES] for c in range(n_maps)]
        values = [dv_ref[pl.ds(k0, KEY_CHUNK), h * hw:(h + 1) * hw] for h in range(hp)]
        return update(carry, keys, values, k0, n_valid_keys, all_keys_valid, masked)

    one = (jnp.full((1, tl), NEG_BIG, F32), jnp.zeros((1, tl), F32))
    n_full = jnp.minimum(_num_full_chunks(q_start, n_valid_keys), n_ch)
    carry = lax.fori_loop(0, n_full, functools.partial(att_body, masked=False), one * n_maps)
    carry = lax.fori_loop(n_full, n_ch, functools.partial(att_body, masked=True), carry)
    if n_tail:
        k_main = n_chunks_total * KEY_CHUNK
        keys = [dkt_ref[:, c * LANES:(c + 1) * LANES] for c in range(n_maps)]
        values = [dvt_ref[:, h * hw:(h + 1) * hw] for h in range(hp)]
        carry = update(carry, keys, values, k_main, k_main + n_tail, True, True)

    def normalised(c):
        l_cols = _row_to_cols(carry[2 * c + 1])
        return acc_scr[c] / jnp.concatenate([l_cols, l_cols], axis=1)

    for h in range(hp):
        o = (normalised(2 * h) - lam * normalised(2 * h + 1))[:tq]
        o = o * lax.rsqrt(jnp.mean(o * o, axis=-1, keepdims=True) + EPS) * g_ref[...]
        o_ref[:, h * hw:(h + 1) * hw] = (o * (1.0 - lam_init)).astype(o_ref.dtype)


def _diff_attention(dq_all, dk_bf, dv_bf, lams, g_subln, *, batch, q_len, k_len, n_valid_keys, p_len, tq,
                    n_heads, lam_init, tail=None):
    hw = 2 * HEAD_DIM
    hp = 2 if n_heads % 2 == 0 else 1
    width = n_heads * hw
    assert dq_all.shape[1] == width
    nq = q_len // tq
    tl = max(tq, LANES)
    n_chunks = k_len // KEY_CHUNK
    n_tail = 0 if tail is None else q_len
    assert tail is None or (n_valid_keys == k_len and nq == 1)
    vec = pl.BlockSpec((1, HEAD_DIM), lambda b, h, i: (0, 0))
    kv_spec = pl.BlockSpec((k_len, hp * hw), lambda b, h, i: (b, h))
    tail_specs = [] if tail is None else [pl.BlockSpec((n_tail, hp * hw), lambda b, h, i: (b, h))] * 2
    return pl.pallas_call(
        functools.partial(_diff_kernel, tq=tq, tl=tl, hp=hp, n_chunks_total=n_chunks, n_tail=n_tail, p_len=p_len,
                          n_valid_keys=n_valid_keys, all_keys_valid=n_valid_keys == k_len, lam_init=lam_init),
        grid=(batch, n_heads // hp, nq),
        in_specs=[pl.BlockSpec((tq, hp * hw), lambda b, h, i: (b * nq + i, h)), kv_spec, kv_spec] + tail_specs
                 + [vec, vec, vec, vec, pl.BlockSpec((1, hw), lambda b, h, i: (0, 0))],
        out_specs=pl.BlockSpec((tq, hp * hw), lambda b, h, i: (b * nq + i, h)),
        out_shape=jax.ShapeDtypeStruct((batch * q_len, width), BF16),
        scratch_shapes=[pltpu.VMEM((2 * hp, tl, hw), F32)],
        compiler_params=_params(("arbitrary", "arbitrary", "arbitrary")),
        name="diff_attention",
    )(dq_all, dk_bf, dv_bf, *(tail or ()), *[v.reshape(1, HEAD_DIM) for v in lams], g_subln.reshape(1, hw))


def _outproj_kernel(a_ref, d_ref, wa_ref, wd_ref, x_ref, ga_ref, o_ref, *, per_token):
    mix = (jnp.dot(a_ref[...], wa_ref[...], preferred_element_type=F32)
           + jnp.dot(d_ref[...], wd_ref[...], preferred_element_type=F32))
    o_ref[...] = x_ref[...] + _mod_rows(ga_ref, per_token) * mix


def _outproj(a_out, d_out, w_out_bf, x2, mod, *, per_token, rows_per_batch, tm, tn):
    t, d = x2.shape
    wa = a_out.shape[1]
    wd = d_out.shape[1]
    assert wa == wd
    cb = 2 * (d // tn)
    return pl.pallas_call(
        functools.partial(_outproj_kernel, per_token=per_token),
        grid=(t // tm, d // tn),
        in_specs=[pl.BlockSpec((tm, wa), lambda m, n: (m, 0)),
                  pl.BlockSpec((tm, wd), lambda m, n: (m, 0)),
                  pl.BlockSpec((wa, tn), lambda m, n: (0, n)),
                  pl.BlockSpec((wd, tn), lambda m, n: (1, n)),
                  pl.BlockSpec((tm, tn), lambda m, n: (m, n)),
                  _mod_spec(mod, per_token, tm, tn, cb, rows_per_batch)],
        out_specs=pl.BlockSpec((tm, tn), lambda m, n: (m, n)),
        out_shape=jax.ShapeDtypeStruct((t, d), F32),
        compiler_params=_params(("arbitrary", "arbitrary")),
        name="outproj",
    )(a_out, d_out, w_out_bf, w_out_bf, x2, mod)


def _top_rows(x, k, with_rank=False):
    tops = []
    rank = jnp.full(x.shape, float(k), F32) if with_rank else None
    for i in range(k):
        mx = jnp.max(x, axis=0, keepdims=True)
        tops.append(mx)
        hit = x == mx
        if with_rank:
            rank = jnp.where(hit, float(i), rank)
        x = jnp.where(hit, -jnp.inf, x)
    return (tops, rank) if with_rank else tops


def _pack_bf16_pair(lo, hi):
    def bits(x):
        b = pltpu.bitcast(x, jnp.uint32)
        return (b + jnp.uint32(0x7FFF) + ((b >> 16) & jnp.uint32(1))) >> 16
    return bits(lo) | (bits(hi) << 16)


def _peer_route_kernel(h_ref, wq_ref, keys_ref, cnt_ref, e1_ref, rank_ref, e2_ref, q_scr, s1_scr, top_scr):
    hc = pl.program_id(1)
    c = hc % 2
    half = keys_ref.shape[2]

    @pl.when(hc == 0)
    def _():
        q_scr[...] = jnp.dot(wq_ref[...], h_ref[...], preferred_element_type=F32).astype(BF16)

    q_t = q_scr[pl.ds(pl.multiple_of(hc * half, half), half), :]
    s_t = jnp.dot(keys_ref[0].astype(BF16), q_t, preferred_element_type=F32)

    @pl.when(c == 0)
    def _():
        s1_scr[...] = s_t
        top_scr[...] = jnp.concatenate(_top_rows(s_t, PEER_TOPK), axis=0)

    @pl.when(c == 1)
    def _():
        tops2, rank2 = _top_rows(s_t, PEER_TOPK, with_rank=True)
        tops = jnp.concatenate(tops2, axis=0)
        top1 = top_scr[...]
        cand = jnp.concatenate([top1[0:1, :] + tops]
                               + [top1[i:i + 1, :] + tops[:PEER_TOPK // 2] for i in range(1, PEER_TOPK)], axis=0)
        best = _top_rows(cand, PEER_TOPK)
        m = best[0]
        z = jnp.zeros_like(m)
        for bk in best:
            z = z + jnp.exp(bk - m)
        thr = best[PEER_TOPK - 1]
        s1 = s1_scr[...]
        cnt = jnp.zeros_like(s1)
        for j in range(PEER_TOPK):
            cnt = cnt + jnp.where(s1 + tops2[j] >= thr, 1.0, 0.0)
        e1 = jnp.exp(s1 - top1[0:1, :]) * (0.5 / z)
        e2 = jnp.exp(s_t - tops2[0])
        hn = s_t.shape[0] // 2
        cnt_ref[0] = _pack_bf16_pair(cnt, cnt)
        e1_ref[0] = _pack_bf16_pair(e1, e1)
        rank_ref[0] = _pack_bf16_pair(rank2[:hn], rank2[hn:])
        e2_ref[0] = _pack_bf16_pair(e2[:hn], e2[hn:])


def _peer_route(h_t, wq_t_bf, sub_keys, *, tm):
    d, t = h_t.shape
    heads, _, n_keys, half = sub_keys.shape
    keys2 = sub_keys.reshape(heads * 2, n_keys, half)
    a_spec = pl.BlockSpec((1, n_keys, tm), lambda m, hc: (hc // 2, 0, m))
    b_spec = pl.BlockSpec((1, n_keys // 2, tm), lambda m, hc: (hc // 2, 0, m))
    a_tab = jax.ShapeDtypeStruct((heads, n_keys, t), jnp.uint32)
    b_tab = jax.ShapeDtypeStruct((heads, n_keys // 2, t), jnp.uint32)
    return pl.pallas_call(
        _peer_route_kernel,
        grid=(t // tm, heads * 2),
        in_specs=[pl.BlockSpec((d, tm), lambda m, hc: (0, m)),
                  pl.BlockSpec((heads * 2 * half, d), lambda m, hc: (0, 0), pipeline_mode=pl.Buffered(1)),
                  pl.BlockSpec((1, n_keys, half), lambda m, hc: (hc, 0, 0))],
        out_specs=[a_spec, a_spec, b_spec, b_spec],
        out_shape=[a_tab, a_tab, b_tab, b_tab],
        scratch_shapes=[pltpu.VMEM((heads * 2 * half, tm), BF16), pltpu.VMEM((n_keys, tm), F32),
                        pltpu.VMEM((PEER_TOPK, tm), F32)],
        compiler_params=_params(("arbitrary", "arbitrary")),
        name="peer_route",
    )(h_t, wq_t_bf, keys2)


def _gated_gelu(x, half_gate):
    c = 0.7978845608028654
    inner = x * (c + (c * 0.044715) * (x * x))
    return (x * half_gate) * (1.0 + jnp.tanh(inner))


def _rows_bf16(row_words, n_rows):
    tile = jnp.broadcast_to(row_words, (8, row_words.shape[1]))
    return pltpu.bitcast(jnp.concatenate([tile] * (n_rows // 16), axis=0), BF16)


def _peer_ffn_kernel(h_ref, u_ref, vt_ref, cnt_ref, e1_ref, rank_ref, e2_ref, o_ref, *, n_keys):
    e = pl.program_id(1)
    te, tm = u_ref.shape[0], h_ref.shape[1]
    heads = cnt_ref.shape[0]
    a0 = e * (te // n_keys)

    @pl.when(e == 0)
    def _():
        o_ref[...] = jnp.zeros_like(o_ref)

    act = jnp.dot(u_ref[...], h_ref[...], preferred_element_type=F32)
    strip = min(tm, LANES)
    hn = n_keys // 2
    w_rows = [[None] * (tm // strip) for _ in range(2 * te // n_keys)]
    for ai in range(te // n_keys):
        cnt_a = [cnt_ref[h, pl.ds(a0 + ai, 1), :] for h in range(heads)]
        e1_a = [e1_ref[h, pl.ds(a0 + ai, 1), :] for h in range(heads)]
        for c in range(tm // strip):
            cols = slice(c * strip, (c + 1) * strip)
            gate = jnp.zeros((n_keys, strip), BF16)
            for h in range(heads):
                cnt_b = _rows_bf16(cnt_a[h][:, cols], n_keys)
                e1_b = _rows_bf16(e1_a[h][:, cols], n_keys)
                keep = pltpu.bitcast(rank_ref[h, :, cols], BF16) < cnt_b
                gate = gate + jnp.where(keep, pltpu.bitcast(e2_ref[h, :, cols], BF16) * e1_b,
                                        jnp.zeros_like(e1_b))
            words = pltpu.bitcast(gate, jnp.uint32)
            halves = (pltpu.bitcast(words << 16, F32), pltpu.bitcast(words & jnp.uint32(0xFFFF0000), F32))
            for k, g in enumerate(halves):
                rows = slice(ai * n_keys + k * hn, ai * n_keys + (k + 1) * hn)
                w_rows[2 * ai + k][c] = _gated_gelu(act[rows, cols], g).astype(BF16)
    w = jnp.concatenate([r[0] if len(r) == 1 else jnp.concatenate(r, axis=1) for r in w_rows], axis=0)
    o_ref[...] += jnp.dot(vt_ref[...], w, preferred_element_type=F32)


def _peer_ffn(h_t, u_bf, vt_bf, cnt, e1, rank, e2, *, tm, te):
    d, t = h_t.shape
    n_exp = u_bf.shape[0]
    n_blocks = n_exp // te
    heads, n_keys, _ = cnt.shape
    once = pl.Buffered(1)
    a_spec = pl.BlockSpec((heads, n_keys, tm), lambda m, e: (0, 0, m), pipeline_mode=once)
    b_spec = pl.BlockSpec((heads, n_keys // 2, tm), lambda m, e: (0, 0, m), pipeline_mode=once)
    return pl.pallas_call(
        functools.partial(_peer_ffn_kernel, n_keys=n_keys),
        grid=(t // tm, n_blocks),
        in_specs=[pl.BlockSpec((d, tm), lambda m, e: (0, m), pipeline_mode=once),
                  pl.BlockSpec((te, d), lambda m, e: (e, 0)),
                  pl.BlockSpec((d, te), lambda m, e: (0, e)),
                  a_spec, a_spec, b_spec, b_spec],
        out_specs=pl.BlockSpec((d, tm), lambda m, e: (0, m), pipeline_mode=once),
        out_shape=jax.ShapeDtypeStruct((d, t), F32),
        compiler_params=_params(("arbitrary", "arbitrary"), vmem=PEER_FFN_VMEM),
        name="peer_ffn",
    )(h_t, u_bf, vt_bf, cnt, e1, rank, e2)


def _transpose_cast_kernel(x_ref, o_ref):
    o_ref[...] = x_ref[...].T.astype(o_ref.dtype)


def _transpose_cast(x, dtype):
    r, c = x.shape
    tr, tc = _pick_tile(r, (2048, 1024, 512, 256, 128)), _pick_tile(c, (1024, 512, 256, 128))
    return pl.pallas_call(
        _transpose_cast_kernel,
        grid=(r // tr, c // tc),
        in_specs=[pl.BlockSpec((tr, tc), lambda i, j: (i, j))],
        out_specs=pl.BlockSpec((tc, tr), lambda i, j: (j, i)),
        out_shape=jax.ShapeDtypeStruct((c, r), dtype),
        compiler_params=_params(("arbitrary", "arbitrary")),
        name="transpose_cast",
    )(x)


def _final_kernel(x_ref, pt_ref, ga_ref, g_ref, o_ref, *, per_token, normalize):
    x = x_ref[...] + _mod_rows(ga_ref, per_token) * pt_ref[...].T
    if normalize:
        x = x * lax.rsqrt(jnp.mean(x * x, axis=-1, keepdims=True) + EPS) * g_ref[...]
    o_ref[...] = x


def _final(x1, peer_t, mod, g_final, *, per_token, rows_per_batch, tm, normalize):
    t, d = x1.shape
    row = pl.BlockSpec((tm, d), lambda m: (m, 0))
    return pl.pallas_call(
        functools.partial(_final_kernel, per_token=per_token, normalize=normalize),
        grid=(t // tm,),
        in_specs=[row, pl.BlockSpec((d, tm), lambda m: (0, m)), _mod_spec(mod, per_token, tm, d, 5, rows_per_batch),
                  pl.BlockSpec((1, d), lambda m: (0, 0))],
        out_specs=row,
        out_shape=jax.ShapeDtypeStruct((t, d), F32),
        compiler_params=_params(("arbitrary",)),
        name="final",
    )(x1, peer_t, mod, g_final.reshape(1, d))


def _rope_tables(pos):
    half = HEAD_DIM // 2
    inv = ROPE_THETA ** (-jnp.arange(half, dtype=F32) / half)
    ang = pos.astype(F32)[:, None] * inv[None, :]
    cos, sin = jnp.cos(ang), jnp.sin(ang)
    return jnp.concatenate([cos, cos], axis=1), jnp.concatenate([-sin, sin], axis=1)


def _pick_tile(n, prefs):
    for p in prefs:
        if n % p == 0:
            return p
    return n


def _layer(x, mod_rows, past, layer, last_layer, w, dims):
    b, s, d = x.shape
    t = b * s
    n_heads, n_kv, n_idx, n_diff = dims
    p_len = 0 if past is None else past[0].shape[1]
    n_keys_valid = p_len + s
    n_sel = min(DSA_TOPK, n_keys_valid // 4)
    x2 = x.reshape(t, d)

    per_token = s if s % 256 != 0 else 0
    mod = mod_rows if per_token else mod_rows.reshape(b, 1, 6 * d)
    tm_big = _pick_tile(t if per_token else s, (1024, 512, 256, 128))
    tm_mid = _pick_tile(t if per_token else s, (512, 256, 128))
    tm_small = _pick_tile(t if per_token else s, (256, 128))

    pos = p_len + jnp.arange(s)
    cos, sin = _rope_tables(pos)
    if per_token:
        cos, sin = jnp.tile(cos, (b, 1)), jnp.tile(sin, (b, 1))

    h = _norm_mod(x2, w["g_norm_mix"], mod, which=1, per_token=per_token, rows_per_batch=s, tm=tm_mid,
                  transpose=False)

    qw, kvw, dw = n_heads * HEAD_DIM, n_kv * HEAD_DIM, n_diff * 2 * HEAD_DIM
    head = functools.partial(_proj, h, w["w_head"], cos=cos, sin=sin, tm=tm_big)
    tail = functools.partial(_proj, h, w["w_tail"], cos=cos, sin=sin, tm=tm_big)
    q_scale = HEAD_DIM ** -0.5 * math.log2(math.e)
    wide = 1024
    bf_only = dict(out_f32=False, out_bf16=True, tn=wide)
    both = dict(out_f32=True, out_bf16=True)
    (q_all,) = head(col0=0, n=qw, rope_cols=wide, scale=q_scale, name="proj_q", **bf_only)
    (qi_all,) = head(col0=qw + 2 * kvw, n=n_idx * LANES, rope_cols=wide, scale=q_scale, name="proj_qi", **bf_only)
    (dq_all,) = tail(col0=0, n=dw, rope_cols=wide, scale=q_scale, name="proj_dq", **bf_only)
    k_f, k_b = head(col0=qw, n=kvw, rope_cols=512, tn=512, name="proj_k", **both)
    v_f, v_b, v_aug = head(col0=qw + kvw, n=kvw, rope_cols=0, tn=512, name="proj_v", out_aug=True, **both)
    dk_f, dk_b = tail(col0=dw, n=dw, rope_cols=wide, tn=wide, name="proj_dk", **both)
    dv_f, dv_b = tail(col0=2 * dw, n=dw, rope_cols=0, tn=wide, name="proj_dv", **both)
    kiw_f, kiw_b = head(col0=qw + 2 * kvw + n_idx * LANES, n=2 * LANES, rope_cols=LANES, out_f32=True,
                        out_bf16=True, tn=2 * LANES, name="proj_kiw")

    ki_f = kiw_f[:, :LANES]
    new_rows = (k_f.reshape(b, s, n_kv, HEAD_DIM), v_f.reshape(b, s, n_kv, HEAD_DIM), ki_f.reshape(b, s, LANES),
                dk_f.reshape(b, s, n_diff, 2, HEAD_DIM), dv_f.reshape(b, s, n_diff, 2 * HEAD_DIM))

    diff_keys = None
    if past is None:
        k_len = s
        k_all, v_all, ki_all, dk_all, dv_all = k_b, v_b, kiw_b, dk_b, dv_b
    else:
        k_len = -(-n_keys_valid // KEY_CHUNK) * KEY_CHUNK

        def join(cache, new, width):
            new = new.reshape(b, s, -1)[:, :, :width]
            old = lax.optimization_barrier(cache.reshape(b, p_len, width))
            both_ = jnp.concatenate([old.astype(BF16), new], axis=1)
            both_ = jnp.pad(both_, ((0, 0), (0, k_len - n_keys_valid), (0, 0)))
            return both_.reshape(b * k_len, width)

        k_all = join(past[0], k_b, n_kv * HEAD_DIM)
        v_all = join(past[1], v_b, n_kv * HEAD_DIM)
        ki_all = join(past[2], kiw_b, LANES)
        if p_len % KEY_CHUNK == 0 and s % 16 == 0 and s <= 256:
            def flat(cache):
                return lax.optimization_barrier(cache.reshape(b, p_len, dw)).astype(BF16).reshape(b * p_len, dw)
            diff_keys = dict(dk=flat(past[3]), dv=flat(past[4]), k_len=p_len, n_valid_keys=p_len,
                             tail=(dk_b, dv_b))
            dk_all = dv_all = None
        else:
            dk_all = join(past[3], dk_b, dw)
            dv_all = join(past[4], dv_b, dw)

    if past is None:
        vaug_all = v_aug
    else:
        ones = jnp.ones((v_all.shape[0], HEAD_DIM), BF16)
        vaug_all = jnp.concatenate(
            [piece for g in range(n_kv) for piece in (v_all[:, g * HEAD_DIM:(g + 1) * HEAD_DIM], ones)], axis=1)
    tq_a = _pick_tile(s, (128, 64, 32))
    a_out = _dsa_attention(q_all, qi_all, kiw_f, k_all, vaug_all, ki_all, batch=b, q_len=s, k_len=k_len,
                           n_valid_keys=n_keys_valid, p_len=p_len, tq=tq_a, n_heads=n_heads, n_kv_heads=n_kv,
                           n_idx_heads=n_idx, n_sel=n_sel)
    tq_d = _pick_tile(s, (256, 128, 64, 32))
    if diff_keys is None:
        diff_keys = dict(dk=dk_all, dv=dv_all, k_len=k_len, n_valid_keys=n_keys_valid, tail=None)
    d_out = _diff_attention(dq_all, diff_keys["dk"], diff_keys["dv"], w["lams"], w["g_subln"], batch=b, q_len=s,
                            k_len=diff_keys["k_len"], n_valid_keys=diff_keys["n_valid_keys"], p_len=p_len,
                            tq=tq_d, n_heads=n_diff, lam_init=_lambda_init(layer), tail=diff_keys["tail"])

    x1 = _outproj(a_out, d_out, w["w_out"], x2, mod, per_token=per_token, rows_per_batch=s, tm=tm_big,
                  tn=_pick_tile(d, (1024, 512)))

    h2_t = _norm_mod(x1, w["g_norm_ffn"], mod, which=2, per_token=per_token, rows_per_batch=s, tm=tm_mid,
                     transpose=True)
    cnt, e1, rank, e2 = _peer_route(h2_t, w["peer_wq_t"], w["peer_keys"], tm=tm_mid)
    peer_t = _peer_ffn(h2_t, w["peer_u"], w["peer_v_t"], cnt, e1, rank, e2, tm=tm_mid, te=1024)
    x_out = _final(x1, peer_t, mod, w["g_final"], per_token=per_token, rows_per_batch=s, tm=tm_small,
                   normalize=last_layer)
    return x_out.reshape(b, s, d), new_rows


def kernel(x_prompt, x_sample, cache_dsa_k, cache_dsa_v, cache_idx_k, cache_diff_k, cache_diff_v, c_prompt, c_sample, w_ada, b_ada, g_norm_mix, g_norm_ffn, w_in, diff_lambda_q1, diff_lambda_k1, diff_lambda_q2, diff_lambda_k2, g_diff_subln, w_out, peer_w_query, peer_sub_keys, peer_u, peer_v, g_final):
    depth = w_in.shape[0]
    bp, bs = x_prompt.shape[0], x_sample.shape[0]
    n_kv = cache_dsa_k.shape[3]
    n_diff = cache_diff_k.shape[3]
    n_heads, n_idx_heads = DSA_HEADS, IDX_HEADS
    qw, kvw, dw = n_heads * HEAD_DIM, n_kv * HEAD_DIM, n_diff * 2 * HEAD_DIM
    tail0 = qw + 2 * kvw + n_idx_heads * LANES + LANES + n_idx_heads
    assert w_in.shape[2] == tail0 + 3 * dw
    dims = (n_heads, n_kv, n_idx_heads, n_diff)

    hp, hs = x_prompt, x_sample
    rows_p, rows_s = [], []
    n_c = bp + bs
    c_pad = jnp.pad(jnp.concatenate([c_prompt, c_sample], axis=0), ((0, (-n_c) % 16), (0, 0)))
    for l in range(depth):
        mod_all = _adaln(c_pad, w_ada[l], b_ada[l])
        w_in_bf = w_in[l].astype(BF16)
        weights = {
            "w_head": w_in_bf,
            "w_tail": w_in_bf[:, tail0:],
            "w_out": w_out[l].astype(BF16),
            "peer_wq_t": _transpose_cast(peer_w_query[l], BF16),
            "peer_keys": peer_sub_keys[l],
            "peer_u": peer_u[l].astype(BF16), "peer_v_t": _transpose_cast(peer_v[l], BF16),
            "g_norm_mix": g_norm_mix[l], "g_norm_ffn": g_norm_ffn[l], "g_subln": g_diff_subln[l],
            "lams": (diff_lambda_q1[l], diff_lambda_k1[l], diff_lambda_q2[l], diff_lambda_k2[l]),
            "g_final": g_final,
        }
        last = l == depth - 1
        hp, rp = _layer(hp, mod_all[:bp], None, l, last, weights, dims)
        past = (cache_dsa_k[l], cache_dsa_v[l], cache_idx_k[l], cache_diff_k[l], cache_diff_v[l])
        hs, rs = _layer(hs, mod_all[bp:bp + bs], past, l, last, weights, dims)
        rows_p.append(rp)
        rows_s.append(rs)
    stack = lambda rows, i: jnp.stack([r[i] for r in rows])
    return (hp, hs) + tuple(stack(rows_p, i) for i in range(5)) + tuple(stack(rows_s, i) for i in range(5))
```
